```python
import math
import jax, jax.numpy as jnp
from jax import lax
import numpy as np

D_MODEL = 1024
BATCH = 32
SEQ = 256
DEPTH = 2
DEC_BATCH = 2
DEC_SEQ = 2048
PAST_LEN = 256

GRID_W = 64
N_EVEN = (DEPTH + 1) // 2
N_ODD = DEPTH // 2
ML_HEADS = 4
ML_DK = 128
ML_DV = 128
ML_WIDTH = ML_HEADS * ML_DV
ML_CHUNK = 64
HG_HEADS = 4
HG_DK = 128
HG_DV = 128
HG_WIDTH = HG_HEADS * HG_DK
HG_CHUNK = 32
DA_HEADS = 8
DA_DQK = 64
DA_DV = 2 * DA_DQK
DA_WIDTH = DA_HEADS * DA_DV
Q_BLOCK = 128
ROPE_BASE = 10000.0
EPS = 1e-6
D_FF = ((8 * D_MODEL // 3 + 255) // 256) * 256
EV_SIZES = (ML_HEADS * ML_DK, ML_HEADS * ML_DK, ML_WIDTH, ML_WIDTH, 4 * ML_HEADS,
            HG_WIDTH, HG_WIDTH, HG_WIDTH, HG_HEADS * HG_DV, HG_HEADS * HG_DV)
EV_IN = sum(EV_SIZES)
OD_SIZES = (DA_HEADS * 2 * DA_DQK, DA_HEADS * 2 * DA_DQK, DA_WIDTH)
OD_IN = sum(OD_SIZES)

kernel_name = 'hybrid_mlstm_hgrn2_diffattn_prefix_dit'

f32 = jnp.float32


def _split(x, sizes):
    idx, acc = [], 0
    for s in sizes[:-1]:
        acc += s
        idx.append(acc)
    return jnp.split(x, idx, axis=-1)


def _rmsnorm(x, g):
    xf = x.astype(f32)
    return xf * lax.rsqrt(jnp.mean(xf * xf, axis=-1, keepdims=True) + EPS) * g.astype(f32)


def _head_rmsnorm(x, n_heads, g):
    B, S, W = x.shape
    xh = x.reshape(B, S, n_heads, W // n_heads)
    xh = xh * lax.rsqrt(jnp.mean(xh * xh, axis=-1, keepdims=True) + EPS)
    return xh.reshape(B, S, W) * g.astype(f32)


def _heads(x, n):
    B, S, _ = x.shape
    return x.astype(f32).reshape(B, S, n, -1).transpose(0, 2, 1, 3)


def _merge(x):
    B, H, S, d = x.shape
    return x.transpose(0, 2, 1, 3).reshape(B, S, H * d)


def _chunks(t, L):
    B, H, S = t.shape[:3]
    t = t.astype(f32).reshape((B, H, S // L, L) + t.shape[3:])
    return jnp.moveaxis(t, 2, 0)


def _unchunk(y):
    nc, B, H, L = y.shape[:4]
    y = jnp.moveaxis(y, 0, 2)
    return y.reshape((B, H, nc * L) + y.shape[4:])


def _flip(t):
    return jnp.flip(t, axis=2)


def _mlstm_chunkwise(q, k, v, ig, lf, C0, n0, m0):
    L = ML_CHUNK
    tri = jnp.tril(jnp.ones((L, L), bool))

    def step(carry, xs):
        C, n, m = carry
        qc, kc, vc, ic, fc = xs
        b = jnp.cumsum(fc, axis=-1)
        dmat = jnp.where(tri, b[..., :, None] - b[..., None, :] + ic[..., None, :], -jnp.inf)
        inter = b + m[..., None]
        mt = jnp.maximum(inter, jnp.max(dmat, axis=-1))
        w_inter = jnp.exp(inter - mt)
        s = jnp.einsum('bhtd,bhsd->bhts', qc, kc) * jnp.exp(dmat - mt[..., None])
        num = w_inter[..., None] * jnp.einsum('bhtd,bhde->bhte', qc, C) + jnp.einsum('bhts,bhse->bhte', s, vc)
        den = w_inter * jnp.einsum('bhtd,bhd->bht', qc, n) + jnp.sum(s, axis=-1)
        h = num / jnp.maximum(jnp.abs(den), jnp.exp(-mt))[..., None]
        bl = b[..., -1]
        g = bl[..., None] - b + ic
        m_new = jnp.maximum(bl + m, jnp.max(g, axis=-1))
        decay = jnp.exp(bl + m - m_new)
        wg = jnp.exp(g - m_new[..., None])
        C_new = decay[..., None, None] * C + jnp.einsum('bhs,bhsd,bhse->bhde', wg, kc, vc)
        n_new = decay[..., None] * n + jnp.einsum('bhs,bhsd->bhd', wg, kc)
        return (C_new, n_new, m_new), h

    xs = tuple(_chunks(t, L) for t in (q, k, v, ig, lf))
    (C, n, m), h = lax.scan(step, (C0.astype(f32), n0.astype(f32), m0.astype(f32)), xs)
    return _unchunk(h), C, n, m


def _hgrn2_chunkwise(q, lf, i, S0):
    L = HG_CHUNK
    tri = jnp.tril(jnp.ones((L, L), bool))[..., None]

    def step(S, xs):
        qc, fc, ic = xs
        a = jnp.cumsum(fc, axis=2)
        kc = -jnp.expm1(fc)
        inter = jnp.einsum('bhtd,bhde->bhte', qc * jnp.exp(a), S)
        rel = jnp.exp(jnp.where(tri, a[:, :, :, None, :] - a[:, :, None, :, :], -jnp.inf))
        sc = jnp.einsum('bhtd,bhtsd,bhsd->bhts', qc, rel, kc)
        o = inter + jnp.einsum('bhts,bhse->bhte', sc, ic)
        al = a[:, :, -1]
        S_new = jnp.exp(al)[..., None] * S + jnp.einsum('bhsd,bhse->bhde', kc * jnp.exp(al[:, :, None, :] - a), ic)
        return S_new, o

    xs = tuple(_chunks(t, L) for t in (q, lf, i))
    S, o = lax.scan(step, S0.astype(f32), xs)
    return _unchunk(o), S


def _axial_rope(x):
    B, H, N, _ = x.shape
    rows = N // GRID_W
    row = jnp.repeat(jnp.arange(rows, dtype=f32), GRID_W)
    col = jnp.tile(jnp.arange(GRID_W, dtype=f32), rows)
    half = DA_DQK // 2
    quarter = half // 2
    inv = ROPE_BASE ** (-jnp.arange(quarter, dtype=f32) / quarter)
    xr = x.astype(f32).reshape(B, H, N, 2, DA_DQK)

    def rot(xa, pos):
        ang = (pos[:, None] * inv[None, :])[:, None, :]
        cos, sin = jnp.cos(ang), jnp.sin(ang)
        x1, x2 = xa[..., :quarter], xa[..., quarter:]
        return jnp.concatenate([x1 * cos - x2 * sin, x1 * sin + x2 * cos], axis=-1)

    out = jnp.concatenate([rot(xr[..., :half], row), rot(xr[..., half:], col)], axis=-1)
    return out.reshape(B, H, N, 2 * DA_DQK)


def _diff_attention(q, k, v, lam):
    B, H, S, _ = q.shape
    nb = S // Q_BLOCK
    qb = jnp.moveaxis(q.astype(f32).reshape(B, H, nb, Q_BLOCK, 2 * DA_DQK), 2, 0)
    k = k.astype(f32)
    v = v.astype(f32)
    k1, k2 = k[..., :DA_DQK], k[..., DA_DQK:]
    scale = DA_DQK ** -0.5

    def blk(qblk):
        s1 = jnp.einsum('bhqd,bhkd->bhqk', qblk[..., :DA_DQK], k1) * scale
        s2 = jnp.einsum('bhqd,bhkd->bhqk', qblk[..., DA_DQK:], k2) * scale
        a = jax.nn.softmax(s1, axis=-1) - lam * jax.nn.softmax(s2, axis=-1)
        return jnp.einsum('bhqk,bhkd->bhqd', a, v)

    o = lax.map(blk, qb)
    return jnp.moveaxis(o, 0, 2).reshape(B, H, S, DA_DV)


def _modulation(cond, w, b):
    mod = jnp.einsum('...d,de->...e', jax.nn.silu(cond.astype(f32)), w.astype(f32)) + b.astype(f32)
    return jnp.split(mod[..., None, :], 6, axis=-1)


def _even_mixer(h, e, ev_w_in, ev_gate_b, ev_lb_logits, ml_norm_g, hg_norm_g, ev_w_out, C0, n0, m0, S0):
    B, S, _ = h.shape
    proj = jnp.einsum('bsd,de->bse', h, ev_w_in[e])
    ml_q, ml_k, ml_v, ml_o, ml_g, hg_q, hg_ff, hg_fb, hg_i, hg_o = _split(proj, EV_SIZES)
    q = _heads(ml_q, ML_HEADS) * (ML_DK ** -0.5)
    k = _heads(ml_k, ML_HEADS)
    v = _heads(ml_v, ML_HEADS)
    gates = (ml_g + ev_gate_b[e]).astype(f32).reshape(B, S, 4, ML_HEADS).transpose(2, 0, 3, 1)
    ig_f, ig_b, fg_f, fg_b = gates[0], gates[1], gates[2], gates[3]
    hf, Cf, nf, mf = _mlstm_chunkwise(q, k, v, ig_f, jax.nn.log_sigmoid(fg_f), C0[:, 0], n0[:, 0], m0[:, 0])
    hb, Cb, nb, mb = _mlstm_chunkwise(_flip(q), _flip(k), _flip(v), _flip(ig_b), _flip(jax.nn.log_sigmoid(fg_b)),
                                      C0[:, 1], n0[:, 1], m0[:, 1])
    ml_out = _head_rmsnorm(_merge(hf + _flip(hb)), ML_HEADS, ml_norm_g[e]) * jax.nn.sigmoid(ml_o.astype(f32))
    lb = jnp.cumsum(jax.nn.softmax(ev_lb_logits.astype(f32), axis=0), axis=0)[e]
    lf_f = _heads(jnp.log(lb + (1.0 - lb) * jax.nn.sigmoid(hg_ff.astype(f32))), HG_HEADS)
    lf_b = _heads(jnp.log(lb + (1.0 - lb) * jax.nn.sigmoid(hg_fb.astype(f32))), HG_HEADS)
    qh = _heads(hg_q, HG_HEADS)
    ih = _heads(hg_i, HG_HEADS)
    of, Sf = _hgrn2_chunkwise(qh, lf_f, ih, S0[:, 0])
    ob, Sb = _hgrn2_chunkwise(_flip(qh), _flip(lf_b), _flip(ih), S0[:, 1])
    hg_out = _head_rmsnorm(_merge(of + _flip(ob)), HG_HEADS, hg_norm_g[e]) * jax.nn.silu(hg_o.astype(f32))
    out = jnp.einsum('bse,ed->bsd', jnp.concatenate([ml_out, hg_out], axis=-1), ev_w_out[e])
    states = (jnp.stack([Cf, Cb], axis=1), jnp.stack([nf, nb], axis=1),
              jnp.stack([mf, mb], axis=1), jnp.stack([Sf, Sb], axis=1))
    return out, states


def _odd_mixer(h, o, layer_idx, od_w_in, od_lambda, da_norm_g, od_w_out, ctx_k, ctx_v):
    proj = jnp.einsum('bsd,de->bse', h, od_w_in[o])
    q, k, v = _split(proj, OD_SIZES)
    q = _heads(q, DA_HEADS)
    k = _heads(k, DA_HEADS)
    v = _heads(v, DA_HEADS)
    lam_init = 0.8 - 0.6 * math.exp(-0.3 * layer_idx)
    lp = od_lambda[o].astype(f32)
    lam = jnp.exp(jnp.sum(lp[0] * lp[1])) - jnp.exp(jnp.sum(lp[2] * lp[3])) + lam_init
    if ctx_k is None:
        att = _diff_attention(q, k, v, lam)
        cache = (k.transpose(0, 2, 1, 3), v.transpose(0, 2, 1, 3))
    else:
        q = _axial_rope(q)
        k = _axial_rope(k)
        kk = jnp.concatenate([ctx_k.astype(f32).transpose(0, 2, 1, 3), k], axis=2)
        vv = jnp.concatenate([ctx_v.astype(f32).transpose(0, 2, 1, 3), v], axis=2)
        att = _diff_attention(q, kk, vv, lam)
        cache = None
    att = _merge(att)
    att = _head_rmsnorm(att, DA_HEADS, jnp.tile(da_norm_g[o], DA_HEADS)) * (1.0 - lam_init)
    return jnp.einsum('bse,ed->bsd', att, od_w_out[o]), cache


def _swiglu(h, w1, w3, w2):
    return jnp.einsum('bsf,fd->bsd', jax.nn.silu(jnp.einsum('bsd,df->bsf', h, w1)) * jnp.einsum('bsd,df->bsf', h, w3), w2)


def setup_inputs(seed: int = 0) -> dict:
    key = jax.random.key(seed)
    ks = jax.random.split(key, 32)
    D = D_MODEL

    def nrm(k, shape, scale=1.0):
        return scale * jax.random.normal(k, shape, f32)

    gate_i = nrm(ks[10], (N_EVEN, 2 * ML_HEADS), 0.1)
    gate_f = 3.0 + 3.0 * jax.random.uniform(ks[11], (N_EVEN, 2 * ML_HEADS), f32)
    lb_logits = 2.0 * jnp.arange(N_EVEN + 1, dtype=f32)[:, None] + nrm(ks[12], (N_EVEN + 1, HG_WIDTH), 0.1)
    return {
        'x_prompt': nrm(ks[0], (BATCH, SEQ, D)),
        'x_sample': nrm(ks[1], (DEC_BATCH, DEC_SEQ, D)),
        'c': nrm(ks[2], (DEC_BATCH, D)),
        'c_ctx': nrm(ks[3], (D,)),
        'cache_attn_k': nrm(ks[4], (DEC_BATCH, N_ODD, PAST_LEN, DA_HEADS, 2 * DA_DQK)),
        'cache_attn_v': nrm(ks[5], (DEC_BATCH, N_ODD, PAST_LEN, DA_HEADS, DA_DV)),
        'state_mlstm_C': nrm(ks[6], (DEC_BATCH, N_EVEN, 2, ML_HEADS, ML_DK, ML_DV), 0.3),
        'state_mlstm_n': nrm(ks[7], (DEC_BATCH, N_EVEN, 2, ML_HEADS, ML_DK), 0.3),
        'state_mlstm_m': nrm(ks[8], (DEC_BATCH, N_EVEN, 2, ML_HEADS), 0.5),
        'state_hgrn_S': nrm(ks[9], (DEC_BATCH, N_EVEN, 2, HG_HEADS, HG_DK, HG_DV), 0.5),
        'ada_w': nrm(ks[13], (DEPTH, D, 6 * D), 0.3 * D ** -0.5),
        'ada_b': nrm(ks[14], (DEPTH, 6 * D), 0.05),
        'norm_mix_g': 1.0 + nrm(ks[15], (DEPTH, D), 0.05),
        'norm_ffn_g': 1.0 + nrm(ks[16], (DEPTH, D), 0.05),
        'ev_w_in': nrm(ks[17], (N_EVEN, D, EV_IN), D ** -0.5),
        'ev_gate_b': jnp.concatenate([gate_i, gate_f], axis=1),
        'ev_lb_logits': lb_logits,
        'ml_norm_g': 1.0 + nrm(ks[18], (N_EVEN, ML_WIDTH), 0.05),
        'hg_norm_g': 1.0 + nrm(ks[19], (N_EVEN, HG_WIDTH), 0.05),
        'ev_w_out': nrm(ks[20], (N_EVEN, ML_WIDTH + HG_WIDTH, D), (ML_WIDTH + HG_WIDTH) ** -0.5),
        'od_w_in': nrm(ks[21], (N_ODD, D, OD_IN), D ** -0.5),
        'od_lambda': nrm(ks[22], (N_ODD, 4, DA_DQK), 0.1),
        'da_norm_g': 1.0 + nrm(ks[23], (N_ODD, DA_DV), 0.05),
        'od_w_out': nrm(ks[24], (N_ODD, DA_WIDTH, D), DA_WIDTH ** -0.5),
        'ffn_w1': nrm(ks[25], (DEPTH, D, D_FF), D ** -0.5),
        'ffn_w3': nrm(ks[26], (DEPTH, D, D_FF), D ** -0.5),
        'ffn_w2': nrm(ks[27], (DEPTH, D_FF, D), D_FF ** -0.5),
        'final_norm_g': 1.0 + nrm(ks[28], (D,), 0.05),
    }


def reference(x_prompt, x_sample, c, c_ctx, cache_attn_k, cache_attn_v, state_mlstm_C, state_mlstm_n,
              state_mlstm_m, state_hgrn_S, ada_w, ada_b, norm_mix_g, norm_ffn_g, ev_w_in, ev_gate_b,
              ev_lb_logits, ml_norm_g, hg_norm_g, ev_w_out, od_w_in, od_lambda, da_norm_g, od_w_out,
              ffn_w1, ffn_w3, ffn_w2, final_norm_g):
    Bp = x_prompt.shape[0]
    zC = jnp.zeros((Bp, 2, ML_HEADS, ML_DK, ML_DV), f32)
    zn = jnp.zeros((Bp, 2, ML_HEADS, ML_DK), f32)
    zm = jnp.zeros((Bp, 2, ML_HEADS), f32)
    zS = jnp.zeros((Bp, 2, HG_HEADS, HG_DK, HG_DV), f32)
    xp = x_prompt.astype(f32)
    xs = x_sample.astype(f32)
    ks_, vs_, Cs_, ns_, ms_, Ss_ = [], [], [], [], [], []
    for l in range(DEPTH):
        sh_p, sc_p, g_p, sh2_p, sc2_p, g2_p = _modulation(c_ctx, ada_w[l], ada_b[l])
        sh_s, sc_s, g_s, sh2_s, sc2_s, g2_s = _modulation(c, ada_w[l], ada_b[l])
        hp = _rmsnorm(xp, norm_mix_g[l]) * (1.0 + sc_p) + sh_p
        hs = _rmsnorm(xs, norm_mix_g[l]) * (1.0 + sc_s) + sh_s
        if l % 2 == 0:
            e = l // 2
            mp, st = _even_mixer(hp, e, ev_w_in, ev_gate_b, ev_lb_logits, ml_norm_g, hg_norm_g, ev_w_out,
                                 zC, zn, zm, zS)
            ms, _ = _even_mixer(hs, e, ev_w_in, ev_gate_b, ev_lb_logits, ml_norm_g, hg_norm_g, ev_w_out,
                                state_mlstm_C[:, e], state_mlstm_n[:, e], state_mlstm_m[:, e], state_hgrn_S[:, e])
            Cs_.append(st[0])
            ns_.append(st[1])
            ms_.append(st[2])
            Ss_.append(st[3])
        else:
            o = l // 2
            mp, cache = _odd_mixer(hp, o, l, od_w_in, od_lambda, da_norm_g, od_w_out, None, None)
            ms, _ = _odd_mixer(hs, o, l, od_w_in, od_lambda, da_norm_g, od_w_out,
                               cache_attn_k[:, o], cache_attn_v[:, o])
            ks_.append(cache[0])
            vs_.append(cache[1])
        xp = xp + g_p * mp
        xs = xs + g_s * ms
        hp = _rmsnorm(xp, norm_ffn_g[l]) * (1.0 + sc2_p) + sh2_p
        hs = _rmsnorm(xs, norm_ffn_g[l]) * (1.0 + sc2_s) + sh2_s
        xp = xp + g2_p * _swiglu(hp, ffn_w1[l], ffn_w3[l], ffn_w2[l])
        xs = xs + g2_s * _swiglu(hs, ffn_w1[l], ffn_w3[l], ffn_w2[l])
    y_prompt = _rmsnorm(xp, final_norm_g)
    y_sample = _rmsnorm(xs, final_norm_g)
    new_attn_k = jnp.stack(ks_, axis=1)
    new_attn_v = jnp.stack(vs_, axis=1)
    new_mlstm_C = jnp.stack(Cs_, axis=1)
    new_mlstm_n = jnp.stack(ns_, axis=1)
    new_mlstm_m = jnp.stack(ms_, axis=1)
    new_hgrn_S = jnp.stack(Ss_, axis=1)
    return (y_prompt, y_sample, new_attn_k, new_attn_v, new_mlstm_C, new_mlstm_n, new_mlstm_m, new_hgrn_S)
```

```python
import functools
import math

import jax
import jax.numpy as jnp
from jax import lax
from jax.experimental import pallas as pl
from jax.experimental.pallas import tpu as pltpu

F32 = jnp.float32
BF16 = jnp.bfloat16

D = 1024
DEPTH = 2
GRID_W = 64
ML_HEADS = 4
HG_HEADS = 4
HEAD = 128
DA_HEADS = 8
DA_DQK = 64
ROPE_BASE = 10000.0
EPS = 1e-6
D_FF = ((8 * D // 3 + 255) // 256) * 256
EV_SIZES = (512, 512, 512, 512, 16, 512, 512, 512, 512, 512)
EV_COLS = 9 * 512 + 128

SEG = 256
LEVELS = (1, 2, 4, 8, 16, 32, 64, 128)
TOKEN_TILE = 512
FFN_CHUNK = 256
VMEM_LIMIT = 56 * 1024 * 1024


def _cparams(n_axes):
    return pltpu.CompilerParams(dimension_semantics=("arbitrary",) * n_axes, vmem_limit_bytes=VMEM_LIMIT)


def _sigmoid(x):
    return 1.0 / (1.0 + jnp.exp(-x))


def _silu(x):
    return x * _sigmoid(x)


def _log_sigmoid(x):
    return jnp.minimum(x, 0.0) - jnp.log(1.0 + jnp.exp(-jnp.abs(x)))


def _dot(a, b):
    return jnp.dot(a, b, preferred_element_type=F32)


def _dot_nt(a, b):
    return lax.dot_general(a, b, (((1,), (1,)), ((), ())), preferred_element_type=F32)


def _norm_mod(x, gain, mod, k):
    ms = jnp.mean(x * x, axis=-1, keepdims=True)
    return x * lax.rsqrt(ms + EPS) * gain * (1.0 + mod[3 * k + 1:3 * k + 2]) + mod[3 * k:3 * k + 1]


def _mod_kernel(c_ref, w_ref, b_ref, o_ref):
    s = _silu(c_ref[...]).astype(BF16)
    o_ref[...] = _dot(s, w_ref[...].astype(BF16)) + b_ref[...]


def _modulation(cond8, ada_w, ada_b):
    n_layers = ada_w.shape[0]
    tn = 1536
    return pl.pallas_call(
        _mod_kernel,
        grid=(n_layers, 6 * D // tn),
        in_specs=[pl.BlockSpec((8, D), lambda l, n: (0, 0)),
                  pl.BlockSpec((None, D, tn), lambda l, n: (l, 0, n)),
                  pl.BlockSpec((None, 1, tn), lambda l, n: (l, 0, n))],
        out_specs=pl.BlockSpec((None, 8, tn), lambda l, n: (l, 0, n)),
        out_shape=jax.ShapeDtypeStruct((n_layers, 8, 6 * D), F32),
        compiler_params=_cparams(2),
        name="ada_modulation",
    )(cond8, ada_w, ada_b.reshape(n_layers, 1, 6 * D))


def _inproj_even_kernel(x_ref, mod_ref, g_ref, w_ref, gb_ref, lbl_ref, qkv_ref, gate_ref, hqi_ref, hgf_ref,
                        og_ref, *, e_idx):
    h = _norm_mod(x_ref[...], g_ref[...], mod_ref[...], 0).astype(BF16)

    def proj(c0, n):
        return _dot(h, w_ref[:, c0:c0 + n])

    qkv_ref[:, 0:512] = (proj(0, 512) * (HEAD ** -0.5)).astype(BF16)
    qkv_ref[:, 512:1536] = proj(512, 1024).astype(BF16)
    og_ref[:, 0:512] = proj(1536, 512)
    hqi_ref[...] = proj(2048, 1024).astype(BF16)
    lg = lbl_ref[...]
    mx = jnp.max(lg, axis=0, keepdims=True)
    ex = jnp.exp(lg - mx)
    lb = jnp.sum(ex[0:e_idx + 1], axis=0, keepdims=True) / jnp.sum(ex, axis=0, keepdims=True)
    hgf_ref[:, 0:512] = jnp.log(lb + (1.0 - lb) * _sigmoid(proj(3072, 512)))
    hgf_ref[:, 512:1024] = jnp.log(lb + (1.0 - lb) * _sigmoid(proj(3584, 512)))
    og_ref[:, 512:1024] = proj(4096, 512)
    gt = proj(4608, 128) + gb_ref[...]
    lane = lax.broadcasted_iota(jnp.int32, gt.shape, 1)
    gate_ref[...] = jnp.where(lane < 8, gt, jnp.where(lane < 16, _log_sigmoid(gt), 0.0))


def _inproj_even(x, mod_l, row_of_tile, gain, w, gate_b, lb_logits, e_idx):
    n_tok = x.shape[0]
    tm = TOKEN_TILE
    tok = lambda n: pl.BlockSpec((tm, n), lambda i: (i, 0))
    full = lambda a: pl.BlockSpec(a.shape, lambda i: (0,) * a.ndim)
    return pl.pallas_call(
        functools.partial(_inproj_even_kernel, e_idx=e_idx),
        grid=(n_tok // tm,),
        in_specs=[tok(D), pl.BlockSpec((None, 6, D), lambda i: (row_of_tile(i), 0, 0)),
                  full(gain), full(w), full(gate_b), full(lb_logits)],
        out_specs=[tok(1536), tok(128), tok(1024), tok(1024), tok(1024)],
        out_shape=[jax.ShapeDtypeStruct((n_tok, 1536), BF16), jax.ShapeDtypeStruct((n_tok, 128), F32),
                   jax.ShapeDtypeStruct((n_tok, 1024), BF16), jax.ShapeDtypeStruct((n_tok, 1024), F32),
                   jax.ShapeDtypeStruct((n_tok, 1024), F32)],
        compiler_params=_cparams(1),
        name="inproj_even",
    )(x, mod_l, gain, w, gate_b, lb_logits)


def _shift_rows(x, k, fill, up):
    n = x.shape[0]
    if k % 8 == 0:
        pad = jnp.full((k,) + x.shape[1:], 0.0 if fill is None else fill, x.dtype)
        return jnp.concatenate([x[k:], pad], axis=0) if up else jnp.concatenate([pad, x[:n - k]], axis=0)
    y = pltpu.roll(x, (n - k) if up else k, 0)
    if fill is None:
        return y
    row = lax.broadcasted_iota(jnp.int32, x.shape, 0)
    return jnp.where(row >= n - k, fill, y) if up else jnp.where(row < k, fill, y)


def _cummax_rows(x, rev):
    k = 1
    while k < x.shape[0]:
        x = jnp.maximum(x, _shift_rows(x, k, -jnp.inf, up=rev))
        k *= 2
    return x


def _cumsum_rows(tri, x):
    hi = x.astype(BF16)
    r1 = x - hi.astype(F32)
    mid = r1.astype(BF16)
    lo = (r1 - mid.astype(F32)).astype(BF16)
    return _dot(tri, hi) + _dot(tri, mid) + _dot(tri, lo)


def _scan_kernel(*refs, has_init, emit_state, nseg):
    n_in = 8 + (4 if has_init else 0)
    n_out = 2 + (4 if emit_state else 0)
    ins, outs, scr = refs[:n_in], refs[n_in:n_in + n_out], refs[n_in + n_out:]
    dir_refs = (ins[0:4], ins[4:8])
    out_refs = outs[0:2]
    tri_sc, mask_sc, caug_sc, st_sc, m_sc = scr
    use_state = has_init or nseg > 1
    b_id, j = pl.program_id(0), pl.program_id(1)

    row = lax.broadcasted_iota(jnp.int32, (SEG, SEG), 0)
    col = lax.broadcasted_iota(jnp.int32, (SEG, SEG), 1)

    @pl.when(jnp.logical_and(b_id == 0, j == 0))
    def _build_constants():
        tri_sc[0] = (col <= row).astype(BF16)
        tri_sc[1] = (col >= row).astype(BF16)
        for li, m in enumerate(LEVELS):
            sh = m.bit_length() - 1
            same = (row >> (sh + 1)) == (col >> (sh + 1))
            t_up = ((row >> sh) & 1) == 1
            s_up = ((col >> sh) & 1) == 1
            mask_sc[0, li] = (same & t_up & jnp.logical_not(s_up)).astype(F32)
            mask_sc[1, li] = (same & s_up & jnp.logical_not(t_up)).astype(F32)
        eye = (row == col).astype(F32)
        mask_sc[0, len(LEVELS)] = eye
        mask_sc[1, len(LEVELS)] = eye

    if use_state:
        @pl.when(j == 0)
        def _init_state():
            if has_init:
                c0_ref, n0_ref, m0_ref, s0_ref = ins[8:12]
                for d in range(2):
                    for h in range(ML_HEADS):
                        caug_sc[d, h, :, 0:HEAD] = c0_ref[d, h]
                        caug_sc[d, h, :, HEAD:2 * HEAD] = jnp.broadcast_to(n0_ref[d, h:h + 1, :], (HEAD, HEAD)).T
                        st_sc[d, h] = s0_ref[d, h].T
                m_sc[...] = jnp.broadcast_to(m0_ref[...], m_sc.shape)
            else:
                caug_sc[...] = jnp.zeros(caug_sc.shape, F32)
                st_sc[...] = jnp.zeros(st_sc.shape, F32)
                m_sc[...] = jnp.zeros(m_sc.shape, F32)

    lane128 = lax.broadcasted_iota(jnp.int32, (SEG, HEAD), 1)
    row128 = lax.broadcasted_iota(jnp.int32, (SEG, HEAD), 0)
    e0 = (lane128 == 0).astype(BF16)
    up_masks = [((row128 >> (m.bit_length() - 1)) & 1) == 1 for m in LEVELS]
    tmasks = (col <= row, col >= row)
    m_in = m_sc[0:1, :] if use_state else jnp.zeros((1, HEAD), F32)
    m_out_rows = []

    for d in range(2):
        qkv_ref, gate_ref, hqi_ref, hgf_ref = dir_refs[d]
        out_ref = out_refs[d]
        rev = d == 1
        last = 0 if rev else SEG - 1
        tri = tri_sc[d]

        slab = gate_ref[...]
        b_al = pltpu.roll(_cumsum_rows(tri, slab), HEAD - 8, 1)
        u = slab - b_al
        mx = jnp.maximum(_cummax_rows(u, rev), m_in)
        w_inter = jnp.exp(m_in - mx)
        e_den = jnp.exp(-(b_al + mx))
        mx_last = mx[last:last + 1, :]
        m_out_rows.append(b_al[last:last + 1, :] + mx_last)
        decay = jnp.exp(m_in - mx_last)
        wg = jnp.exp(u - mx_last)
        u_t = u.T

        for h in range(ML_HEADS):
            c = 4 * d + h
            q = qkv_ref[:, h * HEAD:(h + 1) * HEAD]
            k = qkv_ref[:, 512 + h * HEAD:512 + (h + 1) * HEAD]
            v = qkv_ref[:, 1024 + h * HEAD:1024 + (h + 1) * HEAD]
            dm = jnp.where(tmasks[d], jnp.exp(u_t[c:c + 1, :] - mx[:, c:c + 1]), 0.0)
            s = (_dot_nt(q, k) * dm).astype(BF16)
            v_aug = jnp.concatenate([v, e0], axis=1)
            numden = _dot(s, v_aug)
            if use_state:
                numden = numden + w_inter[:, c:c + 1] * _dot(q, caug_sc[d, h].astype(BF16))
            den = jnp.maximum(jnp.abs(numden[:, HEAD:HEAD + 1]), e_den[:, c:c + 1])
            out_ref[:, h * HEAD:(h + 1) * HEAD] = numden[:, 0:HEAD] / den
            if use_state or emit_state:
                kw_t = (k.astype(F32) * wg[:, c:c + 1]).T.astype(BF16)
                upd = _dot(kw_t, v_aug)
                if use_state:
                    upd = upd + decay[:, c:c + 1] * caug_sc[d, h]
                    caug_sc[d, h] = upd
                if emit_state:
                    @pl.when(j == nseg - 1)
                    def _emit_ml(upd=upd, d=d, h=h):
                        outs[2][d, h] = upd[:, 0:HEAD]
                        outs[3][d, h:h + 1, :] = upd[:, HEAD:2 * HEAD].T[0:1, :]

        lf_all = hgf_ref[:, 512 * d:512 * (d + 1)]
        a_all = _cumsum_rows(tri, lf_all)
        kk_all = 1.0 - jnp.exp(lf_all)
        for h in range(HG_HEADS):
            sl = slice(h * HEAD, (h + 1) * HEAD)
            a = a_all[:, sl]
            kk = kk_all[:, sl]
            q_b = hqi_ref[:, h * HEAD:(h + 1) * HEAD]
            i_b = hqi_ref[:, 512 + h * HEAD:512 + (h + 1) * HEAD]
            q = q_b.astype(F32)
            p = _dot_nt(q_b, kk.astype(BF16)) * mask_sc[d, len(LEVELS)]
            bm = a
            for li, m in enumerate(LEVELS):
                qrole = jnp.logical_not(up_masks[li]) if rev else up_masks[li]
                x = jnp.where(qrole, _shift_rows(bm, m, 0.0, up=rev), bm)
                arg = jnp.where(qrole, a - x, x - a)
                r = (jnp.where(qrole, q, kk) * jnp.exp(arg)).astype(BF16)
                p = p + _dot_nt(r, r) * mask_sc[d, li]
                if m < LEVELS[-1]:
                    bm = jnp.where(qrole, bm, _shift_rows(bm, m, None, up=not rev))
            o = _dot(p.astype(BF16), i_b)
            if use_state:
                st = st_sc[d, h]
                o = o + _dot_nt((q * jnp.exp(a)).astype(BF16), st.astype(BF16))
            out_ref[:, 512 + h * HEAD:512 + (h + 1) * HEAD] = o
            if use_state or emit_state:
                a_l = a[last:last + 1, :]
                kd = (kk * jnp.exp(a_l - a)).astype(BF16)
                st_new = _dot(i_b.astype(F32).T.astype(BF16), kd)
                if use_state:
                    st_new = st_new + st * jnp.exp(a_l)
                    st_sc[d, h] = st_new
                if emit_state:
                    @pl.when(j == nseg - 1)
                    def _emit_hg(st_new=st_new, d=d, h=h):
                        outs[5][d, h] = st_new.T

    lane_row = lax.broadcasted_iota(jnp.int32, (1, HEAD), 1)
    m_new = jnp.where(lane_row < 4, m_out_rows[0], jnp.where(lane_row < 8, m_out_rows[1], 0.0))
    if use_state:
        m_sc[...] = jnp.broadcast_to(m_new, m_sc.shape)
    if emit_state:
        @pl.when(j == nseg - 1)
        def _emit_m():
            outs[4][...] = m_new


def _scan(qkv, gate, hqi, hgf, n_seq, nseg, init=None, emit_state=False):
    n_tok = qkv.shape[0]
    fwd = lambda n: pl.BlockSpec((SEG, n), lambda b, j: (b * nseg + j, 0))
    bwd = lambda n: pl.BlockSpec((SEG, n), lambda b, j: (b * nseg + nseg - 1 - j, 0))
    widths = (1536, 128, 1024, 1024)
    in_specs = [fwd(n) for n in widths] + [bwd(n) for n in widths]
    args = [qkv, gate, hqi, hgf, qkv, gate, hqi, hgf]
    mat = pl.BlockSpec((None, 2, 4, HEAD, HEAD), lambda b, j: (b, 0, 0, 0, 0))
    vec = pl.BlockSpec((None, 2, 4, HEAD), lambda b, j: (b, 0, 0, 0))
    sca = pl.BlockSpec((None, 1, HEAD), lambda b, j: (b, 0, 0))
    if init is not None:
        in_specs += [mat, vec, sca, mat]
        args += list(init)
    out_specs = [fwd(1024), bwd(1024)]
    out_shape = [jax.ShapeDtypeStruct((n_tok, 1024), F32)] * 2
    if emit_state:
        out_specs += [mat, vec, sca, mat]
        out_shape += [jax.ShapeDtypeStruct((n_seq, 2, 4, HEAD, HEAD), F32),
                      jax.ShapeDtypeStruct((n_seq, 2, 4, HEAD), F32),
                      jax.ShapeDtypeStruct((n_seq, 1, HEAD), F32),
                      jax.ShapeDtypeStruct((n_seq, 2, 4, HEAD, HEAD), F32)]
    return pl.pallas_call(
        functools.partial(_scan_kernel, has_init=init is not None, emit_state=emit_state, nseg=nseg),
        grid=(n_seq, nseg),
        in_specs=in_specs,
        out_specs=out_specs,
        out_shape=out_shape,
        scratch_shapes=[pltpu.VMEM((2, SEG, SEG), BF16),
                        pltpu.VMEM((2, len(LEVELS) + 1, SEG, SEG), F32),
                        pltpu.VMEM((2, 4, HEAD, 2 * HEAD), F32),
                        pltpu.VMEM((2, 4, HEAD, HEAD), F32),
                        pltpu.VMEM((8, HEAD), F32)],
        compiler_params=_cparams(2),
        name="bidir_scan",
    )(*args)


def _mixout_kernel(*refs, n_h, gated):
    h_refs = refs[:n_h]
    pos = n_h
    og_ref = refs[pos] if gated else None
    pos += 1 if gated else 0
    x_ref, mod_ref, ng_ref, w_ref, o_ref = refs[pos:pos + 5]
    h = h_refs[0][...]
    for r in h_refs[1:]:
        h = h + r[...]
    parts = []
    for g in range(D // HEAD):
        hs = h[:, g * HEAD:(g + 1) * HEAD]
        parts.append(hs * lax.rsqrt(jnp.mean(hs * hs, axis=-1, keepdims=True) + EPS))
    hn = jnp.concatenate(parts, axis=1) * ng_ref[...]
    if gated:
        og = og_ref[...]
        gate = jnp.concatenate([_sigmoid(og[:, 0:512]), _silu(og[:, 512:1024])], axis=1)
        hn = hn * gate
    o_ref[...] = x_ref[...] + mod_ref[2:3, :] * _dot(hn.astype(BF16), w_ref[...])


def _mixout(hs, og, x, mod_l, row_of_tile, norm_gain, w):
    n_tok = x.shape[0]
    tm = TOKEN_TILE
    tok = pl.BlockSpec((tm, D), lambda i: (i, 0))
    full = lambda a: pl.BlockSpec(a.shape, lambda i: (0,) * a.ndim)
    gated = og is not None
    args = list(hs) + ([og] if gated else []) + [x, mod_l, norm_gain, w]
    in_specs = [tok] * (len(hs) + (1 if gated else 0) + 1)
    in_specs += [pl.BlockSpec((None, 6, D), lambda i: (row_of_tile(i), 0, 0)), full(norm_gain), full(w)]
    return pl.pallas_call(
        functools.partial(_mixout_kernel, n_h=len(hs), gated=gated),
        grid=(n_tok // tm,),
        in_specs=in_specs,
        out_specs=tok,
        out_shape=jax.ShapeDtypeStruct((n_tok, D), F32),
        compiler_params=_cparams(1),
        name="mixer_out",
    )(*args)


def _ffn_kernel(*refs, final):
    x_ref, mod_ref, g_ref, w1_ref, w3_ref, w2_ref = refs[:6]
    fg_ref = refs[6] if final else None
    o_ref = refs[-1]
    x = x_ref[...]
    h = _norm_mod(x, g_ref[...], mod_ref[...], 1).astype(BF16)
    acc = jnp.zeros(x.shape, F32)
    for c0 in range(0, D_FF, FFN_CHUNK):
        a = _dot(h, w1_ref[:, c0:c0 + FFN_CHUNK])
        b = _dot(h, w3_ref[:, c0:c0 + FFN_CHUNK])
        acc = acc + _dot((_silu(a) * b).astype(BF16), w2_ref[c0:c0 + FFN_CHUNK, :])
    y = x + mod_ref[5:6, :] * acc
    if final:
        y = y * lax.rsqrt(jnp.mean(y * y, axis=-1, keepdims=True) + EPS) * fg_ref[...]
    o_ref[...] = y


def _ffn(x, mod_l, row_of_tile, gain, w1, w3, w2, final_gain=None):
    n_tok = x.shape[0]
    tm = TOKEN_TILE
    tok = pl.BlockSpec((tm, D), lambda i: (i, 0))
    full = lambda a: pl.BlockSpec(a.shape, lambda i: (0,) * a.ndim)
    final = final_gain is not None
    args = [x, mod_l, gain, w1, w3, w2] + ([final_gain] if final else [])
    in_specs = [tok, pl.BlockSpec((None, 6, D), lambda i: (row_of_tile(i), 0, 0))] + [full(a) for a in args[2:]]
    return pl.pallas_call(
        functools.partial(_ffn_kernel, final=final),
        grid=(n_tok // tm,),
        in_specs=in_specs,
        out_specs=tok,
        out_shape=jax.ShapeDtypeStruct((n_tok, D), F32),
        compiler_params=_cparams(1),
        name="swiglu_ffn",
    )(*args)


def _rope(x, cos, sin_signed):
    lane = lax.broadcasted_iota(jnp.int32, (x.shape[0], HEAD), 1)
    first = (lane & 16) == 0
    parts = []
    for h in range(DA_HEADS):
        xh = x[:, h * HEAD:(h + 1) * HEAD]
        partner = jnp.where(first, pltpu.roll(xh, HEAD - 16, 1), pltpu.roll(xh, 16, 1))
        parts.append(xh * cos + partner * sin_signed)
    return jnp.concatenate(parts, axis=1)


def _inproj_odd_kernel(*refs, rope):
    x_ref, mod_ref, g_ref, w_ref = refs[:4]
    q_ref, k_ref, v_ref = refs[-3:]
    h = _norm_mod(x_ref[...], g_ref[...], mod_ref[...], 0).astype(BF16)
    q = _dot(h, w_ref[:, 0:D])
    k = _dot(h, w_ref[:, D:2 * D])
    v = _dot(h, w_ref[:, 2 * D:3 * D])
    if rope:
        cos, sin_signed = refs[4][...], refs[5][...]
        q = _rope(q, cos, sin_signed)
        k = _rope(k, cos, sin_signed)
    q_ref[...] = (q * (DA_DQK ** -0.5)).astype(q_ref.dtype)
    k_ref[...] = k.astype(k_ref.dtype)
    v_ref[...] = v.astype(v_ref.dtype)


def _inproj_odd(x, mod_l, row_of_tile, gain, w, rope_tables, kv_dtype):
    n_tok = x.shape[0]
    tm = TOKEN_TILE
    tok = pl.BlockSpec((tm, D), lambda i: (i, 0))
    full = lambda a: pl.BlockSpec(a.shape, lambda i: (0,) * a.ndim)
    args = [x, mod_l, gain, w]
    in_specs = [tok, pl.BlockSpec((None, 6, D), lambda i: (row_of_tile(i), 0, 0)), full(gain), full(w)]
    if rope_tables is not None:
        tiles_per_seq = rope_tables[0].shape[0] // tm
        args += list(rope_tables)
        in_specs += [pl.BlockSpec((tm, HEAD), lambda i: (i % tiles_per_seq, 0))] * 2
    return pl.pallas_call(
        functools.partial(_inproj_odd_kernel, rope=rope_tables is not None),
        grid=(n_tok // tm,),
        in_specs=in_specs,
        out_specs=[tok, tok, tok],
        out_shape=[jax.ShapeDtypeStruct((n_tok, D), BF16), jax.ShapeDtypeStruct((n_tok, D), kv_dtype),
                   jax.ShapeDtypeStruct((n_tok, D), kv_dtype)],
        compiler_params=_cparams(1),
        name="inproj_odd",
    )(*args)


def _rope_tables(n_tok):
    quarter = DA_DQK // 4
    tok = jnp.arange(n_tok)
    pos = jnp.stack([(tok // GRID_W).astype(F32), (tok % GRID_W).astype(F32)], axis=1)
    inv = ROPE_BASE ** (-jnp.arange(quarter, dtype=F32) / quarter)
    lane = jnp.arange(HEAD)
    ang = pos[:, (lane // 32) % 2] * inv[lane % quarter][None, :]
    sign = jnp.where((lane % 32) < quarter, -1.0, 1.0).astype(F32)
    return jnp.cos(ang), jnp.sin(ang) * sign[None, :]


def _lambda(lam_ref, lam_init):
    lp = lam_ref[...]
    return (jnp.exp(jnp.sum(lp[0:1] * lp[1:2], axis=-1, keepdims=True))
            - jnp.exp(jnp.sum(lp[2:3] * lp[3:4], axis=-1, keepdims=True)) + lam_init)


def _diff_attn_head(qh, kh, vh, lam):
    lane = lax.broadcasted_iota(jnp.int32, qh.shape, 1)
    zero = jnp.zeros_like(qh)
    s1 = _dot_nt(jnp.where(lane < DA_DQK, qh, zero), kh)
    s2 = _dot_nt(jnp.where(lane >= DA_DQK, qh, zero), kh)
    e1 = jnp.exp(s1 - jnp.max(s1, axis=-1, keepdims=True))
    e2 = jnp.exp(s2 - jnp.max(s2, axis=-1, keepdims=True))
    a = e1 * (1.0 / jnp.sum(e1, axis=-1, keepdims=True)) - e2 * (lam / jnp.sum(e2, axis=-1, keepdims=True))
    return _dot(a.astype(BF16), vh)


def _attn_prompt_kernel(q_ref, k_ref, v_ref, lam_ref, o_ref, *, lam_init):
    lam = _lambda(lam_ref, lam_init)
    for h in range(DA_HEADS):
        sl = slice(h * HEAD, (h + 1) * HEAD)
        o_ref[:, sl] = _diff_attn_head(q_ref[:, sl], k_ref[:, sl].astype(BF16), v_ref[:, sl].astype(BF16), lam)


def _attn_prompt(q, k, v, lam_p, seq, lam_init):
    n_tok = q.shape[0]
    tok = pl.BlockSpec((seq, D), lambda b: (b, 0))
    return pl.pallas_call(
        functools.partial(_attn_prompt_kernel, lam_init=lam_init),
        grid=(n_tok // seq,),
        in_specs=[tok, tok, tok, pl.BlockSpec(lam_p.shape, lambda b: (0, 0))],
        out_specs=tok,
        out_shape=jax.ShapeDtypeStruct((n_tok, D), F32),
        compiler_params=_cparams(1),
        name="diff_attn_prompt",
    )(q, k, v, lam_p)


def _attn_sample_kernel(q_ref, k_ref, v_ref, ck_ref, cv_ref, lam_ref, o_ref, kcat, vcat, *, lam_init, past):
    @pl.when(pl.program_id(1) == 0)
    def _gather_keys():
        kcat[0:past, :] = ck_ref[...].astype(BF16)
        vcat[0:past, :] = cv_ref[...].astype(BF16)
        kcat[past:, :] = k_ref[...]
        vcat[past:, :] = v_ref[...]

    lam = _lambda(lam_ref, lam_init)
    for h in range(DA_HEADS):
        sl = slice(h * HEAD, (h + 1) * HEAD)
        o_ref[:, sl] = _diff_attn_head(q_ref[:, sl], kcat[:, sl], vcat[:, sl], lam)


def _attn_sample(q, k, v, ck, cv, lam_p, n_seq, seq, lam_init, tq=128):
    past = ck.shape[1]
    qb = pl.BlockSpec((tq, D), lambda b, i: (b * (seq // tq) + i, 0))
    kv = pl.BlockSpec((seq, D), lambda b, i: (b, 0))
    cache = pl.BlockSpec((None, past, D), lambda b, i: (b, 0, 0))
    return pl.pallas_call(
        functools.partial(_attn_sample_kernel, lam_init=lam_init, past=past),
        grid=(n_seq, seq // tq),
        in_specs=[qb, kv, kv, cache, cache, pl.BlockSpec(lam_p.shape, lambda b, i: (0, 0))],
        out_specs=qb,
        out_shape=jax.ShapeDtypeStruct((n_seq * seq, D), F32),
        scratch_shapes=[pltpu.VMEM((past + seq, D), BF16), pltpu.VMEM((past + seq, D), BF16)],
        compiler_params=_cparams(2),
        name="diff_attn_sample",
    )(q, k, v, ck, cv, lam_p)


def kernel(x_prompt, x_sample, c, c_ctx, cache_attn_k, cache_attn_v, state_mlstm_C, state_mlstm_n, state_mlstm_m,
           state_hgrn_S, ada_w, ada_b, norm_mix_g, norm_ffn_g, ev_w_in, ev_gate_b, ev_lb_logits, ml_norm_g,
           hg_norm_g, ev_w_out, od_w_in, od_lambda, da_norm_g, od_w_out, ffn_w1, ffn_w3, ffn_w2, final_norm_g):
    n_p, s_p, _ = x_prompt.shape
    n_s, s_s, _ = x_sample.shape
    past = cache_attn_k.shape[2]
    assert s_p == SEG and s_s % SEG == 0 and s_s % TOKEN_TILE == 0 and (n_p * s_p) % TOKEN_TILE == 0
    xp = x_prompt.astype(F32).reshape(n_p * s_p, D)
    xs = x_sample.astype(F32).reshape(n_s * s_s, D)

    cond8 = jnp.zeros((8, D), F32).at[0].set(c_ctx.astype(F32)).at[1:1 + n_s].set(c.astype(F32))
    mod = _modulation(cond8, ada_w.astype(F32), ada_b.astype(F32)).reshape(DEPTH, 8, 6, D)
    row_p = lambda i: 0
    row_s = lambda i: 1 + i // (s_s // TOKEN_TILE)
    streams = [(xp, row_p), (xs, row_s)]
    outputs = {}

    for l in range(DEPTH):
        mix_gain = norm_mix_g[l].astype(F32).reshape(1, D)
        ffn_gain = norm_ffn_g[l].astype(F32).reshape(1, D)
        if l % 2 == 0:
            e = l // 2
            parts = jnp.split(ev_w_in[e], [sum(EV_SIZES[:i + 1]) for i in range(len(EV_SIZES) - 1)], axis=1)
            ml_q, ml_k, ml_v, ml_o, ml_g, hg_q, hg_ff, hg_fb, hg_i, hg_o = parts
            w_in = jnp.concatenate([ml_q, ml_k, ml_v, ml_o, hg_q, hg_i, hg_ff, hg_fb, hg_o,
                                    jnp.pad(ml_g, ((0, 0), (0, HEAD - 16)))], axis=1).astype(BF16)
            gate_b = jnp.pad(ev_gate_b[e].astype(F32), (0, HEAD - 16)).reshape(1, HEAD)
            norm_gain = jnp.concatenate([ml_norm_g[e], hg_norm_g[e]]).astype(F32).reshape(1, D)
            w_out = ev_w_out[e].astype(BF16)
            mixed = []
            for si, (x, row_of) in enumerate(streams):
                qkv, gate, hqi, hgf, og = _inproj_even(x, mod[l], row_of, mix_gain, w_in, gate_b,
                                                       ev_lb_logits.astype(F32), e)
                if si == 0:
                    hf, hb, c_new, n_new, m_new, s_new = _scan(qkv, gate, hqi, hgf, n_p, s_p // SEG, emit_state=True)
                    outputs.setdefault("C", []).append(c_new)
                    outputs.setdefault("n", []).append(n_new)
                    outputs.setdefault("m", []).append(m_new[:, 0, 0:8].reshape(n_p, 2, 4))
                    outputs.setdefault("S", []).append(s_new)
                else:
                    init = (state_mlstm_C[:, e].astype(F32), state_mlstm_n[:, e].astype(F32),
                            jnp.pad(state_mlstm_m[:, e].astype(F32).reshape(n_s, 1, 8), ((0, 0), (0, 0), (0, HEAD - 8))),
                            state_hgrn_S[:, e].astype(F32))
                    hf, hb = _scan(qkv, gate, hqi, hgf, n_s, s_s // SEG, init=init)
                mixed.append(_mixout([hf, hb], og, x, mod[l], row_of, norm_gain, w_out))
        else:
            o = l // 2
            lam_init = 0.8 - 0.6 * math.exp(-0.3 * l)
            w_in = od_w_in[o].astype(BF16)
            norm_gain = (jnp.tile(da_norm_g[o].astype(F32), DA_HEADS) * (1.0 - lam_init)).reshape(1, D)
            w_out = od_w_out[o].astype(BF16)
            lam_p = od_lambda[o].astype(F32)
            mixed = []
            for si, (x, row_of) in enumerate(streams):
                if si == 0:
                    q, k, v = _inproj_odd(x, mod[l], row_of, mix_gain, w_in, None, F32)
                    outputs.setdefault("k", []).append(k.reshape(n_p, s_p, DA_HEADS, HEAD))
                    outputs.setdefault("v", []).append(v.reshape(n_p, s_p, DA_HEADS, HEAD))
                    att = _attn_prompt(q, k, v, lam_p, s_p, lam_init)
                else:
                    q, k, v = _inproj_odd(x, mod[l], row_of, mix_gain, w_in, _rope_tables(s_s), BF16)
                    ck = cache_attn_k[:, o].reshape(n_s, past, D)
                    cv = cache_attn_v[:, o].reshape(n_s, past, D)
                    att = _attn_sample(q, k, v, ck, cv, lam_p, n_s, s_s, lam_init)
                mixed.append(_mixout([att], None, x, mod[l], row_of, norm_gain, w_out))
        w1, w3, w2 = ffn_w1[l].astype(BF16), ffn_w3[l].astype(BF16), ffn_w2[l].astype(BF16)
        final_gain = final_norm_g.astype(F32).reshape(1, D) if l == DEPTH - 1 else None
        streams = [(_ffn(xm, mod[l], row_of, ffn_gain, w1, w3, w2, final_gain), row_of)
                   for xm, (_, row_of) in zip(mixed, streams)]

    y_prompt = streams[0][0].reshape(n_p, s_p, D)
    y_sample = streams[1][0].reshape(n_s, s_s, D)
    stack = lambda name: jnp.stack(outputs[name], axis=1)
    return (y_prompt, y_sample, stack("k"), stack("v"), stack("C"), stack("n"), stack("m"), stack("S"))
```

```python
import functools
import math

import jax
import jax.numpy as jnp
from jax import lax
from jax.experimental import pallas as pl
from jax.experimental.pallas import tpu as pltpu

F32 = jnp.float32
BF16 = jnp.bfloat16

D = 1024
DEPTH = 2
GRID_W = 64
ML_HEADS = 4
HG_HEADS = 4
HEAD = 128
DA_HEADS = 8
DA_DQK = 64
ROPE_BASE = 10000.0
EPS = 1e-6
D_FF = ((8 * D // 3 + 255) // 256) * 256
EV_SIZES = (512, 512, 512, 512, 16, 512, 512, 512, 512, 512)
EV_COLS = 9 * 512 + 128

SEG = 256
LEVELS = (1, 2, 4, 8, 16, 32, 64, 128)
TOKEN_TILE = 512
FFN_CHUNK = 256
KEY_CHUNK = 256
ATTN_LOOKAHEAD = 4
VMEM_LIMIT = 56 * 1024 * 1024


def _cparams(n_axes):
    return pltpu.CompilerParams(dimension_semantics=("arbitrary",) * n_axes, vmem_limit_bytes=VMEM_LIMIT)


def _sigmoid(x):
    return 1.0 / (1.0 + jnp.exp(-x))


def _silu(x):
    return x * _sigmoid(x)


def _log_sigmoid(x):
    return jnp.minimum(x, 0.0) - jnp.log(1.0 + jnp.exp(-jnp.abs(x)))


def _dot(a, b):
    return jnp.dot(a, b, preferred_element_type=F32)


def _dot_nt(a, b):
    return lax.dot_general(a, b, (((1,), (1,)), ((), ())), preferred_element_type=F32)


def _norm_mod(x, gain, mod, k):
    ms = jnp.mean(x * x, axis=-1, keepdims=True)
    return x * lax.rsqrt(ms + EPS) * gain * (1.0 + mod[3 * k + 1:3 * k + 2]) + mod[3 * k:3 * k + 1]


def _mod_kernel(c_ref, w_ref, b_ref, o_ref):
    s = _silu(c_ref[...]).astype(BF16)
    o_ref[...] = _dot(s, w_ref[...].astype(BF16)) + b_ref[...]


def _modulation(cond8, ada_w, ada_b):
    n_layers = ada_w.shape[0]
    tn = 1536
    return pl.pallas_call(
        _mod_kernel,
        grid=(n_layers, 6 * D // tn),
        in_specs=[pl.BlockSpec((8, D), lambda l, n: (0, 0)),
                  pl.BlockSpec((None, D, tn), lambda l, n: (l, 0, n)),
                  pl.BlockSpec((None, 1, tn), lambda l, n: (l, 0, n))],
        out_specs=pl.BlockSpec((None, 8, tn), lambda l, n: (l, 0, n)),
        out_shape=jax.ShapeDtypeStruct((n_layers, 8, 6 * D), F32),
        compiler_params=_cparams(2),
        name="ada_modulation",
    )(cond8, ada_w, ada_b.reshape(n_layers, 1, 6 * D))


def _inproj_even_kernel(x_ref, mod_ref, g_ref, w_ref, gb_ref, lbl_ref, qkv_ref, gate_ref, hqi_ref, hgf_ref,
                        og_ref, *, e_idx):
    h = _norm_mod(x_ref[...], g_ref[...], mod_ref[...], 0).astype(BF16)

    def proj(c0, n):
        return _dot(h, w_ref[:, c0:c0 + n])

    qkv_ref[:, 0:512] = (proj(0, 512) * (HEAD ** -0.5)).astype(BF16)
    qkv_ref[:, 512:1536] = proj(512, 1024).astype(BF16)
    og_ref[:, 0:512] = proj(1536, 512)
    hqi_ref[...] = proj(2048, 1024).astype(BF16)
    lg = lbl_ref[...]
    mx = jnp.max(lg, axis=0, keepdims=True)
    ex = jnp.exp(lg - mx)
    lb = jnp.sum(ex[0:e_idx + 1], axis=0, keepdims=True) / jnp.sum(ex, axis=0, keepdims=True)
    hgf_ref[:, 0:512] = jnp.log(lb + (1.0 - lb) * _sigmoid(proj(3072, 512)))
    hgf_ref[:, 512:1024] = jnp.log(lb + (1.0 - lb) * _sigmoid(proj(3584, 512)))
    og_ref[:, 512:1024] = proj(4096, 512)
    gt = proj(4608, 128) + gb_ref[...]
    lane = lax.broadcasted_iota(jnp.int32, gt.shape, 1)
    gate_ref[...] = jnp.where(lane < 8, gt, jnp.where(lane < 16, _log_sigmoid(gt), 0.0))


def _inproj_even(x, mod_l, row_of_tile, gain, w, gate_b, lb_logits, e_idx):
    n_tok = x.shape[0]
    tm = TOKEN_TILE
    tok = lambda n: pl.BlockSpec((tm, n), lambda i: (i, 0))
    full = lambda a: pl.BlockSpec(a.shape, lambda i: (0,) * a.ndim)
    return pl.pallas_call(
        functools.partial(_inproj_even_kernel, e_idx=e_idx),
        grid=(n_tok // tm,),
        in_specs=[tok(D), pl.BlockSpec((None, 6, D), lambda i: (row_of_tile(i), 0, 0)),
                  full(gain), full(w), full(gate_b), full(lb_logits)],
        out_specs=[tok(1536), tok(128), tok(1024), tok(1024), tok(1024)],
        out_shape=[jax.ShapeDtypeStruct((n_tok, 1536), BF16), jax.ShapeDtypeStruct((n_tok, 128), F32),
                   jax.ShapeDtypeStruct((n_tok, 1024), BF16), jax.ShapeDtypeStruct((n_tok, 1024), F32),
                   jax.ShapeDtypeStruct((n_tok, 1024), F32)],
        compiler_params=_cparams(1),
        name="inproj_even",
    )(x, mod_l, gain, w, gate_b, lb_logits)


def _shift_rows(x, k, fill, up):
    n = x.shape[0]
    if k % 8 == 0:
        pad = jnp.full((k,) + x.shape[1:], 0.0 if fill is None else fill, x.dtype)
        return jnp.concatenate([x[k:], pad], axis=0) if up else jnp.concatenate([pad, x[:n - k]], axis=0)
    y = pltpu.roll(x, (n - k) if up else k, 0)
    if fill is None:
        return y
    row = lax.broadcasted_iota(jnp.int32, x.shape, 0)
    return jnp.where(row >= n - k, fill, y) if up else jnp.where(row < k, fill, y)


def _cummax_rows(x, rev):
    k = 1
    while k < x.shape[0]:
        x = jnp.maximum(x, _shift_rows(x, k, -jnp.inf, up=rev))
        k *= 2
    return x


def _cumsum_rows(tri, x):
    hi = x.astype(BF16)
    r1 = x - hi.astype(F32)
    mid = r1.astype(BF16)
    lo = (r1 - mid.astype(F32)).astype(BF16)
    return _dot(tri, hi) + _dot(tri, mid) + _dot(tri, lo)


def _scan_kernel(*refs, has_init, emit_state, nseg):
    n_in = 8 + (4 if has_init else 0)
    n_out = 2 + (4 if emit_state else 0)
    ins, outs, scr = refs[:n_in], refs[n_in:n_in + n_out], refs[n_in + n_out:]
    dir_refs = (ins[0:4], ins[4:8])
    out_refs = outs[0:2]
    tri_sc, mask_sc, caug_sc, st_sc, m_sc = scr
    use_state = has_init or nseg > 1
    b_id, j = pl.program_id(0), pl.program_id(1)

    row = lax.broadcasted_iota(jnp.int32, (SEG, SEG), 0)
    col = lax.broadcasted_iota(jnp.int32, (SEG, SEG), 1)

    @pl.when(jnp.logical_and(b_id == 0, j == 0))
    def _build_constants():
        tri_sc[0] = (col <= row).astype(BF16)
        tri_sc[1] = (col >= row).astype(BF16)
        for li, m in enumerate(LEVELS):
            sh = m.bit_length() - 1
            same = (row >> (sh + 1)) == (col >> (sh + 1))
            t_up = ((row >> sh) & 1) == 1
            s_up = ((col >> sh) & 1) == 1
            mask_sc[0, li] = (same & t_up & jnp.logical_not(s_up)).astype(F32)
            mask_sc[1, li] = (same & s_up & jnp.logical_not(t_up)).astype(F32)
        eye = (row == col).astype(F32)
        mask_sc[0, len(LEVELS)] = eye
        mask_sc[1, len(LEVELS)] = eye

    if use_state:
        @pl.when(j == 0)
        def _init_state():
            if has_init:
                c0_ref, n0_ref, m0_ref, s0_ref = ins[8:12]
                for d in range(2):
                    for h in range(ML_HEADS):
                        caug_sc[d, h, :, 0:HEAD] = c0_ref[d, h]
                        caug_sc[d, h, :, HEAD:2 * HEAD] = jnp.broadcast_to(n0_ref[d, h:h + 1, :], (HEAD, HEAD)).T
                        st_sc[d, h] = s0_ref[d, h].T
                m_sc[...] = jnp.broadcast_to(m0_ref[...], m_sc.shape)
            else:
                caug_sc[...] = jnp.zeros(caug_sc.shape, F32)
                st_sc[...] = jnp.zeros(st_sc.shape, F32)
                m_sc[...] = jnp.zeros(m_sc.shape, F32)

    lane128 = lax.broadcasted_iota(jnp.int32, (SEG, HEAD), 1)
    row128 = lax.broadcasted_iota(jnp.int32, (SEG, HEAD), 0)
    e0 = (lane128 == 0).astype(BF16)
    up_masks = [((row128 >> (m.bit_length() - 1)) & 1) == 1 for m in LEVELS]
    tmasks = (col <= row, col >= row)
    m_in = m_sc[0:1, :] if use_state else jnp.zeros((1, HEAD), F32)
    m_out_rows = []

    for d in range(2):
        qkv_ref, gate_ref, hqi_ref, hgf_ref = dir_refs[d]
        out_ref = out_refs[d]
        rev = d == 1
        last = 0 if rev else SEG - 1
        tri = tri_sc[d]

        slab = gate_ref[...]
        b_al = pltpu.roll(_cumsum_rows(tri, slab), HEAD - 8, 1)
        u = slab - b_al
        mx = jnp.maximum(_cummax_rows(u, rev), m_in)
        w_inter = jnp.exp(m_in - mx)
        e_den = jnp.exp(-(b_al + mx))
        mx_last = mx[last:last + 1, :]
        m_out_rows.append(b_al[last:last + 1, :] + mx_last)
        decay = jnp.exp(m_in - mx_last)
        wg = jnp.exp(u - mx_last)
        u_t = u.T

        for h in range(ML_HEADS):
            c = 4 * d + h
            q = qkv_ref[:, h * HEAD:(h + 1) * HEAD]
            k = qkv_ref[:, 512 + h * HEAD:512 + (h + 1) * HEAD]
            v = qkv_ref[:, 1024 + h * HEAD:1024 + (h + 1) * HEAD]
            dm = jnp.where(tmasks[d], jnp.exp(u_t[c:c + 1, :] - mx[:, c:c + 1]), 0.0)
            s = (_dot_nt(q, k) * dm).astype(BF16)
            v_aug = jnp.concatenate([v, e0], axis=1)
            numden = _dot(s, v_aug)
            if use_state:
                numden = numden + w_inter[:, c:c + 1] * _dot(q, caug_sc[d, h].astype(BF16))
            den = jnp.maximum(jnp.abs(numden[:, HEAD:HEAD + 1]), e_den[:, c:c + 1])
            out_ref[:, h * HEAD:(h + 1) * HEAD] = numden[:, 0:HEAD] / den
            if use_state or emit_state:
                kw_t = (k.astype(F32) * wg[:, c:c + 1]).T.astype(BF16)
                upd = _dot(kw_t, v_aug)
                if use_state:
                    upd = upd + decay[:, c:c + 1] * caug_sc[d, h]
                    caug_sc[d, h] = upd
                if emit_state:
                    @pl.when(j == nseg - 1)
                    def _emit_ml(upd=upd, d=d, h=h):
                        outs[2][d, h] = upd[:, 0:HEAD]
                        outs[3][d, h:h + 1, :] = upd[:, HEAD:2 * HEAD].T[0:1, :]

        lf_all = hgf_ref[:, 512 * d:512 * (d + 1)]
        a_all = _cumsum_rows(tri, lf_all)
        kk_all = 1.0 - jnp.exp(lf_all)
        for h in range(HG_HEADS):
            sl = slice(h * HEAD, (h + 1) * HEAD)
            a = a_all[:, sl]
            kk = kk_all[:, sl]
            q_b = hqi_ref[:, h * HEAD:(h + 1) * HEAD]
            i_b = hqi_ref[:, 512 + h * HEAD:512 + (h + 1) * HEAD]
            q = q_b.astype(F32)
            p = _dot_nt(q_b, kk.astype(BF16)) * mask_sc[d, len(LEVELS)]
            bm = a
            for li, m in enumerate(LEVELS):
                qrole = jnp.logical_not(up_masks[li]) if rev else up_masks[li]
                x = jnp.where(qrole, _shift_rows(bm, m, 0.0, up=rev), bm)
                arg = jnp.where(qrole, a - x, x - a)
                r = (jnp.where(qrole, q, kk) * jnp.exp(arg)).astype(BF16)
                p = p + _dot_nt(r, r) * mask_sc[d, li]
                if m < LEVELS[-1]:
                    bm = jnp.where(qrole, bm, _shift_rows(bm, m, None, up=not rev))
            o = _dot(p.astype(BF16), i_b)
            if use_state:
                st = st_sc[d, h]
                o = o + _dot_nt((q * jnp.exp(a)).astype(BF16), st.astype(BF16))
            out_ref[:, 512 + h * HEAD:512 + (h + 1) * HEAD] = o
            if use_state or emit_state:
                a_l = a[last:last + 1, :]
                kd = (kk * jnp.exp(a_l - a)).astype(BF16)
                st_new = _dot(i_b.astype(F32).T.astype(BF16), kd)
                if use_state:
                    st_new = st_new + st * jnp.exp(a_l)
                    st_sc[d, h] = st_new
                if emit_state:
                    @pl.when(j == nseg - 1)
                    def _emit_hg(st_new=st_new, d=d, h=h):
                        outs[5][d, h] = st_new.T

    lane_row = lax.broadcasted_iota(jnp.int32, (1, HEAD), 1)
    m_new = jnp.where(lane_row < 4, m_out_rows[0], jnp.where(lane_row < 8, m_out_rows[1], 0.0))
    if use_state:
        m_sc[...] = jnp.broadcast_to(m_new, m_sc.shape)
    if emit_state:
        @pl.when(j == nseg - 1)
        def _emit_m():
            outs[4][...] = m_new


def _scan(qkv, gate, hqi, hgf, n_seq, nseg, init=None, emit_state=False):
    n_tok = qkv.shape[0]
    fwd = lambda n: pl.BlockSpec((SEG, n), lambda b, j: (b * nseg + j, 0))
    bwd = lambda n: pl.BlockSpec((SEG, n), lambda b, j: (b * nseg + nseg - 1 - j, 0))
    widths = (1536, 128, 1024, 1024)
    in_specs = [fwd(n) for n in widths] + [bwd(n) for n in widths]
    args = [qkv, gate, hqi, hgf, qkv, gate, hqi, hgf]
    mat = pl.BlockSpec((None, 2, 4, HEAD, HEAD), lambda b, j: (b, 0, 0, 0, 0))
    vec = pl.BlockSpec((None, 2, 4, HEAD), lambda b, j: (b, 0, 0, 0))
    sca = pl.BlockSpec((None, 1, HEAD), lambda b, j: (b, 0, 0))
    if init is not None:
        in_specs += [mat, vec, sca, mat]
        args += list(init)
    out_specs = [fwd(1024), bwd(1024)]
    out_shape = [jax.ShapeDtypeStruct((n_tok, 1024), F32)] * 2
    if emit_state:
        out_specs += [mat, vec, sca, mat]
        out_shape += [jax.ShapeDtypeStruct((n_seq, 2, 4, HEAD, HEAD), F32),
                      jax.ShapeDtypeStruct((n_seq, 2, 4, HEAD), F32),
                      jax.ShapeDtypeStruct((n_seq, 1, HEAD), F32),
                      jax.ShapeDtypeStruct((n_seq, 2, 4, HEAD, HEAD), F32)]
    return pl.pallas_call(
        functools.partial(_scan_kernel, has_init=init is not None, emit_state=emit_state, nseg=nseg),
        grid=(n_seq, nseg),
        in_specs=in_specs,
        out_specs=out_specs,
        out_shape=out_shape,
        scratch_shapes=[pltpu.VMEM((2, SEG, SEG), BF16),
                        pltpu.VMEM((2, len(LEVELS) + 1, SEG, SEG), F32),
                        pltpu.VMEM((2, 4, HEAD, 2 * HEAD), F32),
                        pltpu.VMEM((2, 4, HEAD, HEAD), F32),
                        pltpu.VMEM((8, HEAD), F32)],
        compiler_params=_cparams(2),
        name="bidir_scan",
    )(*args)


def _mix_ffn_kernel(*refs, n_h, gated, final):
    h_refs = refs[:n_h]
    pos = n_h
    og_ref = refs[pos] if gated else None
    pos += 1 if gated else 0
    x_ref, mod_ref, ng_ref, wo_ref, g_ref, w1_ref, w3_ref, w2_ref = refs[pos:pos + 8]
    fg_ref = refs[pos + 8] if final else None
    o_ref = refs[-1]
    mod = mod_ref[...]
    h = h_refs[0][...]
    for r in h_refs[1:]:
        h = h + r[...]
    parts = []
    for g in range(D // HEAD):
        hs = h[:, g * HEAD:(g + 1) * HEAD]
        parts.append(hs * lax.rsqrt(jnp.mean(hs * hs, axis=-1, keepdims=True) + EPS))
    hn = jnp.concatenate(parts, axis=1) * ng_ref[...]
    if gated:
        og = og_ref[...]
        hn = hn * jnp.concatenate([_sigmoid(og[:, 0:512]), _silu(og[:, 512:1024])], axis=1)
    x = x_ref[...] + mod[2:3, :] * _dot(hn.astype(BF16), wo_ref[...])
    hf = _norm_mod(x, g_ref[...], mod, 1).astype(BF16)
    acc = jnp.zeros(x.shape, F32)
    for c0 in range(0, D_FF, FFN_CHUNK):
        a = _dot(hf, w1_ref[:, c0:c0 + FFN_CHUNK])
        b = _dot(hf, w3_ref[:, c0:c0 + FFN_CHUNK])
        acc = acc + _dot((_silu(a) * b).astype(BF16), w2_ref[c0:c0 + FFN_CHUNK, :])
    y = x + mod[5:6, :] * acc
    if final:
        y = y * lax.rsqrt(jnp.mean(y * y, axis=-1, keepdims=True) + EPS) * fg_ref[...]
    o_ref[...] = y


def _mix_ffn(hs, og, x, mod_l, row_of_tile, norm_gain, w_out, ffn_gain, w1, w3, w2, final_gain=None):
    n_tok = x.shape[0]
    tm = TOKEN_TILE
    tok = pl.BlockSpec((tm, D), lambda i: (i, 0))
    full = lambda a: pl.BlockSpec(a.shape, lambda i: (0,) * a.ndim, pipeline_mode=pl.Buffered(1))
    gated = og is not None
    final = final_gain is not None
    consts = [norm_gain, w_out, ffn_gain, w1, w3, w2] + ([final_gain] if final else [])
    args = list(hs) + ([og] if gated else []) + [x, mod_l] + consts
    in_specs = [tok] * (len(hs) + (1 if gated else 0) + 1)
    in_specs += [pl.BlockSpec((None, 6, D), lambda i: (row_of_tile(i), 0, 0))] + [full(a) for a in consts]
    return pl.pallas_call(
        functools.partial(_mix_ffn_kernel, n_h=len(hs), gated=gated, final=final),
        grid=(n_tok // tm,),
        in_specs=in_specs,
        out_specs=tok,
        out_shape=jax.ShapeDtypeStruct((n_tok, D), F32),
        compiler_params=_cparams(1),
        name="mix_ffn",
    )(*args)


def _rope(x, cos, sin_signed):
    lane = lax.broadcasted_iota(jnp.int32, (x.shape[0], HEAD), 1)
    first = (lane & 16) == 0
    parts = []
    for h in range(DA_HEADS):
        xh = x[:, h * HEAD:(h + 1) * HEAD]
        partner = jnp.where(first, pltpu.roll(xh, HEAD - 16, 1), pltpu.roll(xh, 16, 1))
        parts.append(xh * cos + partner * sin_signed)
    return jnp.concatenate(parts, axis=1)


def _inproj_odd_kernel(*refs, rope):
    x_ref, mod_ref, g_ref, w_ref = refs[:4]
    q_ref, k_ref, v_ref = refs[-3:]
    h = _norm_mod(x_ref[...], g_ref[...], mod_ref[...], 0).astype(BF16)
    q = _dot(h, w_ref[:, 0:D])
    k = _dot(h, w_ref[:, D:2 * D])
    v = _dot(h, w_ref[:, 2 * D:3 * D])
    if rope:
        cos, sin_signed = refs[4][...], refs[5][...]
        q = _rope(q, cos, sin_signed)
        k = _rope(k, cos, sin_signed)
    q_ref[...] = (q * (DA_DQK ** -0.5)).astype(q_ref.dtype)
    k_ref[...] = k.astype(k_ref.dtype)
    v_ref[...] = v.astype(v_ref.dtype)


def _inproj_odd(x, mod_l, row_of_tile, gain, w, rope_tables, kv_dtype):
    n_tok = x.shape[0]
    tm = TOKEN_TILE
    tok = pl.BlockSpec((tm, D), lambda i: (i, 0))
    full = lambda a: pl.BlockSpec(a.shape, lambda i: (0,) * a.ndim)
    args = [x, mod_l, gain, w]
    in_specs = [tok, pl.BlockSpec((None, 6, D), lambda i: (row_of_tile(i), 0, 0)), full(gain), full(w)]
    if rope_tables is not None:
        tiles_per_seq = rope_tables[0].shape[0] // tm
        args += list(rope_tables)
        in_specs += [pl.BlockSpec((tm, HEAD), lambda i: (i % tiles_per_seq, 0))] * 2
    return pl.pallas_call(
        functools.partial(_inproj_odd_kernel, rope=rope_tables is not None),
        grid=(n_tok // tm,),
        in_specs=in_specs,
        out_specs=[tok, tok, tok],
        out_shape=[jax.ShapeDtypeStruct((n_tok, D), BF16), jax.ShapeDtypeStruct((n_tok, D), kv_dtype),
                   jax.ShapeDtypeStruct((n_tok, D), kv_dtype)],
        compiler_params=_cparams(1),
        name="inproj_odd",
    )(*args)


def _rope_tables(n_tok):
    quarter = DA_DQK // 4
    tok = jnp.arange(n_tok)
    pos = jnp.stack([(tok // GRID_W).astype(F32), (tok % GRID_W).astype(F32)], axis=1)
    inv = ROPE_BASE ** (-jnp.arange(quarter, dtype=F32) / quarter)
    lane = jnp.arange(HEAD)
    ang = pos[:, (lane // 32) % 2] * inv[lane % quarter][None, :]
    sign = jnp.where((lane % 32) < quarter, -1.0, 1.0).astype(F32)
    return jnp.cos(ang), jnp.sin(ang) * sign[None, :]


def _lambda(lam_ref, lam_init):
    lp = lam_ref[...]
    return (jnp.exp(jnp.sum(lp[0:1] * lp[1:2], axis=-1, keepdims=True))
            - jnp.exp(jnp.sum(lp[2:3] * lp[3:4], axis=-1, keepdims=True)) + lam_init)


VT_ROWS = HEAD + 16


def _qt2(qh):
    qt = qh.astype(F32).T
    row = lax.broadcasted_iota(jnp.int32, qt.shape, 0)
    return jnp.concatenate([jnp.where(row < DA_DQK, qt, 0.0), jnp.where(row >= DA_DQK, qt, 0.0)],
                           axis=1).astype(BF16)


def _vt_aug(vh):
    tk = vh.shape[0]
    row = lax.broadcasted_iota(jnp.int32, (VT_ROWS - HEAD, tk), 0)
    return jnp.concatenate([vh.astype(F32).T, (row == 0).astype(F32)], axis=0).astype(BF16)


def _diff_attn_heads(q_ref, k_fn, vt_fn, n_chunks, lam, o_ref):
    tq = q_ref.shape[0]
    items = [(h, c) for h in range(DA_HEADS) for c in range(n_chunks)]
    qt2, state, pending = {}, {}, []

    def finish(h, c, st):
        cm = jnp.max(st, axis=0, keepdims=True)
        if c == 0:
            m_new = cm
            acc = _dot(vt_fn(h, c), jnp.exp(st - m_new).astype(BF16))
        else:
            m, acc = state[h]
            m_new = jnp.maximum(m, cm)
            acc = acc * jnp.exp(m - m_new) + _dot(vt_fn(h, c), jnp.exp(st - m_new).astype(BF16))
        state[h] = (m_new, acc)
        if c == n_chunks - 1:
            den = acc[HEAD:HEAD + 1, :]
            o_t = acc[0:HEAD, 0:tq] * (1.0 / den[:, 0:tq]) - acc[0:HEAD, tq:] * (lam / den[:, tq:])
            o_ref[:, h * HEAD:(h + 1) * HEAD] = o_t.T

    for h, c in items:
        if c == 0:
            qt2[h] = _qt2(q_ref[:, h * HEAD:(h + 1) * HEAD])
        pending.append((h, c, _dot(k_fn(h, c), qt2[h])))
        if len(pending) > ATTN_LOOKAHEAD:
            finish(*pending.pop(0))
    while pending:
        finish(*pending.pop(0))


def _attn_prompt_kernel(q_ref, k_ref, v_ref, lam_ref, o_ref, *, lam_init):
    lam = _lambda(lam_ref, lam_init)
    _diff_attn_heads(q_ref, lambda h, c: k_ref[:, h * HEAD:(h + 1) * HEAD].astype(BF16),
                     lambda h, c: _vt_aug(v_ref[:, h * HEAD:(h + 1) * HEAD]), 1, lam, o_ref)


def _attn_prompt(q, k, v, lam_p, seq, lam_init):
    n_tok = q.shape[0]
    tok = pl.BlockSpec((seq, D), lambda b: (b, 0))
    return pl.pallas_call(
        functools.partial(_attn_prompt_kernel, lam_init=lam_init),
        grid=(n_tok // seq,),
        in_specs=[tok, tok, tok, pl.BlockSpec(lam_p.shape, lambda b: (0, 0))],
        out_specs=tok,
        out_shape=jax.ShapeDtypeStruct((n_tok, D), F32),
        compiler_params=_cparams(1),
        name="diff_attn_prompt",
    )(q, k, v, lam_p)


def _attn_sample_kernel(q_ref, k_ref, v_ref, ck_ref, cv_ref, lam_ref, o_ref, kcat, vtcat, *, lam_init, past):
    n_keys = kcat.shape[0]
    chunks = [(c0, min(KEY_CHUNK, n_keys - c0)) for c0 in range(0, n_keys, KEY_CHUNK)]

    @pl.when(pl.program_id(1) == 0)
    def _gather_keys():
        kcat[0:past, :] = ck_ref[...].astype(BF16)
        kcat[past:, :] = k_ref[...]
        for h in range(DA_HEADS):
            sl = slice(h * HEAD, (h + 1) * HEAD)
            for c0, n in chunks:
                src, r0 = (cv_ref, c0) if c0 < past else (v_ref, c0 - past)
                vtcat[h, :, c0:c0 + n] = _vt_aug(src[r0:r0 + n, sl])

    lam = _lambda(lam_ref, lam_init)
    _diff_attn_heads(q_ref, lambda h, c: kcat[chunks[c][0]:chunks[c][0] + chunks[c][1], h * HEAD:(h + 1) * HEAD],
                     lambda h, c: vtcat[h, :, chunks[c][0]:chunks[c][0] + chunks[c][1]], len(chunks), lam, o_ref)


def _attn_sample(q, k, v, ck, cv, lam_p, n_seq, seq, lam_init, tq=128):
    past = ck.shape[1]
    qb = pl.BlockSpec((tq, D), lambda b, i: (b * (seq // tq) + i, 0))
    kv = pl.BlockSpec((seq, D), lambda b, i: (b, 0))
    cache = pl.BlockSpec((None, past, D), lambda b, i: (b, 0, 0))
    return pl.pallas_call(
        functools.partial(_attn_sample_kernel, lam_init=lam_init, past=past),
        grid=(n_seq, seq // tq),
        in_specs=[qb, kv, kv, cache, cache, pl.BlockSpec(lam_p.shape, lambda b, i: (0, 0))],
        out_specs=qb,
        out_shape=jax.ShapeDtypeStruct((n_seq * seq, D), F32),
        scratch_shapes=[pltpu.VMEM((past + seq, D), BF16), pltpu.VMEM((DA_HEADS, VT_ROWS, past + seq), BF16)],
        compiler_params=_cparams(2),
        name="diff_attn_sample",
    )(q, k, v, ck, cv, lam_p)


def kernel(x_prompt, x_sample, c, c_ctx, cache_attn_k, cache_attn_v, state_mlstm_C, state_mlstm_n, state_mlstm_m,
           state_hgrn_S, ada_w, ada_b, norm_mix_g, norm_ffn_g, ev_w_in, ev_gate_b, ev_lb_logits, ml_norm_g,
           hg_norm_g, ev_w_out, od_w_in, od_lambda, da_norm_g, od_w_out, ffn_w1, ffn_w3, ffn_w2, final_norm_g):
    n_p, s_p, _ = x_prompt.shape
    n_s, s_s, _ = x_sample.shape
    past = cache_attn_k.shape[2]
    assert s_p == SEG and s_s % SEG == 0 and s_s % TOKEN_TILE == 0 and (n_p * s_p) % TOKEN_TILE == 0
    xp = x_prompt.astype(F32).reshape(n_p * s_p, D)
    xs = x_sample.astype(F32).reshape(n_s * s_s, D)

    cond8 = jnp.zeros((8, D), F32).at[0].set(c_ctx.astype(F32)).at[1:1 + n_s].set(c.astype(F32))
    mod = _modulation(cond8, ada_w.astype(F32), ada_b.astype(F32)).reshape(DEPTH, 8, 6, D)
    row_p = lambda i: 0
    row_s = lambda i: 1 + i // (s_s // TOKEN_TILE)
    streams = [(xp, row_p), (xs, row_s)]
    outputs = {}

    for l in range(DEPTH):
        mix_gain = norm_mix_g[l].astype(F32).reshape(1, D)
        ffn_gain = norm_ffn_g[l].astype(F32).reshape(1, D)
        if l % 2 == 0:
            e = l // 2
            parts = jnp.split(ev_w_in[e], [sum(EV_SIZES[:i + 1]) for i in range(len(EV_SIZES) - 1)], axis=1)
            ml_q, ml_k, ml_v, ml_o, ml_g, hg_q, hg_ff, hg_fb, hg_i, hg_o = parts
            w_in = jnp.concatenate([ml_q, ml_k, ml_v, ml_o, hg_q, hg_i, hg_ff, hg_fb, hg_o,
                                    jnp.pad(ml_g, ((0, 0), (0, HEAD - 16)))], axis=1).astype(BF16)
            gate_b = jnp.pad(ev_gate_b[e].astype(F32), (0, HEAD - 16)).reshape(1, HEAD)
            norm_gain = jnp.concatenate([ml_norm_g[e], hg_norm_g[e]]).astype(F32).reshape(1, D)
            w_out = ev_w_out[e].astype(BF16)
            mixed = []
            for si, (x, row_of) in enumerate(streams):
                qkv, gate, hqi, hgf, og = _inproj_even(x, mod[l], row_of, mix_gain, w_in, gate_b,
                                                       ev_lb_logits.astype(F32), e)
                if si == 0:
                    hf, hb, c_new, n_new, m_new, s_new = _scan(qkv, gate, hqi, hgf, n_p, s_p // SEG, emit_state=True)
                    outputs.setdefault("C", []).append(c_new)
                    outputs.setdefault("n", []).append(n_new)
                    outputs.setdefault("m", []).append(m_new[:, 0, 0:8].reshape(n_p, 2, 4))
                    outputs.setdefault("S", []).append(s_new)
                else:
                    init = (state_mlstm_C[:, e].astype(F32), state_mlstm_n[:, e].astype(F32),
                            jnp.pad(state_mlstm_m[:, e].astype(F32).reshape(n_s, 1, 8), ((0, 0), (0, 0), (0, HEAD - 8))),
                            state_hgrn_S[:, e].astype(F32))
                    hf, hb = _scan(qkv, gate, hqi, hgf, n_s, s_s // SEG, init=init)
                mixed.append(([hf, hb], og, x))
        else:
            o = l // 2
            lam_init = 0.8 - 0.6 * math.exp(-0.3 * l)
            w_in = od_w_in[o].astype(BF16)
            norm_gain = (jnp.tile(da_norm_g[o].astype(F32), DA_HEADS) * (1.0 - lam_init)).reshape(1, D)
            w_out = od_w_out[o].astype(BF16)
            lam_p = od_lambda[o].astype(F32)
            mixed = []
            for si, (x, row_of) in enumerate(streams):
                if si == 0:
                    q, k, v = _inproj_odd(x, mod[l], row_of, mix_gain, w_in, None, F32)
                    outputs.setdefault("k", []).append(k.reshape(n_p, s_p, DA_HEADS, HEAD))
                    outputs.setdefault("v", []).append(v.reshape(n_p, s_p, DA_HEADS, HEAD))
                    att = _attn_prompt(q, k, v, lam_p, s_p, lam_init)
                else:
                    q, k, v = _inproj_odd(x, mod[l], row_of, mix_gain, w_in, _rope_tables(s_s), BF16)
                    ck = cache_attn_k[:, o].reshape(n_s, past, D)
                    cv = cache_attn_v[:, o].reshape(n_s, past, D)
                    att = _attn_sample(q, k, v, ck, cv, lam_p, n_s, s_s, lam_init)
                mixed.append(([att], None, x))
        w1, w3, w2 = ffn_w1[l].astype(BF16), ffn_w3[l].astype(BF16), ffn_w2[l].astype(BF16)
        final_gain = final_norm_g.astype(F32).reshape(1, D) if l == DEPTH - 1 else None
        streams = [(_mix_ffn(hs, og, x, mod[l], row_of, norm_gain, w_out, ffn_gain, w1, w3, w2, final_gain), row_of)
                   for (hs, og, x), (_, row_of) in zip(mixed, streams)]

    y_prompt = streams[0][0].reshape(n_p, s_p, D)
    y_sample = streams[1][0].reshape(n_s, s_s, D)
    stack = lambda name: jnp.stack(outputs[name], axis=1)
    return (y_prompt, y_sample, stack("k"), stack("v"), stack("C"), stack("n"), stack("m"), stack("S"))
```

```python
import functools
import math

import jax
import jax.numpy as jnp
from jax import lax
from jax.experimental import pallas as pl
from jax.experimental.pallas import tpu as pltpu

F32 = jnp.float32
BF16 = jnp.bfloat16

D = 1024
DEPTH = 2
GRID_W = 64
ML_HEADS = 4
HG_HEADS = 4
HEAD = 128
DA_HEADS = 8
DA_DQK = 64
ROPE_BASE = 10000.0
EPS = 1e-6
D_FF = ((8 * D // 3 + 255) // 256) * 256
EV_SIZES = (512, 512, 512, 512, 16, 512, 512, 512, 512, 512)
EV_COLS = 9 * 512 + 128

SEG = 256
HALF = SEG // 2
TILE_LEVELS = (1, 2, 4, 8, 16, 32, 64)
TOKEN_TILE = 512
FFN_CHUNK = 256
KEY_CHUNK = 256
ATTN_LOOKAHEAD = 4
VMEM_LIMIT = 56 * 1024 * 1024


def _cparams(n_axes):
    return pltpu.CompilerParams(dimension_semantics=("arbitrary",) * n_axes, vmem_limit_bytes=VMEM_LIMIT)


def _sigmoid(x):
    return 1.0 / (1.0 + jnp.exp(-x))


def _silu(x):
    return x * _sigmoid(x)


def _log_sigmoid(x):
    return jnp.minimum(x, 0.0) - jnp.log(1.0 + jnp.exp(-jnp.abs(x)))


def _dot(a, b):
    return jnp.dot(a, b, preferred_element_type=F32)


def _dot_nt(a, b):
    return lax.dot_general(a, b, (((1,), (1,)), ((), ())), preferred_element_type=F32)


def _norm_mod(x, gain, mod, k):
    ms = jnp.mean(x * x, axis=-1, keepdims=True)
    return x * lax.rsqrt(ms + EPS) * gain * (1.0 + mod[3 * k + 1:3 * k + 2]) + mod[3 * k:3 * k + 1]


def _mod_kernel(c_ref, w_ref, b_ref, o_ref):
    s = _silu(c_ref[...]).astype(BF16)
    o_ref[...] = _dot(s, w_ref[...].astype(BF16)) + b_ref[...]


def _modulation(cond8, ada_w, ada_b):
    n_layers = ada_w.shape[0]
    tn = 1536
    return pl.pallas_call(
        _mod_kernel,
        grid=(n_layers, 6 * D // tn),
        in_specs=[pl.BlockSpec((8, D), lambda l, n: (0, 0)),
                  pl.BlockSpec((None, D, tn), lambda l, n: (l, 0, n)),
                  pl.BlockSpec((None, 1, tn), lambda l, n: (l, 0, n))],
        out_specs=pl.BlockSpec((None, 8, tn), lambda l, n: (l, 0, n)),
        out_shape=jax.ShapeDtypeStruct((n_layers, 8, 6 * D), F32),
        compiler_params=_cparams(2),
        name="ada_modulation",
    )(cond8, ada_w, ada_b.reshape(n_layers, 1, 6 * D))


def _inproj_even_kernel(x_ref, mod_ref, g_ref, w_ref, gb_ref, lbl_ref, qkv_ref, gate_ref, hqi_ref, hgf_ref,
                        og_ref, *, e_idx):
    h = _norm_mod(x_ref[...], g_ref[...], mod_ref[...], 0).astype(BF16)

    def proj(c0, n):
        return _dot(h, w_ref[:, c0:c0 + n])

    qkv_ref[:, 0:512] = (proj(0, 512) * (HEAD ** -0.5)).astype(BF16)
    qkv_ref[:, 512:1536] = proj(512, 1024).astype(BF16)
    og_ref[:, 0:512] = proj(1536, 512)
    hqi_ref[...] = proj(2048, 1024).astype(BF16)
    lg = lbl_ref[...]
    mx = jnp.max(lg, axis=0, keepdims=True)
    ex = jnp.exp(lg - mx)
    lb = jnp.sum(ex[0:e_idx + 1], axis=0, keepdims=True) / jnp.sum(ex, axis=0, keepdims=True)
    hgf_ref[:, 0:512] = jnp.log(lb + (1.0 - lb) * _sigmoid(proj(3072, 512)))
    hgf_ref[:, 512:1024] = jnp.log(lb + (1.0 - lb) * _sigmoid(proj(3584, 512)))
    og_ref[:, 512:1024] = proj(4096, 512)
    gt = proj(4608, 128) + gb_ref[...]
    lane = lax.broadcasted_iota(jnp.int32, gt.shape, 1)
    gate_ref[...] = jnp.where(lane < 8, gt, jnp.where(lane < 16, _log_sigmoid(gt), 0.0))


def _inproj_even(x, mod_l, row_of_tile, gain, w, gate_b, lb_logits, e_idx):
    n_tok = x.shape[0]
    tm = TOKEN_TILE
    tok = lambda n: pl.BlockSpec((tm, n), lambda i: (i, 0))
    full = lambda a: pl.BlockSpec(a.shape, lambda i: (0,) * a.ndim)
    return pl.pallas_call(
        functools.partial(_inproj_even_kernel, e_idx=e_idx),
        grid=(n_tok // tm,),
        in_specs=[tok(D), pl.BlockSpec((None, 6, D), lambda i: (row_of_tile(i), 0, 0)),
                  full(gain), full(w), full(gate_b), full(lb_logits)],
        out_specs=[tok(1536), tok(128), tok(1024), tok(1024), tok(1024)],
        out_shape=[jax.ShapeDtypeStruct((n_tok, 1536), BF16), jax.ShapeDtypeStruct((n_tok, 128), F32),
                   jax.ShapeDtypeStruct((n_tok, 1024), BF16), jax.ShapeDtypeStruct((n_tok, 1024), F32),
                   jax.ShapeDtypeStruct((n_tok, 1024), F32)],
        compiler_params=_cparams(1),
        name="inproj_even",
    )(x, mod_l, gain, w, gate_b, lb_logits)


def _shift_rows(x, k, fill, up):
    n = x.shape[0]
    if k % 8 == 0:
        pad = jnp.full((k,) + x.shape[1:], 0.0 if fill is None else fill, x.dtype)
        return jnp.concatenate([x[k:], pad], axis=0) if up else jnp.concatenate([pad, x[:n - k]], axis=0)
    y = pltpu.roll(x, (n - k) if up else k, 0)
    if fill is None:
        return y
    row = lax.broadcasted_iota(jnp.int32, x.shape, 0)
    return jnp.where(row >= n - k, fill, y) if up else jnp.where(row < k, fill, y)


def _cummax_rows(x, rev):
    k = 1
    while k < x.shape[0]:
        x = jnp.maximum(x, _shift_rows(x, k, -jnp.inf, up=rev))
        k *= 2
    return x


def _cumsum_rows(tri, x):
    hi = x.astype(BF16)
    r1 = x - hi.astype(F32)
    mid = r1.astype(BF16)
    lo = (r1 - mid.astype(F32)).astype(BF16)
    return _dot(tri, hi) + _dot(tri, mid) + _dot(tri, lo)


def _hgrn_tile(q, kk, a, q_b, rev, mask_sc, d, up_small):
    p = _dot_nt(q_b, kk.astype(BF16)) * mask_sc[d, len(TILE_LEVELS)]
    bm = a
    for li, m in enumerate(TILE_LEVELS):
        if m < 8:
            qrole = jnp.logical_not(up_small[li]) if rev else up_small[li]
            x = jnp.where(qrole, _shift_rows(bm, m, None, up=rev), bm)
            r = jnp.where(qrole, q, kk) * jnp.exp(jnp.where(qrole, a - x, x - a))
            if 2 * m < 8:
                bm = jnp.where(qrole, bm, _shift_rows(bm, m, None, up=not rev))
        else:
            pieces = []
            for b0 in range(0, HALF, 2 * m):
                lo, up = slice(b0, b0 + m), slice(b0 + m, b0 + 2 * m)
                if rev:
                    ref = a[b0 + m:b0 + m + 1]
                    pieces += [q[lo] * jnp.exp(a[lo] - ref), kk[up] * jnp.exp(ref - a[up])]
                else:
                    ref = a[b0 + m - 1:b0 + m]
                    pieces += [kk[lo] * jnp.exp(ref - a[lo]), q[up] * jnp.exp(a[up] - ref)]
            r = jnp.concatenate(pieces, axis=0)
        rb = r.astype(BF16)
        p = p + _dot_nt(rb, rb) * mask_sc[d, li]
    return p


def _scan_kernel(*refs, has_init, emit_state, nseg):
    n_in = 8 + (4 if has_init else 0)
    n_out = 2 + (4 if emit_state else 0)
    ins, outs, scr = refs[:n_in], refs[n_in:n_in + n_out], refs[n_in + n_out:]
    dir_refs = (ins[0:4], ins[4:8])
    out_refs = outs[0:2]
    tri_sc, mask_sc, caug_sc, st_sc, m_sc = scr
    use_state = has_init or nseg > 1
    b_id, j = pl.program_id(0), pl.program_id(1)

    row = lax.broadcasted_iota(jnp.int32, (SEG, SEG), 0)
    col = lax.broadcasted_iota(jnp.int32, (SEG, SEG), 1)

    @pl.when(jnp.logical_and(b_id == 0, j == 0))
    def _build_constants():
        tri_sc[0] = (col <= row).astype(BF16)
        tri_sc[1] = (col >= row).astype(BF16)
        trow = lax.broadcasted_iota(jnp.int32, (HALF, HALF), 0)
        tcol = lax.broadcasted_iota(jnp.int32, (HALF, HALF), 1)
        for li, m in enumerate(TILE_LEVELS):
            sh = m.bit_length() - 1
            same = (trow >> (sh + 1)) == (tcol >> (sh + 1))
            t_up = ((trow >> sh) & 1) == 1
            s_up = ((tcol >> sh) & 1) == 1
            mask_sc[0, li] = (same & t_up & jnp.logical_not(s_up)).astype(F32)
            mask_sc[1, li] = (same & s_up & jnp.logical_not(t_up)).astype(F32)
        eye = (trow == tcol).astype(F32)
        mask_sc[0, len(TILE_LEVELS)] = eye
        mask_sc[1, len(TILE_LEVELS)] = eye

    if use_state:
        @pl.when(j == 0)
        def _init_state():
            if has_init:
                c0_ref, n0_ref, m0_ref, s0_ref = ins[8:12]
                for d in range(2):
                    for h in range(ML_HEADS):
                        caug_sc[d, h, :, 0:HEAD] = c0_ref[d, h]
                        caug_sc[d, h, :, HEAD:2 * HEAD] = jnp.broadcast_to(n0_ref[d, h:h + 1, :], (HEAD, HEAD)).T
                        st_sc[d, h] = s0_ref[d, h].T
                m_sc[...] = jnp.broadcast_to(m0_ref[...], m_sc.shape)
            else:
                caug_sc[...] = jnp.zeros(caug_sc.shape, F32)
                st_sc[...] = jnp.zeros(st_sc.shape, F32)
                m_sc[...] = jnp.zeros(m_sc.shape, F32)

    lane128 = lax.broadcasted_iota(jnp.int32, (SEG, HEAD), 1)
    row128 = lax.broadcasted_iota(jnp.int32, (SEG, HEAD), 0)
    e0 = (lane128 == 0).astype(BF16)
    row_half = lax.broadcasted_iota(jnp.int32, (HALF, HEAD), 0)
    up_masks = [((row_half >> (m.bit_length() - 1)) & 1) == 1 for m in TILE_LEVELS if m < 8]
    tmasks = (col <= row, col >= row)
    m_in = m_sc[0:1, :] if use_state else jnp.zeros((1, HEAD), F32)
    m_out_rows = []

    for d in range(2):
        qkv_ref, gate_ref, hqi_ref, hgf_ref = dir_refs[d]
        out_ref = out_refs[d]
        rev = d == 1
        last = 0 if rev else SEG - 1
        tri = tri_sc[d]

        slab = gate_ref[...]
        b_al = pltpu.roll(_cumsum_rows(tri, slab), HEAD - 8, 1)
        u = slab - b_al
        mx = jnp.maximum(_cummax_rows(u, rev), m_in)
        w_inter = jnp.exp(m_in - mx)
        e_den = jnp.exp(-(b_al + mx))
        mx_last = mx[last:last + 1, :]
        m_out_rows.append(b_al[last:last + 1, :] + mx_last)
        decay = jnp.exp(m_in - mx_last)
        wg = jnp.exp(u - mx_last)
        u_t = u.T

        for h in range(ML_HEADS):
            c = 4 * d + h
            q = qkv_ref[:, h * HEAD:(h + 1) * HEAD]
            k = qkv_ref[:, 512 + h * HEAD:512 + (h + 1) * HEAD]
            v = qkv_ref[:, 1024 + h * HEAD:1024 + (h + 1) * HEAD]
            dm = jnp.where(tmasks[d], jnp.exp(u_t[c:c + 1, :] - mx[:, c:c + 1]), 0.0)
            s = (_dot_nt(q, k) * dm).astype(BF16)
            v_aug = jnp.concatenate([v, e0], axis=1)
            numden = _dot(s, v_aug)
            if use_state:
                numden = numden + w_inter[:, c:c + 1] * _dot(q, caug_sc[d, h].astype(BF16))
            den = jnp.maximum(jnp.abs(numden[:, HEAD:HEAD + 1]), e_den[:, c:c + 1])
            out_ref[:, h * HEAD:(h + 1) * HEAD] = numden[:, 0:HEAD] / den
            if use_state or emit_state:
                kw_t = (k.astype(F32) * wg[:, c:c + 1]).T.astype(BF16)
                upd = _dot(kw_t, v_aug)
                if use_state:
                    upd = upd + decay[:, c:c + 1] * caug_sc[d, h]
                    caug_sc[d, h] = upd
                if emit_state:
                    @pl.when(j == nseg - 1)
                    def _emit_ml(upd=upd, d=d, h=h):
                        outs[2][d, h] = upd[:, 0:HEAD]
                        outs[3][d, h:h + 1, :] = upd[:, HEAD:2 * HEAD].T[0:1, :]

        lf_all = hgf_ref[:, 512 * d:512 * (d + 1)]
        a_all = _cumsum_rows(tri, lf_all)
        kk_all = 1.0 - jnp.exp(lf_all)
        for h in range(HG_HEADS):
            sl = slice(h * HEAD, (h + 1) * HEAD)
            a = a_all[:, sl]
            kk = kk_all[:, sl]
            q_b = hqi_ref[:, h * HEAD:(h + 1) * HEAD]
            i_b = hqi_ref[:, 512 + h * HEAD:512 + (h + 1) * HEAD]
            q = q_b.astype(F32)
            qh, kh = (0, 1) if rev else (1, 0)
            a_q, a_k = a[qh * HALF:(qh + 1) * HALF], a[kh * HALF:(kh + 1) * HALF]
            a_mid = a[HALF:HALF + 1] if rev else a[HALF - 1:HALF]
            r_q = (q[qh * HALF:(qh + 1) * HALF] * jnp.exp(a_q - a_mid)).astype(BF16)
            r_k = (kk[kh * HALF:(kh + 1) * HALF] * jnp.exp(a_mid - a_k)).astype(BF16)
            p_cross = _dot_nt(r_q, r_k).astype(BF16)
            p_tiles = []
            for t in range(2):
                rows = slice(t * HALF, (t + 1) * HALF)
                p_tiles.append(_hgrn_tile(q[rows], kk[rows], a[rows], q_b[rows], rev, mask_sc, d,
                                          up_masks).astype(BF16))
            i_lo, i_hi = i_b[0:HALF], i_b[HALF:SEG]
            if rev:
                o = jnp.concatenate([_dot(p_tiles[0], i_lo) + _dot(p_cross, i_hi), _dot(p_tiles[1], i_hi)], axis=0)
            else:
                o = jnp.concatenate([_dot(p_tiles[0], i_lo), _dot(p_cross, i_lo) + _dot(p_tiles[1], i_hi)], axis=0)
            if use_state:
                st = st_sc[d, h]
                o = o + _dot_nt((q * jnp.exp(a)).astype(BF16), st.astype(BF16))
            out_ref[:, 512 + h * HEAD:512 + (h + 1) * HEAD] = o
            if use_state or emit_state:
                a_l = a[last:last + 1, :]
                kd = (kk * jnp.exp(a_l - a)).astype(BF16)
                st_new = _dot(i_b.astype(F32).T.astype(BF16), kd)
                if use_state:
                    st_new = st_new + st * jnp.exp(a_l)
                    st_sc[d, h] = st_new
                if emit_state:
                    @pl.when(j == nseg - 1)
                    def _emit_hg(st_new=st_new, d=d, h=h):
                        outs[5][d, h] = st_new.T

    lane_row = lax.broadcasted_iota(jnp.int32, (1, HEAD), 1)
    m_new = jnp.where(lane_row < 4, m_out_rows[0], jnp.where(lane_row < 8, m_out_rows[1], 0.0))
    if use_state:
        m_sc[...] = jnp.broadcast_to(m_new, m_sc.shape)
    if emit_state:
        @pl.when(j == nseg - 1)
        def _emit_m():
            outs[4][...] = m_new


def _scan(qkv, gate, hqi, hgf, n_seq, nseg, init=None, emit_state=False):
    n_tok = qkv.shape[0]
    fwd = lambda n: pl.BlockSpec((SEG, n), lambda b, j: (b * nseg + j, 0))
    bwd = lambda n: pl.BlockSpec((SEG, n), lambda b, j: (b * nseg + nseg - 1 - j, 0))
    widths = (1536, 128, 1024, 1024)
    in_specs = [fwd(n) for n in widths] + [bwd(n) for n in widths]
    args = [qkv, gate, hqi, hgf, qkv, gate, hqi, hgf]
    mat = pl.BlockSpec((None, 2, 4, HEAD, HEAD), lambda b, j: (b, 0, 0, 0, 0))
    vec = pl.BlockSpec((None, 2, 4, HEAD), lambda b, j: (b, 0, 0, 0))
    sca = pl.BlockSpec((None, 1, HEAD), lambda b, j: (b, 0, 0))
    if init is not None:
        in_specs += [mat, vec, sca, mat]
        args += list(init)
    out_specs = [fwd(1024), bwd(1024)]
    out_shape = [jax.ShapeDtypeStruct((n_tok, 1024), F32)] * 2
    if emit_state:
        out_specs += [mat, vec, sca, mat]
        out_shape += [jax.ShapeDtypeStruct((n_seq, 2, 4, HEAD, HEAD), F32),
                      jax.ShapeDtypeStruct((n_seq, 2, 4, HEAD), F32),
                      jax.ShapeDtypeStruct((n_seq, 1, HEAD), F32),
                      jax.ShapeDtypeStruct((n_seq, 2, 4, HEAD, HEAD), F32)]
    return pl.pallas_call(
        functools.partial(_scan_kernel, has_init=init is not None, emit_state=emit_state, nseg=nseg),
        grid=(n_seq, nseg),
        in_specs=in_specs,
        out_specs=out_specs,
        out_shape=out_shape,
        scratch_shapes=[pltpu.VMEM((2, SEG, SEG), BF16),
                        pltpu.VMEM((2, len(TILE_LEVELS) + 1, HALF, HALF), F32),
                        pltpu.VMEM((2, 4, HEAD, 2 * HEAD), F32),
                        pltpu.VMEM((2, 4, HEAD, HEAD), F32),
                        pltpu.VMEM((8, HEAD), F32)],
        compiler_params=_cparams(2),
        name="bidir_scan",
    )(*args)


def _mix_ffn_kernel(*refs, n_h, gated, final):
    h_refs = refs[:n_h]
    pos = n_h
    og_ref = refs[pos] if gated else None
    pos += 1 if gated else 0
    x_ref, mod_ref, ng_ref, wo_ref, g_ref, w1_ref, w3_ref, w2_ref = refs[pos:pos + 8]
    fg_ref = refs[pos + 8] if final else None
    o_ref = refs[-1]
    mod = mod_ref[...]
    h = h_refs[0][...]
    for r in h_refs[1:]:
        h = h + r[...]
    parts = []
    for g in range(D // HEAD):
        hs = h[:, g * HEAD:(g + 1) * HEAD]
        parts.append(hs * lax.rsqrt(jnp.mean(hs * hs, axis=-1, keepdims=True) + EPS))
    hn = jnp.concatenate(parts, axis=1) * ng_ref[...]
    if gated:
        og = og_ref[...]
        hn = hn * jnp.concatenate([_sigmoid(og[:, 0:512]), _silu(og[:, 512:1024])], axis=1)
    x = x_ref[...] + mod[2:3, :] * _dot(hn.astype(BF16), wo_ref[...])
    hf = _norm_mod(x, g_ref[...], mod, 1).astype(BF16)
    acc = jnp.zeros(x.shape, F32)
    for c0 in range(0, D_FF, FFN_CHUNK):
        a = _dot(hf, w1_ref[:, c0:c0 + FFN_CHUNK])
        b = _dot(hf, w3_ref[:, c0:c0 + FFN_CHUNK])
        acc = acc + _dot((_silu(a) * b).astype(BF16), w2_ref[c0:c0 + FFN_CHUNK, :])
    y = x + mod[5:6, :] * acc
    if final:
        y = y * lax.rsqrt(jnp.mean(y * y, axis=-1, keepdims=True) + EPS) * fg_ref[...]
    o_ref[...] = y


def _mix_ffn(hs, og, x, mod_l, row_of_tile, norm_gain, w_out, ffn_gain, w1, w3, w2, final_gain=None):
    n_tok = x.shape[0]
    tm = TOKEN_TILE
    tok = pl.BlockSpec((tm, D), lambda i: (i, 0))
    full = lambda a: pl.BlockSpec(a.shape, lambda i: (0,) * a.ndim, pipeline_mode=pl.Buffered(1))
    gated = og is not None
    final = final_gain is not None
    consts = [norm_gain, w_out, ffn_gain, w1, w3, w2] + ([final_gain] if final else [])
    args = list(hs) + ([og] if gated else []) + [x, mod_l] + consts
    in_specs = [tok] * (len(hs) + (1 if gated else 0) + 1)
    in_specs += [pl.BlockSpec((None, 6, D), lambda i: (row_of_tile(i), 0, 0))] + [full(a) for a in consts]
    return pl.pallas_call(
        functools.partial(_mix_ffn_kernel, n_h=len(hs), gated=gated, final=final),
        grid=(n_tok // tm,),
        in_specs=in_specs,
        out_specs=tok,
        out_shape=jax.ShapeDtypeStruct((n_tok, D), F32),
        compiler_params=_cparams(1),
        name="mix_ffn",
    )(*args)


def _rope(x, cos, sin_signed):
    lane = lax.broadcasted_iota(jnp.int32, (x.shape[0], HEAD), 1)
    first = (lane & 16) == 0
    parts = []
    for h in range(DA_HEADS):
        xh = x[:, h * HEAD:(h + 1) * HEAD]
        partner = jnp.where(first, pltpu.roll(xh, HEAD - 16, 1), pltpu.roll(xh, 16, 1))
        parts.append(xh * cos + partner * sin_signed)
    return jnp.concatenate(parts, axis=1)


def _inproj_odd_kernel(*refs, rope):
    x_ref, mod_ref, g_ref, w_ref = refs[:4]
    q_ref, k_ref, v_ref = refs[-3:]
    h = _norm_mod(x_ref[...], g_ref[...], mod_ref[...], 0).astype(BF16)
    q = _dot(h, w_ref[:, 0:D])
    k = _dot(h, w_ref[:, D:2 * D])
    v = _dot(h, w_ref[:, 2 * D:3 * D])
    if rope:
        cos, sin_signed = refs[4][...], refs[5][...]
        q = _rope(q, cos, sin_signed)
        k = _rope(k, cos, sin_signed)
    q_ref[...] = (q * (DA_DQK ** -0.5)).astype(q_ref.dtype)
    k_ref[...] = k.astype(k_ref.dtype)
    v_ref[...] = v.astype(v_ref.dtype)


def _inproj_odd(x, mod_l, row_of_tile, gain, w, rope_tables, kv_dtype):
    n_tok = x.shape[0]
    tm = TOKEN_TILE
    tok = pl.BlockSpec((tm, D), lambda i: (i, 0))
    full = lambda a: pl.BlockSpec(a.shape, lambda i: (0,) * a.ndim)
    args = [x, mod_l, gain, w]
    in_specs = [tok, pl.BlockSpec((None, 6, D), lambda i: (row_of_tile(i), 0, 0)), full(gain), full(w)]
    if rope_tables is not None:
        tiles_per_seq = rope_tables[0].shape[0] // tm
        args += list(rope_tables)
        in_specs += [pl.BlockSpec((tm, HEAD), lambda i: (i % tiles_per_seq, 0))] * 2
    return pl.pallas_call(
        functools.partial(_inproj_odd_kernel, rope=rope_tables is not None),
        grid=(n_tok // tm,),
        in_specs=in_specs,
        out_specs=[tok, tok, tok],
        out_shape=[jax.ShapeDtypeStruct((n_tok, D), BF16), jax.ShapeDtypeStruct((n_tok, D), kv_dtype),
                   jax.ShapeDtypeStruct((n_tok, D), kv_dtype)],
        compiler_params=_cparams(1),
        name="inproj_odd",
    )(*args)


def _rope_tables(n_tok):
    quarter = DA_DQK // 4
    tok = jnp.arange(n_tok)
    pos = jnp.stack([(tok // GRID_W).astype(F32), (tok % GRID_W).astype(F32)], axis=1)
    inv = ROPE_BASE ** (-jnp.arange(quarter, dtype=F32) / quarter)
    lane = jnp.arange(HEAD)
    ang = pos[:, (lane // 32) % 2] * inv[lane % quarter][None, :]
    sign = jnp.where((lane % 32) < quarter, -1.0, 1.0).astype(F32)
    return jnp.cos(ang), jnp.sin(ang) * sign[None, :]


def _lambda(lam_ref, lam_init):
    lp = lam_ref[...]
    return (jnp.exp(jnp.sum(lp[0:1] * lp[1:2], axis=-1, keepdims=True))
            - jnp.exp(jnp.sum(lp[2:3] * lp[3:4], axis=-1, keepdims=True)) + lam_init)


VT_ROWS = HEAD + 16


def _qt2(qh):
    qt = qh.astype(F32).T
    row = lax.broadcasted_iota(jnp.int32, qt.shape, 0)
    return jnp.concatenate([jnp.where(row < DA_DQK, qt, 0.0), jnp.where(row >= DA_DQK, qt, 0.0)],
                           axis=1).astype(BF16)


def _vt_aug(vh):
    tk = vh.shape[0]
    row = lax.broadcasted_iota(jnp.int32, (VT_ROWS - HEAD, tk), 0)
    return jnp.concatenate([vh.astype(F32).T, (row == 0).astype(F32)], axis=0).astype(BF16)


def _diff_attn_heads(q_ref, k_fn, vt_fn, n_chunks, lam, o_ref):
    tq = q_ref.shape[0]
    items = [(h, c) for h in range(DA_HEADS) for c in range(n_chunks)]
    qt2, state, pending = {}, {}, []

    def finish(h, c, st):
        cm = jnp.max(st, axis=0, keepdims=True)
        if c == 0:
            m_new = cm
            acc = _dot(vt_fn(h, c), jnp.exp(st - m_new).astype(BF16))
        else:
            m, acc = state[h]
            m_new = jnp.maximum(m, cm)
            acc = acc * jnp.exp(m - m_new) + _dot(vt_fn(h, c), jnp.exp(st - m_new).astype(BF16))
        state[h] = (m_new, acc)
        if c == n_chunks - 1:
            den = acc[HEAD:HEAD + 1, :]
            o_t = acc[0:HEAD, 0:tq] * (1.0 / den[:, 0:tq]) - acc[0:HEAD, tq:] * (lam / den[:, tq:])
            o_ref[:, h * HEAD:(h + 1) * HEAD] = o_t.T

    for h, c in items:
        if c == 0:
            qt2[h] = _qt2(q_ref[:, h * HEAD:(h + 1) * HEAD])
        pending.append((h, c, _dot(k_fn(h, c), qt2[h])))
        if len(pending) > ATTN_LOOKAHEAD:
            finish(*pending.pop(0))
    while pending:
        finish(*pending.pop(0))


def _attn_prompt_kernel(q_ref, k_ref, v_ref, lam_ref, o_ref, *, lam_init):
    lam = _lambda(lam_ref, lam_init)
    _diff_attn_heads(q_ref, lambda h, c: k_ref[:, h * HEAD:(h + 1) * HEAD].astype(BF16),
                     lambda h, c: _vt_aug(v_ref[:, h * HEAD:(h + 1) * HEAD]), 1, lam, o_ref)


def _attn_prompt(q, k, v, lam_p, seq, lam_init):
    n_tok = q.shape[0]
    tok = pl.BlockSpec((seq, D), lambda b: (b, 0))
    return pl.pallas_call(
        functools.partial(_attn_prompt_kernel, lam_init=lam_init),
        grid=(n_tok // seq,),
        in_specs=[tok, tok, tok, pl.BlockSpec(lam_p.shape, lambda b: (0, 0))],
        out_specs=tok,
        out_shape=jax.ShapeDtypeStruct((n_tok, D), F32),
        compiler_params=_cparams(1),
        name="diff_attn_prompt",
    )(q, k, v, lam_p)


def _attn_sample_kernel(q_ref, k_ref, v_ref, ck_ref, cv_ref, lam_ref, o_ref, kcat, vtcat, *, lam_init, past):
    n_keys = kcat.shape[0]
    chunks = [(c0, min(KEY_CHUNK, n_keys - c0)) for c0 in range(0, n_keys, KEY_CHUNK)]

    @pl.when(pl.program_id(1) == 0)
    def _gather_keys():
        kcat[0:past, :] = ck_ref[...].astype(BF16)
        kcat[past:, :] = k_ref[...]
        for h in range(DA_HEADS):
            sl = slice(h * HEAD, (h + 1) * HEAD)
            for c0, n in chunks:
                src, r0 = (cv_ref, c0) if c0 < past else (v_ref, c0 - past)
                vtcat[h, :, c0:c0 + n] = _vt_aug(src[r0:r0 + n, sl])

    lam = _lambda(lam_ref, lam_init)
    _diff_attn_heads(q_ref, lambda h, c: kcat[chunks[c][0]:chunks[c][0] + chunks[c][1], h * HEAD:(h + 1) * HEAD],
                     lambda h, c: vtcat[h, :, chunks[c][0]:chunks[c][0] + chunks[c][1]], len(chunks), lam, o_ref)


def _attn_sample(q, k, v, ck, cv, lam_p, n_seq, seq, lam_init, tq=128):
    past = ck.shape[1]
    qb = pl.BlockSpec((tq, D), lambda b, i: (b * (seq // tq) + i, 0))
    kv = pl.BlockSpec((seq, D), lambda b, i: (b, 0))
    cache = pl.BlockSpec((None, past, D), lambda b, i: (b, 0, 0))
    return pl.pallas_call(
        functools.partial(_attn_sample_kernel, lam_init=lam_init, past=past),
        grid=(n_seq, seq // tq),
        in_specs=[qb, kv, kv, cache, cache, pl.BlockSpec(lam_p.shape, lambda b, i: (0, 0))],
        out_specs=qb,
        out_shape=jax.ShapeDtypeStruct((n_seq * seq, D), F32),
        scratch_shapes=[pltpu.VMEM((past + seq, D), BF16), pltpu.VMEM((DA_HEADS, VT_ROWS, past + seq), BF16)],
        compiler_params=_cparams(2),
        name="diff_attn_sample",
    )(q, k, v, ck, cv, lam_p)


def kernel(x_prompt, x_sample, c, c_ctx, cache_attn_k, cache_attn_v, state_mlstm_C, state_mlstm_n, state_mlstm_m,
           state_hgrn_S, ada_w, ada_b, norm_mix_g, norm_ffn_g, ev_w_in, ev_gate_b, ev_lb_logits, ml_norm_g,
           hg_norm_g, ev_w_out, od_w_in, od_lambda, da_norm_g, od_w_out, ffn_w1, ffn_w3, ffn_w2, final_norm_g):
    n_p, s_p, _ = x_prompt.shape
    n_s, s_s, _ = x_sample.shape
    past = cache_attn_k.shape[2]
    assert s_p == SEG and s_s % SEG == 0 and s_s % TOKEN_TILE == 0 and (n_p * s_p) % TOKEN_TILE == 0
    xp = x_prompt.astype(F32).reshape(n_p * s_p, D)
    xs = x_sample.astype(F32).reshape(n_s * s_s, D)

    cond8 = jnp.zeros((8, D), F32).at[0].set(c_ctx.astype(F32)).at[1:1 + n_s].set(c.astype(F32))
    mod = _modulation(cond8, ada_w.astype(F32), ada_b.astype(F32)).reshape(DEPTH, 8, 6, D)
    row_p = lambda i: 0
    row_s = lambda i: 1 + i // (s_s // TOKEN_TILE)
    streams = [(xp, row_p), (xs, row_s)]
    outputs = {}

    for l in range(DEPTH):
        mix_gain = norm_mix_g[l].astype(F32).reshape(1, D)
        ffn_gain = norm_ffn_g[l].astype(F32).reshape(1, D)
        if l % 2 == 0:
            e = l // 2
            parts = jnp.split(ev_w_in[e], [sum(EV_SIZES[:i + 1]) for i in range(len(EV_SIZES) - 1)], axis=1)
            ml_q, ml_k, ml_v, ml_o, ml_g, hg_q, hg_ff, hg_fb, hg_i, hg_o = parts
            w_in = jnp.concatenate([ml_q, ml_k, ml_v, ml_o, hg_q, hg_i, hg_ff, hg_fb, hg_o,
                                    jnp.pad(ml_g, ((0, 0), (0, HEAD - 16)))], axis=1).astype(BF16)
            gate_b = jnp.pad(ev_gate_b[e].astype(F32), (0, HEAD - 16)).reshape(1, HEAD)
            norm_gain = jnp.concatenate([ml_norm_g[e], hg_norm_g[e]]).astype(F32).reshape(1, D)
            w_out = ev_w_out[e].astype(BF16)
            mixed = []
            for si, (x, row_of) in enumerate(streams):
                qkv, gate, hqi, hgf, og = _inproj_even(x, mod[l], row_of, mix_gain, w_in, gate_b,
                                                       ev_lb_logits.astype(F32), e)
                if si == 0:
                    hf, hb, c_new, n_new, m_new, s_new = _scan(qkv, gate, hqi, hgf, n_p, s_p // SEG, emit_state=True)
                    outputs.setdefault("C", []).append(c_new)
                    outputs.setdefault("n", []).append(n_new)
                    outputs.setdefault("m", []).append(m_new[:, 0, 0:8].reshape(n_p, 2, 4))
                    outputs.setdefault("S", []).append(s_new)
                else:
                    init = (state_mlstm_C[:, e].astype(F32), state_mlstm_n[:, e].astype(F32),
                            jnp.pad(state_mlstm_m[:, e].astype(F32).reshape(n_s, 1, 8), ((0, 0), (0, 0), (0, HEAD - 8))),
                            state_hgrn_S[:, e].astype(F32))
                    hf, hb = _scan(qkv, gate, hqi, hgf, n_s, s_s // SEG, init=init)
                mixed.append(([hf, hb], og, x))
        else:
            o = l // 2
            lam_init = 0.8 - 0.6 * math.exp(-0.3 * l)
            w_in = od_w_in[o].astype(BF16)
            norm_gain = (jnp.tile(da_norm_g[o].astype(F32), DA_HEADS) * (1.0 - lam_init)).reshape(1, D)
            w_out = od_w_out[o].astype(BF16)
            lam_p = od_lambda[o].astype(F32)
            mixed = []
            for si, (x, row_of) in enumerate(streams):
                if si == 0:
                    q, k, v = _inproj_odd(x, mod[l], row_of, mix_gain, w_in, None, F32)
                    outputs.setdefault("k", []).append(k.reshape(n_p, s_p, DA_HEADS, HEAD))
                    outputs.setdefault("v", []).append(v.reshape(n_p, s_p, DA_HEADS, HEAD))
                    att = _attn_prompt(q, k, v, lam_p, s_p, lam_init)
                else:
                    q, k, v = _inproj_odd(x, mod[l], row_of, mix_gain, w_in, _rope_tables(s_s), BF16)
                    ck = cache_attn_k[:, o].reshape(n_s, past, D)
                    cv = cache_attn_v[:, o].reshape(n_s, past, D)
                    att = _attn_sample(q, k, v, ck, cv, lam_p, n_s, s_s, lam_init)
                mixed.append(([att], None, x))
        w1, w3, w2 = ffn_w1[l].astype(BF16), ffn_w3[l].astype(BF16), ffn_w2[l].astype(BF16)
        final_gain = final_norm_g.astype(F32).reshape(1, D) if l == DEPTH - 1 else None
        streams = [(_mix_ffn(hs, og, x, mod[l], row_of, norm_gain, w_out, ffn_gain, w1, w3, w2, final_gain), row_of)
                   for (hs, og, x), (_, row_of) in zip(mixed, streams)]

    y_prompt = streams[0][0].reshape(n_p, s_p, D)
    y_sample = streams[1][0].reshape(n_s, s_s, D)
    stack = lambda name: jnp.stack(outputs[name], axis=1)
    return (y_prompt, y_sample, stack("k"), stack("v"), stack("C"), stack("n"), stack("m"), stack("S"))
```

```python
import functools
import math

import jax
import jax.numpy as jnp
import numpy as np
from jax import lax
from jax.experimental import pallas as pl
from jax.experimental.pallas import tpu as pltpu

F32 = jnp.float32
BF16 = jnp.bfloat16

D = 1024
DEPTH = 2
GRID_W = 64
ML_HEADS = 4
HG_HEADS = 4
HEAD = 128
DA_HEADS = 8
DA_DQK = 64
ROPE_BASE = 10000.0
EPS = 1e-6
D_FF = ((8 * D // 3 + 255) // 256) * 256
EV_SIZES = (512, 512, 512, 512, 16, 512, 512, 512, 512, 512)
EV_COLS = 9 * 512 + 128

SEG = 256
HALF = SEG // 2
TILE_LEVELS = (1, 2, 4, 8, 16, 32, 64)
TOKEN_TILE = 512
FFN_CHUNK = 256
KEY_CHUNK = 256
ATTN_LOOKAHEAD = 4
VMEM_LIMIT = 56 * 1024 * 1024


def _cparams(n_axes):
    return pltpu.CompilerParams(dimension_semantics=("arbitrary",) * n_axes, vmem_limit_bytes=VMEM_LIMIT)


def _sigmoid(x):
    return 1.0 / (1.0 + jnp.exp(-x))


def _silu(x):
    return x * _sigmoid(x)


def _log_sigmoid(x):
    return jnp.minimum(x, 0.0) - jnp.log(1.0 + jnp.exp(-jnp.abs(x)))


def _dot(a, b):
    return jnp.dot(a, b, preferred_element_type=F32)


def _dot_nt(a, b):
    return lax.dot_general(a, b, (((1,), (1,)), ((), ())), preferred_element_type=F32)


def _norm_mod(x, gain, mod, k):
    ms = jnp.mean(x * x, axis=-1, keepdims=True)
    return x * lax.rsqrt(ms + EPS) * gain * (1.0 + mod[3 * k + 1:3 * k + 2]) + mod[3 * k:3 * k + 1]


def _mod_kernel(c_ref, w_ref, b_ref, o_ref):
    s = _silu(c_ref[...]).astype(BF16)
    o_ref[...] = _dot(s, w_ref[...].astype(BF16)) + b_ref[...]


def _modulation(cond8, ada_w, ada_b):
    n_layers = ada_w.shape[0]
    tn = 1536
    return pl.pallas_call(
        _mod_kernel,
        grid=(n_layers, 6 * D // tn),
        in_specs=[pl.BlockSpec((8, D), lambda l, n: (0, 0)),
                  pl.BlockSpec((None, D, tn), lambda l, n: (l, 0, n)),
                  pl.BlockSpec((None, 1, tn), lambda l, n: (l, 0, n))],
        out_specs=pl.BlockSpec((None, 8, tn), lambda l, n: (l, 0, n)),
        out_shape=jax.ShapeDtypeStruct((n_layers, 8, 6 * D), F32),
        compiler_params=_cparams(2),
        name="ada_modulation",
    )(cond8, ada_w, ada_b.reshape(n_layers, 1, 6 * D))


def _inproj_even_kernel(x_ref, mod_ref, g_ref, w_ref, gb_ref, lbl_ref, qkv_ref, gate_ref, hqi_ref, hgf_ref,
                        og_ref, *, e_idx):
    h = _norm_mod(x_ref[...], g_ref[...], mod_ref[...], 0).astype(BF16)

    def proj(c0, n):
        return _dot(h, w_ref[:, c0:c0 + n])

    qkv_ref[:, 0:512] = (proj(0, 512) * (HEAD ** -0.5)).astype(BF16)
    qkv_ref[:, 512:1536] = proj(512, 1024).astype(BF16)
    og_ref[:, 0:512] = proj(1536, 512)
    hqi_ref[...] = proj(2048, 1024).astype(BF16)
    lg = lbl_ref[...]
    mx = jnp.max(lg, axis=0, keepdims=True)
    ex = jnp.exp(lg - mx)
    lb = jnp.sum(ex[0:e_idx + 1], axis=0, keepdims=True) / jnp.sum(ex, axis=0, keepdims=True)
    hgf_ref[:, 0:512] = jnp.log(lb + (1.0 - lb) * _sigmoid(proj(3072, 512)))
    hgf_ref[:, 512:1024] = jnp.log(lb + (1.0 - lb) * _sigmoid(proj(3584, 512)))
    og_ref[:, 512:1024] = proj(4096, 512)
    gt = proj(4608, 128) + gb_ref[...]
    lane = lax.broadcasted_iota(jnp.int32, gt.shape, 1)
    gate_ref[...] = jnp.where(lane < 8, gt, jnp.where(lane < 16, _log_sigmoid(gt), 0.0))


def _inproj_even(x, mod_l, row_of_tile, gain, w, gate_b, lb_logits, e_idx):
    n_tok = x.shape[0]
    tm = TOKEN_TILE
    tok = lambda n: pl.BlockSpec((tm, n), lambda i: (i, 0))
    full = lambda a: pl.BlockSpec(a.shape, lambda i: (0,) * a.ndim)
    return pl.pallas_call(
        functools.partial(_inproj_even_kernel, e_idx=e_idx),
        grid=(n_tok // tm,),
        in_specs=[tok(D), pl.BlockSpec((None, 6, D), lambda i: (row_of_tile(i), 0, 0)),
                  full(gain), full(w), full(gate_b), full(lb_logits)],
        out_specs=[tok(1536), tok(128), tok(1024), tok(1024), tok(1024)],
        out_shape=[jax.ShapeDtypeStruct((n_tok, 1536), BF16), jax.ShapeDtypeStruct((n_tok, 128), F32),
                   jax.ShapeDtypeStruct((n_tok, 1024), BF16), jax.ShapeDtypeStruct((n_tok, 1024), F32),
                   jax.ShapeDtypeStruct((n_tok, 1024), F32)],
        compiler_params=_cparams(1),
        name="inproj_even",
    )(x, mod_l, gain, w, gate_b, lb_logits)


def _shift_rows(x, k, fill, up):
    n = x.shape[0]
    if k % 8 == 0:
        pad = jnp.full((k,) + x.shape[1:], 0.0 if fill is None else fill, x.dtype)
        return jnp.concatenate([x[k:], pad], axis=0) if up else jnp.concatenate([pad, x[:n - k]], axis=0)
    y = pltpu.roll(x, (n - k) if up else k, 0)
    if fill is None:
        return y
    row = lax.broadcasted_iota(jnp.int32, x.shape, 0)
    return jnp.where(row >= n - k, fill, y) if up else jnp.where(row < k, fill, y)


def _cummax_rows(x, rev):
    k = 1
    while k < x.shape[0]:
        x = jnp.maximum(x, _shift_rows(x, k, -jnp.inf, up=rev))
        k *= 2
    return x


def _cumsum_rows(tri, x):
    hi = x.astype(BF16)
    r1 = x - hi.astype(F32)
    mid = r1.astype(BF16)
    lo = (r1 - mid.astype(F32)).astype(BF16)
    return _dot(tri, hi) + _dot(tri, mid) + _dot(tri, lo)


def _hgrn_tile(q, kk, a, q_b, rev, mask_sc, d, up_small):
    p = _dot_nt(q_b, kk.astype(BF16)) * mask_sc[d, len(TILE_LEVELS)]
    bm = a
    for li, m in enumerate(TILE_LEVELS):
        if m < 8:
            qrole = jnp.logical_not(up_small[li]) if rev else up_small[li]
            x = jnp.where(qrole, _shift_rows(bm, m, None, up=rev), bm)
            r = jnp.where(qrole, q, kk) * jnp.exp(jnp.where(qrole, a - x, x - a))
            if 2 * m < 8:
                bm = jnp.where(qrole, bm, _shift_rows(bm, m, None, up=not rev))
        else:
            pieces = []
            for b0 in range(0, HALF, 2 * m):
                lo, up = slice(b0, b0 + m), slice(b0 + m, b0 + 2 * m)
                if rev:
                    ref = a[b0 + m:b0 + m + 1]
                    pieces += [q[lo] * jnp.exp(a[lo] - ref), kk[up] * jnp.exp(ref - a[up])]
                else:
                    ref = a[b0 + m - 1:b0 + m]
                    pieces += [kk[lo] * jnp.exp(ref - a[lo]), q[up] * jnp.exp(a[up] - ref)]
            r = jnp.concatenate(pieces, axis=0)
        rb = r.astype(BF16)
        p = p + _dot_nt(rb, rb) * mask_sc[d, li]
    return p


def _scan_kernel(*refs, has_init, emit_state, nseg):
    n_in = 8 + (4 if has_init else 0)
    n_out = 2 + (4 if emit_state else 0)
    ins, outs, scr = refs[:n_in], refs[n_in:n_in + n_out], refs[n_in + n_out:]
    dir_refs = (ins[0:4], ins[4:8])
    out_refs = outs[0:2]
    tri_sc, mask_sc, caug_sc, st_sc, m_sc = scr
    use_state = has_init or nseg > 1
    b_id, j = pl.program_id(0), pl.program_id(1)

    row = lax.broadcasted_iota(jnp.int32, (SEG, SEG), 0)
    col = lax.broadcasted_iota(jnp.int32, (SEG, SEG), 1)

    @pl.when(jnp.logical_and(b_id == 0, j == 0))
    def _build_constants():
        tri_sc[0] = (col <= row).astype(BF16)
        tri_sc[1] = (col >= row).astype(BF16)
        trow = lax.broadcasted_iota(jnp.int32, (HALF, HALF), 0)
        tcol = lax.broadcasted_iota(jnp.int32, (HALF, HALF), 1)
        for li, m in enumerate(TILE_LEVELS):
            sh = m.bit_length() - 1
            same = (trow >> (sh + 1)) == (tcol >> (sh + 1))
            t_up = ((trow >> sh) & 1) == 1
            s_up = ((tcol >> sh) & 1) == 1
            mask_sc[0, li] = (same & t_up & jnp.logical_not(s_up)).astype(F32)
            mask_sc[1, li] = (same & s_up & jnp.logical_not(t_up)).astype(F32)
        eye = (trow == tcol).astype(F32)
        mask_sc[0, len(TILE_LEVELS)] = eye
        mask_sc[1, len(TILE_LEVELS)] = eye

    if use_state:
        @pl.when(j == 0)
        def _init_state():
            if has_init:
                c0_ref, n0_ref, m0_ref, s0_ref = ins[8:12]
                for d in range(2):
                    for h in range(ML_HEADS):
                        caug_sc[d, h, :, 0:HEAD] = c0_ref[d, h]
                        caug_sc[d, h, :, HEAD:2 * HEAD] = jnp.broadcast_to(n0_ref[d, h:h + 1, :], (HEAD, HEAD)).T
                        st_sc[d, h] = s0_ref[d, h].T
                m_sc[...] = jnp.broadcast_to(m0_ref[...], m_sc.shape)
            else:
                caug_sc[...] = jnp.zeros(caug_sc.shape, F32)
                st_sc[...] = jnp.zeros(st_sc.shape, F32)
                m_sc[...] = jnp.zeros(m_sc.shape, F32)

    lane128 = lax.broadcasted_iota(jnp.int32, (SEG, HEAD), 1)
    row128 = lax.broadcasted_iota(jnp.int32, (SEG, HEAD), 0)
    e0 = (lane128 == 0).astype(BF16)
    row_half = lax.broadcasted_iota(jnp.int32, (HALF, HEAD), 0)
    up_masks = [((row_half >> (m.bit_length() - 1)) & 1) == 1 for m in TILE_LEVELS if m < 8]
    tmasks = (col <= row, col >= row)
    m_in = m_sc[0:1, :] if use_state else jnp.zeros((1, HEAD), F32)
    m_out_rows = []

    for d in range(2):
        qkv_ref, gate_ref, hqi_ref, hgf_ref = dir_refs[d]
        out_ref = out_refs[d]
        rev = d == 1
        last = 0 if rev else SEG - 1
        tri = tri_sc[d]

        slab = gate_ref[...]
        b_al = pltpu.roll(_cumsum_rows(tri, slab), HEAD - 8, 1)
        u = slab - b_al
        mx = jnp.maximum(_cummax_rows(u, rev), m_in)
        w_inter = jnp.exp(m_in - mx)
        e_den = jnp.exp(-(b_al + mx))
        mx_last = mx[last:last + 1, :]
        m_out_rows.append(b_al[last:last + 1, :] + mx_last)
        decay = jnp.exp(m_in - mx_last)
        wg = jnp.exp(u - mx_last)
        u_t = u.T

        for h in range(ML_HEADS):
            c = 4 * d + h
            q = qkv_ref[:, h * HEAD:(h + 1) * HEAD]
            k = qkv_ref[:, 512 + h * HEAD:512 + (h + 1) * HEAD]
            v = qkv_ref[:, 1024 + h * HEAD:1024 + (h + 1) * HEAD]
            dm = jnp.where(tmasks[d], jnp.exp(u_t[c:c + 1, :] - mx[:, c:c + 1]), 0.0)
            s = (_dot_nt(q, k) * dm).astype(BF16)
            v_aug = jnp.concatenate([v, e0], axis=1)
            numden = _dot(s, v_aug)
            if use_state:
                numden = numden + w_inter[:, c:c + 1] * _dot(q, caug_sc[d, h].astype(BF16))
            den = jnp.maximum(jnp.abs(numden[:, HEAD:HEAD + 1]), e_den[:, c:c + 1])
            out_ref[:, h * HEAD:(h + 1) * HEAD] = numden[:, 0:HEAD] / den
            if use_state or emit_state:
                kw_t = (k.astype(F32) * wg[:, c:c + 1]).T.astype(BF16)
                upd = _dot(kw_t, v_aug)
                if use_state:
                    upd = upd + decay[:, c:c + 1] * caug_sc[d, h]
                    caug_sc[d, h] = upd
                if emit_state:
                    @pl.when(j == nseg - 1)
                    def _emit_ml(upd=upd, d=d, h=h):
                        outs[2][d, h] = upd[:, 0:HEAD]
                        outs[3][d, h:h + 1, :] = upd[:, HEAD:2 * HEAD].T[0:1, :]

        lf_all = hgf_ref[:, 512 * d:512 * (d + 1)]
        a_all = _cumsum_rows(tri, lf_all)
        kk_all = 1.0 - jnp.exp(lf_all)
        for h in range(HG_HEADS):
            sl = slice(h * HEAD, (h + 1) * HEAD)
            a = a_all[:, sl]
            kk = kk_all[:, sl]
            q_b = hqi_ref[:, h * HEAD:(h + 1) * HEAD]
            i_b = hqi_ref[:, 512 + h * HEAD:512 + (h + 1) * HEAD]
            q = q_b.astype(F32)
            qh, kh = (0, 1) if rev else (1, 0)
            a_q, a_k = a[qh * HALF:(qh + 1) * HALF], a[kh * HALF:(kh + 1) * HALF]
            a_mid = a[HALF:HALF + 1] if rev else a[HALF - 1:HALF]
            r_q = (q[qh * HALF:(qh + 1) * HALF] * jnp.exp(a_q - a_mid)).astype(BF16)
            r_k = (kk[kh * HALF:(kh + 1) * HALF] * jnp.exp(a_mid - a_k)).astype(BF16)
            p_cross = _dot_nt(r_q, r_k).astype(BF16)
            p_tiles = []
            for t in range(2):
                rows = slice(t * HALF, (t + 1) * HALF)
                p_tiles.append(_hgrn_tile(q[rows], kk[rows], a[rows], q_b[rows], rev, mask_sc, d,
                                          up_masks).astype(BF16))
            i_lo, i_hi = i_b[0:HALF], i_b[HALF:SEG]
            if rev:
                o = jnp.concatenate([_dot(p_tiles[0], i_lo) + _dot(p_cross, i_hi), _dot(p_tiles[1], i_hi)], axis=0)
            else:
                o = jnp.concatenate([_dot(p_tiles[0], i_lo), _dot(p_cross, i_lo) + _dot(p_tiles[1], i_hi)], axis=0)
            if use_state:
                st = st_sc[d, h]
                o = o + _dot_nt((q * jnp.exp(a)).astype(BF16), st.astype(BF16))
            out_ref[:, 512 + h * HEAD:512 + (h + 1) * HEAD] = o
            if use_state or emit_state:
                a_l = a[last:last + 1, :]
                kd = (kk * jnp.exp(a_l - a)).astype(BF16)
                st_new = _dot(i_b.astype(F32).T.astype(BF16), kd)
                if use_state:
                    st_new = st_new + st * jnp.exp(a_l)
                    st_sc[d, h] = st_new
                if emit_state:
                    @pl.when(j == nseg - 1)
                    def _emit_hg(st_new=st_new, d=d, h=h):
                        outs[5][d, h] = st_new.T

    lane_row = lax.broadcasted_iota(jnp.int32, (1, HEAD), 1)
    m_new = jnp.where(lane_row < 4, m_out_rows[0], jnp.where(lane_row < 8, m_out_rows[1], 0.0))
    if use_state:
        m_sc[...] = jnp.broadcast_to(m_new, m_sc.shape)
    if emit_state:
        @pl.when(j == nseg - 1)
        def _emit_m():
            outs[4][...] = m_new


def _scan(qkv, gate, hqi, hgf, n_seq, nseg, init=None, emit_state=False):
    n_tok = qkv.shape[0]
    fwd = lambda n: pl.BlockSpec((SEG, n), lambda b, j: (b * nseg + j, 0))
    bwd = lambda n: pl.BlockSpec((SEG, n), lambda b, j: (b * nseg + nseg - 1 - j, 0))
    widths = (1536, 128, 1024, 1024)
    in_specs = [fwd(n) for n in widths] + [bwd(n) for n in widths]
    args = [qkv, gate, hqi, hgf, qkv, gate, hqi, hgf]
    mat = pl.BlockSpec((None, 2, 4, HEAD, HEAD), lambda b, j: (b, 0, 0, 0, 0))
    vec = pl.BlockSpec((None, 2, 4, HEAD), lambda b, j: (b, 0, 0, 0))
    sca = pl.BlockSpec((None, 1, HEAD), lambda b, j: (b, 0, 0))
    if init is not None:
        in_specs += [mat, vec, sca, mat]
        args += list(init)
    out_specs = [fwd(1024), bwd(1024)]
    out_shape = [jax.ShapeDtypeStruct((n_tok, 1024), F32)] * 2
    if emit_state:
        out_specs += [mat, vec, sca, mat]
        out_shape += [jax.ShapeDtypeStruct((n_seq, 2, 4, HEAD, HEAD), F32),
                      jax.ShapeDtypeStruct((n_seq, 2, 4, HEAD), F32),
                      jax.ShapeDtypeStruct((n_seq, 1, HEAD), F32),
                      jax.ShapeDtypeStruct((n_seq, 2, 4, HEAD, HEAD), F32)]
    return pl.pallas_call(
        functools.partial(_scan_kernel, has_init=init is not None, emit_state=emit_state, nseg=nseg),
        grid=(n_seq, nseg),
        in_specs=in_specs,
        out_specs=out_specs,
        out_shape=out_shape,
        scratch_shapes=[pltpu.VMEM((2, SEG, SEG), BF16),
                        pltpu.VMEM((2, len(TILE_LEVELS) + 1, HALF, HALF), F32),
                        pltpu.VMEM((2, 4, HEAD, 2 * HEAD), F32),
                        pltpu.VMEM((2, 4, HEAD, HEAD), F32),
                        pltpu.VMEM((8, HEAD), F32)],
        compiler_params=_cparams(2),
        name="bidir_scan",
    )(*args)


def _mix_ffn_kernel(*refs, n_h, gated, final):
    h_refs = refs[:n_h]
    pos = n_h
    og_ref = refs[pos] if gated else None
    pos += 1 if gated else 0
    x_ref, mod_ref, ng_ref, wo_ref, g_ref, w1_ref, w3_ref, w2_ref = refs[pos:pos + 8]
    fg_ref = refs[pos + 8] if final else None
    o_ref = refs[-1]
    mod = mod_ref[...]
    h = h_refs[0][...]
    for r in h_refs[1:]:
        h = h + r[...]
    parts = []
    for g in range(D // HEAD):
        hs = h[:, g * HEAD:(g + 1) * HEAD]
        parts.append(hs * lax.rsqrt(jnp.mean(hs * hs, axis=-1, keepdims=True) + EPS))
    hn = jnp.concatenate(parts, axis=1) * ng_ref[...]
    if gated:
        og = og_ref[...]
        hn = hn * jnp.concatenate([_sigmoid(og[:, 0:512]), _silu(og[:, 512:1024])], axis=1)
    x = x_ref[...] + mod[2:3, :] * _dot(hn.astype(BF16), wo_ref[...])
    hf = _norm_mod(x, g_ref[...], mod, 1).astype(BF16)
    acc = jnp.zeros(x.shape, F32)
    for c0 in range(0, D_FF, FFN_CHUNK):
        a = _dot(hf, w1_ref[:, c0:c0 + FFN_CHUNK])
        b = _dot(hf, w3_ref[:, c0:c0 + FFN_CHUNK])
        acc = acc + _dot((_silu(a) * b).astype(BF16), w2_ref[c0:c0 + FFN_CHUNK, :])
    y = x + mod[5:6, :] * acc
    if final:
        y = y * lax.rsqrt(jnp.mean(y * y, axis=-1, keepdims=True) + EPS) * fg_ref[...]
    o_ref[...] = y


def _mix_ffn(hs, og, x, mod_l, row_of_tile, norm_gain, w_out, ffn_gain, layer, w1, w3, w2, final_gain=None):
    n_tok = x.shape[0]
    tm = TOKEN_TILE
    tok = pl.BlockSpec((tm, D), lambda i: (i, 0))
    full = lambda a: pl.BlockSpec(a.shape, lambda i: (0,) * a.ndim, pipeline_mode=pl.Buffered(1))
    of_layer = lambda a: pl.BlockSpec((None,) + a.shape[1:], lambda i: (layer, 0, 0), pipeline_mode=pl.Buffered(1))
    gated = og is not None
    final = final_gain is not None
    consts = [norm_gain, w_out, ffn_gain, w1, w3, w2] + ([final_gain] if final else [])
    args = list(hs) + ([og] if gated else []) + [x, mod_l] + consts
    in_specs = [tok] * (len(hs) + (1 if gated else 0) + 1)
    in_specs += [pl.BlockSpec((None, 6, D), lambda i: (row_of_tile(i), 0, 0))]
    in_specs += [of_layer(a) if a.ndim == 3 else full(a) for a in consts]
    return pl.pallas_call(
        functools.partial(_mix_ffn_kernel, n_h=len(hs), gated=gated, final=final),
        grid=(n_tok // tm,),
        in_specs=in_specs,
        out_specs=tok,
        out_shape=jax.ShapeDtypeStruct((n_tok, D), F32),
        compiler_params=_cparams(1),
        name="mix_ffn",
    )(*args)


def _rope(x, cos, sin_signed):
    lane = lax.broadcasted_iota(jnp.int32, (x.shape[0], HEAD), 1)
    first = (lane & 16) == 0
    parts = []
    for h in range(DA_HEADS):
        xh = x[:, h * HEAD:(h + 1) * HEAD]
        partner = jnp.where(first, pltpu.roll(xh, HEAD - 16, 1), pltpu.roll(xh, 16, 1))
        parts.append(xh * cos + partner * sin_signed)
    return jnp.concatenate(parts, axis=1)


def _inproj_odd_kernel(*refs, rope):
    x_ref, mod_ref, g_ref, w_ref = refs[:4]
    q_ref, k_ref, v_ref = refs[-3:]
    h = _norm_mod(x_ref[...], g_ref[...], mod_ref[...], 0).astype(BF16)
    q = _dot(h, w_ref[:, 0:D])
    k = _dot(h, w_ref[:, D:2 * D])
    v = _dot(h, w_ref[:, 2 * D:3 * D])
    if rope:
        cos, sin_signed = refs[4][...], refs[5][...]
        q = _rope(q, cos, sin_signed)
        k = _rope(k, cos, sin_signed)
    q_ref[...] = (q * (DA_DQK ** -0.5)).astype(q_ref.dtype)
    k_ref[...] = k.astype(k_ref.dtype)
    v_ref[...] = v.astype(v_ref.dtype)


def _inproj_odd(x, mod_l, row_of_tile, gain, w, rope_tables, kv_dtype):
    n_tok = x.shape[0]
    tm = TOKEN_TILE
    tok = pl.BlockSpec((tm, D), lambda i: (i, 0))
    full = lambda a: pl.BlockSpec(a.shape, lambda i: (0,) * a.ndim)
    args = [x, mod_l, gain, w]
    in_specs = [tok, pl.BlockSpec((None, 6, D), lambda i: (row_of_tile(i), 0, 0)), full(gain), full(w)]
    if rope_tables is not None:
        tiles_per_seq = rope_tables[0].shape[0] // tm
        args += list(rope_tables)
        in_specs += [pl.BlockSpec((tm, HEAD), lambda i: (i % tiles_per_seq, 0))] * 2
    return pl.pallas_call(
        functools.partial(_inproj_odd_kernel, rope=rope_tables is not None),
        grid=(n_tok // tm,),
        in_specs=in_specs,
        out_specs=[tok, tok, tok],
        out_shape=[jax.ShapeDtypeStruct((n_tok, D), BF16), jax.ShapeDtypeStruct((n_tok, D), kv_dtype),
                   jax.ShapeDtypeStruct((n_tok, D), kv_dtype)],
        compiler_params=_cparams(1),
        name="inproj_odd",
    )(*args)


def _rope_tables(n_tok):
    quarter = DA_DQK // 4
    tok = np.arange(n_tok)
    pos = np.stack([tok // GRID_W, tok % GRID_W], axis=1).astype(np.float32)
    inv = (np.float32(ROPE_BASE) ** (-np.arange(quarter, dtype=np.float32) / np.float32(quarter))).astype(np.float32)
    lane = np.arange(HEAD)
    ang = (pos[:, (lane // 32) % 2] * inv[lane % quarter][None, :]).astype(np.float32)
    sign = np.where((lane % 32) < quarter, -1.0, 1.0).astype(np.float32)
    return jnp.asarray(np.cos(ang), F32), jnp.asarray(np.sin(ang) * sign[None, :], F32)


def _lambda(lam_ref, lam_init):
    lp = lam_ref[...]
    return (jnp.exp(jnp.sum(lp[0:1] * lp[1:2], axis=-1, keepdims=True))
            - jnp.exp(jnp.sum(lp[2:3] * lp[3:4], axis=-1, keepdims=True)) + lam_init)


VT_ROWS = HEAD + 16


def _qt2(qh):
    qt = qh.astype(F32).T
    row = lax.broadcasted_iota(jnp.int32, qt.shape, 0)
    return jnp.concatenate([jnp.where(row < DA_DQK, qt, 0.0), jnp.where(row >= DA_DQK, qt, 0.0)],
                           axis=1).astype(BF16)


def _vt_aug(vh):
    tk = vh.shape[0]
    row = lax.broadcasted_iota(jnp.int32, (VT_ROWS - HEAD, tk), 0)
    return jnp.concatenate([vh.astype(F32).T, (row == 0).astype(F32)], axis=0).astype(BF16)


def _diff_attn_heads(q_ref, k_fn, vt_fn, n_chunks, lam, o_ref):
    tq = q_ref.shape[0]
    items = [(h, c) for h in range(DA_HEADS) for c in range(n_chunks)]
    qt2, state, pending = {}, {}, []

    def finish(h, c, st):
        cm = jnp.max(st, axis=0, keepdims=True)
        if c == 0:
            m_new = cm
            acc = _dot(vt_fn(h, c), jnp.exp(st - m_new).astype(BF16))
        else:
            m, acc = state[h]
            m_new = jnp.maximum(m, cm)
            acc = acc * jnp.exp(m - m_new) + _dot(vt_fn(h, c), jnp.exp(st - m_new).astype(BF16))
        state[h] = (m_new, acc)
        if c == n_chunks - 1:
            den = acc[HEAD:HEAD + 1, :]
            o_t = acc[0:HEAD, 0:tq] * (1.0 / den[:, 0:tq]) - acc[0:HEAD, tq:] * (lam / den[:, tq:])
            o_ref[:, h * HEAD:(h + 1) * HEAD] = o_t.T

    for h, c in items:
        if c == 0:
            qt2[h] = _qt2(q_ref[:, h * HEAD:(h + 1) * HEAD])
        pending.append((h, c, _dot(k_fn(h, c), qt2[h])))
        if len(pending) > ATTN_LOOKAHEAD:
            finish(*pending.pop(0))
    while pending:
        finish(*pending.pop(0))


def _attn_prompt_kernel(q_ref, k_ref, v_ref, lam_ref, o_ref, *, lam_init):
    lam = _lambda(lam_ref, lam_init)
    _diff_attn_heads(q_ref, lambda h, c: k_ref[:, h * HEAD:(h + 1) * HEAD].astype(BF16),
                     lambda h, c: _vt_aug(v_ref[:, h * HEAD:(h + 1) * HEAD]), 1, lam, o_ref)


def _attn_prompt(q, k, v, lam_p, seq, lam_init):
    n_tok = q.shape[0]
    tok = pl.BlockSpec((seq, D), lambda b: (b, 0))
    return pl.pallas_call(
        functools.partial(_attn_prompt_kernel, lam_init=lam_init),
        grid=(n_tok // seq,),
        in_specs=[tok, tok, tok, pl.BlockSpec(lam_p.shape, lambda b: (0, 0))],
        out_specs=tok,
        out_shape=jax.ShapeDtypeStruct((n_tok, D), F32),
        compiler_params=_cparams(1),
        name="diff_attn_prompt",
    )(q, k, v, lam_p)


def _attn_sample_kernel(q_ref, k_ref, v_ref, ck_ref, cv_ref, lam_ref, o_ref, kcat, vtcat, *, lam_init, past):
    n_keys = kcat.shape[0]
    chunks = [(c0, min(KEY_CHUNK, n_keys - c0)) for c0 in range(0, n_keys, KEY_CHUNK)]

    @pl.when(pl.program_id(1) == 0)
    def _gather_keys():
        kcat[0:past, :] = ck_ref[...].astype(BF16)
        kcat[past:, :] = k_ref[...]
        for h in range(DA_HEADS):
            sl = slice(h * HEAD, (h + 1) * HEAD)
            for c0, n in chunks:
                src, r0 = (cv_ref, c0) if c0 < past else (v_ref, c0 - past)
                vtcat[h, :, c0:c0 + n] = _vt_aug(src[r0:r0 + n, sl])

    lam = _lambda(lam_ref, lam_init)
    _diff_attn_heads(q_ref, lambda h, c: kcat[chunks[c][0]:chunks[c][0] + chunks[c][1], h * HEAD:(h + 1) * HEAD],
                     lambda h, c: vtcat[h, :, chunks[c][0]:chunks[c][0] + chunks[c][1]], len(chunks), lam, o_ref)


def _attn_sample(q, k, v, ck, cv, lam_p, n_seq, seq, lam_init, tq=128):
    past = ck.shape[1]
    qb = pl.BlockSpec((tq, D), lambda b, i: (b * (seq // tq) + i, 0))
    kv = pl.BlockSpec((seq, D), lambda b, i: (b, 0))
    cache = pl.BlockSpec((None, past, D), lambda b, i: (b, 0, 0))
    return pl.pallas_call(
        functools.partial(_attn_sample_kernel, lam_init=lam_init, past=past),
        grid=(n_seq, seq // tq),
        in_specs=[qb, kv, kv, cache, cache, pl.BlockSpec(lam_p.shape, lambda b, i: (0, 0))],
        out_specs=qb,
        out_shape=jax.ShapeDtypeStruct((n_seq * seq, D), F32),
        scratch_shapes=[pltpu.VMEM((past + seq, D), BF16), pltpu.VMEM((DA_HEADS, VT_ROWS, past + seq), BF16)],
        compiler_params=_cparams(2),
        name="diff_attn_sample",
    )(q, k, v, ck, cv, lam_p)


def kernel(x_prompt, x_sample, c, c_ctx, cache_attn_k, cache_attn_v, state_mlstm_C, state_mlstm_n, state_mlstm_m,
           state_hgrn_S, ada_w, ada_b, norm_mix_g, norm_ffn_g, ev_w_in, ev_gate_b, ev_lb_logits, ml_norm_g,
           hg_norm_g, ev_w_out, od_w_in, od_lambda, da_norm_g, od_w_out, ffn_w1, ffn_w3, ffn_w2, final_norm_g):
    n_p, s_p, _ = x_prompt.shape
    n_s, s_s, _ = x_sample.shape
    past = cache_attn_k.shape[2]
    assert s_p == SEG and s_s % SEG == 0 and s_s % TOKEN_TILE == 0 and (n_p * s_p) % TOKEN_TILE == 0
    xp = x_prompt.astype(F32).reshape(n_p * s_p, D)
    xs = x_sample.astype(F32).reshape(n_s * s_s, D)

    cond8 = jnp.zeros((8, D), F32).at[0].set(c_ctx.astype(F32)).at[1:1 + n_s].set(c.astype(F32))
    mod = _modulation(cond8, ada_w.astype(F32), ada_b.astype(F32)).reshape(DEPTH, 8, 6, D)
    row_p = lambda i: 0
    row_s = lambda i: 1 + i // (s_s // TOKEN_TILE)
    streams = [(xp, row_p), (xs, row_s)]
    outputs = {}
    w1_all, w3_all, w2_all = ffn_w1.astype(BF16), ffn_w3.astype(BF16), ffn_w2.astype(BF16)

    for l in range(DEPTH):
        mix_gain = norm_mix_g[l].astype(F32).reshape(1, D)
        ffn_gain = norm_ffn_g[l].astype(F32).reshape(1, D)
        if l % 2 == 0:
            e = l // 2
            parts = jnp.split(ev_w_in[e], [sum(EV_SIZES[:i + 1]) for i in range(len(EV_SIZES) - 1)], axis=1)
            ml_q, ml_k, ml_v, ml_o, ml_g, hg_q, hg_ff, hg_fb, hg_i, hg_o = parts
            w_in = jnp.concatenate([ml_q, ml_k, ml_v, ml_o, hg_q, hg_i, hg_ff, hg_fb, hg_o,
                                    jnp.pad(ml_g, ((0, 0), (0, HEAD - 16)))], axis=1).astype(BF16)
            gate_b = jnp.pad(ev_gate_b[e].astype(F32), (0, HEAD - 16)).reshape(1, HEAD)
            norm_gain = jnp.concatenate([ml_norm_g[e], hg_norm_g[e]]).astype(F32).reshape(1, D)
            w_out = ev_w_out[e].astype(BF16)
            mixed = []
            for si, (x, row_of) in enumerate(streams):
                qkv, gate, hqi, hgf, og = _inproj_even(x, mod[l], row_of, mix_gain, w_in, gate_b,
                                                       ev_lb_logits.astype(F32), e)
                if si == 0:
                    hf, hb, c_new, n_new, m_new, s_new = _scan(qkv, gate, hqi, hgf, n_p, s_p // SEG, emit_state=True)
                    outputs.setdefault("C", []).append(c_new)
                    outputs.setdefault("n", []).append(n_new)
                    outputs.setdefault("m", []).append(m_new[:, 0, 0:8].reshape(n_p, 2, 4))
                    outputs.setdefault("S", []).append(s_new)
                else:
                    init = (state_mlstm_C[:, e].astype(F32), state_mlstm_n[:, e].astype(F32),
                            jnp.pad(state_mlstm_m[:, e].astype(F32).reshape(n_s, 1, 8), ((0, 0), (0, 0), (0, HEAD - 8))),
                            state_hgrn_S[:, e].astype(F32))
                    hf, hb = _scan(qkv, gate, hqi, hgf, n_s, s_s // SEG, init=init)
                mixed.append(([hf, hb], og, x))
        else:
            o = l // 2
            lam_init = 0.8 - 0.6 * math.exp(-0.3 * l)
            w_in = od_w_in[o].astype(BF16)
            norm_gain = (jnp.tile(da_norm_g[o].astype(F32), DA_HEADS) * (1.0 - lam_init)).reshape(1, D)
            w_out = od_w_out[o].astype(BF16)
            lam_p = od_lambda[o].astype(F32)
            mixed = []
            for si, (x, row_of) in enumerate(streams):
                if si == 0:
                    q, k, v = _inproj_odd(x, mod[l], row_of, mix_gain, w_in, None, F32)
                    outputs.setdefault("k", []).append(k.reshape(n_p, s_p, DA_HEADS, HEAD))
                    outputs.setdefault("v", []).append(v.reshape(n_p, s_p, DA_HEADS, HEAD))
                    att = _attn_prompt(q, k, v, lam_p, s_p, lam_init)
                else:
                    q, k, v = _inproj_odd(x, mod[l], row_of, mix_gain, w_in, _rope_tables(s_s), BF16)
                    ck = cache_attn_k[:, o].reshape(n_s, past, D)
                    cv = cache_attn_v[:, o].reshape(n_s, past, D)
                    att = _attn_sample(q, k, v, ck, cv, lam_p, n_s, s_s, lam_init)
                mixed.append(([att], None, x))
        final_gain = final_norm_g.astype(F32).reshape(1, D) if l == DEPTH - 1 else None
        streams = [(_mix_ffn(hs, og, x, mod[l], row_of, norm_gain, w_out, ffn_gain, l, w1_all, w3_all, w2_all,
                             final_gain), row_of)
                   for (hs, og, x), (_, row_of) in zip(mixed, streams)]

    y_prompt = streams[0][0].reshape(n_p, s_p, D)
    y_sample = streams[1][0].reshape(n_s, s_s, D)
    stack = lambda name: jnp.stack(outputs[name], axis=1)
    return (y_prompt, y_sample, stack("k"), stack("v"), stack("C"), stack("n"), stack("m"), stack("S"))
```

```python
import functools
import math

import jax
import jax.numpy as jnp
import numpy as np
from jax import lax
from jax.experimental import pallas as pl
from jax.experimental.pallas import tpu as pltpu

F32 = jnp.float32
BF16 = jnp.bfloat16

D = 1024
DEPTH = 2
GRID_W = 64
ML_HEADS = 4
HG_HEADS = 4
HEAD = 128
DA_HEADS = 8
DA_DQK = 64
ROPE_BASE = 10000.0
LOG2E = math.log2(math.e)
EPS = 1e-6
D_FF = ((8 * D // 3 + 255) // 256) * 256
EV_SIZES = (512, 512, 512, 512, 16, 512, 512, 512, 512, 512)
EV_COLS = 9 * 512 + 128

SEG = 256
HALF = SEG // 2
TILE_LEVELS = (1, 2, 4, 8, 16, 32, 64)
TOKEN_TILE = 512
FFN_CHUNK = 256
KEY_CHUNK = 256
ATTN_LOOKAHEAD = 6
VMEM_LIMIT = 56 * 1024 * 1024


def _cparams(n_axes):
    return pltpu.CompilerParams(dimension_semantics=("arbitrary",) * n_axes, vmem_limit_bytes=VMEM_LIMIT)


def _sigmoid(x):
    return 1.0 / (1.0 + jnp.exp(-x))


def _silu(x):
    return x * _sigmoid(x)


def _log_sigmoid(x):
    return jnp.minimum(x, 0.0) - jnp.log(1.0 + jnp.exp(-jnp.abs(x)))


def _dot(a, b):
    return jnp.dot(a, b, preferred_element_type=F32)


def _dot_nt(a, b):
    return lax.dot_general(a, b, (((1,), (1,)), ((), ())), preferred_element_type=F32)


def _norm_mod(x, gain, mod, k):
    ms = jnp.mean(x * x, axis=-1, keepdims=True)
    return x * lax.rsqrt(ms + EPS) * gain * (1.0 + mod[3 * k + 1:3 * k + 2]) + mod[3 * k:3 * k + 1]


def _mod_kernel(c_ref, w_ref, b_ref, o_ref):
    s = _silu(c_ref[...]).astype(BF16)
    o_ref[...] = _dot(s, w_ref[...].astype(BF16)) + b_ref[...]


def _modulation(cond8, ada_w, ada_b):
    n_layers = ada_w.shape[0]
    tn = 1536
    return pl.pallas_call(
        _mod_kernel,
        grid=(n_layers, 6 * D // tn),
        in_specs=[pl.BlockSpec((8, D), lambda l, n: (0, 0)),
                  pl.BlockSpec((None, D, tn), lambda l, n: (l, 0, n)),
                  pl.BlockSpec((None, 1, tn), lambda l, n: (l, 0, n))],
        out_specs=pl.BlockSpec((None, 8, tn), lambda l, n: (l, 0, n)),
        out_shape=jax.ShapeDtypeStruct((n_layers, 8, 6 * D), F32),
        compiler_params=_cparams(2),
        name="ada_modulation",
    )(cond8, ada_w, ada_b.reshape(n_layers, 1, 6 * D))


def _inproj_even_body(x, mod, gain, w_ref, gate_b, lb_logits, e_idx, qkv_ref, gate_ref, hqi_ref, hgf_ref, og_ref):
    h = _norm_mod(x, gain, mod, 0).astype(BF16)

    def proj(c0, n):
        return _dot(h, w_ref[:, c0:c0 + n])

    qkv_ref[:, 0:512] = (proj(0, 512) * (HEAD ** -0.5)).astype(BF16)
    qkv_ref[:, 512:1536] = proj(512, 1024).astype(BF16)
    og_ref[:, 0:512] = proj(1536, 512)
    hqi_ref[...] = proj(2048, 1024).astype(BF16)
    mx = jnp.max(lb_logits, axis=0, keepdims=True)
    ex = jnp.exp(lb_logits - mx)
    lb = jnp.sum(ex[0:e_idx + 1], axis=0, keepdims=True) / jnp.sum(ex, axis=0, keepdims=True)
    hgf_ref[:, 0:512] = jnp.log(lb + (1.0 - lb) * _sigmoid(proj(3072, 512)))
    hgf_ref[:, 512:1024] = jnp.log(lb + (1.0 - lb) * _sigmoid(proj(3584, 512)))
    og_ref[:, 512:1024] = proj(4096, 512)
    gt = proj(4608, 128) + gate_b
    lane = lax.broadcasted_iota(jnp.int32, gt.shape, 1)
    gate_ref[...] = jnp.where(lane < 8, gt, jnp.where(lane < 16, _log_sigmoid(gt), 0.0))


def _inproj_even_kernel(x_ref, mod_ref, g_ref, w_ref, gb_ref, lbl_ref, qkv_ref, gate_ref, hqi_ref, hgf_ref,
                        og_ref, *, e_idx):
    _inproj_even_body(x_ref[...], mod_ref[...], g_ref[...], w_ref, gb_ref[...], lbl_ref[...], e_idx,
                      qkv_ref, gate_ref, hqi_ref, hgf_ref, og_ref)


def _inproj_even(x, mod_l, row_of_tile, gain, w, gate_b, lb_logits, e_idx):
    n_tok = x.shape[0]
    tm = TOKEN_TILE
    tok = lambda n: pl.BlockSpec((tm, n), lambda i: (i, 0))
    full = lambda a: pl.BlockSpec(a.shape, lambda i: (0,) * a.ndim)
    return pl.pallas_call(
        functools.partial(_inproj_even_kernel, e_idx=e_idx),
        grid=(n_tok // tm,),
        in_specs=[tok(D), pl.BlockSpec((None, 6, D), lambda i: (row_of_tile(i), 0, 0)),
                  full(gain), full(w), full(gate_b), full(lb_logits)],
        out_specs=[tok(1536), tok(128), tok(1024), tok(1024), tok(1024)],
        out_shape=[jax.ShapeDtypeStruct((n_tok, 1536), BF16), jax.ShapeDtypeStruct((n_tok, 128), F32),
                   jax.ShapeDtypeStruct((n_tok, 1024), BF16), jax.ShapeDtypeStruct((n_tok, 1024), F32),
                   jax.ShapeDtypeStruct((n_tok, 1024), F32)],
        compiler_params=_cparams(1),
        name="inproj_even",
    )(x, mod_l, gain, w, gate_b, lb_logits)


def _shift_rows(x, k, fill, up):
    n = x.shape[0]
    if k % 8 == 0:
        pad = jnp.full((k,) + x.shape[1:], 0.0 if fill is None else fill, x.dtype)
        return jnp.concatenate([x[k:], pad], axis=0) if up else jnp.concatenate([pad, x[:n - k]], axis=0)
    y = pltpu.roll(x, (n - k) if up else k, 0)
    if fill is None:
        return y
    row = lax.broadcasted_iota(jnp.int32, x.shape, 0)
    return jnp.where(row >= n - k, fill, y) if up else jnp.where(row < k, fill, y)


def _cummax_rows(x, rev):
    k = 1
    while k < x.shape[0]:
        x = jnp.maximum(x, _shift_rows(x, k, -jnp.inf, up=rev))
        k *= 2
    return x


def _cumsum_rows(tri, x):
    hi = x.astype(BF16)
    r1 = x - hi.astype(F32)
    mid = r1.astype(BF16)
    lo = (r1 - mid.astype(F32)).astype(BF16)
    return _dot(tri, hi) + _dot(tri, mid) + _dot(tri, lo)


def _hgrn_tile(q, kk, f, a, q_b, rev, mask_sc, d, up_small):
    p = _dot_nt(q_b, kk.astype(BF16)) * mask_sc[d, len(TILE_LEVELS)]
    bm = a
    for li, m in enumerate(TILE_LEVELS):
        if m < 8:
            qrole = jnp.logical_not(up_small[li]) if rev else up_small[li]
            if m == 1:
                r = jnp.where(qrole, q * f, kk)
            else:
                x = jnp.where(qrole, _shift_rows(bm, m, None, up=rev), bm)
                r = jnp.where(qrole, q, kk) * jnp.exp(jnp.where(qrole, a - x, x - a))
            if 2 * m < 8:
                bm = jnp.where(qrole, bm, _shift_rows(bm, m, None, up=not rev))
        else:
            pieces = []
            for b0 in range(0, HALF, 2 * m):
                lo, up = slice(b0, b0 + m), slice(b0 + m, b0 + 2 * m)
                if rev:
                    ref = a[b0 + m:b0 + m + 1]
                    pieces += [q[lo] * jnp.exp(a[lo] - ref), kk[up] * jnp.exp(ref - a[up])]
                else:
                    ref = a[b0 + m - 1:b0 + m]
                    pieces += [kk[lo] * jnp.exp(ref - a[lo]), q[up] * jnp.exp(a[up] - ref)]
            r = jnp.concatenate(pieces, axis=0)
        rb = r.astype(BF16)
        p = p + _dot_nt(rb, rb) * mask_sc[d, li]
    return p


def _build_scan_constants(tri_sc, mask_sc):
    row = lax.broadcasted_iota(jnp.int32, (SEG, SEG), 0)
    col = lax.broadcasted_iota(jnp.int32, (SEG, SEG), 1)
    tri_sc[0] = (col <= row).astype(BF16)
    tri_sc[1] = (col >= row).astype(BF16)
    trow = lax.broadcasted_iota(jnp.int32, (HALF, HALF), 0)
    tcol = lax.broadcasted_iota(jnp.int32, (HALF, HALF), 1)
    for li, m in enumerate(TILE_LEVELS):
        sh = m.bit_length() - 1
        same = (trow >> (sh + 1)) == (tcol >> (sh + 1))
        t_up = ((trow >> sh) & 1) == 1
        s_up = ((tcol >> sh) & 1) == 1
        mask_sc[0, li] = (same & t_up & jnp.logical_not(s_up)).astype(F32)
        mask_sc[1, li] = (same & s_up & jnp.logical_not(t_up)).astype(F32)
    eye = (trow == tcol).astype(F32)
    mask_sc[0, len(TILE_LEVELS)] = eye
    mask_sc[1, len(TILE_LEVELS)] = eye


def _scan_units(dir_refs, write_h, tri_sc, mask_sc, m_in, state, emit, result):
    row = lax.broadcasted_iota(jnp.int32, (SEG, SEG), 0)
    col = lax.broadcasted_iota(jnp.int32, (SEG, SEG), 1)
    lane128 = lax.broadcasted_iota(jnp.int32, (SEG, HEAD), 1)
    e0 = (lane128 == 0).astype(BF16)
    row_half = lax.broadcasted_iota(jnp.int32, (HALF, HEAD), 0)
    up_masks = [((row_half >> (m.bit_length() - 1)) & 1) == 1 for m in TILE_LEVELS if m < 8]
    tmasks = (col <= row, col >= row)
    use_state = state is not None
    want_state = use_state or emit is not None
    m_out_rows = []

    for d in range(2):
        qkv_ref, gate_ref, hqi_ref, hgf_ref = dir_refs[d]
        rev = d == 1
        last = 0 if rev else SEG - 1
        tri = tri_sc[d]

        slab = gate_ref[...]
        b_al = pltpu.roll(_cumsum_rows(tri, slab), HEAD - 8, 1)
        u = slab - b_al
        mx = jnp.maximum(_cummax_rows(u, rev), m_in)
        w_inter = jnp.exp(m_in - mx)
        e_den = jnp.exp(-(b_al + mx))
        mx_last = mx[last:last + 1, :]
        m_out_rows.append(b_al[last:last + 1, :] + mx_last)
        decay = jnp.exp(m_in - mx_last)
        wg = jnp.exp(u - mx_last)
        u_t = u.T

        for h in range(ML_HEADS):
            c = 4 * d + h
            q = qkv_ref[:, h * HEAD:(h + 1) * HEAD]
            k = qkv_ref[:, 512 + h * HEAD:512 + (h + 1) * HEAD]
            v = qkv_ref[:, 1024 + h * HEAD:1024 + (h + 1) * HEAD]
            dm = jnp.where(tmasks[d], jnp.exp(u_t[c:c + 1, :] - mx[:, c:c + 1]), 0.0)
            s = (_dot_nt(q, k) * dm).astype(BF16)
            v_aug = jnp.concatenate([v, e0], axis=1)
            numden = _dot(s, v_aug)
            if use_state:
                numden = numden + w_inter[:, c:c + 1] * _dot(q, state[0][d, h].astype(BF16))
            den = jnp.maximum(jnp.abs(numden[:, HEAD:HEAD + 1]), e_den[:, c:c + 1])
            write_h(d, h * HEAD, numden[:, 0:HEAD] / den)
            if want_state:
                kw_t = (k.astype(F32) * wg[:, c:c + 1]).T.astype(BF16)
                upd = _dot(kw_t, v_aug)
                if use_state:
                    upd = upd + decay[:, c:c + 1] * state[0][d, h]
                    state[0][d, h] = upd
                if emit is not None:
                    emit("ml", d, h, upd)
            yield

        lf_all = hgf_ref[:, 512 * d:512 * (d + 1)]
        a_all = _cumsum_rows(tri, lf_all)
        f_all = jnp.exp(lf_all)
        kk_all = 1.0 - f_all
        for h in range(HG_HEADS):
            sl = slice(h * HEAD, (h + 1) * HEAD)
            a = a_all[:, sl]
            kk = kk_all[:, sl]
            f = f_all[:, sl]
            q_b = hqi_ref[:, h * HEAD:(h + 1) * HEAD]
            i_b = hqi_ref[:, 512 + h * HEAD:512 + (h + 1) * HEAD]
            q = q_b.astype(F32)
            qh, kh = (0, 1) if rev else (1, 0)
            a_q, a_k = a[qh * HALF:(qh + 1) * HALF], a[kh * HALF:(kh + 1) * HALF]
            a_mid = a[HALF:HALF + 1] if rev else a[HALF - 1:HALF]
            r_q = (q[qh * HALF:(qh + 1) * HALF] * jnp.exp(a_q - a_mid)).astype(BF16)
            r_k = (kk[kh * HALF:(kh + 1) * HALF] * jnp.exp(a_mid - a_k)).astype(BF16)
            p_cross = _dot_nt(r_q, r_k).astype(BF16)
            p_tiles = []
            for t in range(2):
                rows = slice(t * HALF, (t + 1) * HALF)
                p_tiles.append(_hgrn_tile(q[rows], kk[rows], f[rows], a[rows], q_b[rows], rev, mask_sc, d,
                                          up_masks).astype(BF16))
            i_lo, i_hi = i_b[0:HALF], i_b[HALF:SEG]
            if rev:
                o = jnp.concatenate([_dot(p_tiles[0], i_lo) + _dot(p_cross, i_hi), _dot(p_tiles[1], i_hi)], axis=0)
            else:
                o = jnp.concatenate([_dot(p_tiles[0], i_lo), _dot(p_cross, i_lo) + _dot(p_tiles[1], i_hi)], axis=0)
            if use_state:
                st = state[1][d, h]
                o = o + _dot_nt((q * jnp.exp(a)).astype(BF16), st.astype(BF16))
            write_h(d, 512 + h * HEAD, o)
            if want_state:
                a_l = a[last:last + 1, :]
                kd = (kk * jnp.exp(a_l - a)).astype(BF16)
                st_new = _dot(i_b.astype(F32).T.astype(BF16), kd)
                if use_state:
                    st_new = st_new + st * jnp.exp(a_l)
                    state[1][d, h] = st_new
                if emit is not None:
                    emit("hg", d, h, st_new)
            yield

    lane_row = lax.broadcasted_iota(jnp.int32, (1, HEAD), 1)
    result["m"] = jnp.where(lane_row < 4, m_out_rows[0], jnp.where(lane_row < 8, m_out_rows[1], 0.0))


def _emit_states(kind, d, h, value, c_ref, n_ref, s_ref):
    if kind == "ml":
        c_ref[d, h] = value[:, 0:HEAD]
        n_ref[d, h:h + 1, :] = value[:, HEAD:2 * HEAD].T[0:1, :]
    else:
        s_ref[d, h] = value.T


def _scan_kernel(*refs, has_init, emit_state, nseg):
    n_in = 8 + (4 if has_init else 0)
    n_out = 2 + (4 if emit_state else 0)
    ins, outs, scr = refs[:n_in], refs[n_in:n_in + n_out], refs[n_in + n_out:]
    dir_refs = (ins[0:4], ins[4:8])
    out_refs = outs[0:2]
    tri_sc, mask_sc, caug_sc, st_sc, m_sc = scr
    use_state = has_init or nseg > 1
    b_id, j = pl.program_id(0), pl.program_id(1)

    @pl.when(jnp.logical_and(b_id == 0, j == 0))
    def _build_constants():
        _build_scan_constants(tri_sc, mask_sc)

    if use_state:
        @pl.when(j == 0)
        def _init_state():
            if has_init:
                c0_ref, n0_ref, m0_ref, s0_ref = ins[8:12]
                for d in range(2):
                    for h in range(ML_HEADS):
                        caug_sc[d, h, :, 0:HEAD] = c0_ref[d, h]
                        caug_sc[d, h, :, HEAD:2 * HEAD] = jnp.broadcast_to(n0_ref[d, h:h + 1, :], (HEAD, HEAD)).T
                        st_sc[d, h] = s0_ref[d, h].T
                m_sc[...] = jnp.broadcast_to(m0_ref[...], m_sc.shape)
            else:
                caug_sc[...] = jnp.zeros(caug_sc.shape, F32)
                st_sc[...] = jnp.zeros(st_sc.shape, F32)
                m_sc[...] = jnp.zeros(m_sc.shape, F32)

    def write_h(d, c0, value):
        out_refs[d][:, c0:c0 + HEAD] = value

    def emit(kind, d, h, value):
        if nseg == 1:
            _emit_states(kind, d, h, value, outs[2], outs[3], outs[5])
        else:
            pl.when(j == nseg - 1)(lambda: _emit_states(kind, d, h, value, outs[2], outs[3], outs[5]))

    m_in = m_sc[0:1, :] if use_state else jnp.zeros((1, HEAD), F32)
    result = {}
    for _ in _scan_units(dir_refs, write_h, tri_sc, mask_sc, m_in, (caug_sc, st_sc) if use_state else None,
                         emit if emit_state else None, result):
        pass
    m_new = result["m"]
    if use_state:
        m_sc[...] = jnp.broadcast_to(m_new, m_sc.shape)
    if emit_state:
        if nseg == 1:
            outs[4][...] = m_new
        else:
            @pl.when(j == nseg - 1)
            def _emit_m():
                outs[4][...] = m_new


def _scan(qkv, gate, hqi, hgf, n_seq, nseg, init=None, emit_state=False):
    n_tok = qkv.shape[0]
    fwd = lambda n: pl.BlockSpec((SEG, n), lambda b, j: (b * nseg + j, 0))
    bwd = lambda n: pl.BlockSpec((SEG, n), lambda b, j: (b * nseg + nseg - 1 - j, 0))
    widths = (1536, 128, 1024, 1024)
    in_specs = [fwd(n) for n in widths] + [bwd(n) for n in widths]
    args = [qkv, gate, hqi, hgf, qkv, gate, hqi, hgf]
    mat = pl.BlockSpec((None, 2, 4, HEAD, HEAD), lambda b, j: (b, 0, 0, 0, 0))
    vec = pl.BlockSpec((None, 2, 4, HEAD), lambda b, j: (b, 0, 0, 0))
    sca = pl.BlockSpec((None, 1, HEAD), lambda b, j: (b, 0, 0))
    if init is not None:
        in_specs += [mat, vec, sca, mat]
        args += list(init)
    out_specs = [fwd(1024), bwd(1024)]
    out_shape = [jax.ShapeDtypeStruct((n_tok, 1024), F32)] * 2
    if emit_state:
        out_specs += [mat, vec, sca, mat]
        out_shape += [jax.ShapeDtypeStruct((n_seq, 2, 4, HEAD, HEAD), F32),
                      jax.ShapeDtypeStruct((n_seq, 2, 4, HEAD), F32),
                      jax.ShapeDtypeStruct((n_seq, 1, HEAD), F32),
                      jax.ShapeDtypeStruct((n_seq, 2, 4, HEAD, HEAD), F32)]
    return pl.pallas_call(
        functools.partial(_scan_kernel, has_init=init is not None, emit_state=emit_state, nseg=nseg),
        grid=(n_seq, nseg),
        in_specs=in_specs,
        out_specs=out_specs,
        out_shape=out_shape,
        scratch_shapes=[pltpu.VMEM((2, SEG, SEG), BF16),
                        pltpu.VMEM((2, len(TILE_LEVELS) + 1, HALF, HALF), F32),
                        pltpu.VMEM((2, 4, HEAD, 2 * HEAD), F32),
                        pltpu.VMEM((2, 4, HEAD, HEAD), F32),
                        pltpu.VMEM((8, HEAD), F32)],
        compiler_params=_cparams(2),
        name="bidir_scan",
    )(*args)


def _mix_ffn_pieces(h, og, x, mod, norm_gain, wo_ref, ffn_gain, w1_ref, w3_ref, w2_ref, final_gain, o_ref):
    parts = []
    for g in range(D // HEAD):
        hs = h[:, g * HEAD:(g + 1) * HEAD]
        parts.append(hs * lax.rsqrt(jnp.mean(hs * hs, axis=-1, keepdims=True) + EPS))
    hn = jnp.concatenate(parts, axis=1) * norm_gain
    if og is not None:
        hn = hn * jnp.concatenate([_sigmoid(og[:, 0:512]), _silu(og[:, 512:1024])], axis=1)
    x = x + mod[2:3, :] * _dot(hn.astype(BF16), wo_ref[...])
    yield
    hf = _norm_mod(x, ffn_gain, mod, 1).astype(BF16)
    acc = jnp.zeros(x.shape, F32)
    for c0 in range(0, D_FF, FFN_CHUNK):
        a = _dot(hf, w1_ref[:, c0:c0 + FFN_CHUNK])
        b = _dot(hf, w3_ref[:, c0:c0 + FFN_CHUNK])
        acc = acc + _dot((_silu(a) * b).astype(BF16), w2_ref[c0:c0 + FFN_CHUNK, :])
        yield
    y = x + mod[5:6, :] * acc
    if final_gain is not None:
        y = y * lax.rsqrt(jnp.mean(y * y, axis=-1, keepdims=True) + EPS) * final_gain
    o_ref[...] = y


def _mix_ffn_kernel(*refs, n_h, gated, final):
    h_refs = refs[:n_h]
    pos = n_h
    og_ref = refs[pos] if gated else None
    pos += 1 if gated else 0
    x_ref, mod_ref, ng_ref, wo_ref, g_ref, w1_ref, w3_ref, w2_ref = refs[pos:pos + 8]
    fg_ref = refs[pos + 8] if final else None
    o_ref = refs[-1]
    h = h_refs[0][...]
    for r in h_refs[1:]:
        h = h + r[...]
    for _ in _mix_ffn_pieces(h, og_ref[...] if gated else None, x_ref[...], mod_ref[...], ng_ref[...], wo_ref,
                             g_ref[...], w1_ref, w3_ref, w2_ref, fg_ref[...] if final else None, o_ref):
        pass


def _mix_ffn(hs, og, x, mod_l, row_of_tile, norm_gain, w_out, ffn_gain, layer, w1, w3, w2, final_gain=None):
    n_tok = x.shape[0]
    tm = TOKEN_TILE
    tok = pl.BlockSpec((tm, D), lambda i: (i, 0))
    full = lambda a: pl.BlockSpec(a.shape, lambda i: (0,) * a.ndim, pipeline_mode=pl.Buffered(1))
    of_layer = lambda a: pl.BlockSpec((None,) + a.shape[1:], lambda i: (layer, 0, 0), pipeline_mode=pl.Buffered(1))
    gated = og is not None
    final = final_gain is not None
    consts = [norm_gain, w_out, ffn_gain, w1, w3, w2] + ([final_gain] if final else [])
    args = list(hs) + ([og] if gated else []) + [x, mod_l] + consts
    in_specs = [tok] * (len(hs) + (1 if gated else 0) + 1)
    in_specs += [pl.BlockSpec((None, 6, D), lambda i: (row_of_tile(i), 0, 0))]
    in_specs += [of_layer(a) if a.ndim == 3 else full(a) for a in consts]
    return pl.pallas_call(
        functools.partial(_mix_ffn_kernel, n_h=len(hs), gated=gated, final=final),
        grid=(n_tok // tm,),
        in_specs=in_specs,
        out_specs=tok,
        out_shape=jax.ShapeDtypeStruct((n_tok, D), F32),
        compiler_params=_cparams(1),
        name="mix_ffn",
    )(*args)


def _rope(x, cos, sin_signed):
    lane = lax.broadcasted_iota(jnp.int32, (x.shape[0], HEAD), 1)
    first = (lane & 16) == 0
    parts = []
    for h in range(DA_HEADS):
        xh = x[:, h * HEAD:(h + 1) * HEAD]
        partner = jnp.where(first, pltpu.roll(xh, HEAD - 16, 1), pltpu.roll(xh, 16, 1))
        parts.append(xh * cos + partner * sin_signed)
    return jnp.concatenate(parts, axis=1)


def _inproj_odd_kernel(*refs, rope):
    x_ref, mod_ref, g_ref, w_ref = refs[:4]
    q_ref, k_ref, v_ref = refs[-3:]
    h = _norm_mod(x_ref[...], g_ref[...], mod_ref[...], 0).astype(BF16)
    q = _dot(h, w_ref[:, 0:D])
    k = _dot(h, w_ref[:, D:2 * D])
    v = _dot(h, w_ref[:, 2 * D:3 * D])
    if rope:
        cos, sin_signed = refs[4][...], refs[5][...]
        q = _rope(q, cos, sin_signed)
        k = _rope(k, cos, sin_signed)
    q_ref[...] = (q * (DA_DQK ** -0.5 * LOG2E)).astype(q_ref.dtype)
    k_ref[...] = k.astype(k_ref.dtype)
    v_ref[...] = v.astype(v_ref.dtype)


def _inproj_odd(x, mod_l, row_of_tile, gain, w, rope_tables, kv_dtype):
    n_tok = x.shape[0]
    tm = TOKEN_TILE
    tok = pl.BlockSpec((tm, D), lambda i: (i, 0))
    full = lambda a: pl.BlockSpec(a.shape, lambda i: (0,) * a.ndim)
    args = [x, mod_l, gain, w]
    in_specs = [tok, pl.BlockSpec((None, 6, D), lambda i: (row_of_tile(i), 0, 0)), full(gain), full(w)]
    if rope_tables is not None:
        tiles_per_seq = rope_tables[0].shape[0] // tm
        args += list(rope_tables)
        in_specs += [pl.BlockSpec((tm, HEAD), lambda i: (i % tiles_per_seq, 0))] * 2
    return pl.pallas_call(
        functools.partial(_inproj_odd_kernel, rope=rope_tables is not None),
        grid=(n_tok // tm,),
        in_specs=in_specs,
        out_specs=[tok, tok, tok],
        out_shape=[jax.ShapeDtypeStruct((n_tok, D), BF16), jax.ShapeDtypeStruct((n_tok, D), kv_dtype),
                   jax.ShapeDtypeStruct((n_tok, D), kv_dtype)],
        compiler_params=_cparams(1),
        name="inproj_odd",
    )(*args)


def _rope_tables(n_tok):
    quarter = DA_DQK // 4
    tok = np.arange(n_tok)
    pos = np.stack([tok // GRID_W, tok % GRID_W], axis=1).astype(np.float32)
    inv = (np.float32(ROPE_BASE) ** (-np.arange(quarter, dtype=np.float32) / np.float32(quarter))).astype(np.float32)
    lane = np.arange(HEAD)
    ang = (pos[:, (lane // 32) % 2] * inv[lane % quarter][None, :]).astype(np.float32)
    sign = np.where((lane % 32) < quarter, -1.0, 1.0).astype(np.float32)
    return jnp.asarray(np.cos(ang), F32), jnp.asarray(np.sin(ang) * sign[None, :], F32)


def _lambda(lam_ref, lam_init):
    lp = lam_ref[...]
    return (jnp.exp(jnp.sum(lp[0:1] * lp[1:2], axis=-1, keepdims=True))
            - jnp.exp(jnp.sum(lp[2:3] * lp[3:4], axis=-1, keepdims=True)) + lam_init)


VT_ROWS = HEAD + 16


def _qt2(qh):
    qt = qh.astype(F32).T
    row = lax.broadcasted_iota(jnp.int32, qt.shape, 0)
    return jnp.concatenate([jnp.where(row < DA_DQK, qt, 0.0), jnp.where(row >= DA_DQK, qt, 0.0)],
                           axis=1).astype(BF16)


def _vt_aug(vh):
    tk = vh.shape[0]
    row = lax.broadcasted_iota(jnp.int32, (VT_ROWS - HEAD, tk), 0)
    return jnp.concatenate([vh.astype(F32).T, (row == 0).astype(F32)], axis=0).astype(BF16)


def _diff_attn_heads(q_ref, k_fn, vt_fn, n_chunks, lam, o_ref):
    tq = q_ref.shape[0]
    items = [(h, c) for h in range(DA_HEADS) for c in range(n_chunks)]
    qt2, state, pending = {}, {}, []

    def finish(h, c, st):
        cm = jnp.max(st, axis=0, keepdims=True)
        if c == 0:
            m_new = cm
            acc = _dot(vt_fn(h, c), jnp.exp2(st - m_new).astype(BF16))
        else:
            m, acc = state[h]
            m_new = jnp.maximum(m, cm)
            acc = acc * jnp.exp2(m - m_new) + _dot(vt_fn(h, c), jnp.exp2(st - m_new).astype(BF16))
        state[h] = (m_new, acc)
        if c == n_chunks - 1:
            den = acc[HEAD:HEAD + 1, :]
            o_t = acc[0:HEAD, 0:tq] * (1.0 / den[:, 0:tq]) - acc[0:HEAD, tq:] * (lam / den[:, tq:])
            o_ref[:, h * HEAD:(h + 1) * HEAD] = o_t.T

    for h, c in items:
        if c == 0:
            qt2[h] = _qt2(q_ref[:, h * HEAD:(h + 1) * HEAD])
        pending.append((h, c, _dot(k_fn(h, c), qt2[h])))
        if len(pending) > ATTN_LOOKAHEAD:
            finish(*pending.pop(0))
    while pending:
        finish(*pending.pop(0))


def _attn_prompt_kernel(q_ref, k_ref, v_ref, lam_ref, o_ref, *, lam_init):
    lam = _lambda(lam_ref, lam_init)
    _diff_attn_heads(q_ref, lambda h, c: k_ref[:, h * HEAD:(h + 1) * HEAD].astype(BF16),
                     lambda h, c: _vt_aug(v_ref[:, h * HEAD:(h + 1) * HEAD]), 1, lam, o_ref)


def _attn_prompt(q, k, v, lam_p, seq, lam_init):
    n_tok = q.shape[0]
    tok = pl.BlockSpec((seq, D), lambda b: (b, 0))
    return pl.pallas_call(
        functools.partial(_attn_prompt_kernel, lam_init=lam_init),
        grid=(n_tok // seq,),
        in_specs=[tok, tok, tok, pl.BlockSpec(lam_p.shape, lambda b: (0, 0))],
        out_specs=tok,
        out_shape=jax.ShapeDtypeStruct((n_tok, D), F32),
        compiler_params=_cparams(1),
        name="diff_attn_prompt",
    )(q, k, v, lam_p)


def _attn_sample_kernel(q_ref, k_ref, v_ref, ck_ref, cv_ref, lam_ref, o_ref, kcat, vtcat, *, lam_init, past):
    n_keys = kcat.shape[0]
    chunks = [(c0, min(KEY_CHUNK, n_keys - c0)) for c0 in range(0, n_keys, KEY_CHUNK)]

    @pl.when(pl.program_id(1) == 0)
    def _gather_keys():
        kcat[0:past, :] = ck_ref[...].astype(BF16)
        kcat[past:, :] = k_ref[...]
        for h in range(DA_HEADS):
            sl = slice(h * HEAD, (h + 1) * HEAD)
            for c0, n in chunks:
                src, r0 = (cv_ref, c0) if c0 < past else (v_ref, c0 - past)
                vtcat[h, :, c0:c0 + n] = _vt_aug(src[r0:r0 + n, sl])

    lam = _lambda(lam_ref, lam_init)
    _diff_attn_heads(q_ref, lambda h, c: kcat[chunks[c][0]:chunks[c][0] + chunks[c][1], h * HEAD:(h + 1) * HEAD],
                     lambda h, c: vtcat[h, :, chunks[c][0]:chunks[c][0] + chunks[c][1]], len(chunks), lam, o_ref)


def _attn_sample(q, k, v, ck, cv, lam_p, n_seq, seq, lam_init, tq=128):
    past = ck.shape[1]
    qb = pl.BlockSpec((tq, D), lambda b, i: (b * (seq // tq) + i, 0))
    kv = pl.BlockSpec((seq, D), lambda b, i: (b, 0))
    cache = pl.BlockSpec((None, past, D), lambda b, i: (b, 0, 0))
    return pl.pallas_call(
        functools.partial(_attn_sample_kernel, lam_init=lam_init, past=past),
        grid=(n_seq, seq // tq),
        in_specs=[qb, kv, kv, cache, cache, pl.BlockSpec(lam_p.shape, lambda b, i: (0, 0))],
        out_specs=qb,
        out_shape=jax.ShapeDtypeStruct((n_seq * seq, D), F32),
        scratch_shapes=[pltpu.VMEM((past + seq, D), BF16), pltpu.VMEM((DA_HEADS, VT_ROWS, past + seq), BF16)],
        compiler_params=_cparams(2),
        name="diff_attn_sample",
    )(q, k, v, ck, cv, lam_p)


def kernel(x_prompt, x_sample, c, c_ctx, cache_attn_k, cache_attn_v, state_mlstm_C, state_mlstm_n, state_mlstm_m,
           state_hgrn_S, ada_w, ada_b, norm_mix_g, norm_ffn_g, ev_w_in, ev_gate_b, ev_lb_logits, ml_norm_g,
           hg_norm_g, ev_w_out, od_w_in, od_lambda, da_norm_g, od_w_out, ffn_w1, ffn_w3, ffn_w2, final_norm_g):
    assert DEPTH % 2 == 0
    n_p, s_p, _ = x_prompt.shape
    n_s, s_s, _ = x_sample.shape
    past = cache_attn_k.shape[2]
    assert s_p == SEG and s_s % SEG == 0 and s_s % TOKEN_TILE == 0 and (n_p * s_p) % TOKEN_TILE == 0
    xp = x_prompt.astype(F32).reshape(n_p * s_p, D)
    xs = x_sample.astype(F32).reshape(n_s * s_s, D)

    cond8 = jnp.zeros((8, D), F32).at[0].set(c_ctx.astype(F32)).at[1:1 + n_s].set(c.astype(F32))
    mod = _modulation(cond8, ada_w.astype(F32), ada_b.astype(F32)).reshape(DEPTH, 8, 6, D)
    row_p = lambda i: 0
    row_s = lambda i: 1 + i // (s_s // TOKEN_TILE)
    streams = [(xp, row_p), (xs, row_s)]
    outputs = {}
    w1_all, w3_all, w2_all = ffn_w1.astype(BF16), ffn_w3.astype(BF16), ffn_w2.astype(BF16)

    for l in range(DEPTH):
        mix_gain = norm_mix_g[l].astype(F32).reshape(1, D)
        ffn_gain = norm_ffn_g[l].astype(F32).reshape(1, D)
        if l % 2 == 0:
            e = l // 2
            parts = jnp.split(ev_w_in[e], [sum(EV_SIZES[:i + 1]) for i in range(len(EV_SIZES) - 1)], axis=1)
            ml_q, ml_k, ml_v, ml_o, ml_g, hg_q, hg_ff, hg_fb, hg_i, hg_o = parts
            w_in = jnp.concatenate([ml_q, ml_k, ml_v, ml_o, hg_q, hg_i, hg_ff, hg_fb, hg_o,
                                    jnp.pad(ml_g, ((0, 0), (0, HEAD - 16)))], axis=1).astype(BF16)
            gate_b = jnp.pad(ev_gate_b[e].astype(F32), (0, HEAD - 16)).reshape(1, HEAD)
            norm_gain = jnp.concatenate([ml_norm_g[e], hg_norm_g[e]]).astype(F32).reshape(1, D)
            w_out = ev_w_out[e].astype(BF16)
            mixed = []
            for si, (x, row_of) in enumerate(streams):
                qkv, gate, hqi, hgf, og = _inproj_even(x, mod[l], row_of, mix_gain, w_in, gate_b,
                                                       ev_lb_logits.astype(F32), e)
                if si == 0:
                    hf, hb, c_new, n_new, m_new, s_new = _scan(qkv, gate, hqi, hgf, n_p, s_p // SEG, emit_state=True)
                    outputs.setdefault("C", []).append(c_new)
                    outputs.setdefault("n", []).append(n_new)
                    outputs.setdefault("m", []).append(m_new[:, 0, 0:8].reshape(n_p, 2, 4))
                    outputs.setdefault("S", []).append(s_new)
                else:
                    init = (state_mlstm_C[:, e].astype(F32), state_mlstm_n[:, e].astype(F32),
                            jnp.pad(state_mlstm_m[:, e].astype(F32).reshape(n_s, 1, 8), ((0, 0), (0, 0), (0, HEAD - 8))),
                            state_hgrn_S[:, e].astype(F32))
                    hf, hb = _scan(qkv, gate, hqi, hgf, n_s, s_s // SEG, init=init)
                mixed.append(([hf, hb], og, x))
        else:
            o = l // 2
            lam_init = 0.8 - 0.6 * math.exp(-0.3 * l)
            w_in = od_w_in[o].astype(BF16)
            norm_gain = (jnp.tile(da_norm_g[o].astype(F32), DA_HEADS) * (1.0 - lam_init)).reshape(1, D)
            w_out = od_w_out[o].astype(BF16)
            lam_p = od_lambda[o].astype(F32)
            mixed = []
            for si, (x, row_of) in enumerate(streams):
                if si == 0:
                    q, k, v = _inproj_odd(x, mod[l], row_of, mix_gain, w_in, None, F32)
                    outputs.setdefault("k", []).append(k.reshape(n_p, s_p, DA_HEADS, HEAD))
                    outputs.setdefault("v", []).append(v.reshape(n_p, s_p, DA_HEADS, HEAD))
                    att = _attn_prompt(q, k, v, lam_p, s_p, lam_init)
                else:
                    q, k, v = _inproj_odd(x, mod[l], row_of, mix_gain, w_in, _rope_tables(s_s), BF16)
                    ck = cache_attn_k[:, o].reshape(n_s, past, D)
                    cv = cache_attn_v[:, o].reshape(n_s, past, D)
                    att = _attn_sample(q, k, v, ck, cv, lam_p, n_s, s_s, lam_init)
                mixed.append(([att], None, x))
        final_gain = final_norm_g.astype(F32).reshape(1, D) if l == DEPTH - 1 else None
        streams = [(_mix_ffn(hs, og, x, mod[l], row_of, norm_gain, w_out, ffn_gain, l, w1_all, w3_all, w2_all,
                             final_gain), row_of)
                   for (hs, og, x), (_, row_of) in zip(mixed, streams)]

    y_prompt = streams[0][0].reshape(n_p, s_p, D)
    y_sample = streams[1][0].reshape(n_s, s_s, D)
    stack = lambda name: outputs[name][0][:, None] if len(outputs[name]) == 1 else jnp.stack(outputs[name], axis=1)
    return (y_prompt, y_sample, stack("k"), stack("v"), stack("C"), stack("n"), stack("m"), stack("S"))
```

```python
import functools
import math

import jax
import jax.numpy as jnp
import numpy as np
from jax import lax
from jax.experimental import pallas as pl
from jax.experimental.pallas import tpu as pltpu

F32 = jnp.float32
BF16 = jnp.bfloat16

D = 1024
DEPTH = 2
GRID_W = 64
ML_HEADS = 4
HG_HEADS = 4
HEAD = 128
DA_HEADS = 8
DA_DQK = 64
ROPE_BASE = 10000.0
LOG2E = math.log2(math.e)
EPS = 1e-6
D_FF = ((8 * D // 3 + 255) // 256) * 256
EV_SIZES = (512, 512, 512, 512, 16, 512, 512, 512, 512, 512)
EV_COLS = 9 * 512 + 128

SEG = 256
HALF = SEG // 2
TILE_LEVELS = (1, 2, 4, 8, 16, 32, 64)
TOKEN_TILE = 512
FFN_CHUNK = 256
KEY_CHUNK = 256
ATTN_LOOKAHEAD = 6
VMEM_LIMIT = 56 * 1024 * 1024


def _cparams(n_axes):
    return pltpu.CompilerParams(dimension_semantics=("arbitrary",) * n_axes, vmem_limit_bytes=VMEM_LIMIT)


def _sigmoid(x):
    return 1.0 / (1.0 + jnp.exp(-x))


def _silu(x):
    return x * _sigmoid(x)


def _log_sigmoid(x):
    return jnp.minimum(x, 0.0) - jnp.log(1.0 + jnp.exp(-jnp.abs(x)))


def _dot(a, b):
    return jnp.dot(a, b, preferred_element_type=F32)


def _dot_nt(a, b):
    return lax.dot_general(a, b, (((1,), (1,)), ((), ())), preferred_element_type=F32)


def _norm_mod(x, gain, mod, k):
    ms = jnp.mean(x * x, axis=-1, keepdims=True)
    return x * lax.rsqrt(ms + EPS) * gain * (1.0 + mod[3 * k + 1:3 * k + 2]) + mod[3 * k:3 * k + 1]


def _mod_kernel(c_ref, w_ref, b_ref, o_ref):
    s = _silu(c_ref[...]).astype(BF16)
    o_ref[...] = _dot(s, w_ref[...].astype(BF16)) + b_ref[...]


def _modulation(cond8, ada_w, ada_b):
    n_layers = ada_w.shape[0]
    tn = 1536
    return pl.pallas_call(
        _mod_kernel,
        grid=(n_layers, 6 * D // tn),
        in_specs=[pl.BlockSpec((8, D), lambda l, n: (0, 0)),
                  pl.BlockSpec((None, D, tn), lambda l, n: (l, 0, n)),
                  pl.BlockSpec((None, 1, tn), lambda l, n: (l, 0, n))],
        out_specs=pl.BlockSpec((None, 8, tn), lambda l, n: (l, 0, n)),
        out_shape=jax.ShapeDtypeStruct((n_layers, 8, 6 * D), F32),
        compiler_params=_cparams(2),
        name="ada_modulation",
    )(cond8, ada_w, ada_b.reshape(n_layers, 1, 6 * D))


def _inproj_even_body(x, mod, gain, w_ref, gate_b, lb_logits, e_idx, qkv_ref, gate_ref, hqi_ref, hgf_ref, og_ref):
    h = _norm_mod(x, gain, mod, 0).astype(BF16)

    def proj(c0, n):
        return _dot(h, w_ref[:, c0:c0 + n])

    qkv_ref[:, 0:512] = (proj(0, 512) * (HEAD ** -0.5)).astype(BF16)
    qkv_ref[:, 512:1536] = proj(512, 1024).astype(BF16)
    og_ref[:, 0:512] = proj(1536, 512)
    hqi_ref[:, 0:512] = proj(2048, 512).astype(BF16)
    hqi_ref[:, 512:1024] = proj(3584, 512).astype(BF16)
    mx = jnp.max(lb_logits, axis=0, keepdims=True)
    ex = jnp.exp(lb_logits - mx)
    lb = jnp.sum(ex[0:e_idx + 1], axis=0, keepdims=True) / jnp.sum(ex, axis=0, keepdims=True)
    hgf_ref[:, 0:512] = jnp.log(lb + (1.0 - lb) * _sigmoid(proj(2560, 512)))
    hgf_ref[:, 512:1024] = jnp.log(lb + (1.0 - lb) * _sigmoid(proj(3072, 512)))
    og_ref[:, 512:1024] = proj(4096, 512)
    gt = proj(4608, 128) + gate_b
    lane = lax.broadcasted_iota(jnp.int32, gt.shape, 1)
    gate_ref[...] = jnp.where(lane < 8, gt, jnp.where(lane < 16, _log_sigmoid(gt), 0.0))


def _inproj_even_kernel(x_ref, mod_ref, g_ref, w_ref, gb_ref, lbl_ref, qkv_ref, gate_ref, hqi_ref, hgf_ref,
                        og_ref, *, e_idx):
    _inproj_even_body(x_ref[...], mod_ref[...], g_ref[...], w_ref, gb_ref[...], lbl_ref[...], e_idx,
                      qkv_ref, gate_ref, hqi_ref, hgf_ref, og_ref)


def _inproj_even(x, mod_l, row_of_tile, gain, w, gate_b, lb_logits, e_idx):
    n_tok = x.shape[0]
    tm = TOKEN_TILE
    tok = lambda n: pl.BlockSpec((tm, n), lambda i: (i, 0))
    full = lambda a: pl.BlockSpec(a.shape, lambda i: (0,) * a.ndim)
    return pl.pallas_call(
        functools.partial(_inproj_even_kernel, e_idx=e_idx),
        grid=(n_tok // tm,),
        in_specs=[tok(D), pl.BlockSpec((None, 6, D), lambda i: (row_of_tile(i), 0, 0)),
                  full(gain), full(w), full(gate_b), full(lb_logits)],
        out_specs=[tok(1536), tok(128), tok(1024), tok(1024), tok(1024)],
        out_shape=[jax.ShapeDtypeStruct((n_tok, 1536), BF16), jax.ShapeDtypeStruct((n_tok, 128), F32),
                   jax.ShapeDtypeStruct((n_tok, 1024), BF16), jax.ShapeDtypeStruct((n_tok, 1024), F32),
                   jax.ShapeDtypeStruct((n_tok, 1024), F32)],
        compiler_params=_cparams(1),
        name="inproj_even",
    )(x, mod_l, gain, w, gate_b, lb_logits)


def _shift_rows(x, k, fill, up):
    n = x.shape[0]
    if k % 8 == 0:
        pad = jnp.full((k,) + x.shape[1:], 0.0 if fill is None else fill, x.dtype)
        return jnp.concatenate([x[k:], pad], axis=0) if up else jnp.concatenate([pad, x[:n - k]], axis=0)
    y = pltpu.roll(x, (n - k) if up else k, 0)
    if fill is None:
        return y
    row = lax.broadcasted_iota(jnp.int32, x.shape, 0)
    return jnp.where(row >= n - k, fill, y) if up else jnp.where(row < k, fill, y)


def _cummax_rows(x, rev):
    k = 1
    while k < x.shape[0]:
        x = jnp.maximum(x, _shift_rows(x, k, -jnp.inf, up=rev))
        k *= 2
    return x


def _cumsum_rows(tri, x):
    hi = x.astype(BF16)
    r1 = x - hi.astype(F32)
    mid = r1.astype(BF16)
    lo = (r1 - mid.astype(F32)).astype(BF16)
    return _dot(tri, hi) + _dot(tri, mid) + _dot(tri, lo)


def _hgrn_tile(q, kk, f, a, q_b, rev, mask_sc, d, up_small):
    p = _dot_nt(q_b, kk.astype(BF16)) * mask_sc[d, len(TILE_LEVELS)]
    bm = a
    for li, m in enumerate(TILE_LEVELS):
        if m < 8:
            qrole = jnp.logical_not(up_small[li]) if rev else up_small[li]
            if m == 1:
                r = jnp.where(qrole, q * f, kk)
            else:
                x = jnp.where(qrole, _shift_rows(bm, m, None, up=rev), bm)
                r = jnp.where(qrole, q, kk) * jnp.exp(jnp.where(qrole, a - x, x - a))
            if 2 * m < 8:
                bm = jnp.where(qrole, bm, _shift_rows(bm, m, None, up=not rev))
        else:
            pieces = []
            for b0 in range(0, HALF, 2 * m):
                lo, up = slice(b0, b0 + m), slice(b0 + m, b0 + 2 * m)
                if rev:
                    ref = a[b0 + m:b0 + m + 1]
                    pieces += [q[lo] * jnp.exp(a[lo] - ref), kk[up] * jnp.exp(ref - a[up])]
                else:
                    ref = a[b0 + m - 1:b0 + m]
                    pieces += [kk[lo] * jnp.exp(ref - a[lo]), q[up] * jnp.exp(a[up] - ref)]
            r = jnp.concatenate(pieces, axis=0)
        rb = r.astype(BF16)
        p = p + _dot_nt(rb, rb) * mask_sc[d, li]
    return p


def _build_scan_constants(tri_sc, mask_sc):
    row = lax.broadcasted_iota(jnp.int32, (SEG, SEG), 0)
    col = lax.broadcasted_iota(jnp.int32, (SEG, SEG), 1)
    tri_sc[0] = (col <= row).astype(BF16)
    tri_sc[1] = (col >= row).astype(BF16)
    trow = lax.broadcasted_iota(jnp.int32, (HALF, HALF), 0)
    tcol = lax.broadcasted_iota(jnp.int32, (HALF, HALF), 1)
    for li, m in enumerate(TILE_LEVELS):
        sh = m.bit_length() - 1
        same = (trow >> (sh + 1)) == (tcol >> (sh + 1))
        t_up = ((trow >> sh) & 1) == 1
        s_up = ((tcol >> sh) & 1) == 1
        mask_sc[0, li] = (same & t_up & jnp.logical_not(s_up)).astype(F32)
        mask_sc[1, li] = (same & s_up & jnp.logical_not(t_up)).astype(F32)
    eye = (trow == tcol).astype(F32)
    mask_sc[0, len(TILE_LEVELS)] = eye
    mask_sc[1, len(TILE_LEVELS)] = eye


def _scan_units(dir_refs, write_h, tri_sc, mask_sc, m_in, state, emit, result):
    row = lax.broadcasted_iota(jnp.int32, (SEG, SEG), 0)
    col = lax.broadcasted_iota(jnp.int32, (SEG, SEG), 1)
    lane128 = lax.broadcasted_iota(jnp.int32, (SEG, HEAD), 1)
    e0 = (lane128 == 0).astype(BF16)
    row_half = lax.broadcasted_iota(jnp.int32, (HALF, HEAD), 0)
    up_masks = [((row_half >> (m.bit_length() - 1)) & 1) == 1 for m in TILE_LEVELS if m < 8]
    tmasks = (col <= row, col >= row)
    use_state = state is not None
    want_state = use_state or emit is not None
    m_out_rows = []

    for d in range(2):
        qkv_ref, gate_ref, hqi_ref, hgf_ref = dir_refs[d]
        rev = d == 1
        last = 0 if rev else SEG - 1
        tri = tri_sc[d]

        slab = gate_ref[...]
        b_al = pltpu.roll(_cumsum_rows(tri, slab), HEAD - 8, 1)
        u = slab - b_al
        mx = jnp.maximum(_cummax_rows(u, rev), m_in)
        w_inter = jnp.exp(m_in - mx)
        e_den = jnp.exp(-(b_al + mx))
        mx_last = mx[last:last + 1, :]
        m_out_rows.append(b_al[last:last + 1, :] + mx_last)
        decay = jnp.exp(m_in - mx_last)
        wg = jnp.exp(u - mx_last)
        u_t = u.T

        for h in range(ML_HEADS):
            c = 4 * d + h
            q = qkv_ref[:, h * HEAD:(h + 1) * HEAD]
            k = qkv_ref[:, 512 + h * HEAD:512 + (h + 1) * HEAD]
            v = qkv_ref[:, 1024 + h * HEAD:1024 + (h + 1) * HEAD]
            dm = jnp.where(tmasks[d], jnp.exp(u_t[c:c + 1, :] - mx[:, c:c + 1]), 0.0)
            s = (_dot_nt(q, k) * dm).astype(BF16)
            v_aug = jnp.concatenate([v, e0], axis=1)
            numden = _dot(s, v_aug)
            if use_state:
                numden = numden + w_inter[:, c:c + 1] * _dot(q, state[0][d, h].astype(BF16))
            den = jnp.maximum(jnp.abs(numden[:, HEAD:HEAD + 1]), e_den[:, c:c + 1])
            write_h(d, h * HEAD, numden[:, 0:HEAD] / den)
            if want_state:
                kw_t = (k.astype(F32) * wg[:, c:c + 1]).T.astype(BF16)
                upd = _dot(kw_t, v_aug)
                if use_state:
                    upd = upd + decay[:, c:c + 1] * state[0][d, h]
                    state[0][d, h] = upd
                if emit is not None:
                    emit("ml", d, h, upd)
            yield

        lf_all = hgf_ref[:, 512 * d:512 * (d + 1)]
        a_all = _cumsum_rows(tri, lf_all)
        f_all = jnp.exp(lf_all)
        kk_all = 1.0 - f_all
        for h in range(HG_HEADS):
            sl = slice(h * HEAD, (h + 1) * HEAD)
            a = a_all[:, sl]
            kk = kk_all[:, sl]
            f = f_all[:, sl]
            q_b = hqi_ref[:, h * HEAD:(h + 1) * HEAD]
            i_b = hqi_ref[:, 512 + h * HEAD:512 + (h + 1) * HEAD]
            q = q_b.astype(F32)
            qh, kh = (0, 1) if rev else (1, 0)
            a_q, a_k = a[qh * HALF:(qh + 1) * HALF], a[kh * HALF:(kh + 1) * HALF]
            a_mid = a[HALF:HALF + 1] if rev else a[HALF - 1:HALF]
            r_q = (q[qh * HALF:(qh + 1) * HALF] * jnp.exp(a_q - a_mid)).astype(BF16)
            r_k = (kk[kh * HALF:(kh + 1) * HALF] * jnp.exp(a_mid - a_k)).astype(BF16)
            p_cross = _dot_nt(r_q, r_k).astype(BF16)
            p_tiles = []
            for t in range(2):
                rows = slice(t * HALF, (t + 1) * HALF)
                p_tiles.append(_hgrn_tile(q[rows], kk[rows], f[rows], a[rows], q_b[rows], rev, mask_sc, d,
                                          up_masks).astype(BF16))
            i_lo, i_hi = i_b[0:HALF], i_b[HALF:SEG]
            if rev:
                o = jnp.concatenate([_dot(p_tiles[0], i_lo) + _dot(p_cross, i_hi), _dot(p_tiles[1], i_hi)], axis=0)
            else:
                o = jnp.concatenate([_dot(p_tiles[0], i_lo), _dot(p_cross, i_lo) + _dot(p_tiles[1], i_hi)], axis=0)
            if use_state:
                st = state[1][d, h]
                o = o + _dot_nt((q * jnp.exp(a)).astype(BF16), st.astype(BF16))
            write_h(d, 512 + h * HEAD, o)
            if want_state:
                a_l = a[last:last + 1, :]
                kd = (kk * jnp.exp(a_l - a)).astype(BF16)
                st_new = _dot(i_b.astype(F32).T.astype(BF16), kd)
                if use_state:
                    st_new = st_new + st * jnp.exp(a_l)
                    state[1][d, h] = st_new
                if emit is not None:
                    emit("hg", d, h, st_new)
            yield

    lane_row = lax.broadcasted_iota(jnp.int32, (1, HEAD), 1)
    result["m"] = jnp.where(lane_row < 4, m_out_rows[0], jnp.where(lane_row < 8, m_out_rows[1], 0.0))


def _emit_states(kind, d, h, value, c_ref, n_ref, s_ref):
    if kind == "ml":
        c_ref[d, h] = value[:, 0:HEAD]
        n_ref[d, h:h + 1, :] = value[:, HEAD:2 * HEAD].T[0:1, :]
    else:
        s_ref[d, h] = value.T


def _scan_kernel(*refs, has_init, emit_state, nseg):
    single = nseg == 1
    n_tok_in = 4 if single else 8
    n_h = 1 if single else 2
    n_in = n_tok_in + (4 if has_init else 0)
    n_out = n_h + (4 if emit_state else 0)
    ins, outs, scr = refs[:n_in], refs[n_in:n_in + n_out], refs[n_in + n_out:]
    dir_refs = (ins[0:4], ins[0:4] if single else ins[4:8])
    st_outs = outs[n_h:]
    tri_sc, mask_sc, caug_sc, st_sc, m_sc = scr
    use_state = has_init or nseg > 1
    b_id, j = pl.program_id(0), pl.program_id(1)

    @pl.when(jnp.logical_and(b_id == 0, j == 0))
    def _build_constants():
        _build_scan_constants(tri_sc, mask_sc)

    if use_state:
        @pl.when(j == 0)
        def _init_state():
            if has_init:
                c0_ref, n0_ref, m0_ref, s0_ref = ins[n_tok_in:n_tok_in + 4]
                for d in range(2):
                    for h in range(ML_HEADS):
                        caug_sc[d, h, :, 0:HEAD] = c0_ref[d, h]
                        caug_sc[d, h, :, HEAD:2 * HEAD] = jnp.broadcast_to(n0_ref[d, h:h + 1, :], (HEAD, HEAD)).T
                        st_sc[d, h] = s0_ref[d, h].T
                m_sc[...] = jnp.broadcast_to(m0_ref[...], m_sc.shape)
            else:
                caug_sc[...] = jnp.zeros(caug_sc.shape, F32)
                st_sc[...] = jnp.zeros(st_sc.shape, F32)
                m_sc[...] = jnp.zeros(m_sc.shape, F32)

    def write_h(d, c0, value):
        if single and d == 1:
            outs[0][:, c0:c0 + HEAD] += value
        else:
            outs[0 if single else d][:, c0:c0 + HEAD] = value

    def emit(kind, d, h, value):
        if single:
            _emit_states(kind, d, h, value, st_outs[0], st_outs[1], st_outs[3])
        else:
            pl.when(j == nseg - 1)(lambda: _emit_states(kind, d, h, value, st_outs[0], st_outs[1], st_outs[3]))

    m_in = m_sc[0:1, :] if use_state else jnp.zeros((1, HEAD), F32)
    result = {}
    for _ in _scan_units(dir_refs, write_h, tri_sc, mask_sc, m_in, (caug_sc, st_sc) if use_state else None,
                         emit if emit_state else None, result):
        pass
    m_new = result["m"]
    if use_state:
        m_sc[...] = jnp.broadcast_to(m_new, m_sc.shape)
    if emit_state:
        if single:
            st_outs[2][...] = m_new
        else:
            @pl.when(j == nseg - 1)
            def _emit_m():
                st_outs[2][...] = m_new


def _scan(qkv, gate, hqi, hgf, n_seq, nseg, init=None, emit_state=False):
    n_tok = qkv.shape[0]
    fwd = lambda n: pl.BlockSpec((SEG, n), lambda b, j: (b * nseg + j, 0))
    bwd = lambda n: pl.BlockSpec((SEG, n), lambda b, j: (b * nseg + nseg - 1 - j, 0))
    widths = (1536, 128, 1024, 1024)
    single = nseg == 1
    in_specs = [fwd(n) for n in widths] + ([] if single else [bwd(n) for n in widths])
    args = [qkv, gate, hqi, hgf] + ([] if single else [qkv, gate, hqi, hgf])
    mat = pl.BlockSpec((None, 2, 4, HEAD, HEAD), lambda b, j: (b, 0, 0, 0, 0))
    vec = pl.BlockSpec((None, 2, 4, HEAD), lambda b, j: (b, 0, 0, 0))
    sca = pl.BlockSpec((None, 1, HEAD), lambda b, j: (b, 0, 0))
    if init is not None:
        in_specs += [mat, vec, sca, mat]
        args += list(init)
    out_specs = [fwd(1024)] if single else [fwd(1024), bwd(1024)]
    out_shape = [jax.ShapeDtypeStruct((n_tok, 1024), F32)] * len(out_specs)
    if emit_state:
        out_specs += [mat, vec, sca, mat]
        out_shape += [jax.ShapeDtypeStruct((n_seq, 2, 4, HEAD, HEAD), F32),
                      jax.ShapeDtypeStruct((n_seq, 2, 4, HEAD), F32),
                      jax.ShapeDtypeStruct((n_seq, 1, HEAD), F32),
                      jax.ShapeDtypeStruct((n_seq, 2, 4, HEAD, HEAD), F32)]
    return pl.pallas_call(
        functools.partial(_scan_kernel, has_init=init is not None, emit_state=emit_state, nseg=nseg),
        grid=(n_seq, nseg),
        in_specs=in_specs,
        out_specs=out_specs,
        out_shape=out_shape,
        scratch_shapes=[pltpu.VMEM((2, SEG, SEG), BF16),
                        pltpu.VMEM((2, len(TILE_LEVELS) + 1, HALF, HALF), F32),
                        pltpu.VMEM((2, 4, HEAD, 2 * HEAD), F32),
                        pltpu.VMEM((2, 4, HEAD, HEAD), F32),
                        pltpu.VMEM((8, HEAD), F32)],
        compiler_params=_cparams(2),
        name="bidir_scan",
    )(*args)


def _mix_ffn_pieces(h, og, x, mod, norm_gain, wo_ref, ffn_gain, w1_ref, w3_ref, w2_ref, final_gain, o_ref):
    parts = []
    for g in range(D // HEAD):
        hs = h[:, g * HEAD:(g + 1) * HEAD]
        parts.append(hs * lax.rsqrt(jnp.mean(hs * hs, axis=-1, keepdims=True) + EPS))
    hn = jnp.concatenate(parts, axis=1) * norm_gain
    if og is not None:
        hn = hn * jnp.concatenate([_sigmoid(og[:, 0:512]), _silu(og[:, 512:1024])], axis=1)
    x = x + mod[2:3, :] * _dot(hn.astype(BF16), wo_ref[...])
    yield
    hf = _norm_mod(x, ffn_gain, mod, 1).astype(BF16)
    acc = jnp.zeros(x.shape, F32)
    for c0 in range(0, D_FF, FFN_CHUNK):
        a = _dot(hf, w1_ref[:, c0:c0 + FFN_CHUNK])
        b = _dot(hf, w3_ref[:, c0:c0 + FFN_CHUNK])
        acc = acc + _dot((_silu(a) * b).astype(BF16), w2_ref[c0:c0 + FFN_CHUNK, :])
        yield
    y = x + mod[5:6, :] * acc
    if final_gain is not None:
        y = y * lax.rsqrt(jnp.mean(y * y, axis=-1, keepdims=True) + EPS) * final_gain
    o_ref[...] = y


def _mix_ffn_kernel(*refs, n_h, gated, final):
    h_refs = refs[:n_h]
    pos = n_h
    og_ref = refs[pos] if gated else None
    pos += 1 if gated else 0
    x_ref, mod_ref, ng_ref, wo_ref, g_ref, w1_ref, w3_ref, w2_ref = refs[pos:pos + 8]
    fg_ref = refs[pos + 8] if final else None
    o_ref = refs[-1]
    h = h_refs[0][...]
    for r in h_refs[1:]:
        h = h + r[...]
    for _ in _mix_ffn_pieces(h, og_ref[...] if gated else None, x_ref[...], mod_ref[...], ng_ref[...], wo_ref,
                             g_ref[...], w1_ref, w3_ref, w2_ref, fg_ref[...] if final else None, o_ref):
        pass


def _mix_ffn(hs, og, x, mod_l, row_of_tile, norm_gain, w_out, ffn_gain, layer, w1, w3, w2, final_gain=None):
    n_tok = x.shape[0]
    tm = TOKEN_TILE
    tok = pl.BlockSpec((tm, D), lambda i: (i, 0))
    full = lambda a: pl.BlockSpec(a.shape, lambda i: (0,) * a.ndim, pipeline_mode=pl.Buffered(1))
    of_layer = lambda a: pl.BlockSpec((None,) + a.shape[1:], lambda i: (layer, 0, 0), pipeline_mode=pl.Buffered(1))
    gated = og is not None
    final = final_gain is not None
    consts = [norm_gain, w_out, ffn_gain, w1, w3, w2] + ([final_gain] if final else [])
    args = list(hs) + ([og] if gated else []) + [x, mod_l] + consts
    in_specs = [tok] * (len(hs) + (1 if gated else 0) + 1)
    in_specs += [pl.BlockSpec((None, 6, D), lambda i: (row_of_tile(i), 0, 0))]
    in_specs += [of_layer(a) if a.ndim == 3 else full(a) for a in consts]
    return pl.pallas_call(
        functools.partial(_mix_ffn_kernel, n_h=len(hs), gated=gated, final=final),
        grid=(n_tok // tm,),
        in_specs=in_specs,
        out_specs=tok,
        out_shape=jax.ShapeDtypeStruct((n_tok, D), F32),
        compiler_params=_cparams(1),
        name="mix_ffn",
    )(*args)


def _rope(x, cos, sin_signed):
    lane = lax.broadcasted_iota(jnp.int32, (x.shape[0], HEAD), 1)
    first = (lane & 16) == 0
    parts = []
    for h in range(DA_HEADS):
        xh = x[:, h * HEAD:(h + 1) * HEAD]
        partner = jnp.where(first, pltpu.roll(xh, HEAD - 16, 1), pltpu.roll(xh, 16, 1))
        parts.append(xh * cos + partner * sin_signed)
    return jnp.concatenate(parts, axis=1)


def _inproj_odd_kernel(*refs, rope):
    x_ref, mod_ref, g_ref, w_ref = refs[:4]
    q_ref, k_ref, v_ref = refs[-3:]
    h = _norm_mod(x_ref[...], g_ref[...], mod_ref[...], 0).astype(BF16)
    q = _dot(h, w_ref[:, 0:D])
    k = _dot(h, w_ref[:, D:2 * D])
    v = _dot(h, w_ref[:, 2 * D:3 * D])
    if rope:
        cos, sin_signed = refs[4][...], refs[5][...]
        q = _rope(q, cos, sin_signed)
        k = _rope(k, cos, sin_signed)
    q_ref[...] = (q * (DA_DQK ** -0.5 * LOG2E)).astype(q_ref.dtype)
    k_ref[...] = k.astype(k_ref.dtype)
    v_ref[...] = v.astype(v_ref.dtype)


def _inproj_odd(x, mod_l, row_of_tile, gain, w, rope_tables, kv_dtype):
    n_tok = x.shape[0]
    tm = TOKEN_TILE
    tok = pl.BlockSpec((tm, D), lambda i: (i, 0))
    full = lambda a: pl.BlockSpec(a.shape, lambda i: (0,) * a.ndim)
    args = [x, mod_l, gain, w]
    in_specs = [tok, pl.BlockSpec((None, 6, D), lambda i: (row_of_tile(i), 0, 0)), full(gain), full(w)]
    if rope_tables is not None:
        tiles_per_seq = rope_tables[0].shape[0] // tm
        args += list(rope_tables)
        in_specs += [pl.BlockSpec((tm, HEAD), lambda i: (i % tiles_per_seq, 0))] * 2
    return pl.pallas_call(
        functools.partial(_inproj_odd_kernel, rope=rope_tables is not None),
        grid=(n_tok // tm,),
        in_specs=in_specs,
        out_specs=[tok, tok, tok],
        out_shape=[jax.ShapeDtypeStruct((n_tok, D), BF16), jax.ShapeDtypeStruct((n_tok, D), kv_dtype),
                   jax.ShapeDtypeStruct((n_tok, D), kv_dtype)],
        compiler_params=_cparams(1),
        name="inproj_odd",
    )(*args)


def _rope_tables(n_tok):
    quarter = DA_DQK // 4
    tok = np.arange(n_tok)
    pos = np.stack([tok // GRID_W, tok % GRID_W], axis=1).astype(np.float32)
    inv = (np.float32(ROPE_BASE) ** (-np.arange(quarter, dtype=np.float32) / np.float32(quarter))).astype(np.float32)
    lane = np.arange(HEAD)
    ang = (pos[:, (lane // 32) % 2] * inv[lane % quarter][None, :]).astype(np.float32)
    sign = np.where((lane % 32) < quarter, -1.0, 1.0).astype(np.float32)
    return jnp.asarray(np.cos(ang), F32), jnp.asarray(np.sin(ang) * sign[None, :], F32)


def _lambda(lam_ref, lam_init):
    lp = lam_ref[...]
    return (jnp.exp(jnp.sum(lp[0:1] * lp[1:2], axis=-1, keepdims=True))
            - jnp.exp(jnp.sum(lp[2:3] * lp[3:4], axis=-1, keepdims=True)) + lam_init)


VT_ROWS = HEAD + 16


def _qt2(qh):
    qt = qh.astype(F32).T
    row = lax.broadcasted_iota(jnp.int32, qt.shape, 0)
    return jnp.concatenate([jnp.where(row < DA_DQK, qt, 0.0), jnp.where(row >= DA_DQK, qt, 0.0)],
                           axis=1).astype(BF16)


def _vt_aug(vh):
    tk = vh.shape[0]
    row = lax.broadcasted_iota(jnp.int32, (VT_ROWS - HEAD, tk), 0)
    return jnp.concatenate([vh.astype(F32).T, (row == 0).astype(F32)], axis=0).astype(BF16)


def _diff_attn_heads(q_ref, k_fn, vt_fn, n_chunks, lam, o_ref):
    tq = q_ref.shape[0]
    items = [(h, c) for h in range(DA_HEADS) for c in range(n_chunks)]
    qt2, state, pending = {}, {}, []

    def finish(h, c, st):
        cm = jnp.max(st, axis=0, keepdims=True)
        if c == 0:
            m_new = cm
            acc = _dot(vt_fn(h, c), jnp.exp2(st - m_new).astype(BF16))
        else:
            m, acc = state[h]
            m_new = jnp.maximum(m, cm)
            acc = acc * jnp.exp2(m - m_new) + _dot(vt_fn(h, c), jnp.exp2(st - m_new).astype(BF16))
        state[h] = (m_new, acc)
        if c == n_chunks - 1:
            den = acc[HEAD:HEAD + 1, :]
            o_t = acc[0:HEAD, 0:tq] * (1.0 / den[:, 0:tq]) - acc[0:HEAD, tq:] * (lam / den[:, tq:])
            o_ref[:, h * HEAD:(h + 1) * HEAD] = o_t.T

    for h, c in items:
        if c == 0:
            qt2[h] = _qt2(q_ref[:, h * HEAD:(h + 1) * HEAD])
        pending.append((h, c, _dot(k_fn(h, c), qt2[h])))
        if len(pending) > ATTN_LOOKAHEAD:
            finish(*pending.pop(0))
    while pending:
        finish(*pending.pop(0))


def _attn_prompt_kernel(q_ref, k_ref, v_ref, lam_ref, o_ref, *, lam_init):
    lam = _lambda(lam_ref, lam_init)
    _diff_attn_heads(q_ref, lambda h, c: k_ref[:, h * HEAD:(h + 1) * HEAD].astype(BF16),
                     lambda h, c: _vt_aug(v_ref[:, h * HEAD:(h + 1) * HEAD]), 1, lam, o_ref)


def _attn_prompt(q, k, v, lam_p, seq, lam_init):
    n_tok = q.shape[0]
    tok = pl.BlockSpec((seq, D), lambda b: (b, 0))
    return pl.pallas_call(
        functools.partial(_attn_prompt_kernel, lam_init=lam_init),
        grid=(n_tok // seq,),
        in_specs=[tok, tok, tok, pl.BlockSpec(lam_p.shape, lambda b: (0, 0))],
        out_specs=tok,
        out_shape=jax.ShapeDtypeStruct((n_tok, D), F32),
        compiler_params=_cparams(1),
        name="diff_attn_prompt",
    )(q, k, v, lam_p)


def _attn_sample_kernel(q_ref, k_ref, v_ref, ck_ref, cv_ref, lam_ref, o_ref, kcat, vtcat, *, lam_init, past):
    n_keys = kcat.shape[0]
    chunks = [(c0, min(KEY_CHUNK, n_keys - c0)) for c0 in range(0, n_keys, KEY_CHUNK)]

    @pl.when(pl.program_id(1) == 0)
    def _gather_keys():
        kcat[0:past, :] = ck_ref[...].astype(BF16)
        kcat[past:, :] = k_ref[...]
        for h in range(DA_HEADS):
            sl = slice(h * HEAD, (h + 1) * HEAD)
            for c0, n in chunks:
                src, r0 = (cv_ref, c0) if c0 < past else (v_ref, c0 - past)
                vtcat[h, :, c0:c0 + n] = _vt_aug(src[r0:r0 + n, sl])

    lam = _lambda(lam_ref, lam_init)
    _diff_attn_heads(q_ref, lambda h, c: kcat[chunks[c][0]:chunks[c][0] + chunks[c][1], h * HEAD:(h + 1) * HEAD],
                     lambda h, c: vtcat[h, :, chunks[c][0]:chunks[c][0] + chunks[c][1]], len(chunks), lam, o_ref)


def _attn_sample(q, k, v, ck, cv, lam_p, n_seq, seq, lam_init, tq=128):
    past = ck.shape[1]
    qb = pl.BlockSpec((tq, D), lambda b, i: (b * (seq // tq) + i, 0))
    kv = pl.BlockSpec((seq, D), lambda b, i: (b, 0))
    cache = pl.BlockSpec((None, past, D), lambda b, i: (b, 0, 0))
    return pl.pallas_call(
        functools.partial(_attn_sample_kernel, lam_init=lam_init, past=past),
        grid=(n_seq, seq // tq),
        in_specs=[qb, kv, kv, cache, cache, pl.BlockSpec(lam_p.shape, lambda b, i: (0, 0))],
        out_specs=qb,
        out_shape=jax.ShapeDtypeStruct((n_seq * seq, D), F32),
        scratch_shapes=[pltpu.VMEM((past + seq, D), BF16), pltpu.VMEM((DA_HEADS, VT_ROWS, past + seq), BF16)],
        compiler_params=_cparams(2),
        name="diff_attn_sample",
    )(q, k, v, ck, cv, lam_p)


def kernel(x_prompt, x_sample, c, c_ctx, cache_attn_k, cache_attn_v, state_mlstm_C, state_mlstm_n, state_mlstm_m,
           state_hgrn_S, ada_w, ada_b, norm_mix_g, norm_ffn_g, ev_w_in, ev_gate_b, ev_lb_logits, ml_norm_g,
           hg_norm_g, ev_w_out, od_w_in, od_lambda, da_norm_g, od_w_out, ffn_w1, ffn_w3, ffn_w2, final_norm_g):
    assert DEPTH % 2 == 0
    n_p, s_p, _ = x_prompt.shape
    n_s, s_s, _ = x_sample.shape
    past = cache_attn_k.shape[2]
    assert s_p == SEG and s_s % SEG == 0 and s_s % TOKEN_TILE == 0 and (n_p * s_p) % TOKEN_TILE == 0
    xp = x_prompt.astype(F32).reshape(n_p * s_p, D)
    xs = x_sample.astype(F32).reshape(n_s * s_s, D)

    cond8 = jnp.zeros((8, D), F32).at[0].set(c_ctx.astype(F32)).at[1:1 + n_s].set(c.astype(F32))
    mod = _modulation(cond8, ada_w.astype(F32), ada_b.astype(F32)).reshape(DEPTH, 8, 6, D)
    row_p = lambda i: 0
    row_s = lambda i: 1 + i // (s_s // TOKEN_TILE)
    streams = [(xp, row_p), (xs, row_s)]
    outputs = {}
    w1_all, w3_all, w2_all = ffn_w1.astype(BF16), ffn_w3.astype(BF16), ffn_w2.astype(BF16)

    for l in range(DEPTH):
        mix_gain = norm_mix_g[l].astype(F32).reshape(1, D)
        ffn_gain = norm_ffn_g[l].astype(F32).reshape(1, D)
        if l % 2 == 0:
            e = l // 2
            g0 = sum(EV_SIZES[:4])
            w_e = ev_w_in[e].astype(BF16)
            w_in = jnp.concatenate([w_e[:, :g0], w_e[:, g0 + 16:],
                                    jnp.pad(w_e[:, g0:g0 + 16], ((0, 0), (0, HEAD - 16)))], axis=1)
            gate_b = jnp.pad(ev_gate_b[e].astype(F32), (0, HEAD - 16)).reshape(1, HEAD)
            norm_gain = jnp.concatenate([ml_norm_g[e], hg_norm_g[e]]).astype(F32).reshape(1, D)
            w_out = ev_w_out[e].astype(BF16)
            mixed = []
            for si, (x, row_of) in enumerate(streams):
                qkv, gate, hqi, hgf, og = _inproj_even(x, mod[l], row_of, mix_gain, w_in, gate_b,
                                                       ev_lb_logits.astype(F32), e)
                if si == 0:
                    *hs, c_new, n_new, m_new, s_new = _scan(qkv, gate, hqi, hgf, n_p, s_p // SEG, emit_state=True)
                    outputs.setdefault("C", []).append(c_new)
                    outputs.setdefault("n", []).append(n_new)
                    outputs.setdefault("m", []).append(m_new[:, 0, 0:8].reshape(n_p, 2, 4))
                    outputs.setdefault("S", []).append(s_new)
                else:
                    init = (state_mlstm_C[:, e].astype(F32), state_mlstm_n[:, e].astype(F32),
                            jnp.pad(state_mlstm_m[:, e].astype(F32).reshape(n_s, 1, 8), ((0, 0), (0, 0), (0, HEAD - 8))),
                            state_hgrn_S[:, e].astype(F32))
                    hs = _scan(qkv, gate, hqi, hgf, n_s, s_s // SEG, init=init)
                mixed.append((list(hs), og, x))
        else:
            o = l // 2
            lam_init = 0.8 - 0.6 * math.exp(-0.3 * l)
            w_in = od_w_in[o].astype(BF16)
            norm_gain = (jnp.tile(da_norm_g[o].astype(F32), DA_HEADS) * (1.0 - lam_init)).reshape(1, D)
            w_out = od_w_out[o].astype(BF16)
            lam_p = od_lambda[o].astype(F32)
            mixed = []
            for si, (x, row_of) in enumerate(streams):
                if si == 0:
                    q, k, v = _inproj_odd(x, mod[l], row_of, mix_gain, w_in, None, F32)
                    outputs.setdefault("k", []).append(k.reshape(n_p, s_p, DA_HEADS, HEAD))
                    outputs.setdefault("v", []).append(v.reshape(n_p, s_p, DA_HEADS, HEAD))
                    att = _attn_prompt(q, k, v, lam_p, s_p, lam_init)
                else:
                    q, k, v = _inproj_odd(x, mod[l], row_of, mix_gain, w_in, _rope_tables(s_s), BF16)
                    ck = cache_attn_k[:, o].reshape(n_s, past, D)
                    cv = cache_attn_v[:, o].reshape(n_s, past, D)
                    att = _attn_sample(q, k, v, ck, cv, lam_p, n_s, s_s, lam_init)
                mixed.append(([att], None, x))
        final_gain = final_norm_g.astype(F32).reshape(1, D) if l == DEPTH - 1 else None
        streams = [(_mix_ffn(hs, og, x, mod[l], row_of, norm_gain, w_out, ffn_gain, l, w1_all, w3_all, w2_all,
                             final_gain), row_of)
                   for (hs, og, x), (_, row_of) in zip(mixed, streams)]

    y_prompt = streams[0][0].reshape(n_p, s_p, D)
    y_sample = streams[1][0].reshape(n_s, s_s, D)
    stack = lambda name: outputs[name][0][:, None] if len(outputs[name]) == 1 else jnp.stack(outputs[name], axis=1)
    return (y_prompt, y_sample, stack("k"), stack("v"), stack("C"), stack("n"), stack("m"), stack("S"))
```

```python
import functools
import math

import jax
import jax.numpy as jnp
import numpy as np
from jax import lax
from jax.experimental import pallas as pl
from jax.experimental.pallas import tpu as pltpu

F32 = jnp.float32
BF16 = jnp.bfloat16

D = 1024
DEPTH = 2
GRID_W = 64
ML_HEADS = 4
HG_HEADS = 4
HEAD = 128
DA_HEADS = 8
DA_DQK = 64
ROPE_BASE = 10000.0
LOG2E = math.log2(math.e)
EPS = 1e-6
D_FF = ((8 * D // 3 + 255) // 256) * 256
EV_SIZES = (512, 512, 512, 512, 16, 512, 512, 512, 512, 512)
EV_COLS = 9 * 512 + 128

SEG = 256
HALF = SEG // 2
TILE_LEVELS = (1, 2, 4, 8, 16, 32, 64)
TOKEN_TILE = 512
PROJ_TILE = 1024
FFN_CHUNK = 256
KEY_CHUNK = 256
ATTN_LOOKAHEAD = 6
VMEM_LIMIT = 56 * 1024 * 1024


def _cparams(n_axes):
    return pltpu.CompilerParams(dimension_semantics=("arbitrary",) * n_axes, vmem_limit_bytes=VMEM_LIMIT)


def _sigmoid(x):
    return 1.0 / (1.0 + jnp.exp(-x))


def _silu(x):
    return x * _sigmoid(x)


def _log_sigmoid(x):
    return jnp.minimum(x, 0.0) - jnp.log(1.0 + jnp.exp(-jnp.abs(x)))


def _dot(a, b):
    return jnp.dot(a, b, preferred_element_type=F32)


def _dot_nt(a, b):
    return lax.dot_general(a, b, (((1,), (1,)), ((), ())), preferred_element_type=F32)


def _norm_mod(x, gain, mod, k):
    ms = jnp.mean(x * x, axis=-1, keepdims=True)
    return x * lax.rsqrt(ms + EPS) * gain * (1.0 + mod[3 * k + 1:3 * k + 2]) + mod[3 * k:3 * k + 1]


def _mod_row(cond_tokens, tm):
    if cond_tokens is None:
        return lambda i: 0
    return lambda i: 1 + i // (cond_tokens // tm)


def _mod_kernel(c_ref, w_ref, b_ref, o_ref):
    s = _silu(c_ref[...]).astype(BF16)
    o_ref[...] = _dot(s, w_ref[...].astype(BF16)) + b_ref[...]


def _modulation(cond8, ada_w, ada_b):
    n_layers = ada_w.shape[0]
    tn = 1536
    return pl.pallas_call(
        _mod_kernel,
        grid=(n_layers, 6 * D // tn),
        in_specs=[pl.BlockSpec((8, D), lambda l, n: (0, 0)),
                  pl.BlockSpec((None, D, tn), lambda l, n: (l, 0, n)),
                  pl.BlockSpec((None, 1, tn), lambda l, n: (l, 0, n))],
        out_specs=pl.BlockSpec((None, 8, tn), lambda l, n: (l, 0, n)),
        out_shape=jax.ShapeDtypeStruct((n_layers, 8, 6 * D), F32),
        compiler_params=_cparams(2),
        name="ada_modulation",
    )(cond8, ada_w, ada_b.reshape(n_layers, 1, 6 * D))


def _inproj_even_body(x, mod, gain, w_ref, gate_b, lb_logits, e_idx, qkv_ref, gate_ref, hqi_ref, hgf_ref, og_ref):
    h = _norm_mod(x, gain, mod, 0).astype(BF16)

    def proj(c0, n):
        return _dot(h, w_ref[:, c0:c0 + n])

    qkv_ref[:, 0:512] = (proj(0, 512) * (HEAD ** -0.5)).astype(BF16)
    qkv_ref[:, 512:1536] = proj(512, 1024).astype(BF16)
    og_ref[:, 0:512] = proj(1536, 512)
    hqi_ref[:, 0:512] = proj(2048, 512).astype(BF16)
    hqi_ref[:, 512:1024] = proj(3584, 512).astype(BF16)
    mx = jnp.max(lb_logits, axis=0, keepdims=True)
    ex = jnp.exp(lb_logits - mx)
    lb = jnp.sum(ex[0:e_idx + 1], axis=0, keepdims=True) / jnp.sum(ex, axis=0, keepdims=True)
    hgf_ref[:, 0:512] = jnp.log(lb + (1.0 - lb) * _sigmoid(proj(2560, 512)))
    hgf_ref[:, 512:1024] = jnp.log(lb + (1.0 - lb) * _sigmoid(proj(3072, 512)))
    og_ref[:, 512:1024] = proj(4096, 512)
    gt = proj(4608, 128) + gate_b
    lane = lax.broadcasted_iota(jnp.int32, gt.shape, 1)
    gate_ref[...] = jnp.where(lane < 8, gt, jnp.where(lane < 16, _log_sigmoid(gt), 0.0))


def _inproj_even_kernel(x_ref, mod_ref, g_ref, w_ref, gb_ref, lbl_ref, qkv_ref, gate_ref, hqi_ref, hgf_ref,
                        og_ref, *, e_idx):
    _inproj_even_body(x_ref[...], mod_ref[...], g_ref[...], w_ref, gb_ref[...], lbl_ref[...], e_idx,
                      qkv_ref, gate_ref, hqi_ref, hgf_ref, og_ref)


def _inproj_even(x, mod_l, cond_tokens, gain, w, gate_b, lb_logits, e_idx):
    n_tok = x.shape[0]
    tm = PROJ_TILE
    tile_mod_row = _mod_row(cond_tokens, tm)
    tok = lambda n: pl.BlockSpec((tm, n), lambda i: (i, 0))
    full = lambda a: pl.BlockSpec(a.shape, lambda i: (0,) * a.ndim, pipeline_mode=pl.Buffered(1))
    return pl.pallas_call(
        functools.partial(_inproj_even_kernel, e_idx=e_idx),
        grid=(n_tok // tm,),
        in_specs=[tok(D), pl.BlockSpec((None, 6, D), lambda i: (tile_mod_row(i), 0, 0)),
                  full(gain), full(w), full(gate_b), full(lb_logits)],
        out_specs=[tok(1536), tok(128), tok(1024), tok(1024), tok(1024)],
        out_shape=[jax.ShapeDtypeStruct((n_tok, 1536), BF16), jax.ShapeDtypeStruct((n_tok, 128), F32),
                   jax.ShapeDtypeStruct((n_tok, 1024), BF16), jax.ShapeDtypeStruct((n_tok, 1024), F32),
                   jax.ShapeDtypeStruct((n_tok, 1024), F32)],
        compiler_params=_cparams(1),
        name="inproj_even",
    )(x, mod_l, gain, w, gate_b, lb_logits)


def _shift_rows(x, k, fill, up):
    n = x.shape[0]
    if k % 8 == 0:
        pad = jnp.full((k,) + x.shape[1:], 0.0 if fill is None else fill, x.dtype)
        return jnp.concatenate([x[k:], pad], axis=0) if up else jnp.concatenate([pad, x[:n - k]], axis=0)
    y = pltpu.roll(x, (n - k) if up else k, 0)
    if fill is None:
        return y
    row = lax.broadcasted_iota(jnp.int32, x.shape, 0)
    return jnp.where(row >= n - k, fill, y) if up else jnp.where(row < k, fill, y)


def _cummax_rows(x, rev):
    k = 1
    while k < x.shape[0]:
        x = jnp.maximum(x, _shift_rows(x, k, -jnp.inf, up=rev))
        k *= 2
    return x


def _cumsum_rows(tri, x):
    hi = x.astype(BF16)
    r1 = x - hi.astype(F32)
    mid = r1.astype(BF16)
    lo = (r1 - mid.astype(F32)).astype(BF16)
    return _dot(tri, hi) + _dot(tri, mid) + _dot(tri, lo)


def _hgrn_tile(q, kk, f, a, q_b, rev, mask_sc, d, up_small):
    p = _dot_nt(q_b, kk.astype(BF16)) * mask_sc[d, len(TILE_LEVELS)]
    bm = a
    for li, m in enumerate(TILE_LEVELS):
        if m < 8:
            qrole = jnp.logical_not(up_small[li]) if rev else up_small[li]
            if m == 1:
                r = jnp.where(qrole, q * f, kk)
            else:
                x = jnp.where(qrole, _shift_rows(bm, m, None, up=rev), bm)
                r = jnp.where(qrole, q, kk) * jnp.exp(jnp.where(qrole, a - x, x - a))
            if 2 * m < 8:
                bm = jnp.where(qrole, bm, _shift_rows(bm, m, None, up=not rev))
        else:
            pieces = []
            for b0 in range(0, HALF, 2 * m):
                lo, up = slice(b0, b0 + m), slice(b0 + m, b0 + 2 * m)
                if rev:
                    ref = a[b0 + m:b0 + m + 1]
                    pieces += [q[lo] * jnp.exp(a[lo] - ref), kk[up] * jnp.exp(ref - a[up])]
                else:
                    ref = a[b0 + m - 1:b0 + m]
                    pieces += [kk[lo] * jnp.exp(ref - a[lo]), q[up] * jnp.exp(a[up] - ref)]
            r = jnp.concatenate(pieces, axis=0)
        rb = r.astype(BF16)
        p = p + _dot_nt(rb, rb) * mask_sc[d, li]
    return p


def _build_scan_constants(tri_sc, mask_sc):
    row = lax.broadcasted_iota(jnp.int32, (SEG, SEG), 0)
    col = lax.broadcasted_iota(jnp.int32, (SEG, SEG), 1)
    tri_sc[0] = (col <= row).astype(BF16)
    tri_sc[1] = (col >= row).astype(BF16)
    trow = lax.broadcasted_iota(jnp.int32, (HALF, HALF), 0)
    tcol = lax.broadcasted_iota(jnp.int32, (HALF, HALF), 1)
    for li, m in enumerate(TILE_LEVELS):
        sh = m.bit_length() - 1
        same = (trow >> (sh + 1)) == (tcol >> (sh + 1))
        t_up = ((trow >> sh) & 1) == 1
        s_up = ((tcol >> sh) & 1) == 1
        mask_sc[0, li] = (same & t_up & jnp.logical_not(s_up)).astype(F32)
        mask_sc[1, li] = (same & s_up & jnp.logical_not(t_up)).astype(F32)
    eye = (trow == tcol).astype(F32)
    mask_sc[0, len(TILE_LEVELS)] = eye
    mask_sc[1, len(TILE_LEVELS)] = eye


def _scan_units(dir_refs, write_h, tri_sc, mask_sc, m_in, state, emit, result):
    row = lax.broadcasted_iota(jnp.int32, (SEG, SEG), 0)
    col = lax.broadcasted_iota(jnp.int32, (SEG, SEG), 1)
    lane128 = lax.broadcasted_iota(jnp.int32, (SEG, HEAD), 1)
    e0 = (lane128 == 0).astype(BF16)
    row_half = lax.broadcasted_iota(jnp.int32, (HALF, HEAD), 0)
    up_masks = [((row_half >> (m.bit_length() - 1)) & 1) == 1 for m in TILE_LEVELS if m < 8]
    tmasks = (col <= row, col >= row)
    use_state = state is not None
    want_state = use_state or emit is not None
    m_out_rows = []

    for d in range(2):
        qkv_ref, gate_ref, hqi_ref, hgf_ref = dir_refs[d]
        rev = d == 1
        last = 0 if rev else SEG - 1
        tri = tri_sc[d]

        slab = gate_ref[...]
        b_al = pltpu.roll(_cumsum_rows(tri, slab), HEAD - 8, 1)
        u = slab - b_al
        mx = jnp.maximum(_cummax_rows(u, rev), m_in)
        w_inter = jnp.exp(m_in - mx)
        e_den = jnp.exp(-(b_al + mx))
        mx_last = mx[last:last + 1, :]
        m_out_rows.append(b_al[last:last + 1, :] + mx_last)
        decay = jnp.exp(m_in - mx_last)
        wg = jnp.exp(u - mx_last)
        u_t = u.T

        for h in range(ML_HEADS):
            c = 4 * d + h
            q = qkv_ref[:, h * HEAD:(h + 1) * HEAD]
            k = qkv_ref[:, 512 + h * HEAD:512 + (h + 1) * HEAD]
            v = qkv_ref[:, 1024 + h * HEAD:1024 + (h + 1) * HEAD]
            dm = jnp.where(tmasks[d], jnp.exp(u_t[c:c + 1, :] - mx[:, c:c + 1]), 0.0)
            s = (_dot_nt(q, k) * dm).astype(BF16)
            v_aug = jnp.concatenate([v, e0], axis=1)
            numden = _dot(s, v_aug)
            if use_state:
                numden = numden + w_inter[:, c:c + 1] * _dot(q, state[0][d, h].astype(BF16))
            den = jnp.maximum(jnp.abs(numden[:, HEAD:HEAD + 1]), e_den[:, c:c + 1])
            write_h(d, h * HEAD, numden[:, 0:HEAD] / den)
            if want_state:
                kw_t = (k.astype(F32) * wg[:, c:c + 1]).T.astype(BF16)
                upd = _dot(kw_t, v_aug)
                if use_state:
                    upd = upd + decay[:, c:c + 1] * state[0][d, h]
                    state[0][d, h] = upd
                if emit is not None:
                    emit("ml", d, h, upd)
            yield

        lf_all = hgf_ref[:, 512 * d:512 * (d + 1)]
        a_all = _cumsum_rows(tri, lf_all)
        f_all = jnp.exp(lf_all)
        kk_all = 1.0 - f_all
        for h in range(HG_HEADS):
            sl = slice(h * HEAD, (h + 1) * HEAD)
            a = a_all[:, sl]
            kk = kk_all[:, sl]
            f = f_all[:, sl]
            q_b = hqi_ref[:, h * HEAD:(h + 1) * HEAD]
            i_b = hqi_ref[:, 512 + h * HEAD:512 + (h + 1) * HEAD]
            q = q_b.astype(F32)
            qh, kh = (0, 1) if rev else (1, 0)
            a_q, a_k = a[qh * HALF:(qh + 1) * HALF], a[kh * HALF:(kh + 1) * HALF]
            a_mid = a[HALF:HALF + 1] if rev else a[HALF - 1:HALF]
            r_q = (q[qh * HALF:(qh + 1) * HALF] * jnp.exp(a_q - a_mid)).astype(BF16)
            r_k = (kk[kh * HALF:(kh + 1) * HALF] * jnp.exp(a_mid - a_k)).astype(BF16)
            p_cross = _dot_nt(r_q, r_k).astype(BF16)
            p_tiles = []
            for t in range(2):
                rows = slice(t * HALF, (t + 1) * HALF)
                p_tiles.append(_hgrn_tile(q[rows], kk[rows], f[rows], a[rows], q_b[rows], rev, mask_sc, d,
                                          up_masks).astype(BF16))
            i_lo, i_hi = i_b[0:HALF], i_b[HALF:SEG]
            if rev:
                o = jnp.concatenate([_dot(p_tiles[0], i_lo) + _dot(p_cross, i_hi), _dot(p_tiles[1], i_hi)], axis=0)
            else:
                o = jnp.concatenate([_dot(p_tiles[0], i_lo), _dot(p_cross, i_lo) + _dot(p_tiles[1], i_hi)], axis=0)
            if use_state:
                st = state[1][d, h]
                o = o + _dot_nt((q * jnp.exp(a)).astype(BF16), st.astype(BF16))
            write_h(d, 512 + h * HEAD, o)
            if want_state:
                a_l = a[last:last + 1, :]
                kd = (kk * jnp.exp(a_l - a)).astype(BF16)
                st_new = _dot(i_b.astype(F32).T.astype(BF16), kd)
                if use_state:
                    st_new = st_new + st * jnp.exp(a_l)
                    state[1][d, h] = st_new
                if emit is not None:
                    emit("hg", d, h, st_new)
            yield

    lane_row = lax.broadcasted_iota(jnp.int32, (1, HEAD), 1)
    result["m"] = jnp.where(lane_row < 4, m_out_rows[0], jnp.where(lane_row < 8, m_out_rows[1], 0.0))


def _emit_states(kind, d, h, value, c_ref, n_ref, s_ref):
    if kind == "ml":
        c_ref[d, h] = value[:, 0:HEAD]
        n_ref[d, h:h + 1, :] = value[:, HEAD:2 * HEAD].T[0:1, :]
    else:
        s_ref[d, h] = value.T


def _scan_kernel(*refs, has_init, emit_state, nseg):
    single = nseg == 1
    n_tok_in = 4 if single else 8
    n_h = 1 if single else 2
    n_in = n_tok_in + (4 if has_init else 0)
    n_out = n_h + (4 if emit_state else 0)
    ins, outs, scr = refs[:n_in], refs[n_in:n_in + n_out], refs[n_in + n_out:]
    dir_refs = (ins[0:4], ins[0:4] if single else ins[4:8])
    st_outs = outs[n_h:]
    tri_sc, mask_sc, caug_sc, st_sc, m_sc = scr
    use_state = has_init or nseg > 1
    b_id, j = pl.program_id(0), pl.program_id(1)

    @pl.when(jnp.logical_and(b_id == 0, j == 0))
    def _build_constants():
        _build_scan_constants(tri_sc, mask_sc)

    if use_state:
        @pl.when(j == 0)
        def _init_state():
            if has_init:
                c0_ref, n0_ref, m0_ref, s0_ref = ins[n_tok_in:n_tok_in + 4]
                for d in range(2):
                    for h in range(ML_HEADS):
                        caug_sc[d, h, :, 0:HEAD] = c0_ref[d, h]
                        caug_sc[d, h, :, HEAD:2 * HEAD] = jnp.broadcast_to(n0_ref[d, h:h + 1, :], (HEAD, HEAD)).T
                        st_sc[d, h] = s0_ref[d, h].T
                m_sc[...] = jnp.broadcast_to(m0_ref[...], m_sc.shape)
            else:
                caug_sc[...] = jnp.zeros(caug_sc.shape, F32)
                st_sc[...] = jnp.zeros(st_sc.shape, F32)
                m_sc[...] = jnp.zeros(m_sc.shape, F32)

    def write_h(d, c0, value):
        if single and d == 1:
            outs[0][:, c0:c0 + HEAD] += value
        else:
            outs[0 if single else d][:, c0:c0 + HEAD] = value

    def emit(kind, d, h, value):
        if single:
            _emit_states(kind, d, h, value, st_outs[0], st_outs[1], st_outs[3])
        else:
            pl.when(j == nseg - 1)(lambda: _emit_states(kind, d, h, value, st_outs[0], st_outs[1], st_outs[3]))

    m_in = m_sc[0:1, :] if use_state else jnp.zeros((1, HEAD), F32)
    result = {}
    for _ in _scan_units(dir_refs, write_h, tri_sc, mask_sc, m_in, (caug_sc, st_sc) if use_state else None,
                         emit if emit_state else None, result):
        pass
    m_new = result["m"]
    if use_state:
        m_sc[...] = jnp.broadcast_to(m_new, m_sc.shape)
    if emit_state:
        if single:
            st_outs[2][...] = m_new
        else:
            @pl.when(j == nseg - 1)
            def _emit_m():
                st_outs[2][...] = m_new


def _scan(qkv, gate, hqi, hgf, n_seq, nseg, init=None, emit_state=False):
    n_tok = qkv.shape[0]
    fwd = lambda n: pl.BlockSpec((SEG, n), lambda b, j: (b * nseg + j, 0))
    bwd = lambda n: pl.BlockSpec((SEG, n), lambda b, j: (b * nseg + nseg - 1 - j, 0))
    widths = (1536, 128, 1024, 1024)
    single = nseg == 1
    in_specs = [fwd(n) for n in widths] + ([] if single else [bwd(n) for n in widths])
    args = [qkv, gate, hqi, hgf] + ([] if single else [qkv, gate, hqi, hgf])
    mat = pl.BlockSpec((None, 2, 4, HEAD, HEAD), lambda b, j: (b, 0, 0, 0, 0))
    vec = pl.BlockSpec((None, 2, 4, HEAD), lambda b, j: (b, 0, 0, 0))
    sca = pl.BlockSpec((None, 1, HEAD), lambda b, j: (b, 0, 0))
    if init is not None:
        in_specs += [mat, vec, sca, mat]
        args += list(init)
    out_specs = [fwd(1024)] if single else [fwd(1024), bwd(1024)]
    out_shape = [jax.ShapeDtypeStruct((n_tok, 1024), F32)] * len(out_specs)
    if emit_state:
        out_specs += [mat, vec, sca, mat]
        out_shape += [jax.ShapeDtypeStruct((n_seq, 2, 4, HEAD, HEAD), F32),
                      jax.ShapeDtypeStruct((n_seq, 2, 4, HEAD), F32),
                      jax.ShapeDtypeStruct((n_seq, 1, HEAD), F32),
                      jax.ShapeDtypeStruct((n_seq, 2, 4, HEAD, HEAD), F32)]
    return pl.pallas_call(
        functools.partial(_scan_kernel, has_init=init is not None, emit_state=emit_state, nseg=nseg),
        grid=(n_seq, nseg),
        in_specs=in_specs,
        out_specs=out_specs,
        out_shape=out_shape,
        scratch_shapes=[pltpu.VMEM((2, SEG, SEG), BF16),
                        pltpu.VMEM((2, len(TILE_LEVELS) + 1, HALF, HALF), F32),
                        pltpu.VMEM((2, 4, HEAD, 2 * HEAD), F32),
                        pltpu.VMEM((2, 4, HEAD, HEAD), F32),
                        pltpu.VMEM((8, HEAD), F32)],
        compiler_params=_cparams(2),
        name="bidir_scan",
    )(*args)


def _mix_ffn_pieces(h, og, x, mod, norm_gain, wo_ref, ffn_gain, w1_ref, w3_ref, w2_ref, final_gain, o_ref):
    parts = []
    for g in range(D // HEAD):
        hs = h[:, g * HEAD:(g + 1) * HEAD]
        parts.append(hs * lax.rsqrt(jnp.mean(hs * hs, axis=-1, keepdims=True) + EPS))
    hn = jnp.concatenate(parts, axis=1) * norm_gain
    if og is not None:
        hn = hn * jnp.concatenate([_sigmoid(og[:, 0:512]), _silu(og[:, 512:1024])], axis=1)
    x = x + mod[2:3, :] * _dot(hn.astype(BF16), wo_ref[...])
    yield
    hf = _norm_mod(x, ffn_gain, mod, 1).astype(BF16)
    acc = jnp.zeros(x.shape, F32)
    for c0 in range(0, D_FF, FFN_CHUNK):
        a = _dot(hf, w1_ref[:, c0:c0 + FFN_CHUNK])
        b = _dot(hf, w3_ref[:, c0:c0 + FFN_CHUNK])
        acc = acc + _dot((_silu(a) * b).astype(BF16), w2_ref[c0:c0 + FFN_CHUNK, :])
        yield
    y = x + mod[5:6, :] * acc
    if final_gain is not None:
        y = y * lax.rsqrt(jnp.mean(y * y, axis=-1, keepdims=True) + EPS) * final_gain
    o_ref[...] = y


def _mix_ffn_kernel(*refs, n_h, gated, final):
    h_refs = refs[:n_h]
    pos = n_h
    og_ref = refs[pos] if gated else None
    pos += 1 if gated else 0
    x_ref, mod_ref, ng_ref, wo_ref, g_ref, w1_ref, w3_ref, w2_ref = refs[pos:pos + 8]
    fg_ref = refs[pos + 8] if final else None
    o_ref = refs[-1]
    h = h_refs[0][...]
    for r in h_refs[1:]:
        h = h + r[...]
    for _ in _mix_ffn_pieces(h, og_ref[...] if gated else None, x_ref[...], mod_ref[...], ng_ref[...], wo_ref,
                             g_ref[...], w1_ref, w3_ref, w2_ref, fg_ref[...] if final else None, o_ref):
        pass


def _mix_ffn(hs, og, x, mod_l, cond_tokens, norm_gain, w_out, ffn_gain, layer, w1, w3, w2, final_gain=None):
    n_tok = x.shape[0]
    tm = TOKEN_TILE
    tile_mod_row = _mod_row(cond_tokens, tm)
    tok = pl.BlockSpec((tm, D), lambda i: (i, 0))
    full = lambda a: pl.BlockSpec(a.shape, lambda i: (0,) * a.ndim, pipeline_mode=pl.Buffered(1))
    of_layer = lambda a: pl.BlockSpec((None,) + a.shape[1:], lambda i: (layer, 0, 0), pipeline_mode=pl.Buffered(1))
    gated = og is not None
    final = final_gain is not None
    consts = [norm_gain, w_out, ffn_gain, w1, w3, w2] + ([final_gain] if final else [])
    args = list(hs) + ([og] if gated else []) + [x, mod_l] + consts
    in_specs = [tok] * (len(hs) + (1 if gated else 0) + 1)
    in_specs += [pl.BlockSpec((None, 6, D), lambda i: (tile_mod_row(i), 0, 0))]
    in_specs += [of_layer(a) if a.ndim == 3 else full(a) for a in consts]
    return pl.pallas_call(
        functools.partial(_mix_ffn_kernel, n_h=len(hs), gated=gated, final=final),
        grid=(n_tok // tm,),
        in_specs=in_specs,
        out_specs=tok,
        out_shape=jax.ShapeDtypeStruct((n_tok, D), F32),
        compiler_params=_cparams(1),
        name="mix_ffn",
    )(*args)


def _rope(x, cos, sin_signed):
    lane = lax.broadcasted_iota(jnp.int32, (x.shape[0], HEAD), 1)
    first = (lane & 16) == 0
    parts = []
    for h in range(DA_HEADS):
        xh = x[:, h * HEAD:(h + 1) * HEAD]
        partner = jnp.where(first, pltpu.roll(xh, HEAD - 16, 1), pltpu.roll(xh, 16, 1))
        parts.append(xh * cos + partner * sin_signed)
    return jnp.concatenate(parts, axis=1)


def _inproj_odd_kernel(*refs, rope):
    x_ref, mod_ref, g_ref, w_ref = refs[:4]
    q_ref, k_ref, v_ref = refs[-3:]
    h = _norm_mod(x_ref[...], g_ref[...], mod_ref[...], 0).astype(BF16)
    q = _dot(h, w_ref[:, 0:D])
    k = _dot(h, w_ref[:, D:2 * D])
    v = _dot(h, w_ref[:, 2 * D:3 * D])
    if rope:
        cos, sin_signed = refs[4][...], refs[5][...]
        q = _rope(q, cos, sin_signed)
        k = _rope(k, cos, sin_signed)
    q_ref[...] = (q * (DA_DQK ** -0.5 * LOG2E)).astype(q_ref.dtype)
    k_ref[...] = k.astype(k_ref.dtype)
    v_ref[...] = v.astype(v_ref.dtype)


def _inproj_odd(x, mod_l, cond_tokens, gain, w, rope_tables, kv_dtype):
    n_tok = x.shape[0]
    tm = PROJ_TILE
    tile_mod_row = _mod_row(cond_tokens, tm)
    tok = pl.BlockSpec((tm, D), lambda i: (i, 0))
    full = lambda a: pl.BlockSpec(a.shape, lambda i: (0,) * a.ndim, pipeline_mode=pl.Buffered(1))
    args = [x, mod_l, gain, w]
    in_specs = [tok, pl.BlockSpec((None, 6, D), lambda i: (tile_mod_row(i), 0, 0)), full(gain), full(w)]
    if rope_tables is not None:
        tiles_per_seq = rope_tables[0].shape[0] // tm
        args += list(rope_tables)
        in_specs += [pl.BlockSpec((tm, HEAD), lambda i: (i % tiles_per_seq, 0))] * 2
    return pl.pallas_call(
        functools.partial(_inproj_odd_kernel, rope=rope_tables is not None),
        grid=(n_tok // tm,),
        in_specs=in_specs,
        out_specs=[tok, tok, tok],
        out_shape=[jax.ShapeDtypeStruct((n_tok, D), BF16), jax.ShapeDtypeStruct((n_tok, D), kv_dtype),
                   jax.ShapeDtypeStruct((n_tok, D), kv_dtype)],
        compiler_params=_cparams(1),
        name="inproj_odd",
    )(*args)


def _rope_tables(n_tok):
    quarter = DA_DQK // 4
    tok = np.arange(n_tok)
    pos = np.stack([tok // GRID_W, tok % GRID_W], axis=1).astype(np.float32)
    inv = (np.float32(ROPE_BASE) ** (-np.arange(quarter, dtype=np.float32) / np.float32(quarter))).astype(np.float32)
    lane = np.arange(HEAD)
    ang = (pos[:, (lane // 32) % 2] * inv[lane % quarter][None, :]).astype(np.float32)
    sign = np.where((lane % 32) < quarter, -1.0, 1.0).astype(np.float32)
    return jnp.asarray(np.cos(ang), F32), jnp.asarray(np.sin(ang) * sign[None, :], F32)


def _lambda(lam_ref, lam_init):
    lp = lam_ref[...]
    return (jnp.exp(jnp.sum(lp[0:1] * lp[1:2], axis=-1, keepdims=True))
            - jnp.exp(jnp.sum(lp[2:3] * lp[3:4], axis=-1, keepdims=True)) + lam_init)


VT_ROWS = HEAD + 16


def _qt2(qh):
    qt = qh.astype(F32).T
    row = lax.broadcasted_iota(jnp.int32, qt.shape, 0)
    return jnp.concatenate([jnp.where(row < DA_DQK, qt, 0.0), jnp.where(row >= DA_DQK, qt, 0.0)],
                           axis=1).astype(BF16)


def _vt_aug(vh):
    tk = vh.shape[0]
    row = lax.broadcasted_iota(jnp.int32, (VT_ROWS - HEAD, tk), 0)
    return jnp.concatenate([vh.astype(F32).T, (row == 0).astype(F32)], axis=0).astype(BF16)


def _diff_attn_heads(q_ref, k_fn, vt_fn, n_chunks, lam, o_ref):
    tq = q_ref.shape[0]
    items = [(h, c) for h in range(DA_HEADS) for c in range(n_chunks)]
    qt2, state, pending = {}, {}, []

    def finish(h, c, st):
        cm = jnp.max(st, axis=0, keepdims=True)
        if c == 0:
            m_new = cm
            acc = _dot(vt_fn(h, c), jnp.exp2(st - m_new).astype(BF16))
        else:
            m, acc = state[h]
            m_new = jnp.maximum(m, cm)
            acc = acc * jnp.exp2(m - m_new) + _dot(vt_fn(h, c), jnp.exp2(st - m_new).astype(BF16))
        state[h] = (m_new, acc)
        if c == n_chunks - 1:
            den = acc[HEAD:HEAD + 1, :]
            o_t = acc[0:HEAD, 0:tq] * (1.0 / den[:, 0:tq]) - acc[0:HEAD, tq:] * (lam / den[:, tq:])
            o_ref[:, h * HEAD:(h + 1) * HEAD] = o_t.T

    for h, c in items:
        if c == 0:
            qt2[h] = _qt2(q_ref[:, h * HEAD:(h + 1) * HEAD])
        pending.append((h, c, _dot(k_fn(h, c), qt2[h])))
        if len(pending) > ATTN_LOOKAHEAD:
            finish(*pending.pop(0))
    while pending:
        finish(*pending.pop(0))


def _attn_prompt_kernel(q_ref, k_ref, v_ref, lam_ref, o_ref, *, lam_init):
    lam = _lambda(lam_ref, lam_init)
    _diff_attn_heads(q_ref, lambda h, c: k_ref[:, h * HEAD:(h + 1) * HEAD].astype(BF16),
                     lambda h, c: _vt_aug(v_ref[:, h * HEAD:(h + 1) * HEAD]), 1, lam, o_ref)


def _attn_prompt(q, k, v, lam_p, seq, lam_init):
    n_tok = q.shape[0]
    tok = pl.BlockSpec((seq, D), lambda b: (b, 0))
    return pl.pallas_call(
        functools.partial(_attn_prompt_kernel, lam_init=lam_init),
        grid=(n_tok // seq,),
        in_specs=[tok, tok, tok, pl.BlockSpec(lam_p.shape, lambda b: (0, 0))],
        out_specs=tok,
        out_shape=jax.ShapeDtypeStruct((n_tok, D), F32),
        compiler_params=_cparams(1),
        name="diff_attn_prompt",
    )(q, k, v, lam_p)


def _attn_sample_kernel(q_ref, k_ref, v_ref, ck_ref, cv_ref, lam_ref, o_ref, kcat, vtcat, *, lam_init, past):
    n_keys = kcat.shape[0]
    chunks = [(c0, min(KEY_CHUNK, n_keys - c0)) for c0 in range(0, n_keys, KEY_CHUNK)]

    @pl.when(pl.program_id(1) == 0)
    def _gather_keys():
        kcat[0:past, :] = ck_ref[...].astype(BF16)
        kcat[past:, :] = k_ref[...]
        for h in range(DA_HEADS):
            sl = slice(h * HEAD, (h + 1) * HEAD)
            for c0, n in chunks:
                src, r0 = (cv_ref, c0) if c0 < past else (v_ref, c0 - past)
                vtcat[h, :, c0:c0 + n] = _vt_aug(src[r0:r0 + n, sl])

    lam = _lambda(lam_ref, lam_init)
    _diff_attn_heads(q_ref, lambda h, c: kcat[chunks[c][0]:chunks[c][0] + chunks[c][1], h * HEAD:(h + 1) * HEAD],
                     lambda h, c: vtcat[h, :, chunks[c][0]:chunks[c][0] + chunks[c][1]], len(chunks), lam, o_ref)


def _attn_sample(q, k, v, ck, cv, lam_p, n_seq, seq, lam_init, tq=128):
    past = ck.shape[1]
    qb = pl.BlockSpec((tq, D), lambda b, i: (b * (seq // tq) + i, 0))
    kv = pl.BlockSpec((seq, D), lambda b, i: (b, 0))
    cache = pl.BlockSpec((None, past, D), lambda b, i: (b, 0, 0))
    return pl.pallas_call(
        functools.partial(_attn_sample_kernel, lam_init=lam_init, past=past),
        grid=(n_seq, seq // tq),
        in_specs=[qb, kv, kv, cache, cache, pl.BlockSpec(lam_p.shape, lambda b, i: (0, 0))],
        out_specs=qb,
        out_shape=jax.ShapeDtypeStruct((n_seq * seq, D), F32),
        scratch_shapes=[pltpu.VMEM((past + seq, D), BF16), pltpu.VMEM((DA_HEADS, VT_ROWS, past + seq), BF16)],
        compiler_params=_cparams(2),
        name="diff_attn_sample",
    )(q, k, v, ck, cv, lam_p)


def kernel(x_prompt, x_sample, c, c_ctx, cache_attn_k, cache_attn_v, state_mlstm_C, state_mlstm_n, state_mlstm_m,
           state_hgrn_S, ada_w, ada_b, norm_mix_g, norm_ffn_g, ev_w_in, ev_gate_b, ev_lb_logits, ml_norm_g,
           hg_norm_g, ev_w_out, od_w_in, od_lambda, da_norm_g, od_w_out, ffn_w1, ffn_w3, ffn_w2, final_norm_g):
    assert DEPTH % 2 == 0
    n_p, s_p, _ = x_prompt.shape
    n_s, s_s, _ = x_sample.shape
    past = cache_attn_k.shape[2]
    assert s_p == SEG and s_s % SEG == 0 and s_s % PROJ_TILE == 0 and (n_p * s_p) % PROJ_TILE == 0
    assert PROJ_TILE % TOKEN_TILE == 0
    xp = x_prompt.astype(F32).reshape(n_p * s_p, D)
    xs = x_sample.astype(F32).reshape(n_s * s_s, D)

    cond8 = jnp.zeros((8, D), F32).at[0].set(c_ctx.astype(F32)).at[1:1 + n_s].set(c.astype(F32))
    mod = _modulation(cond8, ada_w.astype(F32), ada_b.astype(F32)).reshape(DEPTH, 8, 6, D)
    streams = [(xp, None), (xs, s_s)]
    outputs = {}
    w1_all, w3_all, w2_all = ffn_w1.astype(BF16), ffn_w3.astype(BF16), ffn_w2.astype(BF16)

    for l in range(DEPTH):
        mix_gain = norm_mix_g[l].astype(F32).reshape(1, D)
        ffn_gain = norm_ffn_g[l].astype(F32).reshape(1, D)
        if l % 2 == 0:
            e = l // 2
            g0 = sum(EV_SIZES[:4])
            w_e = ev_w_in[e].astype(BF16)
            w_in = jnp.concatenate([w_e[:, :g0], w_e[:, g0 + 16:],
                                    jnp.pad(w_e[:, g0:g0 + 16], ((0, 0), (0, HEAD - 16)))], axis=1)
            gate_b = jnp.pad(ev_gate_b[e].astype(F32), (0, HEAD - 16)).reshape(1, HEAD)
            norm_gain = jnp.concatenate([ml_norm_g[e], hg_norm_g[e]]).astype(F32).reshape(1, D)
            w_out = ev_w_out[e].astype(BF16)
            mixed = []
            for si, (x, cond_tok) in enumerate(streams):
                qkv, gate, hqi, hgf, og = _inproj_even(x, mod[l], cond_tok, mix_gain, w_in, gate_b,
                                                       ev_lb_logits.astype(F32), e)
                if si == 0:
                    *hs, c_new, n_new, m_new, s_new = _scan(qkv, gate, hqi, hgf, n_p, s_p // SEG, emit_state=True)
                    outputs.setdefault("C", []).append(c_new)
                    outputs.setdefault("n", []).append(n_new)
                    outputs.setdefault("m", []).append(m_new[:, 0, 0:8].reshape(n_p, 2, 4))
                    outputs.setdefault("S", []).append(s_new)
                else:
                    init = (state_mlstm_C[:, e].astype(F32), state_mlstm_n[:, e].astype(F32),
                            jnp.pad(state_mlstm_m[:, e].astype(F32).reshape(n_s, 1, 8), ((0, 0), (0, 0), (0, HEAD - 8))),
                            state_hgrn_S[:, e].astype(F32))
                    hs = _scan(qkv, gate, hqi, hgf, n_s, s_s // SEG, init=init)
                mixed.append((list(hs), og, x))
        else:
            o = l // 2
            lam_init = 0.8 - 0.6 * math.exp(-0.3 * l)
            w_in = od_w_in[o].astype(BF16)
            norm_gain = (jnp.tile(da_norm_g[o].astype(F32), DA_HEADS) * (1.0 - lam_init)).reshape(1, D)
            w_out = od_w_out[o].astype(BF16)
            lam_p = od_lambda[o].astype(F32)
            mixed = []
            for si, (x, cond_tok) in enumerate(streams):
                if si == 0:
                    q, k, v = _inproj_odd(x, mod[l], cond_tok, mix_gain, w_in, None, F32)
                    outputs.setdefault("k", []).append(k.reshape(n_p, s_p, DA_HEADS, HEAD))
                    outputs.setdefault("v", []).append(v.reshape(n_p, s_p, DA_HEADS, HEAD))
                    att = _attn_prompt(q, k, v, lam_p, s_p, lam_init)
                else:
                    q, k, v = _inproj_odd(x, mod[l], cond_tok, mix_gain, w_in, _rope_tables(s_s), BF16)
                    ck = cache_attn_k[:, o].reshape(n_s, past, D)
                    cv = cache_attn_v[:, o].reshape(n_s, past, D)
                    att = _attn_sample(q, k, v, ck, cv, lam_p, n_s, s_s, lam_init)
                mixed.append(([att], None, x))
        final_gain = final_norm_g.astype(F32).reshape(1, D) if l == DEPTH - 1 else None
        streams = [(_mix_ffn(hs, og, x, mod[l], cond_tok, norm_gain, w_out, ffn_gain, l, w1_all, w3_all, w2_all,
                             final_gain), cond_tok)
                   for (hs, og, x), (_, cond_tok) in zip(mixed, streams)]

    y_prompt = streams[0][0].reshape(n_p, s_p, D)
    y_sample = streams[1][0].reshape(n_s, s_s, D)
    stack = lambda name: outputs[name][0][:, None] if len(outputs[name]) == 1 else jnp.stack(outputs[name], axis=1)
    return (y_prompt, y_sample, stack("k"), stack("v"), stack("C"), stack("n"), stack("m"), stack("S"))
```

```python
import functools
import math

import jax
import jax.numpy as jnp
import numpy as np
from jax import lax
from jax.experimental import pallas as pl
from jax.experimental.pallas import tpu as pltpu

F32 = jnp.float32
BF16 = jnp.bfloat16

D = 1024
DEPTH = 2
GRID_W = 64
ML_HEADS = 4
HG_HEADS = 4
HEAD = 128
DA_HEADS = 8
DA_DQK = 64
ROPE_BASE = 10000.0
LOG2E = math.log2(math.e)
EPS = 1e-6
D_FF = ((8 * D // 3 + 255) // 256) * 256
EV_SIZES = (512, 512, 512, 512, 16, 512, 512, 512, 512, 512)
EV_COLS = 9 * 512 + 128

SEG = 256
HALF = SEG // 2
TILE_LEVELS = (1, 2, 4, 8, 16, 32, 64)
TOKEN_TILE = 512
PROJ_TILE = 1024
FFN_CHUNK = 256
KEY_CHUNK = 256
PROMPT_SEQS_PER_STEP = 2
ATTN_LOOKAHEAD = 6
VMEM_LIMIT = 56 * 1024 * 1024


def _cparams(n_axes):
    return pltpu.CompilerParams(dimension_semantics=("arbitrary",) * n_axes, vmem_limit_bytes=VMEM_LIMIT)


def _sigmoid(x):
    return 1.0 / (1.0 + jnp.exp(-x))


def _silu(x):
    return x * _sigmoid(x)


def _log_sigmoid(x):
    return jnp.minimum(x, 0.0) - jnp.log(1.0 + jnp.exp(-jnp.abs(x)))


def _dot(a, b):
    return jnp.dot(a, b, preferred_element_type=F32)


def _dot_nt(a, b):
    return lax.dot_general(a, b, (((1,), (1,)), ((), ())), preferred_element_type=F32)


def _norm_mod(x, gain, mod, k):
    ms = jnp.mean(x * x, axis=-1, keepdims=True)
    return x * lax.rsqrt(ms + EPS) * gain * (1.0 + mod[3 * k + 1:3 * k + 2]) + mod[3 * k:3 * k + 1]


def _mod_row(cond_tokens, tm):
    if cond_tokens is None:
        return lambda i: 0
    return lambda i: 1 + i // (cond_tokens // tm)


def _mod_kernel(c_ref, w_ref, b_ref, o_ref):
    s = _silu(c_ref[...]).astype(BF16)
    o_ref[...] = _dot(s, w_ref[...].astype(BF16)) + b_ref[...]


def _modulation(cond8, ada_w, ada_b):
    n_layers = ada_w.shape[0]
    tn = 1536
    return pl.pallas_call(
        _mod_kernel,
        grid=(n_layers, 6 * D // tn),
        in_specs=[pl.BlockSpec((8, D), lambda l, n: (0, 0)),
                  pl.BlockSpec((None, D, tn), lambda l, n: (l, 0, n)),
                  pl.BlockSpec((None, 1, tn), lambda l, n: (l, 0, n))],
        out_specs=pl.BlockSpec((None, 8, tn), lambda l, n: (l, 0, n)),
        out_shape=jax.ShapeDtypeStruct((n_layers, 8, 6 * D), F32),
        compiler_params=_cparams(2),
        name="ada_modulation",
    )(cond8, ada_w, ada_b.reshape(n_layers, 1, 6 * D))


def _inproj_even_body(x, mod, gain, w_ref, gate_b, lb_logits, e_idx, qkv_ref, gate_ref, hqi_ref, hgf_ref, og_ref):
    h = _norm_mod(x, gain, mod, 0).astype(BF16)

    def proj(c0, n):
        return _dot(h, w_ref[:, c0:c0 + n])

    qkv_ref[:, 0:512] = (proj(0, 512) * (HEAD ** -0.5)).astype(BF16)
    qkv_ref[:, 512:1536] = proj(512, 1024).astype(BF16)
    og_ref[:, 0:512] = proj(1536, 512)
    hqi_ref[:, 0:512] = proj(2048, 512).astype(BF16)
    hqi_ref[:, 512:1024] = proj(3584, 512).astype(BF16)
    mx = jnp.max(lb_logits, axis=0, keepdims=True)
    ex = jnp.exp(lb_logits - mx)
    lb = jnp.sum(ex[0:e_idx + 1], axis=0, keepdims=True) / jnp.sum(ex, axis=0, keepdims=True)
    hgf_ref[:, 0:512] = jnp.log(lb + (1.0 - lb) * _sigmoid(proj(2560, 512)))
    hgf_ref[:, 512:1024] = jnp.log(lb + (1.0 - lb) * _sigmoid(proj(3072, 512)))
    og_ref[:, 512:1024] = proj(4096, 512)
    gt = proj(4608, 128) + gate_b
    lane = lax.broadcasted_iota(jnp.int32, gt.shape, 1)
    gate_ref[...] = jnp.where(lane < 8, gt, jnp.where(lane < 16, _log_sigmoid(gt), 0.0))


def _inproj_even_kernel(x_ref, mod_ref, g_ref, w_ref, gb_ref, lbl_ref, qkv_ref, gate_ref, hqi_ref, hgf_ref,
                        og_ref, *, e_idx):
    _inproj_even_body(x_ref[...], mod_ref[...], g_ref[...], w_ref, gb_ref[...], lbl_ref[...], e_idx,
                      qkv_ref, gate_ref, hqi_ref, hgf_ref, og_ref)


def _inproj_even(x, mod_l, cond_tokens, gain, w, gate_b, lb_logits, e_idx):
    n_tok = x.shape[0]
    tm = PROJ_TILE
    tile_mod_row = _mod_row(cond_tokens, tm)
    tok = lambda n: pl.BlockSpec((tm, n), lambda i: (i, 0))
    full = lambda a: pl.BlockSpec(a.shape, lambda i: (0,) * a.ndim, pipeline_mode=pl.Buffered(1))
    return pl.pallas_call(
        functools.partial(_inproj_even_kernel, e_idx=e_idx),
        grid=(n_tok // tm,),
        in_specs=[tok(D), pl.BlockSpec((None, 6, D), lambda i: (tile_mod_row(i), 0, 0)),
                  full(gain), full(w), full(gate_b), full(lb_logits)],
        out_specs=[tok(1536), tok(128), tok(1024), tok(1024), tok(1024)],
        out_shape=[jax.ShapeDtypeStruct((n_tok, 1536), BF16), jax.ShapeDtypeStruct((n_tok, 128), F32),
                   jax.ShapeDtypeStruct((n_tok, 1024), BF16), jax.ShapeDtypeStruct((n_tok, 1024), F32),
                   jax.ShapeDtypeStruct((n_tok, 1024), F32)],
        compiler_params=_cparams(1),
        name="inproj_even",
    )(x, mod_l, gain, w, gate_b, lb_logits)


def _shift_rows(x, k, fill, up):
    n = x.shape[0]
    if k % 8 == 0:
        pad = jnp.full((k,) + x.shape[1:], 0.0 if fill is None else fill, x.dtype)
        return jnp.concatenate([x[k:], pad], axis=0) if up else jnp.concatenate([pad, x[:n - k]], axis=0)
    y = pltpu.roll(x, (n - k) if up else k, 0)
    if fill is None:
        return y
    row = lax.broadcasted_iota(jnp.int32, x.shape, 0)
    return jnp.where(row >= n - k, fill, y) if up else jnp.where(row < k, fill, y)


def _cummax_rows(x, rev):
    k = 1
    while k < x.shape[0]:
        x = jnp.maximum(x, _shift_rows(x, k, -jnp.inf, up=rev))
        k *= 2
    return x


def _cumsum_rows(tri, x):
    hi = x.astype(BF16)
    r1 = x - hi.astype(F32)
    mid = r1.astype(BF16)
    lo = (r1 - mid.astype(F32)).astype(BF16)
    return _dot(tri, hi) + _dot(tri, mid) + _dot(tri, lo)


def _hgrn_tile(q, kk, f, a, q_b, rev, mask_sc, d, up_small):
    ops = []
    bm = a
    for li, m in enumerate(TILE_LEVELS):
        if m < 8:
            qrole = jnp.logical_not(up_small[li]) if rev else up_small[li]
            if m == 1:
                r = jnp.where(qrole, q * f, kk)
            else:
                x = jnp.where(qrole, _shift_rows(bm, m, None, up=rev), bm)
                r = jnp.where(qrole, q, kk) * jnp.exp(jnp.where(qrole, a - x, x - a))
            if 2 * m < 8:
                bm = jnp.where(qrole, bm, _shift_rows(bm, m, None, up=not rev))
        else:
            pieces = []
            for b0 in range(0, HALF, 2 * m):
                lo, up = slice(b0, b0 + m), slice(b0 + m, b0 + 2 * m)
                if rev:
                    ref = a[b0 + m:b0 + m + 1]
                    pieces += [q[lo] * jnp.exp(a[lo] - ref), kk[up] * jnp.exp(ref - a[up])]
                else:
                    ref = a[b0 + m - 1:b0 + m]
                    pieces += [kk[lo] * jnp.exp(ref - a[lo]), q[up] * jnp.exp(a[up] - ref)]
            r = jnp.concatenate(pieces, axis=0)
        ops.append(r.astype(BF16))
    p = _dot_nt(q_b, kk.astype(BF16)).astype(BF16) * mask_sc[d, len(TILE_LEVELS)]
    for li, rb in enumerate(ops):
        p = p + _dot_nt(rb, rb).astype(BF16) * mask_sc[d, li]
    return p


def _build_scan_constants(tri_sc, mask_sc):
    row = lax.broadcasted_iota(jnp.int32, (SEG, SEG), 0)
    col = lax.broadcasted_iota(jnp.int32, (SEG, SEG), 1)
    tri_sc[0] = (col <= row).astype(BF16)
    tri_sc[1] = (col >= row).astype(BF16)
    trow = lax.broadcasted_iota(jnp.int32, (HALF, HALF), 0)
    tcol = lax.broadcasted_iota(jnp.int32, (HALF, HALF), 1)
    for li, m in enumerate(TILE_LEVELS):
        sh = m.bit_length() - 1
        same = (trow >> (sh + 1)) == (tcol >> (sh + 1))
        t_up = ((trow >> sh) & 1) == 1
        s_up = ((tcol >> sh) & 1) == 1
        mask_sc[0, li] = (same & t_up & jnp.logical_not(s_up)).astype(BF16)
        mask_sc[1, li] = (same & s_up & jnp.logical_not(t_up)).astype(BF16)
    eye = (trow == tcol).astype(BF16)
    mask_sc[0, len(TILE_LEVELS)] = eye
    mask_sc[1, len(TILE_LEVELS)] = eye


def _scan_units(dir_refs, write_h, tri_sc, mask_sc, m_in, state, emit, result):
    row = lax.broadcasted_iota(jnp.int32, (SEG, SEG), 0)
    col = lax.broadcasted_iota(jnp.int32, (SEG, SEG), 1)
    lane128 = lax.broadcasted_iota(jnp.int32, (SEG, HEAD), 1)
    e0 = (lane128 == 0).astype(BF16)
    row_half = lax.broadcasted_iota(jnp.int32, (HALF, HEAD), 0)
    up_masks = [((row_half >> (m.bit_length() - 1)) & 1) == 1 for m in TILE_LEVELS if m < 8]
    tmasks = (col <= row, col >= row)
    use_state = state is not None
    want_state = use_state or emit is not None
    m_out_rows = []

    for d in range(2):
        qkv_ref, gate_ref, hqi_ref, hgf_ref = dir_refs[d]
        rev = d == 1
        last = 0 if rev else SEG - 1
        tri = tri_sc[d]

        slab = gate_ref[...]
        b_al = pltpu.roll(_cumsum_rows(tri, slab), HEAD - 8, 1)
        u = slab - b_al
        mx = jnp.maximum(_cummax_rows(u, rev), m_in)
        w_inter = jnp.exp(m_in - mx)
        e_den = jnp.exp(-(b_al + mx))
        mx_last = mx[last:last + 1, :]
        m_out_rows.append(b_al[last:last + 1, :] + mx_last)
        decay = jnp.exp(m_in - mx_last)
        wg = jnp.exp(u - mx_last)
        u_t = u.T

        for h in range(ML_HEADS):
            c = 4 * d + h
            q = qkv_ref[:, h * HEAD:(h + 1) * HEAD]
            k = qkv_ref[:, 512 + h * HEAD:512 + (h + 1) * HEAD]
            v = qkv_ref[:, 1024 + h * HEAD:1024 + (h + 1) * HEAD]
            dm = jnp.where(tmasks[d], jnp.exp(u_t[c:c + 1, :] - mx[:, c:c + 1]), 0.0)
            s = (_dot_nt(q, k) * dm).astype(BF16)
            v_aug = jnp.concatenate([v, e0], axis=1)
            numden = _dot(s, v_aug)
            if use_state:
                numden = numden + w_inter[:, c:c + 1] * _dot(q, state[0][d, h].astype(BF16))
            den = jnp.maximum(jnp.abs(numden[:, HEAD:HEAD + 1]), e_den[:, c:c + 1])
            write_h(d, h * HEAD, numden[:, 0:HEAD] / den)
            if want_state:
                kw_t = (k.astype(F32) * wg[:, c:c + 1]).T.astype(BF16)
                upd = _dot(kw_t, v_aug)
                if use_state:
                    upd = upd + decay[:, c:c + 1] * state[0][d, h]
                    state[0][d, h] = upd
                if emit is not None:
                    emit("ml", d, h, upd)
            yield

        lf_all = hgf_ref[:, 512 * d:512 * (d + 1)]
        a_all = _cumsum_rows(tri, lf_all)
        f_all = jnp.exp(lf_all)
        kk_all = 1.0 - f_all
        for h in range(HG_HEADS):
            sl = slice(h * HEAD, (h + 1) * HEAD)
            a = a_all[:, sl]
            kk = kk_all[:, sl]
            f = f_all[:, sl]
            q_b = hqi_ref[:, h * HEAD:(h + 1) * HEAD]
            i_b = hqi_ref[:, 512 + h * HEAD:512 + (h + 1) * HEAD]
            q = q_b.astype(F32)
            qh, kh = (0, 1) if rev else (1, 0)
            a_q, a_k = a[qh * HALF:(qh + 1) * HALF], a[kh * HALF:(kh + 1) * HALF]
            a_mid = a[HALF:HALF + 1] if rev else a[HALF - 1:HALF]
            r_q = (q[qh * HALF:(qh + 1) * HALF] * jnp.exp(a_q - a_mid)).astype(BF16)
            r_k = (kk[kh * HALF:(kh + 1) * HALF] * jnp.exp(a_mid - a_k)).astype(BF16)
            p_cross = _dot_nt(r_q, r_k).astype(BF16)
            p_tiles = []
            for t in range(2):
                rows = slice(t * HALF, (t + 1) * HALF)
                p_tiles.append(_hgrn_tile(q[rows], kk[rows], f[rows], a[rows], q_b[rows], rev, mask_sc, d, up_masks))
            i_lo, i_hi = i_b[0:HALF], i_b[HALF:SEG]
            if rev:
                o = jnp.concatenate([_dot(p_tiles[0], i_lo) + _dot(p_cross, i_hi), _dot(p_tiles[1], i_hi)], axis=0)
            else:
                o = jnp.concatenate([_dot(p_tiles[0], i_lo), _dot(p_cross, i_lo) + _dot(p_tiles[1], i_hi)], axis=0)
            if use_state:
                st = state[1][d, h]
                o = o + _dot_nt((q * jnp.exp(a)).astype(BF16), st.astype(BF16))
            write_h(d, 512 + h * HEAD, o)
            if want_state:
                a_l = a[last:last + 1, :]
                kd = (kk * jnp.exp(a_l - a)).astype(BF16)
                st_new = _dot(i_b.astype(F32).T.astype(BF16), kd)
                if use_state:
                    st_new = st_new + st * jnp.exp(a_l)
                    state[1][d, h] = st_new
                if emit is not None:
                    emit("hg", d, h, st_new)
            yield

    lane_row = lax.broadcasted_iota(jnp.int32, (1, HEAD), 1)
    result["m"] = jnp.where(lane_row < 4, m_out_rows[0], jnp.where(lane_row < 8, m_out_rows[1], 0.0))


def _emit_states(kind, d, h, value, c_ref, n_ref, s_ref):
    if kind == "ml":
        c_ref[d, h] = value[:, 0:HEAD]
        n_ref[d, h:h + 1, :] = value[:, HEAD:2 * HEAD].T[0:1, :]
    else:
        s_ref[d, h] = value.T


def _scan_kernel(*refs, has_init, emit_state, nseg):
    single = nseg == 1
    n_tok_in = 4 if single else 8
    n_h = 1 if single else 2
    n_in = n_tok_in + (4 if has_init else 0)
    n_out = n_h + (4 if emit_state else 0)
    ins, outs, scr = refs[:n_in], refs[n_in:n_in + n_out], refs[n_in + n_out:]
    dir_refs = (ins[0:4], ins[0:4] if single else ins[4:8])
    st_outs = outs[n_h:]
    tri_sc, mask_sc, caug_sc, st_sc, m_sc = scr
    use_state = has_init or nseg > 1
    b_id, j = pl.program_id(0), pl.program_id(1)

    @pl.when(jnp.logical_and(b_id == 0, j == 0))
    def _build_constants():
        _build_scan_constants(tri_sc, mask_sc)

    if use_state:
        @pl.when(j == 0)
        def _init_state():
            if has_init:
                c0_ref, n0_ref, m0_ref, s0_ref = ins[n_tok_in:n_tok_in + 4]
                for d in range(2):
                    for h in range(ML_HEADS):
                        caug_sc[d, h, :, 0:HEAD] = c0_ref[d, h]
                        caug_sc[d, h, :, HEAD:2 * HEAD] = jnp.broadcast_to(n0_ref[d, h:h + 1, :], (HEAD, HEAD)).T
                        st_sc[d, h] = s0_ref[d, h].T
                m_sc[...] = jnp.broadcast_to(m0_ref[...], m_sc.shape)
            else:
                caug_sc[...] = jnp.zeros(caug_sc.shape, F32)
                st_sc[...] = jnp.zeros(st_sc.shape, F32)
                m_sc[...] = jnp.zeros(m_sc.shape, F32)

    def write_h(d, c0, value):
        if single and d == 1:
            outs[0][:, c0:c0 + HEAD] += value
        else:
            outs[0 if single else d][:, c0:c0 + HEAD] = value

    def emit(kind, d, h, value):
        if single:
            _emit_states(kind, d, h, value, st_outs[0], st_outs[1], st_outs[3])
        else:
            pl.when(j == nseg - 1)(lambda: _emit_states(kind, d, h, value, st_outs[0], st_outs[1], st_outs[3]))

    m_in = m_sc[0:1, :] if use_state else jnp.zeros((1, HEAD), F32)
    result = {}
    for _ in _scan_units(dir_refs, write_h, tri_sc, mask_sc, m_in, (caug_sc, st_sc) if use_state else None,
                         emit if emit_state else None, result):
        pass
    m_new = result["m"]
    if use_state:
        m_sc[...] = jnp.broadcast_to(m_new, m_sc.shape)
    if emit_state:
        if single:
            st_outs[2][...] = m_new
        else:
            @pl.when(j == nseg - 1)
            def _emit_m():
                st_outs[2][...] = m_new


def _scan(qkv, gate, hqi, hgf, n_seq, nseg, init=None, emit_state=False):
    n_tok = qkv.shape[0]
    fwd = lambda n: pl.BlockSpec((SEG, n), lambda b, j: (b * nseg + j, 0))
    bwd = lambda n: pl.BlockSpec((SEG, n), lambda b, j: (b * nseg + nseg - 1 - j, 0))
    widths = (1536, 128, 1024, 1024)
    single = nseg == 1
    in_specs = [fwd(n) for n in widths] + ([] if single else [bwd(n) for n in widths])
    args = [qkv, gate, hqi, hgf] + ([] if single else [qkv, gate, hqi, hgf])
    mat = pl.BlockSpec((None, 2, 4, HEAD, HEAD), lambda b, j: (b, 0, 0, 0, 0))
    vec = pl.BlockSpec((None, 2, 4, HEAD), lambda b, j: (b, 0, 0, 0))
    sca = pl.BlockSpec((None, 1, HEAD), lambda b, j: (b, 0, 0))
    if init is not None:
        in_specs += [mat, vec, sca, mat]
        args += list(init)
    out_specs = [fwd(1024)] if single else [fwd(1024), bwd(1024)]
    out_shape = [jax.ShapeDtypeStruct((n_tok, 1024), F32)] * len(out_specs)
    if emit_state:
        out_specs += [mat, vec, sca, mat]
        out_shape += [jax.ShapeDtypeStruct((n_seq, 2, 4, HEAD, HEAD), F32),
                      jax.ShapeDtypeStruct((n_seq, 2, 4, HEAD), F32),
                      jax.ShapeDtypeStruct((n_seq, 1, HEAD), F32),
                      jax.ShapeDtypeStruct((n_seq, 2, 4, HEAD, HEAD), F32)]
    return pl.pallas_call(
        functools.partial(_scan_kernel, has_init=init is not None, emit_state=emit_state, nseg=nseg),
        grid=(n_seq, nseg),
        in_specs=in_specs,
        out_specs=out_specs,
        out_shape=out_shape,
        scratch_shapes=[pltpu.VMEM((2, SEG, SEG), BF16),
                        pltpu.VMEM((2, len(TILE_LEVELS) + 1, HALF, HALF), BF16),
                        pltpu.VMEM((2, 4, HEAD, 2 * HEAD), F32),
                        pltpu.VMEM((2, 4, HEAD, HEAD), F32),
                        pltpu.VMEM((8, HEAD), F32)],
        compiler_params=_cparams(2),
        name="bidir_scan",
    )(*args)


def _mix_ffn_pieces(h, og, x, mod, norm_gain, wo_ref, ffn_gain, w1_ref, w3_ref, w2_ref, final_gain, o_ref):
    parts = []
    for g in range(D // HEAD):
        hs = h[:, g * HEAD:(g + 1) * HEAD]
        parts.append(hs * lax.rsqrt(jnp.mean(hs * hs, axis=-1, keepdims=True) + EPS))
    hn = jnp.concatenate(parts, axis=1) * norm_gain
    if og is not None:
        hn = hn * jnp.concatenate([_sigmoid(og[:, 0:512]), _silu(og[:, 512:1024])], axis=1)
    x = x + mod[2:3, :] * _dot(hn.astype(BF16), wo_ref[...])
    yield
    hf = _norm_mod(x, ffn_gain, mod, 1).astype(BF16)
    acc = jnp.zeros(x.shape, F32)
    for c0 in range(0, D_FF, FFN_CHUNK):
        a = _dot(hf, w1_ref[:, c0:c0 + FFN_CHUNK])
        b = _dot(hf, w3_ref[:, c0:c0 + FFN_CHUNK])
        acc = acc + _dot((_silu(a) * b).astype(BF16), w2_ref[c0:c0 + FFN_CHUNK, :])
        yield
    y = x + mod[5:6, :] * acc
    if final_gain is not None:
        y = y * lax.rsqrt(jnp.mean(y * y, axis=-1, keepdims=True) + EPS) * final_gain
    o_ref[...] = y


def _mix_ffn_kernel(*refs, n_h, gated, final):
    h_refs = refs[:n_h]
    pos = n_h
    og_ref = refs[pos] if gated else None
    pos += 1 if gated else 0
    x_ref, mod_ref, ng_ref, wo_ref, g_ref, w1_ref, w3_ref, w2_ref = refs[pos:pos + 8]
    fg_ref = refs[pos + 8] if final else None
    o_ref = refs[-1]
    h = h_refs[0][...]
    for r in h_refs[1:]:
        h = h + r[...]
    for _ in _mix_ffn_pieces(h, og_ref[...] if gated else None, x_ref[...], mod_ref[...], ng_ref[...], wo_ref,
                             g_ref[...], w1_ref, w3_ref, w2_ref, fg_ref[...] if final else None, o_ref):
        pass


def _mix_ffn(hs, og, x, mod_l, cond_tokens, norm_gain, w_out, ffn_gain, layer, w1, w3, w2, final_gain=None):
    n_tok = x.shape[0]
    tm = TOKEN_TILE
    tile_mod_row = _mod_row(cond_tokens, tm)
    tok = pl.BlockSpec((tm, D), lambda i: (i, 0))
    full = lambda a: pl.BlockSpec(a.shape, lambda i: (0,) * a.ndim, pipeline_mode=pl.Buffered(1))
    of_layer = lambda a: pl.BlockSpec((None,) + a.shape[1:], lambda i: (layer, 0, 0), pipeline_mode=pl.Buffered(1))
    gated = og is not None
    final = final_gain is not None
    consts = [norm_gain, w_out, ffn_gain, w1, w3, w2] + ([final_gain] if final else [])
    args = list(hs) + ([og] if gated else []) + [x, mod_l] + consts
    in_specs = [tok] * (len(hs) + (1 if gated else 0) + 1)
    in_specs += [pl.BlockSpec((None, 6, D), lambda i: (tile_mod_row(i), 0, 0))]
    in_specs += [of_layer(a) if a.ndim == 3 else full(a) for a in consts]
    return pl.pallas_call(
        functools.partial(_mix_ffn_kernel, n_h=len(hs), gated=gated, final=final),
        grid=(n_tok // tm,),
        in_specs=in_specs,
        out_specs=tok,
        out_shape=jax.ShapeDtypeStruct((n_tok, D), F32),
        compiler_params=_cparams(1),
        name="mix_ffn",
    )(*args)


def _rope(x, cos, sin_signed):
    lane = lax.broadcasted_iota(jnp.int32, (x.shape[0], HEAD), 1)
    first = (lane & 16) == 0
    parts = []
    for h in range(DA_HEADS):
        xh = x[:, h * HEAD:(h + 1) * HEAD]
        partner = jnp.where(first, pltpu.roll(xh, HEAD - 16, 1), pltpu.roll(xh, 16, 1))
        parts.append(xh * cos + partner * sin_signed)
    return jnp.concatenate(parts, axis=1)


def _inproj_odd_kernel(*refs, rope):
    x_ref, mod_ref, g_ref, w_ref = refs[:4]
    q_ref, k_ref, v_ref = refs[-3:]
    h = _norm_mod(x_ref[...], g_ref[...], mod_ref[...], 0).astype(BF16)
    q = _dot(h, w_ref[:, 0:D])
    k = _dot(h, w_ref[:, D:2 * D])
    v = _dot(h, w_ref[:, 2 * D:3 * D])
    if rope:
        cos, sin_signed = refs[4][...], refs[5][...]
        q = _rope(q, cos, sin_signed)
        k = _rope(k, cos, sin_signed)
    q_ref[...] = (q * (DA_DQK ** -0.5 * LOG2E)).astype(q_ref.dtype)
    k_ref[...] = k.astype(k_ref.dtype)
    v_ref[...] = v.astype(v_ref.dtype)


def _inproj_odd(x, mod_l, cond_tokens, gain, w, rope_tables, kv_dtype):
    n_tok = x.shape[0]
    tm = PROJ_TILE
    tile_mod_row = _mod_row(cond_tokens, tm)
    tok = pl.BlockSpec((tm, D), lambda i: (i, 0))
    full = lambda a: pl.BlockSpec(a.shape, lambda i: (0,) * a.ndim, pipeline_mode=pl.Buffered(1))
    args = [x, mod_l, gain, w]
    in_specs = [tok, pl.BlockSpec((None, 6, D), lambda i: (tile_mod_row(i), 0, 0)), full(gain), full(w)]
    if rope_tables is not None:
        tiles_per_seq = rope_tables[0].shape[0] // tm
        args += list(rope_tables)
        in_specs += [pl.BlockSpec((tm, HEAD), lambda i: (i % tiles_per_seq, 0))] * 2
    return pl.pallas_call(
        functools.partial(_inproj_odd_kernel, rope=rope_tables is not None),
        grid=(n_tok // tm,),
        in_specs=in_specs,
        out_specs=[tok, tok, tok],
        out_shape=[jax.ShapeDtypeStruct((n_tok, D), BF16), jax.ShapeDtypeStruct((n_tok, D), kv_dtype),
                   jax.ShapeDtypeStruct((n_tok, D), kv_dtype)],
        compiler_params=_cparams(1),
        name="inproj_odd",
    )(*args)


def _rope_tables(n_tok):
    quarter = DA_DQK // 4
    tok = np.arange(n_tok)
    pos = np.stack([tok // GRID_W, tok % GRID_W], axis=1).astype(np.float32)
    inv = (np.float32(ROPE_BASE) ** (-np.arange(quarter, dtype=np.float32) / np.float32(quarter))).astype(np.float32)
    lane = np.arange(HEAD)
    ang = (pos[:, (lane // 32) % 2] * inv[lane % quarter][None, :]).astype(np.float32)
    sign = np.where((lane % 32) < quarter, -1.0, 1.0).astype(np.float32)
    return jnp.asarray(np.cos(ang), F32), jnp.asarray(np.sin(ang) * sign[None, :], F32)


def _lambda(lam_ref, lam_init):
    lp = lam_ref[...]
    return (jnp.exp(jnp.sum(lp[0:1] * lp[1:2], axis=-1, keepdims=True))
            - jnp.exp(jnp.sum(lp[2:3] * lp[3:4], axis=-1, keepdims=True)) + lam_init)


VT_ROWS = HEAD + 16


def _qt2(qh):
    qt = qh.astype(F32).T
    row = lax.broadcasted_iota(jnp.int32, qt.shape, 0)
    return jnp.concatenate([jnp.where(row < DA_DQK, qt, 0.0), jnp.where(row >= DA_DQK, qt, 0.0)],
                           axis=1).astype(BF16)


def _vt_aug(vh):
    tk = vh.shape[0]
    row = lax.broadcasted_iota(jnp.int32, (VT_ROWS - HEAD, tk), 0)
    return jnp.concatenate([vh.astype(F32).T, (row == 0).astype(F32)], axis=0).astype(BF16)


def _diff_attn_heads(q_ref, k_fn, vt_fn, n_chunks, lam, o_ref, n_seq=1):
    tq = q_ref.shape[0] // n_seq
    items = [(s, h, c) for s in range(n_seq) for h in range(DA_HEADS) for c in range(n_chunks)]
    qt2, state, pending = {}, {}, []

    def finish(s, h, c, st):
        cm = jnp.max(st, axis=0, keepdims=True)
        if c == 0:
            m_new = cm
            acc = _dot(vt_fn(s, h, c), jnp.exp2(st - m_new).astype(BF16))
        else:
            m, acc = state[s, h]
            m_new = jnp.maximum(m, cm)
            acc = acc * jnp.exp2(m - m_new) + _dot(vt_fn(s, h, c), jnp.exp2(st - m_new).astype(BF16))
        state[s, h] = (m_new, acc)
        if c == n_chunks - 1:
            den = acc[HEAD:HEAD + 1, :]
            o_t = acc[0:HEAD, 0:tq] * (1.0 / den[:, 0:tq]) - acc[0:HEAD, tq:] * (lam / den[:, tq:])
            o_ref[s * tq:(s + 1) * tq, h * HEAD:(h + 1) * HEAD] = o_t.T

    for s, h, c in items:
        if c == 0:
            qt2[s, h] = _qt2(q_ref[s * tq:(s + 1) * tq, h * HEAD:(h + 1) * HEAD])
        pending.append((s, h, c, _dot(k_fn(s, h, c), qt2[s, h])))
        if len(pending) > ATTN_LOOKAHEAD:
            finish(*pending.pop(0))
    while pending:
        finish(*pending.pop(0))


def _attn_prompt_kernel(q_ref, k_ref, v_ref, lam_ref, o_ref, *, lam_init, seq):
    lam = _lambda(lam_ref, lam_init)
    rows = lambda s: slice(s * seq, (s + 1) * seq)
    _diff_attn_heads(q_ref, lambda s, h, c: k_ref[rows(s), h * HEAD:(h + 1) * HEAD].astype(BF16),
                     lambda s, h, c: _vt_aug(v_ref[rows(s), h * HEAD:(h + 1) * HEAD]), 1, lam, o_ref,
                     n_seq=q_ref.shape[0] // seq)


def _attn_prompt(q, k, v, lam_p, seq, lam_init):
    n_tok = q.shape[0]
    tok = pl.BlockSpec((PROMPT_SEQS_PER_STEP * seq, D), lambda b: (b, 0))
    return pl.pallas_call(
        functools.partial(_attn_prompt_kernel, lam_init=lam_init, seq=seq),
        grid=(n_tok // (PROMPT_SEQS_PER_STEP * seq),),
        in_specs=[tok, tok, tok, pl.BlockSpec(lam_p.shape, lambda b: (0, 0))],
        out_specs=tok,
        out_shape=jax.ShapeDtypeStruct((n_tok, D), F32),
        compiler_params=_cparams(1),
        name="diff_attn_prompt",
    )(q, k, v, lam_p)


def _attn_sample_kernel(q_ref, k_ref, v_ref, ck_ref, cv_ref, lam_ref, o_ref, kcat, vtcat, *, lam_init, past):
    n_keys = kcat.shape[0]
    chunks = [(c0, min(KEY_CHUNK, n_keys - c0)) for c0 in range(0, n_keys, KEY_CHUNK)]

    @pl.when(pl.program_id(1) == 0)
    def _gather_keys():
        kcat[0:past, :] = ck_ref[...].astype(BF16)
        kcat[past:, :] = k_ref[...]
        for h in range(DA_HEADS):
            sl = slice(h * HEAD, (h + 1) * HEAD)
            for c0 in range(0, n_keys, past):
                src, r0 = (cv_ref, c0) if c0 < past else (v_ref, c0 - past)
                vtcat[h, :, c0:c0 + past] = _vt_aug(src[r0:r0 + past, sl])

    lam = _lambda(lam_ref, lam_init)
    _diff_attn_heads(q_ref, lambda s, h, c: kcat[chunks[c][0]:chunks[c][0] + chunks[c][1], h * HEAD:(h + 1) * HEAD],
                     lambda s, h, c: vtcat[h, :, chunks[c][0]:chunks[c][0] + chunks[c][1]], len(chunks), lam, o_ref)


def _attn_sample(q, k, v, ck, cv, lam_p, n_seq, seq, lam_init, tq=128):
    past = ck.shape[1]
    qb = pl.BlockSpec((tq, D), lambda b, i: (b * (seq // tq) + i, 0))
    kv = pl.BlockSpec((seq, D), lambda b, i: (b, 0))
    cache = pl.BlockSpec((None, past, D), lambda b, i: (b, 0, 0))
    return pl.pallas_call(
        functools.partial(_attn_sample_kernel, lam_init=lam_init, past=past),
        grid=(n_seq, seq // tq),
        in_specs=[qb, kv, kv, cache, cache, pl.BlockSpec(lam_p.shape, lambda b, i: (0, 0))],
        out_specs=qb,
        out_shape=jax.ShapeDtypeStruct((n_seq * seq, D), F32),
        scratch_shapes=[pltpu.VMEM((past + seq, D), BF16), pltpu.VMEM((DA_HEADS, VT_ROWS, past + seq), BF16)],
        compiler_params=_cparams(2),
        name="diff_attn_sample",
    )(q, k, v, ck, cv, lam_p)


def kernel(x_prompt, x_sample, c, c_ctx, cache_attn_k, cache_attn_v, state_mlstm_C, state_mlstm_n, state_mlstm_m,
           state_hgrn_S, ada_w, ada_b, norm_mix_g, norm_ffn_g, ev_w_in, ev_gate_b, ev_lb_logits, ml_norm_g,
           hg_norm_g, ev_w_out, od_w_in, od_lambda, da_norm_g, od_w_out, ffn_w1, ffn_w3, ffn_w2, final_norm_g):
    assert DEPTH % 2 == 0
    n_p, s_p, _ = x_prompt.shape
    n_s, s_s, _ = x_sample.shape
    past = cache_attn_k.shape[2]
    assert s_p == SEG and s_s % SEG == 0 and s_s % PROJ_TILE == 0 and (n_p * s_p) % PROJ_TILE == 0
    assert PROJ_TILE % TOKEN_TILE == 0
    xp = x_prompt.astype(F32).reshape(n_p * s_p, D)
    xs = x_sample.astype(F32).reshape(n_s * s_s, D)

    cond8 = jnp.zeros((8, D), F32).at[0].set(c_ctx.astype(F32)).at[1:1 + n_s].set(c.astype(F32))
    mod = _modulation(cond8, ada_w.astype(F32), ada_b.astype(F32)).reshape(DEPTH, 8, 6, D)
    streams = [(xp, None), (xs, s_s)]
    outputs = {}
    w1_all, w3_all, w2_all = ffn_w1.astype(BF16), ffn_w3.astype(BF16), ffn_w2.astype(BF16)

    for l in range(DEPTH):
        mix_gain = norm_mix_g[l].astype(F32).reshape(1, D)
        ffn_gain = norm_ffn_g[l].astype(F32).reshape(1, D)
        if l % 2 == 0:
            e = l // 2
            g0 = sum(EV_SIZES[:4])
            w_e = ev_w_in[e].astype(BF16)
            w_in = jnp.concatenate([w_e[:, :g0], w_e[:, g0 + 16:],
                                    jnp.pad(w_e[:, g0:g0 + 16], ((0, 0), (0, HEAD - 16)))], axis=1)
            gate_b = jnp.pad(ev_gate_b[e].astype(F32), (0, HEAD - 16)).reshape(1, HEAD)
            norm_gain = jnp.concatenate([ml_norm_g[e], hg_norm_g[e]]).astype(F32).reshape(1, D)
            w_out = ev_w_out[e].astype(BF16)
            mixed = []
            for si, (x, cond_tok) in enumerate(streams):
                qkv, gate, hqi, hgf, og = _inproj_even(x, mod[l], cond_tok, mix_gain, w_in, gate_b,
                                                       ev_lb_logits.astype(F32), e)
                if si == 0:
                    *hs, c_new, n_new, m_new, s_new = _scan(qkv, gate, hqi, hgf, n_p, s_p // SEG, emit_state=True)
                    outputs.setdefault("C", []).append(c_new)
                    outputs.setdefault("n", []).append(n_new)
                    outputs.setdefault("m", []).append(m_new[:, 0, 0:8].reshape(n_p, 2, 4))
                    outputs.setdefault("S", []).append(s_new)
                else:
                    init = (state_mlstm_C[:, e].astype(F32), state_mlstm_n[:, e].astype(F32),
                            jnp.pad(state_mlstm_m[:, e].astype(F32).reshape(n_s, 1, 8), ((0, 0), (0, 0), (0, HEAD - 8))),
                            state_hgrn_S[:, e].astype(F32))
                    hs = _scan(qkv, gate, hqi, hgf, n_s, s_s // SEG, init=init)
                mixed.append((list(hs), og, x))
        else:
            o = l // 2
            lam_init = 0.8 - 0.6 * math.exp(-0.3 * l)
            w_in = od_w_in[o].astype(BF16)
            norm_gain = (jnp.tile(da_norm_g[o].astype(F32), DA_HEADS) * (1.0 - lam_init)).reshape(1, D)
            w_out = od_w_out[o].astype(BF16)
            lam_p = od_lambda[o].astype(F32)
            mixed = []
            for si, (x, cond_tok) in enumerate(streams):
                if si == 0:
                    q, k, v = _inproj_odd(x, mod[l], cond_tok, mix_gain, w_in, None, F32)
                    outputs.setdefault("k", []).append(k.reshape(n_p, s_p, DA_HEADS, HEAD))
                    outputs.setdefault("v", []).append(v.reshape(n_p, s_p, DA_HEADS, HEAD))
                    att = _attn_prompt(q, k, v, lam_p, s_p, lam_init)
                else:
                    q, k, v = _inproj_odd(x, mod[l], cond_tok, mix_gain, w_in, _rope_tables(s_s), BF16)
                    ck = cache_attn_k[:, o].reshape(n_s, past, D)
                    cv = cache_attn_v[:, o].reshape(n_s, past, D)
                    att = _attn_sample(q, k, v, ck, cv, lam_p, n_s, s_s, lam_init)
                mixed.append(([att], None, x))
        final_gain = final_norm_g.astype(F32).reshape(1, D) if l == DEPTH - 1 else None
        streams = [(_mix_ffn(hs, og, x, mod[l], cond_tok, norm_gain, w_out, ffn_gain, l, w1_all, w3_all, w2_all,
                             final_gain), cond_tok)
                   for (hs, og, x), (_, cond_tok) in zip(mixed, streams)]

    y_prompt = streams[0][0].reshape(n_p, s_p, D)
    y_sample = streams[1][0].reshape(n_s, s_s, D)
    stack = lambda name: outputs[name][0][:, None] if len(outputs[name]) == 1 else jnp.stack(outputs[name], axis=1)
    return (y_prompt, y_sample, stack("k"), stack("v"), stack("C"), stack("n"), stack("m"), stack("S"))
```

```python
import functools
import math

import jax
import jax.numpy as jnp
import numpy as np
from jax import lax
from jax.experimental import pallas as pl
from jax.experimental.pallas import tpu as pltpu

F32 = jnp.float32
BF16 = jnp.bfloat16

D = 1024
DEPTH = 2
GRID_W = 64
ML_HEADS = 4
HG_HEADS = 4
HEAD = 128
DA_HEADS = 8
DA_DQK = 64
ROPE_BASE = 10000.0
LOG2E = math.log2(math.e)
EPS = 1e-6
D_FF = ((8 * D // 3 + 255) // 256) * 256
EV_SIZES = (512, 512, 512, 512, 16, 512, 512, 512, 512, 512)
EV_COLS = 9 * 512 + 128

SEG = 256
HALF = SEG // 2
TILE_LEVELS = (1, 2, 4, 8, 16, 32, 64)
TOKEN_TILE = 512
PROJ_TILE = 1024
FFN_CHUNK = 256
KEY_CHUNK = 256
PROMPT_SEQS_PER_STEP = 4
ATTN_LOOKAHEAD = 6
VMEM_LIMIT = 56 * 1024 * 1024


def _cparams(n_axes):
    return pltpu.CompilerParams(dimension_semantics=("arbitrary",) * n_axes, vmem_limit_bytes=VMEM_LIMIT)


def _sigmoid(x):
    return 1.0 / (1.0 + jnp.exp(-x))


def _silu(x):
    return x * _sigmoid(x)


def _log_sigmoid(x):
    return jnp.minimum(x, 0.0) - jnp.log(1.0 + jnp.exp(-jnp.abs(x)))


def _dot(a, b):
    return jnp.dot(a, b, preferred_element_type=F32)


def _dot_nt(a, b):
    return lax.dot_general(a, b, (((1,), (1,)), ((), ())), preferred_element_type=F32)


def _norm_mod(x, gain, mod, k):
    ms = jnp.mean(x * x, axis=-1, keepdims=True)
    return x * lax.rsqrt(ms + EPS) * gain * (1.0 + mod[3 * k + 1:3 * k + 2]) + mod[3 * k:3 * k + 1]


def _mod_row(cond_tokens, tm):
    if cond_tokens is None:
        return lambda i: 0
    return lambda i: 1 + i // (cond_tokens // tm)


def _mod_kernel(c_ref, w_ref, b_ref, o_ref):
    s = _silu(c_ref[...]).astype(BF16)
    o_ref[...] = _dot(s, w_ref[...].astype(BF16)) + b_ref[...]


def _modulation(cond8, ada_w, ada_b):
    n_layers = ada_w.shape[0]
    tn = 1536
    return pl.pallas_call(
        _mod_kernel,
        grid=(n_layers, 6 * D // tn),
        in_specs=[pl.BlockSpec((8, D), lambda l, n: (0, 0)),
                  pl.BlockSpec((None, D, tn), lambda l, n: (l, 0, n)),
                  pl.BlockSpec((None, 1, tn), lambda l, n: (l, 0, n))],
        out_specs=pl.BlockSpec((None, 8, tn), lambda l, n: (l, 0, n)),
        out_shape=jax.ShapeDtypeStruct((n_layers, 8, 6 * D), F32),
        compiler_params=_cparams(2),
        name="ada_modulation",
    )(cond8, ada_w, ada_b.reshape(n_layers, 1, 6 * D))


def _inproj_even_body(x, mod, gain, w_ref, gate_b, lb_logits, e_idx, qkv_ref, gate_ref, hqi_ref, hgf_ref, og_ref):
    h = _norm_mod(x, gain, mod, 0).astype(BF16)

    def proj(c0, n):
        return _dot(h, w_ref[:, c0:c0 + n])

    qkv_ref[:, 0:512] = (proj(0, 512) * (HEAD ** -0.5)).astype(BF16)
    qkv_ref[:, 512:1536] = proj(512, 1024).astype(BF16)
    og_ref[:, 0:512] = proj(1536, 512)
    hqi_ref[:, 0:512] = proj(2048, 512).astype(BF16)
    hqi_ref[:, 512:1024] = proj(3584, 512).astype(BF16)
    mx = jnp.max(lb_logits, axis=0, keepdims=True)
    ex = jnp.exp(lb_logits - mx)
    lb = jnp.sum(ex[0:e_idx + 1], axis=0, keepdims=True) / jnp.sum(ex, axis=0, keepdims=True)
    hgf_ref[:, 0:512] = jnp.log(lb + (1.0 - lb) * _sigmoid(proj(2560, 512)))
    hgf_ref[:, 512:1024] = jnp.log(lb + (1.0 - lb) * _sigmoid(proj(3072, 512)))
    og_ref[:, 512:1024] = proj(4096, 512)
    gt = proj(4608, 128) + gate_b
    lane = lax.broadcasted_iota(jnp.int32, gt.shape, 1)
    gate_ref[...] = jnp.where(lane < 8, gt, jnp.where(lane < 16, _log_sigmoid(gt), 0.0))


def _inproj_even_kernel(x_ref, mod_ref, g_ref, w_ref, gb_ref, lbl_ref, qkv_ref, gate_ref, hqi_ref, hgf_ref,
                        og_ref, *, e_idx):
    _inproj_even_body(x_ref[...], mod_ref[...], g_ref[...], w_ref, gb_ref[...], lbl_ref[...], e_idx,
                      qkv_ref, gate_ref, hqi_ref, hgf_ref, og_ref)


def _inproj_even(x, mod_l, cond_tokens, gain, w, gate_b, lb_logits, e_idx):
    n_tok = x.shape[0]
    tm = PROJ_TILE
    tile_mod_row = _mod_row(cond_tokens, tm)
    tok = lambda n: pl.BlockSpec((tm, n), lambda i: (i, 0))
    full = lambda a: pl.BlockSpec(a.shape, lambda i: (0,) * a.ndim, pipeline_mode=pl.Buffered(1))
    return pl.pallas_call(
        functools.partial(_inproj_even_kernel, e_idx=e_idx),
        grid=(n_tok // tm,),
        in_specs=[tok(D), pl.BlockSpec((None, 6, D), lambda i: (tile_mod_row(i), 0, 0)),
                  full(gain), full(w), full(gate_b), full(lb_logits)],
        out_specs=[tok(1536), tok(128), tok(1024), tok(1024), tok(1024)],
        out_shape=[jax.ShapeDtypeStruct((n_tok, 1536), BF16), jax.ShapeDtypeStruct((n_tok, 128), F32),
                   jax.ShapeDtypeStruct((n_tok, 1024), BF16), jax.ShapeDtypeStruct((n_tok, 1024), F32),
                   jax.ShapeDtypeStruct((n_tok, 1024), F32)],
        compiler_params=_cparams(1),
        name="inproj_even",
    )(x, mod_l, gain, w, gate_b, lb_logits)


def _shift_rows(x, k, fill, up):
    n = x.shape[0]
    if k % 8 == 0:
        pad = jnp.full((k,) + x.shape[1:], 0.0 if fill is None else fill, x.dtype)
        return jnp.concatenate([x[k:], pad], axis=0) if up else jnp.concatenate([pad, x[:n - k]], axis=0)
    y = pltpu.roll(x, (n - k) if up else k, 0)
    if fill is None:
        return y
    row = lax.broadcasted_iota(jnp.int32, x.shape, 0)
    return jnp.where(row >= n - k, fill, y) if up else jnp.where(row < k, fill, y)


def _cummax_rows(x, rev):
    k = 1
    while k < x.shape[0]:
        x = jnp.maximum(x, _shift_rows(x, k, -jnp.inf, up=rev))
        k *= 2
    return x


def _cumsum_rows(tri, x):
    hi = x.astype(BF16)
    r1 = x - hi.astype(F32)
    mid = r1.astype(BF16)
    lo = (r1 - mid.astype(F32)).astype(BF16)
    return _dot(tri, hi) + _dot(tri, mid) + _dot(tri, lo)


def _hgrn_tile(q, kk, f, a, q_b, rev, mask_sc, d, up_small):
    ops = []
    bm = a
    for li, m in enumerate(TILE_LEVELS):
        if m < 8:
            qrole = jnp.logical_not(up_small[li]) if rev else up_small[li]
            if m == 1:
                r = jnp.where(qrole, q * f, kk)
            else:
                x = jnp.where(qrole, _shift_rows(bm, m, None, up=rev), bm)
                r = jnp.where(qrole, q, kk) * jnp.exp(jnp.where(qrole, a - x, x - a))
            if 2 * m < 8:
                bm = jnp.where(qrole, bm, _shift_rows(bm, m, None, up=not rev))
        else:
            pieces = []
            for b0 in range(0, HALF, 2 * m):
                lo, up = slice(b0, b0 + m), slice(b0 + m, b0 + 2 * m)
                if rev:
                    ref = a[b0 + m:b0 + m + 1]
                    pieces += [q[lo] * jnp.exp(a[lo] - ref), kk[up] * jnp.exp(ref - a[up])]
                else:
                    ref = a[b0 + m - 1:b0 + m]
                    pieces += [kk[lo] * jnp.exp(ref - a[lo]), q[up] * jnp.exp(a[up] - ref)]
            r = jnp.concatenate(pieces, axis=0)
        ops.append(r.astype(BF16))
    p = _dot_nt(q_b, kk.astype(BF16)).astype(BF16) * mask_sc[d, len(TILE_LEVELS)]
    for li, rb in enumerate(ops):
        p = p + _dot_nt(rb, rb).astype(BF16) * mask_sc[d, li]
    return p


def _build_scan_constants(tri_sc, mask_sc):
    row = lax.broadcasted_iota(jnp.int32, (SEG, SEG), 0)
    col = lax.broadcasted_iota(jnp.int32, (SEG, SEG), 1)
    tri_sc[0] = (col <= row).astype(BF16)
    tri_sc[1] = (col >= row).astype(BF16)
    trow = lax.broadcasted_iota(jnp.int32, (HALF, HALF), 0)
    tcol = lax.broadcasted_iota(jnp.int32, (HALF, HALF), 1)
    for li, m in enumerate(TILE_LEVELS):
        sh = m.bit_length() - 1
        same = (trow >> (sh + 1)) == (tcol >> (sh + 1))
        t_up = ((trow >> sh) & 1) == 1
        s_up = ((tcol >> sh) & 1) == 1
        mask_sc[0, li] = (same & t_up & jnp.logical_not(s_up)).astype(BF16)
        mask_sc[1, li] = (same & s_up & jnp.logical_not(t_up)).astype(BF16)
    eye = (trow == tcol).astype(BF16)
    mask_sc[0, len(TILE_LEVELS)] = eye
    mask_sc[1, len(TILE_LEVELS)] = eye


def _scan_units(dir_refs, write_h, tri_sc, mask_sc, m_in, state, emit, result):
    row = lax.broadcasted_iota(jnp.int32, (SEG, SEG), 0)
    col = lax.broadcasted_iota(jnp.int32, (SEG, SEG), 1)
    lane128 = lax.broadcasted_iota(jnp.int32, (SEG, HEAD), 1)
    e0 = (lane128 == 0).astype(BF16)
    row_half = lax.broadcasted_iota(jnp.int32, (HALF, HEAD), 0)
    up_masks = [((row_half >> (m.bit_length() - 1)) & 1) == 1 for m in TILE_LEVELS if m < 8]
    tmasks = (col <= row, col >= row)
    use_state = state is not None
    want_state = use_state or emit is not None
    m_out_rows = []

    for d in range(2):
        qkv_ref, gate_ref, hqi_ref, hgf_ref = dir_refs[d]
        rev = d == 1
        last = 0 if rev else SEG - 1
        tri = tri_sc[d]

        slab = gate_ref[...]
        b_al = pltpu.roll(_cumsum_rows(tri, slab), HEAD - 8, 1)
        u = slab - b_al
        mx = jnp.maximum(_cummax_rows(u, rev), m_in)
        w_inter = jnp.exp(m_in - mx)
        e_den = jnp.exp(-(b_al + mx))
        mx_last = mx[last:last + 1, :]
        m_out_rows.append(b_al[last:last + 1, :] + mx_last)
        decay = jnp.exp(m_in - mx_last)
        wg = jnp.exp(u - mx_last)
        u_t = u.T

        for h in range(ML_HEADS):
            c = 4 * d + h
            q = qkv_ref[:, h * HEAD:(h + 1) * HEAD]
            k = qkv_ref[:, 512 + h * HEAD:512 + (h + 1) * HEAD]
            v = qkv_ref[:, 1024 + h * HEAD:1024 + (h + 1) * HEAD]
            dm = jnp.where(tmasks[d], jnp.exp(u_t[c:c + 1, :] - mx[:, c:c + 1]), 0.0)
            s = (_dot_nt(q, k) * dm).astype(BF16)
            v_aug = jnp.concatenate([v, e0], axis=1)
            numden = _dot(s, v_aug)
            if use_state:
                numden = numden + w_inter[:, c:c + 1] * _dot(q, state[0][d, h].astype(BF16))
            den = jnp.maximum(jnp.abs(numden[:, HEAD:HEAD + 1]), e_den[:, c:c + 1])
            write_h(d, h * HEAD, numden[:, 0:HEAD] / den)
            if want_state:
                kw_t = (k.astype(F32) * wg[:, c:c + 1]).T.astype(BF16)
                upd = _dot(kw_t, v_aug)
                if use_state:
                    upd = upd + decay[:, c:c + 1] * state[0][d, h]
                    state[0][d, h] = upd
                if emit is not None:
                    emit("ml", d, h, upd)
            yield

        lf_all = hgf_ref[:, 512 * d:512 * (d + 1)]
        a_all = _cumsum_rows(tri, lf_all)
        f_all = jnp.exp(lf_all)
        kk_all = 1.0 - f_all
        for h in range(HG_HEADS):
            sl = slice(h * HEAD, (h + 1) * HEAD)
            a = a_all[:, sl]
            kk = kk_all[:, sl]
            f = f_all[:, sl]
            q_b = hqi_ref[:, h * HEAD:(h + 1) * HEAD]
            i_b = hqi_ref[:, 512 + h * HEAD:512 + (h + 1) * HEAD]
            q = q_b.astype(F32)
            qh, kh = (0, 1) if rev else (1, 0)
            a_q, a_k = a[qh * HALF:(qh + 1) * HALF], a[kh * HALF:(kh + 1) * HALF]
            a_mid = a[HALF:HALF + 1] if rev else a[HALF - 1:HALF]
            r_q = (q[qh * HALF:(qh + 1) * HALF] * jnp.exp(a_q - a_mid)).astype(BF16)
            r_k = (kk[kh * HALF:(kh + 1) * HALF] * jnp.exp(a_mid - a_k)).astype(BF16)
            p_cross = _dot_nt(r_q, r_k).astype(BF16)
            p_tiles = []
            for t in range(2):
                rows = slice(t * HALF, (t + 1) * HALF)
                p_tiles.append(_hgrn_tile(q[rows], kk[rows], f[rows], a[rows], q_b[rows], rev, mask_sc, d, up_masks))
            i_lo, i_hi = i_b[0:HALF], i_b[HALF:SEG]
            if rev:
                o = jnp.concatenate([_dot(p_tiles[0], i_lo) + _dot(p_cross, i_hi), _dot(p_tiles[1], i_hi)], axis=0)
            else:
                o = jnp.concatenate([_dot(p_tiles[0], i_lo), _dot(p_cross, i_lo) + _dot(p_tiles[1], i_hi)], axis=0)
            if use_state:
                st = state[1][d, h]
                o = o + _dot_nt((q * jnp.exp(a)).astype(BF16), st.astype(BF16))
            write_h(d, 512 + h * HEAD, o)
            if want_state:
                a_l = a[last:last + 1, :]
                kd = (kk * jnp.exp(a_l - a)).astype(BF16)
                st_new = _dot(i_b.astype(F32).T.astype(BF16), kd)
                if use_state:
                    st_new = st_new + st * jnp.exp(a_l)
                    state[1][d, h] = st_new
                if emit is not None:
                    emit("hg", d, h, st_new)
            yield

    lane_row = lax.broadcasted_iota(jnp.int32, (1, HEAD), 1)
    result["m"] = jnp.where(lane_row < 4, m_out_rows[0], jnp.where(lane_row < 8, m_out_rows[1], 0.0))


def _emit_states(kind, d, h, value, c_ref, n_ref, s_ref):
    if kind == "ml":
        c_ref[d, h] = value[:, 0:HEAD]
        n_ref[d, h:h + 1, :] = value[:, HEAD:2 * HEAD].T[0:1, :]
    else:
        s_ref[d, h] = value.T


def _scan_kernel(*refs, has_init, emit_state, nseg):
    single = nseg == 1
    n_tok_in = 4 if single else 8
    n_h = 1 if single else 2
    n_in = n_tok_in + (4 if has_init else 0)
    n_out = n_h + (4 if emit_state else 0)
    ins, outs, scr = refs[:n_in], refs[n_in:n_in + n_out], refs[n_in + n_out:]
    dir_refs = (ins[0:4], ins[0:4] if single else ins[4:8])
    st_outs = outs[n_h:]
    tri_sc, mask_sc, caug_sc, st_sc, m_sc = scr
    use_state = has_init or nseg > 1
    b_id, j = pl.program_id(0), pl.program_id(1)

    @pl.when(jnp.logical_and(b_id == 0, j == 0))
    def _build_constants():
        _build_scan_constants(tri_sc, mask_sc)

    if use_state:
        @pl.when(j == 0)
        def _init_state():
            if has_init:
                c0_ref, n0_ref, m0_ref, s0_ref = ins[n_tok_in:n_tok_in + 4]
                for d in range(2):
                    for h in range(ML_HEADS):
                        caug_sc[d, h, :, 0:HEAD] = c0_ref[d, h]
                        caug_sc[d, h, :, HEAD:2 * HEAD] = jnp.broadcast_to(n0_ref[d, h:h + 1, :], (HEAD, HEAD)).T
                        st_sc[d, h] = s0_ref[d, h].T
                m_sc[...] = jnp.broadcast_to(m0_ref[...], m_sc.shape)
            else:
                caug_sc[...] = jnp.zeros(caug_sc.shape, F32)
                st_sc[...] = jnp.zeros(st_sc.shape, F32)
                m_sc[...] = jnp.zeros(m_sc.shape, F32)

    def write_h(d, c0, value):
        if single and d == 1:
            outs[0][:, c0:c0 + HEAD] += value
        else:
            outs[0 if single else d][:, c0:c0 + HEAD] = value

    def emit(kind, d, h, value):
        if single:
            _emit_states(kind, d, h, value, st_outs[0], st_outs[1], st_outs[3])
        else:
            pl.when(j == nseg - 1)(lambda: _emit_states(kind, d, h, value, st_outs[0], st_outs[1], st_outs[3]))

    m_in = m_sc[0:1, :] if use_state else jnp.zeros((1, HEAD), F32)
    result = {}
    for _ in _scan_units(dir_refs, write_h, tri_sc, mask_sc, m_in, (caug_sc, st_sc) if use_state else None,
                         emit if emit_state else None, result):
        pass
    m_new = result["m"]
    if use_state:
        m_sc[...] = jnp.broadcast_to(m_new, m_sc.shape)
    if emit_state:
        if single:
            st_outs[2][...] = m_new
        else:
            @pl.when(j == nseg - 1)
            def _emit_m():
                st_outs[2][...] = m_new


def _scan(qkv, gate, hqi, hgf, n_seq, nseg, init=None, emit_state=False):
    n_tok = qkv.shape[0]
    fwd = lambda n: pl.BlockSpec((SEG, n), lambda b, j: (b * nseg + j, 0))
    bwd = lambda n: pl.BlockSpec((SEG, n), lambda b, j: (b * nseg + nseg - 1 - j, 0))
    widths = (1536, 128, 1024, 1024)
    single = nseg == 1
    in_specs = [fwd(n) for n in widths] + ([] if single else [bwd(n) for n in widths])
    args = [qkv, gate, hqi, hgf] + ([] if single else [qkv, gate, hqi, hgf])
    mat = pl.BlockSpec((None, 2, 4, HEAD, HEAD), lambda b, j: (b, 0, 0, 0, 0))
    vec = pl.BlockSpec((None, 2, 4, HEAD), lambda b, j: (b, 0, 0, 0))
    sca = pl.BlockSpec((None, 1, HEAD), lambda b, j: (b, 0, 0))
    if init is not None:
        in_specs += [mat, vec, sca, mat]
        args += list(init)
    out_specs = [fwd(1024)] if single else [fwd(1024), bwd(1024)]
    out_shape = [jax.ShapeDtypeStruct((n_tok, 1024), F32)] * len(out_specs)
    if emit_state:
        out_specs += [mat, vec, sca, mat]
        out_shape += [jax.ShapeDtypeStruct((n_seq, 2, 4, HEAD, HEAD), F32),
                      jax.ShapeDtypeStruct((n_seq, 2, 4, HEAD), F32),
                      jax.ShapeDtypeStruct((n_seq, 1, HEAD), F32),
                      jax.ShapeDtypeStruct((n_seq, 2, 4, HEAD, HEAD), F32)]
    return pl.pallas_call(
        functools.partial(_scan_kernel, has_init=init is not None, emit_state=emit_state, nseg=nseg),
        grid=(n_seq, nseg),
        in_specs=in_specs,
        out_specs=out_specs,
        out_shape=out_shape,
        scratch_shapes=[pltpu.VMEM((2, SEG, SEG), BF16),
                        pltpu.VMEM((2, len(TILE_LEVELS) + 1, HALF, HALF), BF16),
                        pltpu.VMEM((2, 4, HEAD, 2 * HEAD), F32),
                        pltpu.VMEM((2, 4, HEAD, HEAD), F32),
                        pltpu.VMEM((8, HEAD), F32)],
        compiler_params=_cparams(2),
        name="bidir_scan",
    )(*args)


def _mix_ffn_tile(h, og, x, mod, norm_gain, wo_ref, ffn_gain, w1_ref, w3_ref, w2_ref, final_gain, o_ref, chunk):
    parts = []
    for g in range(D // HEAD):
        hs = h[:, g * HEAD:(g + 1) * HEAD]
        parts.append(hs * lax.rsqrt(jnp.mean(hs * hs, axis=-1, keepdims=True) + EPS))
    hn = jnp.concatenate(parts, axis=1) * norm_gain
    if og is not None:
        hn = hn * jnp.concatenate([_sigmoid(og[:, 0:512]), _silu(og[:, 512:1024])], axis=1)
    x = x + mod[2:3, :] * _dot(hn.astype(BF16), wo_ref[...])
    hf = _norm_mod(x, ffn_gain, mod, 1).astype(BF16)
    up = lambda c0: (_dot(hf, w1_ref[:, c0:min(c0 + chunk, D_FF)]), _dot(hf, w3_ref[:, c0:min(c0 + chunk, D_FF)]))
    starts = list(range(0, D_FF, chunk))
    acc = jnp.zeros(x.shape, F32)
    nxt = up(starts[0])
    for i, c0 in enumerate(starts):
        a, b = nxt
        if i + 1 < len(starts):
            nxt = up(starts[i + 1])
        acc = acc + _dot((_silu(a) * b).astype(BF16), w2_ref[c0:min(c0 + chunk, D_FF), :])
    y = x + mod[5:6, :] * acc
    if final_gain is not None:
        y = y * lax.rsqrt(jnp.mean(y * y, axis=-1, keepdims=True) + EPS) * final_gain
    o_ref[...] = y


def _mix_ffn_kernel(*refs, n_h, gated, final, chunk):
    h_refs = refs[:n_h]
    pos = n_h
    og_ref = refs[pos] if gated else None
    pos += 1 if gated else 0
    x_ref, mod_ref, ng_ref, wo_ref, g_ref, w1_ref, w3_ref, w2_ref = refs[pos:pos + 8]
    fg_ref = refs[pos + 8] if final else None
    o_ref = refs[-1]
    h = h_refs[0][...]
    for r in h_refs[1:]:
        h = h + r[...]
    _mix_ffn_tile(h, og_ref[...] if gated else None, x_ref[...], mod_ref[...], ng_ref[...], wo_ref, g_ref[...],
                  w1_ref, w3_ref, w2_ref, fg_ref[...] if final else None, o_ref, chunk)


def _mix_ffn(hs, og, x, mod_l, cond_tokens, norm_gain, w_out, ffn_gain, layer, w1, w3, w2, final_gain=None):
    n_tok = x.shape[0]
    tm = TOKEN_TILE if cond_tokens is None else 256
    tile_mod_row = _mod_row(cond_tokens, tm)
    tok = pl.BlockSpec((tm, D), lambda i: (i, 0))
    full = lambda a: pl.BlockSpec(a.shape, lambda i: (0,) * a.ndim, pipeline_mode=pl.Buffered(1))
    of_layer = lambda a: pl.BlockSpec((None,) + a.shape[1:], lambda i: (layer, 0, 0), pipeline_mode=pl.Buffered(1))
    gated = og is not None
    final = final_gain is not None
    consts = [norm_gain, w_out, ffn_gain, w1, w3, w2] + ([final_gain] if final else [])
    args = list(hs) + ([og] if gated else []) + [x, mod_l] + consts
    in_specs = [tok] * (len(hs) + (1 if gated else 0) + 1)
    in_specs += [pl.BlockSpec((None, 6, D), lambda i: (tile_mod_row(i), 0, 0))]
    in_specs += [of_layer(a) if a.ndim == 3 else full(a) for a in consts]
    return pl.pallas_call(
        functools.partial(_mix_ffn_kernel, n_h=len(hs), gated=gated, final=final, chunk=FFN_CHUNK if layer == 0 else 512),
        grid=(n_tok // tm,),
        in_specs=in_specs,
        out_specs=tok,
        out_shape=jax.ShapeDtypeStruct((n_tok, D), F32),
        compiler_params=_cparams(1),
        name="mix_ffn",
    )(*args)


def _rope(x, cos, sin_signed):
    lane = lax.broadcasted_iota(jnp.int32, (x.shape[0], HEAD), 1)
    first = (lane & 16) == 0
    parts = []
    for h in range(DA_HEADS):
        xh = x[:, h * HEAD:(h + 1) * HEAD]
        partner = jnp.where(first, pltpu.roll(xh, HEAD - 16, 1), pltpu.roll(xh, 16, 1))
        parts.append(xh * cos + partner * sin_signed)
    return jnp.concatenate(parts, axis=1)


def _inproj_odd_kernel(*refs, rope):
    x_ref, mod_ref, g_ref, w_ref = refs[:4]
    q_ref, k_ref, v_ref = refs[-3:]
    h = _norm_mod(x_ref[...], g_ref[...], mod_ref[...], 0).astype(BF16)
    q = _dot(h, w_ref[:, 0:D])
    k = _dot(h, w_ref[:, D:2 * D])
    v = _dot(h, w_ref[:, 2 * D:3 * D])
    if rope:
        cos, sin_signed = refs[4][...], refs[5][...]
        q = _rope(q, cos, sin_signed)
        k = _rope(k, cos, sin_signed)
    q_ref[...] = (q * (DA_DQK ** -0.5 * LOG2E)).astype(q_ref.dtype)
    k_ref[...] = k.astype(k_ref.dtype)
    v_ref[...] = v.astype(v_ref.dtype)


def _inproj_odd(x, mod_l, cond_tokens, gain, w, rope_tables, kv_dtype):
    n_tok = x.shape[0]
    tm = PROJ_TILE
    tile_mod_row = _mod_row(cond_tokens, tm)
    tok = pl.BlockSpec((tm, D), lambda i: (i, 0))
    full = lambda a: pl.BlockSpec(a.shape, lambda i: (0,) * a.ndim, pipeline_mode=pl.Buffered(1))
    args = [x, mod_l, gain, w]
    in_specs = [tok, pl.BlockSpec((None, 6, D), lambda i: (tile_mod_row(i), 0, 0)), full(gain), full(w)]
    if rope_tables is not None:
        tiles_per_seq = rope_tables[0].shape[0] // tm
        args += list(rope_tables)
        in_specs += [pl.BlockSpec((tm, HEAD), lambda i: (i % tiles_per_seq, 0))] * 2
    return pl.pallas_call(
        functools.partial(_inproj_odd_kernel, rope=rope_tables is not None),
        grid=(n_tok // tm,),
        in_specs=in_specs,
        out_specs=[tok, tok, tok],
        out_shape=[jax.ShapeDtypeStruct((n_tok, D), BF16), jax.ShapeDtypeStruct((n_tok, D), kv_dtype),
                   jax.ShapeDtypeStruct((n_tok, D), kv_dtype)],
        compiler_params=_cparams(1),
        name="inproj_odd",
    )(*args)


def _rope_tables(n_tok):
    quarter = DA_DQK // 4
    tok = np.arange(n_tok)
    pos = np.stack([tok // GRID_W, tok % GRID_W], axis=1).astype(np.float32)
    inv = (np.float32(ROPE_BASE) ** (-np.arange(quarter, dtype=np.float32) / np.float32(quarter))).astype(np.float32)
    lane = np.arange(HEAD)
    ang = (pos[:, (lane // 32) % 2] * inv[lane % quarter][None, :]).astype(np.float32)
    sign = np.where((lane % 32) < quarter, -1.0, 1.0).astype(np.float32)
    return jnp.asarray(np.cos(ang), F32), jnp.asarray(np.sin(ang) * sign[None, :], F32)


def _lambda(lam_ref, lam_init):
    lp = lam_ref[...]
    return (jnp.exp(jnp.sum(lp[0:1] * lp[1:2], axis=-1, keepdims=True))
            - jnp.exp(jnp.sum(lp[2:3] * lp[3:4], axis=-1, keepdims=True)) + lam_init)


VT_ROWS = HEAD + 16


def _qt2(qh):
    qt = qh.astype(F32).T
    row = lax.broadcasted_iota(jnp.int32, qt.shape, 0)
    return jnp.concatenate([jnp.where(row < DA_DQK, qt, 0.0), jnp.where(row >= DA_DQK, qt, 0.0)],
                           axis=1).astype(BF16)


def _vt_aug(vh):
    tk = vh.shape[0]
    row = lax.broadcasted_iota(jnp.int32, (VT_ROWS - HEAD, tk), 0)
    return jnp.concatenate([vh.astype(F32).T, (row == 0).astype(F32)], axis=0).astype(BF16)


def _diff_attn_heads(q_ref, k_fn, vt_fn, n_chunks, lam, o_ref, n_seq=1):
    tq = q_ref.shape[0] // n_seq
    items = [(s, h, c) for s in range(n_seq) for h in range(DA_HEADS) for c in range(n_chunks)]
    qt2, state, pending = {}, {}, []

    def finish(s, h, c, st):
        cm = jnp.max(st, axis=0, keepdims=True)
        if c == 0:
            m_new = cm
            acc = _dot(vt_fn(s, h, c), jnp.exp2(st - m_new).astype(BF16))
        else:
            m, acc = state[s, h]
            m_new = jnp.maximum(m, cm)
            acc = acc * jnp.exp2(m - m_new) + _dot(vt_fn(s, h, c), jnp.exp2(st - m_new).astype(BF16))
        state[s, h] = (m_new, acc)
        if c == n_chunks - 1:
            den = acc[HEAD:HEAD + 1, :]
            o_t = acc[0:HEAD, 0:tq] * (1.0 / den[:, 0:tq]) - acc[0:HEAD, tq:] * (lam / den[:, tq:])
            o_ref[s * tq:(s + 1) * tq, h * HEAD:(h + 1) * HEAD] = o_t.T

    for s, h, c in items:
        if c == 0:
            qt2[s, h] = _qt2(q_ref[s * tq:(s + 1) * tq, h * HEAD:(h + 1) * HEAD])
        pending.append((s, h, c, _dot(k_fn(s, h, c), qt2[s, h])))
        if len(pending) > ATTN_LOOKAHEAD:
            finish(*pending.pop(0))
    while pending:
        finish(*pending.pop(0))


def _attn_prompt_kernel(q_ref, k_ref, v_ref, lam_ref, o_ref, *, lam_init, seq):
    lam = _lambda(lam_ref, lam_init)
    rows = lambda s: slice(s * seq, (s + 1) * seq)
    _diff_attn_heads(q_ref, lambda s, h, c: k_ref[rows(s), h * HEAD:(h + 1) * HEAD].astype(BF16),
                     lambda s, h, c: _vt_aug(v_ref[rows(s), h * HEAD:(h + 1) * HEAD]), 1, lam, o_ref,
                     n_seq=q_ref.shape[0] // seq)


def _attn_prompt(q, k, v, lam_p, seq, lam_init):
    n_tok = q.shape[0]
    tok = pl.BlockSpec((PROMPT_SEQS_PER_STEP * seq, D), lambda b: (b, 0))
    return pl.pallas_call(
        functools.partial(_attn_prompt_kernel, lam_init=lam_init, seq=seq),
        grid=(n_tok // (PROMPT_SEQS_PER_STEP * seq),),
        in_specs=[tok, tok, tok, pl.BlockSpec(lam_p.shape, lambda b: (0, 0))],
        out_specs=tok,
        out_shape=jax.ShapeDtypeStruct((n_tok, D), F32),
        compiler_params=_cparams(1),
        name="diff_attn_prompt",
    )(q, k, v, lam_p)


def _attn_sample_kernel(q_ref, k_ref, v_ref, ck_ref, cv_ref, lam_ref, o_ref, kcat, vtcat, *, lam_init, past):
    n_keys = kcat.shape[0]
    chunks = [(c0, min(KEY_CHUNK, n_keys - c0)) for c0 in range(0, n_keys, KEY_CHUNK)]

    @pl.when(pl.program_id(1) == 0)
    def _gather_keys():
        kcat[0:past, :] = ck_ref[...].astype(BF16)
        kcat[past:, :] = k_ref[...]
        for h in range(DA_HEADS):
            sl = slice(h * HEAD, (h + 1) * HEAD)
            for c0 in range(0, n_keys, past):
                src, r0 = (cv_ref, c0) if c0 < past else (v_ref, c0 - past)
                vtcat[h, :, c0:c0 + past] = _vt_aug(src[r0:r0 + past, sl])

    lam = _lambda(lam_ref, lam_init)
    _diff_attn_heads(q_ref, lambda s, h, c: kcat[chunks[c][0]:chunks[c][0] + chunks[c][1], h * HEAD:(h + 1) * HEAD],
                     lambda s, h, c: vtcat[h, :, chunks[c][0]:chunks[c][0] + chunks[c][1]], len(chunks), lam, o_ref)


def _attn_sample(q, k, v, ck, cv, lam_p, n_seq, seq, lam_init, tq=128):
    past = ck.shape[1]
    qb = pl.BlockSpec((tq, D), lambda b, i: (b * (seq // tq) + i, 0))
    kv = pl.BlockSpec((seq, D), lambda b, i: (b, 0))
    cache = pl.BlockSpec((None, past, D), lambda b, i: (b, 0, 0))
    return pl.pallas_call(
        functools.partial(_attn_sample_kernel, lam_init=lam_init, past=past),
        grid=(n_seq, seq // tq),
        in_specs=[qb, kv, kv, cache, cache, pl.BlockSpec(lam_p.shape, lambda b, i: (0, 0))],
        out_specs=qb,
        out_shape=jax.ShapeDtypeStruct((n_seq * seq, D), F32),
        scratch_shapes=[pltpu.VMEM((past + seq, D), BF16), pltpu.VMEM((DA_HEADS, VT_ROWS, past + seq), BF16)],
        compiler_params=_cparams(2),
        name="diff_attn_sample",
    )(q, k, v, ck, cv, lam_p)


def kernel(x_prompt, x_sample, c, c_ctx, cache_attn_k, cache_attn_v, state_mlstm_C, state_mlstm_n, state_mlstm_m,
           state_hgrn_S, ada_w, ada_b, norm_mix_g, norm_ffn_g, ev_w_in, ev_gate_b, ev_lb_logits, ml_norm_g,
           hg_norm_g, ev_w_out, od_w_in, od_lambda, da_norm_g, od_w_out, ffn_w1, ffn_w3, ffn_w2, final_norm_g):
    assert DEPTH % 2 == 0
    n_p, s_p, _ = x_prompt.shape
    n_s, s_s, _ = x_sample.shape
    past = cache_attn_k.shape[2]
    assert s_p == SEG and s_s % SEG == 0 and s_s % PROJ_TILE == 0 and (n_p * s_p) % PROJ_TILE == 0
    assert PROJ_TILE % TOKEN_TILE == 0
    xp = x_prompt.astype(F32).reshape(n_p * s_p, D)
    xs = x_sample.astype(F32).reshape(n_s * s_s, D)

    cond8 = jnp.zeros((8, D), F32).at[0].set(c_ctx.astype(F32)).at[1:1 + n_s].set(c.astype(F32))
    mod = _modulation(cond8, ada_w.astype(F32), ada_b.astype(F32)).reshape(DEPTH, 8, 6, D)
    streams = [(xp, None), (xs, s_s)]
    outputs = {}
    w1_all, w3_all, w2_all = ffn_w1.astype(BF16), ffn_w3.astype(BF16), ffn_w2.astype(BF16)

    for l in range(DEPTH):
        mix_gain = norm_mix_g[l].astype(F32).reshape(1, D)
        ffn_gain = norm_ffn_g[l].astype(F32).reshape(1, D)
        if l % 2 == 0:
            e = l // 2
            g0 = sum(EV_SIZES[:4])
            w_e = ev_w_in[e].astype(BF16)
            w_in = jnp.concatenate([w_e[:, :g0], w_e[:, g0 + 16:],
                                    jnp.pad(w_e[:, g0:g0 + 16], ((0, 0), (0, HEAD - 16)))], axis=1)
            gate_b = jnp.pad(ev_gate_b[e].astype(F32), (0, HEAD - 16)).reshape(1, HEAD)
            norm_gain = jnp.concatenate([ml_norm_g[e], hg_norm_g[e]]).astype(F32).reshape(1, D)
            w_out = ev_w_out[e].astype(BF16)
            mixed = []
            for si, (x, cond_tok) in enumerate(streams):
                qkv, gate, hqi, hgf, og = _inproj_even(x, mod[l], cond_tok, mix_gain, w_in, gate_b,
                                                       ev_lb_logits.astype(F32), e)
                if si == 0:
                    *hs, c_new, n_new, m_new, s_new = _scan(qkv, gate, hqi, hgf, n_p, s_p // SEG, emit_state=True)
                    outputs.setdefault("C", []).append(c_new)
                    outputs.setdefault("n", []).append(n_new)
                    outputs.setdefault("m", []).append(m_new[:, 0, 0:8].reshape(n_p, 2, 4))
                    outputs.setdefault("S", []).append(s_new)
                else:
                    init = (state_mlstm_C[:, e].astype(F32), state_mlstm_n[:, e].astype(F32),
                            jnp.pad(state_mlstm_m[:, e].astype(F32).reshape(n_s, 1, 8), ((0, 0), (0, 0), (0, HEAD - 8))),
                            state_hgrn_S[:, e].astype(F32))
                    hs = _scan(qkv, gate, hqi, hgf, n_s, s_s // SEG, init=init)
                mixed.append((list(hs), og, x))
        else:
            o = l // 2
            lam_init = 0.8 - 0.6 * math.exp(-0.3 * l)
            w_in = od_w_in[o].astype(BF16)
            norm_gain = (jnp.tile(da_norm_g[o].astype(F32), DA_HEADS) * (1.0 - lam_init)).reshape(1, D)
            w_out = od_w_out[o].astype(BF16)
            lam_p = od_lambda[o].astype(F32)
            mixed = []
            for si, (x, cond_tok) in enumerate(streams):
                if si == 0:
                    q, k, v = _inproj_odd(x, mod[l], cond_tok, mix_gain, w_in, None, F32)
                    outputs.setdefault("k", []).append(k.reshape(n_p, s_p, DA_HEADS, HEAD))
                    outputs.setdefault("v", []).append(v.reshape(n_p, s_p, DA_HEADS, HEAD))
                    att = _attn_prompt(q, k, v, lam_p, s_p, lam_init)
                else:
                    q, k, v = _inproj_odd(x, mod[l], cond_tok, mix_gain, w_in, _rope_tables(s_s), BF16)
                    ck = cache_attn_k[:, o].reshape(n_s, past, D)
                    cv = cache_attn_v[:, o].reshape(n_s, past, D)
                    att = _attn_sample(q, k, v, ck, cv, lam_p, n_s, s_s, lam_init)
                mixed.append(([att], None, x))
        final_gain = final_norm_g.astype(F32).reshape(1, D) if l == DEPTH - 1 else None
        streams = [(_mix_ffn(hs, og, x, mod[l], cond_tok, norm_gain, w_out, ffn_gain, l, w1_all, w3_all, w2_all,
                             final_gain), cond_tok)
                   for (hs, og, x), (_, cond_tok) in zip(mixed, streams)]

    y_prompt = streams[0][0].reshape(n_p, s_p, D)
    y_sample = streams[1][0].reshape(n_s, s_s, D)
    stack = lambda name: outputs[name][0][:, None] if len(outputs[name]) == 1 else jnp.stack(outputs[name], axis=1)
    return (y_prompt, y_sample, stack("k"), stack("v"), stack("C"), stack("n"), stack("m"), stack("S"))
```

```python
import functools
import math

import jax
import jax.numpy as jnp
import numpy as np
from jax import lax
from jax.experimental import pallas as pl
from jax.experimental.pallas import tpu as pltpu

F32 = jnp.float32
BF16 = jnp.bfloat16

D = 1024
DEPTH = 2
GRID_W = 64
ML_HEADS = 4
HG_HEADS = 4
HEAD = 128
DA_HEADS = 8
DA_DQK = 64
ROPE_BASE = 10000.0
LOG2E = math.log2(math.e)
EPS = 1e-6
D_FF = ((8 * D // 3 + 255) // 256) * 256
EV_SIZES = (512, 512, 512, 512, 16, 512, 512, 512, 512, 512)
EV_COLS = 9 * 512 + 128

SEG = 256
HALF = SEG // 2
TILE_LEVELS = (1, 2, 4, 8, 16, 32, 64)
TOKEN_TILE = 512
PROJ_TILE = 1024
FFN_CHUNK = 256
KEY_CHUNK = 256
PROMPT_SEQS_PER_STEP = 4
ATTN_LOOKAHEAD = 6
VMEM_LIMIT = 56 * 1024 * 1024


def _cparams(n_axes):
    return pltpu.CompilerParams(dimension_semantics=("arbitrary",) * n_axes, vmem_limit_bytes=VMEM_LIMIT)


def _sigmoid(x):
    return 1.0 / (1.0 + jnp.exp(-x))


def _silu(x):
    return x * _sigmoid(x)


def _log_sigmoid(x):
    return jnp.minimum(x, 0.0) - jnp.log(1.0 + jnp.exp(-jnp.abs(x)))


def _dot(a, b):
    return jnp.dot(a, b, preferred_element_type=F32)


def _dot_nt(a, b):
    return lax.dot_general(a, b, (((1,), (1,)), ((), ())), preferred_element_type=F32)


def _norm_mod(x, gain, mod, k):
    ms = jnp.mean(x * x, axis=-1, keepdims=True)
    return x * lax.rsqrt(ms + EPS) * gain * (1.0 + mod[3 * k + 1:3 * k + 2]) + mod[3 * k:3 * k + 1]


def _mod_row(cond_tokens, tm):
    if cond_tokens is None:
        return lambda i: 0
    return lambda i: 1 + i // (cond_tokens // tm)


def _mod_kernel(c_ref, w_ref, b_ref, o_ref):
    s = _silu(c_ref[...]).astype(BF16)
    o_ref[...] = _dot(s, w_ref[...].astype(BF16)) + b_ref[...]


def _modulation(cond8, ada_w, ada_b):
    n_layers = ada_w.shape[0]
    tn = 1536
    return pl.pallas_call(
        _mod_kernel,
        grid=(n_layers, 6 * D // tn),
        in_specs=[pl.BlockSpec((8, D), lambda l, n: (0, 0)),
                  pl.BlockSpec((None, D, tn), lambda l, n: (l, 0, n)),
                  pl.BlockSpec((None, 1, tn), lambda l, n: (l, 0, n))],
        out_specs=pl.BlockSpec((None, 8, tn), lambda l, n: (l, 0, n)),
        out_shape=jax.ShapeDtypeStruct((n_layers, 8, 6 * D), F32),
        compiler_params=_cparams(2),
        name="ada_modulation",
    )(cond8, ada_w, ada_b.reshape(n_layers, 1, 6 * D))


def _regroup_kernel(w_ref, o_ref):
    g0 = sum(EV_SIZES[:4])
    rest = sum(EV_SIZES[5:])
    o_ref[:, 0:g0] = w_ref[:, 0:g0].astype(BF16)
    o_ref[:, g0:g0 + rest] = w_ref[:, g0 + 16:g0 + 16 + rest].astype(BF16)
    gates = jnp.concatenate([w_ref[:, g0:g0 + 16], jnp.zeros((w_ref.shape[0], HEAD - 16), F32)], axis=1)
    o_ref[:, g0 + rest:g0 + rest + HEAD] = gates.astype(BF16)


def _regroup_even_weights(ev_w_in, e):
    rows, cols = ev_w_in.shape[1:]
    tr = 256
    return pl.pallas_call(
        _regroup_kernel,
        grid=(rows // tr,),
        in_specs=[pl.BlockSpec((None, tr, cols), lambda i: (e, i, 0))],
        out_specs=pl.BlockSpec((tr, EV_COLS), lambda i: (i, 0)),
        out_shape=jax.ShapeDtypeStruct((rows, EV_COLS), BF16),
        compiler_params=_cparams(1),
        name="regroup_even_weights",
    )(ev_w_in)


def _inproj_even_body(x, mod, gain, w_ref, gate_b, lb_logits, e_idx, qkv_ref, gate_ref, hqi_ref, hgf_ref, og_ref):
    h = _norm_mod(x, gain, mod, 0).astype(BF16)

    def proj(c0, n):
        return _dot(h, w_ref[:, c0:c0 + n])

    qkv_ref[:, 0:512] = (proj(0, 512) * (HEAD ** -0.5)).astype(BF16)
    qkv_ref[:, 512:1536] = proj(512, 1024).astype(BF16)
    og_ref[:, 0:512] = proj(1536, 512)
    hqi_ref[:, 0:512] = proj(2048, 512).astype(BF16)
    hqi_ref[:, 512:1024] = proj(3584, 512).astype(BF16)
    mx = jnp.max(lb_logits, axis=0, keepdims=True)
    ex = jnp.exp(lb_logits - mx)
    lb = jnp.sum(ex[0:e_idx + 1], axis=0, keepdims=True) / jnp.sum(ex, axis=0, keepdims=True)
    hgf_ref[:, 0:512] = jnp.log(lb + (1.0 - lb) * _sigmoid(proj(2560, 512)))
    hgf_ref[:, 512:1024] = jnp.log(lb + (1.0 - lb) * _sigmoid(proj(3072, 512)))
    og_ref[:, 512:1024] = proj(4096, 512)
    gt = proj(4608, 128) + gate_b
    lane = lax.broadcasted_iota(jnp.int32, gt.shape, 1)
    gate_ref[...] = jnp.where(lane < 8, gt, jnp.where(lane < 16, _log_sigmoid(gt), 0.0))


def _inproj_even_kernel(x_ref, mod_ref, g_ref, w_ref, gb_ref, lbl_ref, qkv_ref, gate_ref, hqi_ref, hgf_ref,
                        og_ref, *, e_idx):
    _inproj_even_body(x_ref[...], mod_ref[...], g_ref[...], w_ref, gb_ref[...], lbl_ref[...], e_idx,
                      qkv_ref, gate_ref, hqi_ref, hgf_ref, og_ref)


def _inproj_even(x, mod_l, cond_tokens, gain, w, gate_b, lb_logits, e_idx):
    n_tok = x.shape[0]
    tm = PROJ_TILE
    tile_mod_row = _mod_row(cond_tokens, tm)
    tok = lambda n: pl.BlockSpec((tm, n), lambda i: (i, 0))
    full = lambda a: pl.BlockSpec(a.shape, lambda i: (0,) * a.ndim, pipeline_mode=pl.Buffered(1))
    return pl.pallas_call(
        functools.partial(_inproj_even_kernel, e_idx=e_idx),
        grid=(n_tok // tm,),
        in_specs=[tok(D), pl.BlockSpec((None, 6, D), lambda i: (tile_mod_row(i), 0, 0)),
                  full(gain), full(w), full(gate_b), full(lb_logits)],
        out_specs=[tok(1536), tok(128), tok(1024), tok(1024), tok(1024)],
        out_shape=[jax.ShapeDtypeStruct((n_tok, 1536), BF16), jax.ShapeDtypeStruct((n_tok, 128), F32),
                   jax.ShapeDtypeStruct((n_tok, 1024), BF16), jax.ShapeDtypeStruct((n_tok, 1024), F32),
                   jax.ShapeDtypeStruct((n_tok, 1024), F32)],
        compiler_params=_cparams(1),
        name="inproj_even",
    )(x, mod_l, gain, w, gate_b, lb_logits)


def _shift_rows(x, k, fill, up):
    n = x.shape[0]
    if k % 8 == 0:
        pad = jnp.full((k,) + x.shape[1:], 0.0 if fill is None else fill, x.dtype)
        return jnp.concatenate([x[k:], pad], axis=0) if up else jnp.concatenate([pad, x[:n - k]], axis=0)
    y = pltpu.roll(x, (n - k) if up else k, 0)
    if fill is None:
        return y
    row = lax.broadcasted_iota(jnp.int32, x.shape, 0)
    return jnp.where(row >= n - k, fill, y) if up else jnp.where(row < k, fill, y)


def _cummax_rows(x, rev):
    k = 1
    while k < x.shape[0]:
        x = jnp.maximum(x, _shift_rows(x, k, -jnp.inf, up=rev))
        k *= 2
    return x


def _cumsum_rows(tri, x):
    hi = x.astype(BF16)
    r1 = x - hi.astype(F32)
    mid = r1.astype(BF16)
    lo = (r1 - mid.astype(F32)).astype(BF16)
    return _dot(tri, hi) + _dot(tri, mid) + _dot(tri, lo)


def _hgrn_tile(q, kk, f, a, q_b, rev, mask_sc, d, up_small):
    ops = []
    bm = a
    for li, m in enumerate(TILE_LEVELS):
        if m < 8:
            qrole = jnp.logical_not(up_small[li]) if rev else up_small[li]
            if m == 1:
                r = jnp.where(qrole, q * f, kk)
            else:
                x = jnp.where(qrole, _shift_rows(bm, m, None, up=rev), bm)
                r = jnp.where(qrole, q, kk) * jnp.exp(jnp.where(qrole, a - x, x - a))
            if 2 * m < 8:
                bm = jnp.where(qrole, bm, _shift_rows(bm, m, None, up=not rev))
        else:
            pieces = []
            for b0 in range(0, HALF, 2 * m):
                lo, up = slice(b0, b0 + m), slice(b0 + m, b0 + 2 * m)
                if rev:
                    ref = a[b0 + m:b0 + m + 1]
                    pieces += [q[lo] * jnp.exp(a[lo] - ref), kk[up] * jnp.exp(ref - a[up])]
                else:
                    ref = a[b0 + m - 1:b0 + m]
                    pieces += [kk[lo] * jnp.exp(ref - a[lo]), q[up] * jnp.exp(a[up] - ref)]
            r = jnp.concatenate(pieces, axis=0)
        ops.append(r.astype(BF16))
    p = _dot_nt(q_b, kk.astype(BF16)).astype(BF16) * mask_sc[d, len(TILE_LEVELS)]
    for li, rb in enumerate(ops):
        p = p + _dot_nt(rb, rb).astype(BF16) * mask_sc[d, li]
    return p


def _build_scan_constants(tri_sc, mask_sc):
    row = lax.broadcasted_iota(jnp.int32, (SEG, SEG), 0)
    col = lax.broadcasted_iota(jnp.int32, (SEG, SEG), 1)
    tri_sc[0] = (col <= row).astype(BF16)
    tri_sc[1] = (col >= row).astype(BF16)
    trow = lax.broadcasted_iota(jnp.int32, (HALF, HALF), 0)
    tcol = lax.broadcasted_iota(jnp.int32, (HALF, HALF), 1)
    for li, m in enumerate(TILE_LEVELS):
        sh = m.bit_length() - 1
        same = (trow >> (sh + 1)) == (tcol >> (sh + 1))
        t_up = ((trow >> sh) & 1) == 1
        s_up = ((tcol >> sh) & 1) == 1
        mask_sc[0, li] = (same & t_up & jnp.logical_not(s_up)).astype(BF16)
        mask_sc[1, li] = (same & s_up & jnp.logical_not(t_up)).astype(BF16)
    eye = (trow == tcol).astype(BF16)
    mask_sc[0, len(TILE_LEVELS)] = eye
    mask_sc[1, len(TILE_LEVELS)] = eye


def _scan_segment(dir_refs, write_h, tri_sc, mask_sc, m_in, state, emit):
    row = lax.broadcasted_iota(jnp.int32, (SEG, SEG), 0)
    col = lax.broadcasted_iota(jnp.int32, (SEG, SEG), 1)
    lane128 = lax.broadcasted_iota(jnp.int32, (SEG, HEAD), 1)
    e0 = (lane128 == 0).astype(BF16)
    row_half = lax.broadcasted_iota(jnp.int32, (HALF, HEAD), 0)
    up_masks = [((row_half >> (m.bit_length() - 1)) & 1) == 1 for m in TILE_LEVELS if m < 8]
    tmasks = (col <= row, col >= row)
    use_state = state is not None
    want_state = use_state or emit is not None
    m_out_rows = []

    for d in range(2):
        qkv_ref, gate_ref, hqi_ref, hgf_ref = dir_refs[d]
        rev = d == 1
        last = 0 if rev else SEG - 1
        tri = tri_sc[d]

        slab = gate_ref[...]
        b_al = pltpu.roll(_cumsum_rows(tri, slab), HEAD - 8, 1)
        u = slab - b_al
        mx = jnp.maximum(_cummax_rows(u, rev), m_in)
        w_inter = jnp.exp(m_in - mx)
        e_den = jnp.exp(-(b_al + mx))
        mx_last = mx[last:last + 1, :]
        m_out_rows.append(b_al[last:last + 1, :] + mx_last)
        decay = jnp.exp(m_in - mx_last)
        wg = jnp.exp(u - mx_last)
        u_t = u.T

        for h in range(ML_HEADS):
            c = 4 * d + h
            q = qkv_ref[:, h * HEAD:(h + 1) * HEAD]
            k = qkv_ref[:, 512 + h * HEAD:512 + (h + 1) * HEAD]
            v = qkv_ref[:, 1024 + h * HEAD:1024 + (h + 1) * HEAD]
            dm = jnp.where(tmasks[d], jnp.exp(u_t[c:c + 1, :] - mx[:, c:c + 1]), 0.0)
            s = (_dot_nt(q, k) * dm).astype(BF16)
            v_aug = jnp.concatenate([v, e0], axis=1)
            numden = _dot(s, v_aug)
            if use_state:
                numden = numden + w_inter[:, c:c + 1] * _dot(q, state[0][d, h].astype(BF16))
            den = jnp.maximum(jnp.abs(numden[:, HEAD:HEAD + 1]), e_den[:, c:c + 1])
            write_h(d, h * HEAD, numden[:, 0:HEAD] / den)
            if want_state:
                kw_t = (k.astype(F32) * wg[:, c:c + 1]).T.astype(BF16)
                upd = _dot(kw_t, v_aug)
                if use_state:
                    upd = upd + decay[:, c:c + 1] * state[0][d, h]
                    state[0][d, h] = upd
                if emit is not None:
                    emit("ml", d, h, upd)

        lf_all = hgf_ref[:, 512 * d:512 * (d + 1)]
        a_all = _cumsum_rows(tri, lf_all)
        f_all = jnp.exp(lf_all)
        kk_all = 1.0 - f_all
        for h in range(HG_HEADS):
            sl = slice(h * HEAD, (h + 1) * HEAD)
            a = a_all[:, sl]
            kk = kk_all[:, sl]
            f = f_all[:, sl]
            q_b = hqi_ref[:, h * HEAD:(h + 1) * HEAD]
            i_b = hqi_ref[:, 512 + h * HEAD:512 + (h + 1) * HEAD]
            q = q_b.astype(F32)
            qh, kh = (0, 1) if rev else (1, 0)
            a_q, a_k = a[qh * HALF:(qh + 1) * HALF], a[kh * HALF:(kh + 1) * HALF]
            a_mid = a[HALF:HALF + 1] if rev else a[HALF - 1:HALF]
            r_q = (q[qh * HALF:(qh + 1) * HALF] * jnp.exp(a_q - a_mid)).astype(BF16)
            r_k = (kk[kh * HALF:(kh + 1) * HALF] * jnp.exp(a_mid - a_k)).astype(BF16)
            p_cross = _dot_nt(r_q, r_k).astype(BF16)
            p_tiles = []
            for t in range(2):
                rows = slice(t * HALF, (t + 1) * HALF)
                p_tiles.append(_hgrn_tile(q[rows], kk[rows], f[rows], a[rows], q_b[rows], rev, mask_sc, d, up_masks))
            i_lo, i_hi = i_b[0:HALF], i_b[HALF:SEG]
            if rev:
                o = jnp.concatenate([_dot(p_tiles[0], i_lo) + _dot(p_cross, i_hi), _dot(p_tiles[1], i_hi)], axis=0)
            else:
                o = jnp.concatenate([_dot(p_tiles[0], i_lo), _dot(p_cross, i_lo) + _dot(p_tiles[1], i_hi)], axis=0)
            if use_state:
                st = state[1][d, h]
                o = o + _dot_nt((q * jnp.exp(a)).astype(BF16), st.astype(BF16))
            write_h(d, 512 + h * HEAD, o)
            if want_state:
                a_l = a[last:last + 1, :]
                kd = (kk * jnp.exp(a_l - a)).astype(BF16)
                st_new = _dot(i_b.astype(F32).T.astype(BF16), kd)
                if use_state:
                    st_new = st_new + st * jnp.exp(a_l)
                    state[1][d, h] = st_new
                if emit is not None:
                    emit("hg", d, h, st_new)

    lane_row = lax.broadcasted_iota(jnp.int32, (1, HEAD), 1)
    return jnp.where(lane_row < 4, m_out_rows[0], jnp.where(lane_row < 8, m_out_rows[1], 0.0))


def _emit_states(kind, d, h, value, c_ref, n_ref, s_ref):
    if kind == "ml":
        c_ref[d, h] = value[:, 0:HEAD]
        n_ref[d, h:h + 1, :] = value[:, HEAD:2 * HEAD].T[0:1, :]
    else:
        s_ref[d, h] = value.T


def _scan_kernel(*refs, has_init, emit_state, nseg):
    single = nseg == 1
    n_tok_in = 4 if single else 8
    n_h = 1 if single else 2
    n_in = n_tok_in + (4 if has_init else 0)
    n_out = n_h + (4 if emit_state else 0)
    ins, outs, scr = refs[:n_in], refs[n_in:n_in + n_out], refs[n_in + n_out:]
    dir_refs = (ins[0:4], ins[0:4] if single else ins[4:8])
    st_outs = outs[n_h:]
    tri_sc, mask_sc, caug_sc, st_sc, m_sc = scr
    use_state = has_init or nseg > 1
    b_id, j = pl.program_id(0), pl.program_id(1)

    @pl.when(jnp.logical_and(b_id == 0, j == 0))
    def _build_constants():
        _build_scan_constants(tri_sc, mask_sc)

    if use_state:
        @pl.when(j == 0)
        def _init_state():
            if has_init:
                c0_ref, n0_ref, m0_ref, s0_ref = ins[n_tok_in:n_tok_in + 4]
                for d in range(2):
                    for h in range(ML_HEADS):
                        caug_sc[d, h, :, 0:HEAD] = c0_ref[d, h]
                        caug_sc[d, h, :, HEAD:2 * HEAD] = jnp.broadcast_to(n0_ref[d, h:h + 1, :], (HEAD, HEAD)).T
                        st_sc[d, h] = s0_ref[d, h].T
                m_sc[...] = jnp.broadcast_to(m0_ref[...], m_sc.shape)
            else:
                caug_sc[...] = jnp.zeros(caug_sc.shape, F32)
                st_sc[...] = jnp.zeros(st_sc.shape, F32)
                m_sc[...] = jnp.zeros(m_sc.shape, F32)

    def write_h(d, c0, value):
        if single and d == 1:
            outs[0][:, c0:c0 + HEAD] += value
        else:
            outs[0 if single else d][:, c0:c0 + HEAD] = value

    def emit(kind, d, h, value):
        if single:
            _emit_states(kind, d, h, value, st_outs[0], st_outs[1], st_outs[3])
        else:
            pl.when(j == nseg - 1)(lambda: _emit_states(kind, d, h, value, st_outs[0], st_outs[1], st_outs[3]))

    m_in = m_sc[0:1, :] if use_state else jnp.zeros((1, HEAD), F32)
    m_new = _scan_segment(dir_refs, write_h, tri_sc, mask_sc, m_in, (caug_sc, st_sc) if use_state else None,
                          emit if emit_state else None)
    if use_state:
        m_sc[...] = jnp.broadcast_to(m_new, m_sc.shape)
    if emit_state:
        if single:
            st_outs[2][...] = m_new
        else:
            @pl.when(j == nseg - 1)
            def _emit_m():
                st_outs[2][...] = m_new


def _scan(qkv, gate, hqi, hgf, n_seq, nseg, init=None, emit_state=False):
    n_tok = qkv.shape[0]
    fwd = lambda n: pl.BlockSpec((SEG, n), lambda b, j: (b * nseg + j, 0))
    bwd = lambda n: pl.BlockSpec((SEG, n), lambda b, j: (b * nseg + nseg - 1 - j, 0))
    widths = (1536, 128, 1024, 1024)
    single = nseg == 1
    in_specs = [fwd(n) for n in widths] + ([] if single else [bwd(n) for n in widths])
    args = [qkv, gate, hqi, hgf] + ([] if single else [qkv, gate, hqi, hgf])
    mat = pl.BlockSpec((None, 2, 4, HEAD, HEAD), lambda b, j: (b, 0, 0, 0, 0))
    vec = pl.BlockSpec((None, 2, 4, HEAD), lambda b, j: (b, 0, 0, 0))
    sca = pl.BlockSpec((None, 1, HEAD), lambda b, j: (b, 0, 0))
    if init is not None:
        in_specs += [mat, vec, sca, mat]
        args += list(init)
    out_specs = [fwd(1024)] if single else [fwd(1024), bwd(1024)]
    out_shape = [jax.ShapeDtypeStruct((n_tok, 1024), F32)] * len(out_specs)
    if emit_state:
        out_specs += [mat, vec, sca, mat]
        out_shape += [jax.ShapeDtypeStruct((n_seq, 2, 4, HEAD, HEAD), F32),
                      jax.ShapeDtypeStruct((n_seq, 2, 4, HEAD), F32),
                      jax.ShapeDtypeStruct((n_seq, 1, HEAD), F32),
                      jax.ShapeDtypeStruct((n_seq, 2, 4, HEAD, HEAD), F32)]
    return pl.pallas_call(
        functools.partial(_scan_kernel, has_init=init is not None, emit_state=emit_state, nseg=nseg),
        grid=(n_seq, nseg),
        in_specs=in_specs,
        out_specs=out_specs,
        out_shape=out_shape,
        scratch_shapes=[pltpu.VMEM((2, SEG, SEG), BF16),
                        pltpu.VMEM((2, len(TILE_LEVELS) + 1, HALF, HALF), BF16),
                        pltpu.VMEM((2, 4, HEAD, 2 * HEAD), F32),
                        pltpu.VMEM((2, 4, HEAD, HEAD), F32),
                        pltpu.VMEM((8, HEAD), F32)],
        compiler_params=_cparams(2),
        name="bidir_scan",
    )(*args)


def _mix_ffn_tile(h, og, x, mod, norm_gain, wo_ref, ffn_gain, w1_ref, w3_ref, w2_ref, final_gain, o_ref, chunk):
    parts = []
    for g in range(D // HEAD):
        hs = h[:, g * HEAD:(g + 1) * HEAD]
        parts.append(hs * lax.rsqrt(jnp.mean(hs * hs, axis=-1, keepdims=True) + EPS))
    hn = jnp.concatenate(parts, axis=1) * norm_gain
    if og is not None:
        hn = hn * jnp.concatenate([_sigmoid(og[:, 0:512]), _silu(og[:, 512:1024])], axis=1)
    x = x + mod[2:3, :] * _dot(hn.astype(BF16), wo_ref[...])
    hf = _norm_mod(x, ffn_gain, mod, 1).astype(BF16)
    up = lambda c0: (_dot(hf, w1_ref[:, c0:min(c0 + chunk, D_FF)]), _dot(hf, w3_ref[:, c0:min(c0 + chunk, D_FF)]))
    starts = list(range(0, D_FF, chunk))
    acc = jnp.zeros(x.shape, F32)
    nxt = up(starts[0])
    for i, c0 in enumerate(starts):
        a, b = nxt
        if i + 1 < len(starts):
            nxt = up(starts[i + 1])
        acc = acc + _dot((_silu(a) * b).astype(BF16), w2_ref[c0:min(c0 + chunk, D_FF), :])
    y = x + mod[5:6, :] * acc
    if final_gain is not None:
        y = y * lax.rsqrt(jnp.mean(y * y, axis=-1, keepdims=True) + EPS) * final_gain
    o_ref[...] = y


def _mix_ffn_kernel(*refs, n_h, gated, final, chunk):
    h_refs = refs[:n_h]
    pos = n_h
    og_ref = refs[pos] if gated else None
    pos += 1 if gated else 0
    x_ref, mod_ref, ng_ref, wo_ref, g_ref, w1_ref, w3_ref, w2_ref = refs[pos:pos + 8]
    fg_ref = refs[pos + 8] if final else None
    o_ref = refs[-1]
    h = h_refs[0][...]
    for r in h_refs[1:]:
        h = h + r[...]
    _mix_ffn_tile(h, og_ref[...] if gated else None, x_ref[...], mod_ref[...], ng_ref[...], wo_ref, g_ref[...],
                  w1_ref, w3_ref, w2_ref, fg_ref[...] if final else None, o_ref, chunk)


def _mix_ffn(hs, og, x, mod_l, cond_tokens, norm_gain, w_out, ffn_gain, layer, w1, w3, w2, final_gain=None):
    n_tok = x.shape[0]
    tm = TOKEN_TILE
    tile_mod_row = _mod_row(cond_tokens, tm)
    tok = pl.BlockSpec((tm, D), lambda i: (i, 0))
    full = lambda a: pl.BlockSpec(a.shape, lambda i: (0,) * a.ndim, pipeline_mode=pl.Buffered(1))
    of_layer = lambda a: pl.BlockSpec((None,) + a.shape[1:], lambda i: (layer, 0, 0), pipeline_mode=pl.Buffered(1))
    gated = og is not None
    final = final_gain is not None
    consts = [norm_gain, w_out, ffn_gain, w1, w3, w2] + ([final_gain] if final else [])
    args = list(hs) + ([og] if gated else []) + [x, mod_l] + consts
    in_specs = [tok] * (len(hs) + (1 if gated else 0) + 1)
    in_specs += [pl.BlockSpec((None, 6, D), lambda i: (tile_mod_row(i), 0, 0))]
    in_specs += [of_layer(a) if a.ndim == 3 else full(a) for a in consts]
    return pl.pallas_call(
        functools.partial(_mix_ffn_kernel, n_h=len(hs), gated=gated, final=final, chunk=FFN_CHUNK),
        grid=(n_tok // tm,),
        in_specs=in_specs,
        out_specs=tok,
        out_shape=jax.ShapeDtypeStruct((n_tok, D), F32),
        compiler_params=_cparams(1),
        name="mix_ffn",
    )(*args)


def _rope(x, cos, sin_signed):
    lane = lax.broadcasted_iota(jnp.int32, (x.shape[0], HEAD), 1)
    first = (lane & 16) == 0
    parts = []
    for h in range(DA_HEADS):
        xh = x[:, h * HEAD:(h + 1) * HEAD]
        partner = jnp.where(first, pltpu.roll(xh, HEAD - 16, 1), pltpu.roll(xh, 16, 1))
        parts.append(xh * cos + partner * sin_signed)
    return jnp.concatenate(parts, axis=1)


def _inproj_odd_kernel(*refs, rope):
    x_ref, mod_ref, g_ref, w_ref = refs[:4]
    q_ref, k_ref, v_ref = refs[-3:]
    h = _norm_mod(x_ref[...], g_ref[...], mod_ref[...], 0).astype(BF16)
    q = _dot(h, w_ref[:, 0:D])
    k = _dot(h, w_ref[:, D:2 * D])
    v = _dot(h, w_ref[:, 2 * D:3 * D])
    if rope:
        cos, sin_signed = refs[4][...], refs[5][...]
        q = _rope(q, cos, sin_signed)
        k = _rope(k, cos, sin_signed)
    q_ref[...] = (q * (DA_DQK ** -0.5 * LOG2E)).astype(q_ref.dtype)
    k_ref[...] = k.astype(k_ref.dtype)
    v_ref[...] = v.astype(v_ref.dtype)


def _inproj_odd(x, mod_l, cond_tokens, gain, w, rope_tables, kv_dtype):
    n_tok = x.shape[0]
    tm = PROJ_TILE
    tile_mod_row = _mod_row(cond_tokens, tm)
    tok = pl.BlockSpec((tm, D), lambda i: (i, 0))
    full = lambda a: pl.BlockSpec(a.shape, lambda i: (0,) * a.ndim, pipeline_mode=pl.Buffered(1))
    args = [x, mod_l, gain, w]
    in_specs = [tok, pl.BlockSpec((None, 6, D), lambda i: (tile_mod_row(i), 0, 0)), full(gain), full(w)]
    if rope_tables is not None:
        tiles_per_seq = rope_tables[0].shape[0] // tm
        args += list(rope_tables)
        in_specs += [pl.BlockSpec((tm, HEAD), lambda i: (i % tiles_per_seq, 0))] * 2
    return pl.pallas_call(
        functools.partial(_inproj_odd_kernel, rope=rope_tables is not None),
        grid=(n_tok // tm,),
        in_specs=in_specs,
        out_specs=[tok, tok, tok],
        out_shape=[jax.ShapeDtypeStruct((n_tok, D), BF16), jax.ShapeDtypeStruct((n_tok, D), kv_dtype),
                   jax.ShapeDtypeStruct((n_tok, D), kv_dtype)],
        compiler_params=_cparams(1),
        name="inproj_odd",
    )(*args)


def _rope_tables(n_tok):
    quarter = DA_DQK // 4
    tok = np.arange(n_tok)
    pos = np.stack([tok // GRID_W, tok % GRID_W], axis=1).astype(np.float32)
    inv = (np.float32(ROPE_BASE) ** (-np.arange(quarter, dtype=np.float32) / np.float32(quarter))).astype(np.float32)
    lane = np.arange(HEAD)
    ang = (pos[:, (lane // 32) % 2] * inv[lane % quarter][None, :]).astype(np.float32)
    sign = np.where((lane % 32) < quarter, -1.0, 1.0).astype(np.float32)
    return jnp.asarray(np.cos(ang), F32), jnp.asarray(np.sin(ang) * sign[None, :], F32)


def _lambda(lam_ref, lam_init):
    lp = lam_ref[...]
    return (jnp.exp(jnp.sum(lp[0:1] * lp[1:2], axis=-1, keepdims=True))
            - jnp.exp(jnp.sum(lp[2:3] * lp[3:4], axis=-1, keepdims=True)) + lam_init)


VT_ROWS = HEAD + 16


def _qt2(qh):
    qt = qh.astype(F32).T
    row = lax.broadcasted_iota(jnp.int32, qt.shape, 0)
    return jnp.concatenate([jnp.where(row < DA_DQK, qt, 0.0), jnp.where(row >= DA_DQK, qt, 0.0)],
                           axis=1).astype(BF16)


def _vt_aug(vh):
    tk = vh.shape[0]
    row = lax.broadcasted_iota(jnp.int32, (VT_ROWS - HEAD, tk), 0)
    return jnp.concatenate([vh.astype(F32).T, (row == 0).astype(F32)], axis=0).astype(BF16)


def _diff_attn_heads(q_ref, k_fn, vt_fn, n_chunks, lam, o_ref, n_seq=1):
    tq = q_ref.shape[0] // n_seq
    items = [(s, h, c) for s in range(n_seq) for h in range(DA_HEADS) for c in range(n_chunks)]
    qt2, state, pending = {}, {}, []

    def finish(s, h, c, st):
        cm = jnp.max(st, axis=0, keepdims=True)
        if c == 0:
            m_new = cm
            acc = _dot(vt_fn(s, h, c), jnp.exp2(st - m_new).astype(BF16))
        else:
            m, acc = state[s, h]
            m_new = jnp.maximum(m, cm)
            acc = acc * jnp.exp2(m - m_new) + _dot(vt_fn(s, h, c), jnp.exp2(st - m_new).astype(BF16))
        state[s, h] = (m_new, acc)
        if c == n_chunks - 1:
            den = acc[HEAD:HEAD + 1, :]
            o_t = acc[0:HEAD, 0:tq] * (1.0 / den[:, 0:tq]) - acc[0:HEAD, tq:] * (lam / den[:, tq:])
            o_ref[s * tq:(s + 1) * tq, h * HEAD:(h + 1) * HEAD] = o_t.T

    for s, h, c in items:
        if c == 0:
            qt2[s, h] = _qt2(q_ref[s * tq:(s + 1) * tq, h * HEAD:(h + 1) * HEAD])
        pending.append((s, h, c, _dot(k_fn(s, h, c), qt2[s, h])))
        if len(pending) > ATTN_LOOKAHEAD:
            finish(*pending.pop(0))
    while pending:
        finish(*pending.pop(0))


def _attn_prompt_kernel(q_ref, k_ref, v_ref, lam_ref, o_ref, *, lam_init, seq):
    lam = _lambda(lam_ref, lam_init)
    rows = lambda s: slice(s * seq, (s + 1) * seq)
    _diff_attn_heads(q_ref, lambda s, h, c: k_ref[rows(s), h * HEAD:(h + 1) * HEAD].astype(BF16),
                     lambda s, h, c: _vt_aug(v_ref[rows(s), h * HEAD:(h + 1) * HEAD]), 1, lam, o_ref,
                     n_seq=q_ref.shape[0] // seq)


def _attn_prompt(q, k, v, lam_p, seq, lam_init):
    n_tok = q.shape[0]
    tok = pl.BlockSpec((PROMPT_SEQS_PER_STEP * seq, D), lambda b: (b, 0))
    return pl.pallas_call(
        functools.partial(_attn_prompt_kernel, lam_init=lam_init, seq=seq),
        grid=(n_tok // (PROMPT_SEQS_PER_STEP * seq),),
        in_specs=[tok, tok, tok, pl.BlockSpec(lam_p.shape, lambda b: (0, 0))],
        out_specs=tok,
        out_shape=jax.ShapeDtypeStruct((n_tok, D), F32),
        compiler_params=_cparams(1),
        name="diff_attn_prompt",
    )(q, k, v, lam_p)


def _attn_sample_kernel(q_ref, k_ref, v_ref, ck_ref, cv_ref, lam_ref, o_ref, kcat, vtcat, *, lam_init, past):
    n_keys = kcat.shape[0]
    chunks = [(c0, min(KEY_CHUNK, n_keys - c0)) for c0 in range(0, n_keys, KEY_CHUNK)]

    @pl.when(pl.program_id(1) == 0)
    def _gather_keys():
        kcat[0:past, :] = ck_ref[...].astype(BF16)
        kcat[past:, :] = k_ref[...]
        for h in range(DA_HEADS):
            sl = slice(h * HEAD, (h + 1) * HEAD)
            for c0 in range(0, n_keys, past):
                src, r0 = (cv_ref, c0) if c0 < past else (v_ref, c0 - past)
                vtcat[h, :, c0:c0 + past] = _vt_aug(src[r0:r0 + past, sl])

    lam = _lambda(lam_ref, lam_init)
    _diff_attn_heads(q_ref, lambda s, h, c: kcat[chunks[c][0]:chunks[c][0] + chunks[c][1], h * HEAD:(h + 1) * HEAD],
                     lambda s, h, c: vtcat[h, :, chunks[c][0]:chunks[c][0] + chunks[c][1]], len(chunks), lam, o_ref)


def _attn_sample(q, k, v, ck, cv, lam_p, n_seq, seq, lam_init, tq=128):
    past = ck.shape[1]
    qb = pl.BlockSpec((tq, D), lambda b, i: (b * (seq // tq) + i, 0))
    kv = pl.BlockSpec((seq, D), lambda b, i: (b, 0))
    cache = pl.BlockSpec((None, past, D), lambda b, i: (b, 0, 0))
    return pl.pallas_call(
        functools.partial(_attn_sample_kernel, lam_init=lam_init, past=past),
        grid=(n_seq, seq // tq),
        in_specs=[qb, kv, kv, cache, cache, pl.BlockSpec(lam_p.shape, lambda b, i: (0, 0))],
        out_specs=qb,
        out_shape=jax.ShapeDtypeStruct((n_seq * seq, D), F32),
        scratch_shapes=[pltpu.VMEM((past + seq, D), BF16), pltpu.VMEM((DA_HEADS, VT_ROWS, past + seq), BF16)],
        compiler_params=_cparams(2),
        name="diff_attn_sample",
    )(q, k, v, ck, cv, lam_p)


def kernel(x_prompt, x_sample, c, c_ctx, cache_attn_k, cache_attn_v, state_mlstm_C, state_mlstm_n, state_mlstm_m,
           state_hgrn_S, ada_w, ada_b, norm_mix_g, norm_ffn_g, ev_w_in, ev_gate_b, ev_lb_logits, ml_norm_g,
           hg_norm_g, ev_w_out, od_w_in, od_lambda, da_norm_g, od_w_out, ffn_w1, ffn_w3, ffn_w2, final_norm_g):
    assert DEPTH % 2 == 0
    n_p, s_p, _ = x_prompt.shape
    n_s, s_s, _ = x_sample.shape
    past = cache_attn_k.shape[2]
    assert s_p == SEG and s_s % SEG == 0 and s_s % PROJ_TILE == 0 and (n_p * s_p) % PROJ_TILE == 0
    assert PROJ_TILE % TOKEN_TILE == 0
    xp = x_prompt.astype(F32).reshape(n_p * s_p, D)
    xs = x_sample.astype(F32).reshape(n_s * s_s, D)

    cond8 = jnp.zeros((8, D), F32).at[0].set(c_ctx.astype(F32)).at[1:1 + n_s].set(c.astype(F32))
    mod = _modulation(cond8, ada_w.astype(F32), ada_b.astype(F32)).reshape(DEPTH, 8, 6, D)
    streams = [(xp, None), (xs, s_s)]
    outputs = {}
    w1_all, w3_all, w2_all = ffn_w1.astype(BF16), ffn_w3.astype(BF16), ffn_w2.astype(BF16)

    for l in range(DEPTH):
        mix_gain = norm_mix_g[l].astype(F32).reshape(1, D)
        ffn_gain = norm_ffn_g[l].astype(F32).reshape(1, D)
        if l % 2 == 0:
            e = l // 2
            w_in = _regroup_even_weights(ev_w_in, e)
            gate_b = jnp.pad(ev_gate_b[e].astype(F32), (0, HEAD - 16)).reshape(1, HEAD)
            norm_gain = jnp.concatenate([ml_norm_g[e], hg_norm_g[e]]).astype(F32).reshape(1, D)
            w_out = ev_w_out[e].astype(BF16)
            mixed = []
            for si, (x, cond_tok) in enumerate(streams):
                qkv, gate, hqi, hgf, og = _inproj_even(x, mod[l], cond_tok, mix_gain, w_in, gate_b,
                                                       ev_lb_logits.astype(F32), e)
                if si == 0:
                    *hs, c_new, n_new, m_new, s_new = _scan(qkv, gate, hqi, hgf, n_p, s_p // SEG, emit_state=True)
                    outputs.setdefault("C", []).append(c_new)
                    outputs.setdefault("n", []).append(n_new)
                    outputs.setdefault("m", []).append(m_new[:, 0, 0:8].reshape(n_p, 2, 4))
                    outputs.setdefault("S", []).append(s_new)
                else:
                    init = (state_mlstm_C[:, e].astype(F32), state_mlstm_n[:, e].astype(F32),
                            jnp.pad(state_mlstm_m[:, e].astype(F32).reshape(n_s, 1, 8), ((0, 0), (0, 0), (0, HEAD - 8))),
                            state_hgrn_S[:, e].astype(F32))
                    hs = _scan(qkv, gate, hqi, hgf, n_s, s_s // SEG, init=init)
                mixed.append((list(hs), og, x))
        else:
            o = l // 2
            lam_init = 0.8 - 0.6 * math.exp(-0.3 * l)
            w_in = od_w_in[o].astype(BF16)
            norm_gain = (jnp.tile(da_norm_g[o].astype(F32), DA_HEADS) * (1.0 - lam_init)).reshape(1, D)
            w_out = od_w_out[o].astype(BF16)
            lam_p = od_lambda[o].astype(F32)
            mixed = []
            for si, (x, cond_tok) in enumerate(streams):
                if si == 0:
                    q, k, v = _inproj_odd(x, mod[l], cond_tok, mix_gain, w_in, None, F32)
                    outputs.setdefault("k", []).append(k.reshape(n_p, s_p, DA_HEADS, HEAD))
                    outputs.setdefault("v", []).append(v.reshape(n_p, s_p, DA_HEADS, HEAD))
                    att = _attn_prompt(q, k, v, lam_p, s_p, lam_init)
                else:
                    q, k, v = _inproj_odd(x, mod[l], cond_tok, mix_gain, w_in, _rope_tables(s_s), BF16)
                    ck = cache_attn_k[:, o].reshape(n_s, past, D)
                    cv = cache_attn_v[:, o].reshape(n_s, past, D)
                    att = _attn_sample(q, k, v, ck, cv, lam_p, n_s, s_s, lam_init)
                mixed.append(([att], None, x))
        final_gain = final_norm_g.astype(F32).reshape(1, D) if l == DEPTH - 1 else None
        streams = [(_mix_ffn(hs, og, x, mod[l], cond_tok, norm_gain, w_out, ffn_gain, l, w1_all, w3_all, w2_all,
                             final_gain), cond_tok)
                   for (hs, og, x), (_, cond_tok) in zip(mixed, streams)]

    y_prompt = streams[0][0].reshape(n_p, s_p, D)
    y_sample = streams[1][0].reshape(n_s, s_s, D)
    stack = lambda name: outputs[name][0][:, None] if len(outputs[name]) == 1 else jnp.stack(outputs[name], axis=1)
    return (y_prompt, y_sample, stack("k"), stack("v"), stack("C"), stack("n"), stack("m"), stack("S"))
```

```python
import functools
import math

import jax
import jax.numpy as jnp
import numpy as np
from jax import lax
from jax.experimental import pallas as pl
from jax.experimental.pallas import tpu as pltpu

F32 = jnp.float32
BF16 = jnp.bfloat16

D = 1024
DEPTH = 2
GRID_W = 64
ML_HEADS = 4
HG_HEADS = 4
HEAD = 128
DA_HEADS = 8
DA_DQK = 64
ROPE_BASE = 10000.0
LOG2E = math.log2(math.e)
EPS = 1e-6
D_FF = ((8 * D // 3 + 255) // 256) * 256
EV_SIZES = (512, 512, 512, 512, 16, 512, 512, 512, 512, 512)
EV_COLS = 9 * 512 + 128

SEG = 256
HALF = SEG // 2
TILE_LEVELS = (1, 2, 4, 8, 16, 32, 64)
TOKEN_TILE = 512
PROJ_TILE = 1024
FFN_CHUNK = 256
KEY_CHUNK = 256
PROMPT_SEQS_PER_STEP = 4
ATTN_LOOKAHEAD = 6
VMEM_LIMIT = 56 * 1024 * 1024


def _cparams(n_axes):
    return pltpu.CompilerParams(dimension_semantics=("arbitrary",) * n_axes, vmem_limit_bytes=VMEM_LIMIT)


def _sigmoid(x):
    return 1.0 / (1.0 + jnp.exp(-x))


def _silu(x):
    return x * _sigmoid(x)


def _log_sigmoid(x):
    return jnp.minimum(x, 0.0) - jnp.log(1.0 + jnp.exp(-jnp.abs(x)))


def _dot(a, b):
    return jnp.dot(a, b, preferred_element_type=F32)


def _dot_nt(a, b):
    return lax.dot_general(a, b, (((1,), (1,)), ((), ())), preferred_element_type=F32)


def _norm_mod(x, gain, mod, k):
    ms = jnp.mean(x * x, axis=-1, keepdims=True)
    return x * lax.rsqrt(ms + EPS) * gain * (1.0 + mod[3 * k + 1:3 * k + 2]) + mod[3 * k:3 * k + 1]


def _mod_row(cond_tokens, tm):
    if cond_tokens is None:
        return lambda i: 0
    return lambda i: 1 + i // (cond_tokens // tm)


def _mod_kernel(c_ref, w_ref, b_ref, o_ref):
    s = _silu(c_ref[...]).astype(BF16)
    o_ref[...] = _dot(s, w_ref[...].astype(BF16)) + b_ref[...]


def _modulation(cond8, ada_w, ada_b):
    n_layers = ada_w.shape[0]
    tn = 3072
    return pl.pallas_call(
        _mod_kernel,
        grid=(n_layers, 6 * D // tn),
        in_specs=[pl.BlockSpec((8, D), lambda l, n: (0, 0)),
                  pl.BlockSpec((None, D, tn), lambda l, n: (l, 0, n)),
                  pl.BlockSpec((None, 1, tn), lambda l, n: (l, 0, n))],
        out_specs=pl.BlockSpec((None, 8, tn), lambda l, n: (l, 0, n)),
        out_shape=jax.ShapeDtypeStruct((n_layers, 8, 6 * D), F32),
        compiler_params=_cparams(2),
        name="ada_modulation",
    )(cond8, ada_w, ada_b.reshape(n_layers, 1, 6 * D))


def _regroup_even_weights(ev_w_in, e):
    g0 = sum(EV_SIZES[:4])
    w_e = ev_w_in[e].astype(BF16)
    return jnp.concatenate([w_e[:, :g0], w_e[:, g0 + 16:], jnp.pad(w_e[:, g0:g0 + 16], ((0, 0), (0, HEAD - 16)))],
                           axis=1)


def _inproj_even_body(x, mod, gain, w_ref, gate_b, lb_logits, e_idx, qkv_ref, gate_ref, hqi_ref, hgf_ref, og_ref):
    h = _norm_mod(x, gain, mod, 0).astype(BF16)

    def proj(c0, n):
        return _dot(h, w_ref[:, c0:c0 + n])

    qkv_ref[:, 0:512] = (proj(0, 512) * (HEAD ** -0.5)).astype(BF16)
    qkv_ref[:, 512:1536] = proj(512, 1024).astype(BF16)
    og_ref[:, 0:512] = proj(1536, 512)
    hqi_ref[:, 0:512] = proj(2048, 512).astype(BF16)
    hqi_ref[:, 512:1024] = proj(3584, 512).astype(BF16)
    mx = jnp.max(lb_logits, axis=0, keepdims=True)
    ex = jnp.exp(lb_logits - mx)
    lb = jnp.sum(ex[0:e_idx + 1], axis=0, keepdims=True) / jnp.sum(ex, axis=0, keepdims=True)
    hgf_ref[:, 0:512] = jnp.log(lb + (1.0 - lb) * _sigmoid(proj(2560, 512)))
    hgf_ref[:, 512:1024] = jnp.log(lb + (1.0 - lb) * _sigmoid(proj(3072, 512)))
    og_ref[:, 512:1024] = proj(4096, 512)
    gt = proj(4608, 128) + gate_b
    lane = lax.broadcasted_iota(jnp.int32, gt.shape, 1)
    gate_ref[...] = jnp.where(lane < 8, gt, jnp.where(lane < 16, _log_sigmoid(gt), 0.0))


def _inproj_even_kernel(x_ref, mod_ref, g_ref, w_ref, gb_ref, lbl_ref, qkv_ref, gate_ref, hqi_ref, hgf_ref,
                        og_ref, *, e_idx):
    _inproj_even_body(x_ref[...], mod_ref[...], g_ref[...], w_ref, gb_ref[...], lbl_ref[...], e_idx,
                      qkv_ref, gate_ref, hqi_ref, hgf_ref, og_ref)


def _inproj_even(x, mod_l, cond_tokens, gain, w, gate_b, lb_logits, e_idx):
    n_tok = x.shape[0]
    tm = PROJ_TILE
    tile_mod_row = _mod_row(cond_tokens, tm)
    tok = lambda n: pl.BlockSpec((tm, n), lambda i: (i, 0))
    full = lambda a: pl.BlockSpec(a.shape, lambda i: (0,) * a.ndim, pipeline_mode=pl.Buffered(1))
    return pl.pallas_call(
        functools.partial(_inproj_even_kernel, e_idx=e_idx),
        grid=(n_tok // tm,),
        in_specs=[tok(D), pl.BlockSpec((None, 6, D), lambda i: (tile_mod_row(i), 0, 0)),
                  full(gain), full(w), full(gate_b), full(lb_logits)],
        out_specs=[tok(1536), tok(128), tok(1024), tok(1024), tok(1024)],
        out_shape=[jax.ShapeDtypeStruct((n_tok, 1536), BF16), jax.ShapeDtypeStruct((n_tok, 128), F32),
                   jax.ShapeDtypeStruct((n_tok, 1024), BF16), jax.ShapeDtypeStruct((n_tok, 1024), F32),
                   jax.ShapeDtypeStruct((n_tok, 1024), F32)],
        compiler_params=_cparams(1),
        name="inproj_even",
    )(x, mod_l, gain, w, gate_b, lb_logits)


def _shift_rows(x, k, fill, up):
    n = x.shape[0]
    if k % 8 == 0:
        pad = jnp.full((k,) + x.shape[1:], 0.0 if fill is None else fill, x.dtype)
        return jnp.concatenate([x[k:], pad], axis=0) if up else jnp.concatenate([pad, x[:n - k]], axis=0)
    y = pltpu.roll(x, (n - k) if up else k, 0)
    if fill is None:
        return y
    row = lax.broadcasted_iota(jnp.int32, x.shape, 0)
    return jnp.where(row >= n - k, fill, y) if up else jnp.where(row < k, fill, y)


def _cummax_rows(x, rev):
    k = 1
    while k < x.shape[0]:
        x = jnp.maximum(x, _shift_rows(x, k, -jnp.inf, up=rev))
        k *= 2
    return x


def _cumsum_rows(tri, x):
    hi = x.astype(BF16)
    r1 = x - hi.astype(F32)
    mid = r1.astype(BF16)
    lo = (r1 - mid.astype(F32)).astype(BF16)
    return _dot(tri, hi) + _dot(tri, mid) + _dot(tri, lo)


def _hgrn_tile(q, kk, f, a, q_b, rev, mask_sc, d, up_small):
    ops = []
    bm = a
    for li, m in enumerate(TILE_LEVELS):
        if m < 8:
            qrole = jnp.logical_not(up_small[li]) if rev else up_small[li]
            if m == 1:
                r = jnp.where(qrole, q * f, kk)
            else:
                x = jnp.where(qrole, _shift_rows(bm, m, None, up=rev), bm)
                r = jnp.where(qrole, q, kk) * jnp.exp(jnp.where(qrole, a - x, x - a))
            if 2 * m < 8:
                bm = jnp.where(qrole, bm, _shift_rows(bm, m, None, up=not rev))
        else:
            pieces = []
            for b0 in range(0, HALF, 2 * m):
                lo, up = slice(b0, b0 + m), slice(b0 + m, b0 + 2 * m)
                if rev:
                    ref = a[b0 + m:b0 + m + 1]
                    pieces += [q[lo] * jnp.exp(a[lo] - ref), kk[up] * jnp.exp(ref - a[up])]
                else:
                    ref = a[b0 + m - 1:b0 + m]
                    pieces += [kk[lo] * jnp.exp(ref - a[lo]), q[up] * jnp.exp(a[up] - ref)]
            r = jnp.concatenate(pieces, axis=0)
        ops.append(r.astype(BF16))
    p = _dot_nt(q_b, kk.astype(BF16)).astype(BF16) * mask_sc[d, len(TILE_LEVELS)]
    for li, rb in enumerate(ops):
        p = p + _dot_nt(rb, rb).astype(BF16) * mask_sc[d, li]
    return p


def _build_scan_constants(tri_sc, mask_sc):
    row = lax.broadcasted_iota(jnp.int32, (SEG, SEG), 0)
    col = lax.broadcasted_iota(jnp.int32, (SEG, SEG), 1)
    tri_sc[0] = (col <= row).astype(BF16)
    tri_sc[1] = (col >= row).astype(BF16)
    trow = lax.broadcasted_iota(jnp.int32, (HALF, HALF), 0)
    tcol = lax.broadcasted_iota(jnp.int32, (HALF, HALF), 1)
    for li, m in enumerate(TILE_LEVELS):
        sh = m.bit_length() - 1
        same = (trow >> (sh + 1)) == (tcol >> (sh + 1))
        t_up = ((trow >> sh) & 1) == 1
        s_up = ((tcol >> sh) & 1) == 1
        mask_sc[0, li] = (same & t_up & jnp.logical_not(s_up)).astype(BF16)
        mask_sc[1, li] = (same & s_up & jnp.logical_not(t_up)).astype(BF16)
    eye = (trow == tcol).astype(BF16)
    mask_sc[0, len(TILE_LEVELS)] = eye
    mask_sc[1, len(TILE_LEVELS)] = eye


def _scan_segment(dir_refs, write_h, tri_sc, mask_sc, m_in, state, emit):
    row = lax.broadcasted_iota(jnp.int32, (SEG, SEG), 0)
    col = lax.broadcasted_iota(jnp.int32, (SEG, SEG), 1)
    lane128 = lax.broadcasted_iota(jnp.int32, (SEG, HEAD), 1)
    e0 = (lane128 == 0).astype(BF16)
    row_half = lax.broadcasted_iota(jnp.int32, (HALF, HEAD), 0)
    up_masks = [((row_half >> (m.bit_length() - 1)) & 1) == 1 for m in TILE_LEVELS if m < 8]
    tmasks = (col <= row, col >= row)
    use_state = state is not None
    want_state = use_state or emit is not None
    m_out_rows = []

    for d in range(2):
        qkv_ref, gate_ref, hqi_ref, hgf_ref = dir_refs[d]
        rev = d == 1
        last = 0 if rev else SEG - 1
        tri = tri_sc[d]

        slab = gate_ref[...]
        b_al = pltpu.roll(_cumsum_rows(tri, slab), HEAD - 8, 1)
        u = slab - b_al
        mx = jnp.maximum(_cummax_rows(u, rev), m_in)
        w_inter = jnp.exp(m_in - mx)
        e_den = jnp.exp(-(b_al + mx))
        mx_last = mx[last:last + 1, :]
        m_out_rows.append(b_al[last:last + 1, :] + mx_last)
        decay = jnp.exp(m_in - mx_last)
        wg = jnp.exp(u - mx_last)
        u_t = u.T

        for h in range(ML_HEADS):
            c = 4 * d + h
            q = qkv_ref[:, h * HEAD:(h + 1) * HEAD]
            k = qkv_ref[:, 512 + h * HEAD:512 + (h + 1) * HEAD]
            v = qkv_ref[:, 1024 + h * HEAD:1024 + (h + 1) * HEAD]
            dm = jnp.where(tmasks[d], jnp.exp(u_t[c:c + 1, :] - mx[:, c:c + 1]), 0.0)
            s = (_dot_nt(q, k) * dm).astype(BF16)
            v_aug = jnp.concatenate([v, e0], axis=1)
            numden = _dot(s, v_aug)
            if use_state:
                numden = numden + w_inter[:, c:c + 1] * _dot(q, state[0][d, h].astype(BF16))
            den = jnp.maximum(jnp.abs(numden[:, HEAD:HEAD + 1]), e_den[:, c:c + 1])
            write_h(d, h * HEAD, numden[:, 0:HEAD] / den)
            if want_state:
                kw_t = (k.astype(F32) * wg[:, c:c + 1]).T.astype(BF16)
                upd = _dot(kw_t, v_aug)
                if use_state:
                    upd = upd + decay[:, c:c + 1] * state[0][d, h]
                    state[0][d, h] = upd
                if emit is not None:
                    emit("ml", d, h, upd)

        lf_all = hgf_ref[:, 512 * d:512 * (d + 1)]
        a_all = _cumsum_rows(tri, lf_all)
        f_all = jnp.exp(lf_all)
        kk_all = 1.0 - f_all
        for h in range(HG_HEADS):
            sl = slice(h * HEAD, (h + 1) * HEAD)
            a = a_all[:, sl]
            kk = kk_all[:, sl]
            f = f_all[:, sl]
            q_b = hqi_ref[:, h * HEAD:(h + 1) * HEAD]
            i_b = hqi_ref[:, 512 + h * HEAD:512 + (h + 1) * HEAD]
            q = q_b.astype(F32)
            qh, kh = (0, 1) if rev else (1, 0)
            a_q, a_k = a[qh * HALF:(qh + 1) * HALF], a[kh * HALF:(kh + 1) * HALF]
            a_mid = a[HALF:HALF + 1] if rev else a[HALF - 1:HALF]
            r_q = (q[qh * HALF:(qh + 1) * HALF] * jnp.exp(a_q - a_mid)).astype(BF16)
            r_k = (kk[kh * HALF:(kh + 1) * HALF] * jnp.exp(a_mid - a_k)).astype(BF16)
            p_cross = _dot_nt(r_q, r_k).astype(BF16)
            p_tiles = []
            for t in range(2):
                rows = slice(t * HALF, (t + 1) * HALF)
                p_tiles.append(_hgrn_tile(q[rows], kk[rows], f[rows], a[rows], q_b[rows], rev, mask_sc, d, up_masks))
            i_lo, i_hi = i_b[0:HALF], i_b[HALF:SEG]
            if rev:
                o = jnp.concatenate([_dot(p_tiles[0], i_lo) + _dot(p_cross, i_hi), _dot(p_tiles[1], i_hi)], axis=0)
            else:
                o = jnp.concatenate([_dot(p_tiles[0], i_lo), _dot(p_cross, i_lo) + _dot(p_tiles[1], i_hi)], axis=0)
            if use_state:
                st = state[1][d, h]
                o = o + _dot_nt((q * jnp.exp(a)).astype(BF16), st.astype(BF16))
            write_h(d, 512 + h * HEAD, o)
            if want_state:
                a_l = a[last:last + 1, :]
                kd = (kk * jnp.exp(a_l - a)).astype(BF16)
                st_new = _dot(i_b.astype(F32).T.astype(BF16), kd)
                if use_state:
                    st_new = st_new + st * jnp.exp(a_l)
                    state[1][d, h] = st_new
                if emit is not None:
                    emit("hg", d, h, st_new)

    lane_row = lax.broadcasted_iota(jnp.int32, (1, HEAD), 1)
    return jnp.where(lane_row < 4, m_out_rows[0], jnp.where(lane_row < 8, m_out_rows[1], 0.0))


def _emit_states(kind, d, h, value, c_ref, n_ref, s_ref):
    if kind == "ml":
        c_ref[d, h] = value[:, 0:HEAD]
        n_ref[d, h:h + 1, :] = value[:, HEAD:2 * HEAD].T[0:1, :]
    else:
        s_ref[d, h] = value.T


def _scan_kernel(*refs, has_init, emit_state, nseg):
    single = nseg == 1
    n_tok_in = 4 if single else 8
    n_h = 1 if single else 2
    n_in = n_tok_in + (4 if has_init else 0)
    n_out = n_h + (4 if emit_state else 0)
    ins, outs, scr = refs[:n_in], refs[n_in:n_in + n_out], refs[n_in + n_out:]
    dir_refs = (ins[0:4], ins[0:4] if single else ins[4:8])
    st_outs = outs[n_h:]
    tri_sc, mask_sc, caug_sc, st_sc, m_sc = scr
    use_state = has_init or nseg > 1
    b_id, j = pl.program_id(0), pl.program_id(1)

    @pl.when(jnp.logical_and(b_id == 0, j == 0))
    def _build_constants():
        _build_scan_constants(tri_sc, mask_sc)

    if use_state:
        @pl.when(j == 0)
        def _init_state():
            if has_init:
                c0_ref, n0_ref, m0_ref, s0_ref = ins[n_tok_in:n_tok_in + 4]
                for d in range(2):
                    for h in range(ML_HEADS):
                        caug_sc[d, h, :, 0:HEAD] = c0_ref[d, h]
                        caug_sc[d, h, :, HEAD:2 * HEAD] = jnp.broadcast_to(n0_ref[d, h:h + 1, :], (HEAD, HEAD)).T
                        st_sc[d, h] = s0_ref[d, h].T
                m_sc[...] = jnp.broadcast_to(m0_ref[...], m_sc.shape)
            else:
                caug_sc[...] = jnp.zeros(caug_sc.shape, F32)
                st_sc[...] = jnp.zeros(st_sc.shape, F32)
                m_sc[...] = jnp.zeros(m_sc.shape, F32)

    def write_h(d, c0, value):
        if single and d == 1:
            outs[0][:, c0:c0 + HEAD] += value
        else:
            outs[0 if single else d][:, c0:c0 + HEAD] = value

    def emit(kind, d, h, value):
        if single:
            _emit_states(kind, d, h, value, st_outs[0], st_outs[1], st_outs[3])
        else:
            pl.when(j == nseg - 1)(lambda: _emit_states(kind, d, h, value, st_outs[0], st_outs[1], st_outs[3]))

    m_in = m_sc[0:1, :] if use_state else jnp.zeros((1, HEAD), F32)
    m_new = _scan_segment(dir_refs, write_h, tri_sc, mask_sc, m_in, (caug_sc, st_sc) if use_state else None,
                          emit if emit_state else None)
    if use_state:
        m_sc[...] = jnp.broadcast_to(m_new, m_sc.shape)
    if emit_state:
        if single:
            st_outs[2][...] = m_new
        else:
            @pl.when(j == nseg - 1)
            def _emit_m():
                st_outs[2][...] = m_new


def _scan(qkv, gate, hqi, hgf, n_seq, nseg, init=None, emit_state=False):
    n_tok = qkv.shape[0]
    fwd = lambda n: pl.BlockSpec((SEG, n), lambda b, j: (b * nseg + j, 0))
    bwd = lambda n: pl.BlockSpec((SEG, n), lambda b, j: (b * nseg + nseg - 1 - j, 0))
    widths = (1536, 128, 1024, 1024)
    single = nseg == 1
    in_specs = [fwd(n) for n in widths] + ([] if single else [bwd(n) for n in widths])
    args = [qkv, gate, hqi, hgf] + ([] if single else [qkv, gate, hqi, hgf])
    mat = pl.BlockSpec((None, 2, 4, HEAD, HEAD), lambda b, j: (b, 0, 0, 0, 0))
    vec = pl.BlockSpec((None, 2, 4, HEAD), lambda b, j: (b, 0, 0, 0))
    sca = pl.BlockSpec((None, 1, HEAD), lambda b, j: (b, 0, 0))
    if init is not None:
        in_specs += [mat, vec, sca, mat]
        args += list(init)
    out_specs = [fwd(1024)] if single else [fwd(1024), bwd(1024)]
    out_shape = [jax.ShapeDtypeStruct((n_tok, 1024), F32)] * len(out_specs)
    if emit_state:
        out_specs += [mat, vec, sca, mat]
        out_shape += [jax.ShapeDtypeStruct((n_seq, 2, 4, HEAD, HEAD), F32),
                      jax.ShapeDtypeStruct((n_seq, 2, 4, HEAD), F32),
                      jax.ShapeDtypeStruct((n_seq, 1, HEAD), F32),
                      jax.ShapeDtypeStruct((n_seq, 2, 4, HEAD, HEAD), F32)]
    return pl.pallas_call(
        functools.partial(_scan_kernel, has_init=init is not None, emit_state=emit_state, nseg=nseg),
        grid=(n_seq, nseg),
        in_specs=in_specs,
        out_specs=out_specs,
        out_shape=out_shape,
        scratch_shapes=[pltpu.VMEM((2, SEG, SEG), BF16),
                        pltpu.VMEM((2, len(TILE_LEVELS) + 1, HALF, HALF), BF16),
                        pltpu.VMEM((2, 4, HEAD, 2 * HEAD), F32),
                        pltpu.VMEM((2, 4, HEAD, HEAD), F32),
                        pltpu.VMEM((8, HEAD), F32)],
        compiler_params=_cparams(2),
        name="bidir_scan",
    )(*args)


def _mix_ffn_tile(h, og, x, mod, norm_gain, wo_ref, ffn_gain, w1_ref, w3_ref, w2_ref, final_gain, o_ref, chunk):
    parts = []
    for g in range(D // HEAD):
        hs = h[:, g * HEAD:(g + 1) * HEAD]
        parts.append(hs * lax.rsqrt(jnp.mean(hs * hs, axis=-1, keepdims=True) + EPS))
    hn = jnp.concatenate(parts, axis=1) * norm_gain
    if og is not None:
        hn = hn * jnp.concatenate([_sigmoid(og[:, 0:512]), _silu(og[:, 512:1024])], axis=1)
    x = x + mod[2:3, :] * _dot(hn.astype(BF16), wo_ref[...])
    hf = _norm_mod(x, ffn_gain, mod, 1).astype(BF16)
    up = lambda c0: (_dot(hf, w1_ref[:, c0:min(c0 + chunk, D_FF)]), _dot(hf, w3_ref[:, c0:min(c0 + chunk, D_FF)]))
    starts = list(range(0, D_FF, chunk))
    acc = jnp.zeros(x.shape, F32)
    nxt = up(starts[0])
    for i, c0 in enumerate(starts):
        a, b = nxt
        if i + 1 < len(starts):
            nxt = up(starts[i + 1])
        acc = acc + _dot((_silu(a) * b).astype(BF16), w2_ref[c0:min(c0 + chunk, D_FF), :])
    y = x + mod[5:6, :] * acc
    if final_gain is not None:
        y = y * lax.rsqrt(jnp.mean(y * y, axis=-1, keepdims=True) + EPS) * final_gain
    o_ref[...] = y


def _mix_ffn_kernel(*refs, n_h, gated, final, chunk):
    h_refs = refs[:n_h]
    pos = n_h
    og_ref = refs[pos] if gated else None
    pos += 1 if gated else 0
    x_ref, mod_ref, ng_ref, wo_ref, g_ref, w1_ref, w3_ref, w2_ref = refs[pos:pos + 8]
    fg_ref = refs[pos + 8] if final else None
    o_ref = refs[-1]
    h = h_refs[0][...]
    for r in h_refs[1:]:
        h = h + r[...]
    _mix_ffn_tile(h, og_ref[...] if gated else None, x_ref[...], mod_ref[...], ng_ref[...], wo_ref, g_ref[...],
                  w1_ref, w3_ref, w2_ref, fg_ref[...] if final else None, o_ref, chunk)


def _mix_ffn(hs, og, x, mod_l, cond_tokens, norm_gain, w_out, ffn_gain, layer, w1, w3, w2, final_gain=None):
    n_tok = x.shape[0]
    tm = TOKEN_TILE
    tile_mod_row = _mod_row(cond_tokens, tm)
    tok = pl.BlockSpec((tm, D), lambda i: (i, 0))
    full = lambda a: pl.BlockSpec(a.shape, lambda i: (0,) * a.ndim, pipeline_mode=pl.Buffered(1))
    of_layer = lambda a: pl.BlockSpec((None,) + a.shape[1:], lambda i: (layer, 0, 0), pipeline_mode=pl.Buffered(1))
    gated = og is not None
    final = final_gain is not None
    consts = [norm_gain, w_out, ffn_gain, w1, w3, w2] + ([final_gain] if final else [])
    args = list(hs) + ([og] if gated else []) + [x, mod_l] + consts
    in_specs = [tok] * (len(hs) + (1 if gated else 0) + 1)
    in_specs += [pl.BlockSpec((None, 6, D), lambda i: (tile_mod_row(i), 0, 0))]
    in_specs += [of_layer(a) if a.ndim == 3 else full(a) for a in consts]
    return pl.pallas_call(
        functools.partial(_mix_ffn_kernel, n_h=len(hs), gated=gated, final=final, chunk=FFN_CHUNK),
        grid=(n_tok // tm,),
        in_specs=in_specs,
        out_specs=tok,
        out_shape=jax.ShapeDtypeStruct((n_tok, D), F32),
        compiler_params=_cparams(1),
        name="mix_ffn",
    )(*args)


def _rope(x, cos, sin_signed):
    lane = lax.broadcasted_iota(jnp.int32, (x.shape[0], HEAD), 1)
    first = (lane & 16) == 0
    parts = []
    for h in range(DA_HEADS):
        xh = x[:, h * HEAD:(h + 1) * HEAD]
        partner = jnp.where(first, pltpu.roll(xh, HEAD - 16, 1), pltpu.roll(xh, 16, 1))
        parts.append(xh * cos + partner * sin_signed)
    return jnp.concatenate(parts, axis=1)


def _inproj_odd_kernel(*refs, rope):
    x_ref, mod_ref, g_ref, w_ref = refs[:4]
    q_ref, k_ref, v_ref = refs[-3:]
    h = _norm_mod(x_ref[...], g_ref[...], mod_ref[...], 0).astype(BF16)
    q = _dot(h, w_ref[:, 0:D])
    k = _dot(h, w_ref[:, D:2 * D])
    v = _dot(h, w_ref[:, 2 * D:3 * D])
    if rope:
        cos, sin_signed = refs[4][...], refs[5][...]
        q = _rope(q, cos, sin_signed)
        k = _rope(k, cos, sin_signed)
    q_ref[...] = (q * (DA_DQK ** -0.5 * LOG2E)).astype(q_ref.dtype)
    k_ref[...] = k.astype(k_ref.dtype)
    v_ref[...] = v.astype(v_ref.dtype)


def _inproj_odd(x, mod_l, cond_tokens, gain, w, rope_tables, kv_dtype):
    n_tok = x.shape[0]
    tm = PROJ_TILE
    tile_mod_row = _mod_row(cond_tokens, tm)
    tok = pl.BlockSpec((tm, D), lambda i: (i, 0))
    full = lambda a: pl.BlockSpec(a.shape, lambda i: (0,) * a.ndim, pipeline_mode=pl.Buffered(1))
    args = [x, mod_l, gain, w]
    in_specs = [tok, pl.BlockSpec((None, 6, D), lambda i: (tile_mod_row(i), 0, 0)), full(gain), full(w)]
    if rope_tables is not None:
        tiles_per_seq = rope_tables[0].shape[0] // tm
        args += list(rope_tables)
        in_specs += [pl.BlockSpec((tm, HEAD), lambda i: (i % tiles_per_seq, 0))] * 2
    return pl.pallas_call(
        functools.partial(_inproj_odd_kernel, rope=rope_tables is not None),
        grid=(n_tok // tm,),
        in_specs=in_specs,
        out_specs=[tok, tok, tok],
        out_shape=[jax.ShapeDtypeStruct((n_tok, D), BF16), jax.ShapeDtypeStruct((n_tok, D), kv_dtype),
                   jax.ShapeDtypeStruct((n_tok, D), kv_dtype)],
        compiler_params=_cparams(1),
        name="inproj_odd",
    )(*args)


def _rope_tables(n_tok):
    quarter = DA_DQK // 4
    tok = np.arange(n_tok)
    pos = np.stack([tok // GRID_W, tok % GRID_W], axis=1).astype(np.float32)
    inv = (np.float32(ROPE_BASE) ** (-np.arange(quarter, dtype=np.float32) / np.float32(quarter))).astype(np.float32)
    lane = np.arange(HEAD)
    ang = (pos[:, (lane // 32) % 2] * inv[lane % quarter][None, :]).astype(np.float32)
    sign = np.where((lane % 32) < quarter, -1.0, 1.0).astype(np.float32)
    return jnp.asarray(np.cos(ang), F32), jnp.asarray(np.sin(ang) * sign[None, :], F32)


def _lambda(lam_ref, lam_init):
    lp = lam_ref[...]
    return (jnp.exp(jnp.sum(lp[0:1] * lp[1:2], axis=-1, keepdims=True))
            - jnp.exp(jnp.sum(lp[2:3] * lp[3:4], axis=-1, keepdims=True)) + lam_init)


VT_ROWS = HEAD + 16


def _qt2(qh):
    qt = qh.astype(F32).T
    row = lax.broadcasted_iota(jnp.int32, qt.shape, 0)
    return jnp.concatenate([jnp.where(row < DA_DQK, qt, 0.0), jnp.where(row >= DA_DQK, qt, 0.0)],
                           axis=1).astype(BF16)


def _vt_aug(vh):
    tk = vh.shape[0]
    row = lax.broadcasted_iota(jnp.int32, (VT_ROWS - HEAD, tk), 0)
    return jnp.concatenate([vh.astype(F32).T, (row == 0).astype(F32)], axis=0).astype(BF16)


def _diff_attn_heads(q_ref, k_fn, vt_fn, n_chunks, lam, o_ref, n_seq=1):
    tq = q_ref.shape[0] // n_seq
    items = [(s, h, c) for s in range(n_seq) for h in range(DA_HEADS) for c in range(n_chunks)]
    qt2, state, pending = {}, {}, []

    def finish(s, h, c, st):
        cm = jnp.max(st, axis=0, keepdims=True)
        if c == 0:
            m_new = cm
            acc = _dot(vt_fn(s, h, c), jnp.exp2(st - m_new).astype(BF16))
        else:
            m, acc = state[s, h]
            m_new = jnp.maximum(m, cm)
            acc = acc * jnp.exp2(m - m_new) + _dot(vt_fn(s, h, c), jnp.exp2(st - m_new).astype(BF16))
        state[s, h] = (m_new, acc)
        if c == n_chunks - 1:
            den = acc[HEAD:HEAD + 1, :]
            o_t = acc[0:HEAD, 0:tq] * (1.0 / den[:, 0:tq]) - acc[0:HEAD, tq:] * (lam / den[:, tq:])
            o_ref[s * tq:(s + 1) * tq, h * HEAD:(h + 1) * HEAD] = o_t.T

    for s, h, c in items:
        if c == 0:
            qt2[s, h] = _qt2(q_ref[s * tq:(s + 1) * tq, h * HEAD:(h + 1) * HEAD])
        pending.append((s, h, c, _dot(k_fn(s, h, c), qt2[s, h])))
        if len(pending) > ATTN_LOOKAHEAD:
            finish(*pending.pop(0))
    while pending:
        finish(*pending.pop(0))


def _attn_prompt_kernel(q_ref, k_ref, v_ref, lam_ref, o_ref, *, lam_init, seq):
    lam = _lambda(lam_ref, lam_init)
    rows = lambda s: slice(s * seq, (s + 1) * seq)
    _diff_attn_heads(q_ref, lambda s, h, c: k_ref[rows(s), h * HEAD:(h + 1) * HEAD].astype(BF16),
                     lambda s, h, c: _vt_aug(v_ref[rows(s), h * HEAD:(h + 1) * HEAD]), 1, lam, o_ref,
                     n_seq=q_ref.shape[0] // seq)


def _attn_prompt(q, k, v, lam_p, seq, lam_init):
    n_tok = q.shape[0]
    tok = pl.BlockSpec((PROMPT_SEQS_PER_STEP * seq, D), lambda b: (b, 0))
    return pl.pallas_call(
        functools.partial(_attn_prompt_kernel, lam_init=lam_init, seq=seq),
        grid=(n_tok // (PROMPT_SEQS_PER_STEP * seq),),
        in_specs=[tok, tok, tok, pl.BlockSpec(lam_p.shape, lambda b: (0, 0))],
        out_specs=tok,
        out_shape=jax.ShapeDtypeStruct((n_tok, D), F32),
        compiler_params=_cparams(1),
        name="diff_attn_prompt",
    )(q, k, v, lam_p)


def _attn_sample_kernel(q_ref, k_ref, v_ref, ck_ref, cv_ref, lam_ref, o_ref, kcat, vtcat, *, lam_init, past):
    n_keys = kcat.shape[0]
    chunks = [(c0, min(KEY_CHUNK, n_keys - c0)) for c0 in range(0, n_keys, KEY_CHUNK)]

    @pl.when(pl.program_id(1) == 0)
    def _gather_keys():
        kcat[0:past, :] = ck_ref[...].astype(BF16)
        kcat[past:, :] = k_ref[...]
        for h in range(DA_HEADS):
            sl = slice(h * HEAD, (h + 1) * HEAD)
            for c0 in range(0, n_keys, past):
                src, r0 = (cv_ref, c0) if c0 < past else (v_ref, c0 - past)
                vtcat[h, :, c0:c0 + past] = _vt_aug(src[r0:r0 + past, sl])

    lam = _lambda(lam_ref, lam_init)
    _diff_attn_heads(q_ref, lambda s, h, c: kcat[chunks[c][0]:chunks[c][0] + chunks[c][1], h * HEAD:(h + 1) * HEAD],
                     lambda s, h, c: vtcat[h, :, chunks[c][0]:chunks[c][0] + chunks[c][1]], len(chunks), lam, o_ref)


def _attn_sample(q, k, v, ck, cv, lam_p, n_seq, seq, lam_init, tq=128):
    past = ck.shape[1]
    qb = pl.BlockSpec((tq, D), lambda b, i: (b * (seq // tq) + i, 0))
    kv = pl.BlockSpec((seq, D), lambda b, i: (b, 0))
    cache = pl.BlockSpec((None, past, D), lambda b, i: (b, 0, 0))
    return pl.pallas_call(
        functools.partial(_attn_sample_kernel, lam_init=lam_init, past=past),
        grid=(n_seq, seq // tq),
        in_specs=[qb, kv, kv, cache, cache, pl.BlockSpec(lam_p.shape, lambda b, i: (0, 0))],
        out_specs=qb,
        out_shape=jax.ShapeDtypeStruct((n_seq * seq, D), F32),
        scratch_shapes=[pltpu.VMEM((past + seq, D), BF16), pltpu.VMEM((DA_HEADS, VT_ROWS, past + seq), BF16)],
        compiler_params=_cparams(2),
        name="diff_attn_sample",
    )(q, k, v, ck, cv, lam_p)


def kernel(x_prompt, x_sample, c, c_ctx, cache_attn_k, cache_attn_v, state_mlstm_C, state_mlstm_n, state_mlstm_m,
           state_hgrn_S, ada_w, ada_b, norm_mix_g, norm_ffn_g, ev_w_in, ev_gate_b, ev_lb_logits, ml_norm_g,
           hg_norm_g, ev_w_out, od_w_in, od_lambda, da_norm_g, od_w_out, ffn_w1, ffn_w3, ffn_w2, final_norm_g):
    assert DEPTH % 2 == 0
    n_p, s_p, _ = x_prompt.shape
    n_s, s_s, _ = x_sample.shape
    past = cache_attn_k.shape[2]
    assert s_p == SEG and s_s % SEG == 0 and s_s % PROJ_TILE == 0 and (n_p * s_p) % PROJ_TILE == 0
    assert PROJ_TILE % TOKEN_TILE == 0
    xp = x_prompt.astype(F32).reshape(n_p * s_p, D)
    xs = x_sample.astype(F32).reshape(n_s * s_s, D)

    cond8 = jnp.zeros((8, D), F32).at[0].set(c_ctx.astype(F32)).at[1:1 + n_s].set(c.astype(F32))
    mod = _modulation(cond8, ada_w.astype(F32), ada_b.astype(F32)).reshape(DEPTH, 8, 6, D)
    streams = [(xp, None), (xs, s_s)]
    outputs = {}
    w1_all, w3_all, w2_all = ffn_w1.astype(BF16), ffn_w3.astype(BF16), ffn_w2.astype(BF16)

    for l in range(DEPTH):
        mix_gain = norm_mix_g[l].astype(F32).reshape(1, D)
        ffn_gain = norm_ffn_g[l].astype(F32).reshape(1, D)
        if l % 2 == 0:
            e = l // 2
            w_in = _regroup_even_weights(ev_w_in, e)
            gate_b = jnp.pad(ev_gate_b[e].astype(F32), (0, HEAD - 16)).reshape(1, HEAD)
            norm_gain = jnp.concatenate([ml_norm_g[e], hg_norm_g[e]]).astype(F32).reshape(1, D)
            w_out = ev_w_out[e].astype(BF16)
            mixed = []
            for si, (x, cond_tok) in enumerate(streams):
                qkv, gate, hqi, hgf, og = _inproj_even(x, mod[l], cond_tok, mix_gain, w_in, gate_b,
                                                       ev_lb_logits.astype(F32), e)
                if si == 0:
                    *hs, c_new, n_new, m_new, s_new = _scan(qkv, gate, hqi, hgf, n_p, s_p // SEG, emit_state=True)
                    outputs.setdefault("C", []).append(c_new)
                    outputs.setdefault("n", []).append(n_new)
                    outputs.setdefault("m", []).append(m_new[:, 0, 0:8].reshape(n_p, 2, 4))
                    outputs.setdefault("S", []).append(s_new)
                else:
                    init = (state_mlstm_C[:, e].astype(F32), state_mlstm_n[:, e].astype(F32),
                            jnp.pad(state_mlstm_m[:, e].astype(F32).reshape(n_s, 1, 8), ((0, 0), (0, 0), (0, HEAD - 8))),
                            state_hgrn_S[:, e].astype(F32))
                    hs = _scan(qkv, gate, hqi, hgf, n_s, s_s // SEG, init=init)
                mixed.append((list(hs), og, x))
        else:
            o = l // 2
            lam_init = 0.8 - 0.6 * math.exp(-0.3 * l)
            w_in = od_w_in[o].astype(BF16)
            norm_gain = (jnp.tile(da_norm_g[o].astype(F32), DA_HEADS) * (1.0 - lam_init)).reshape(1, D)
            w_out = od_w_out[o].astype(BF16)
            lam_p = od_lambda[o].astype(F32)
            mixed = []
            for si, (x, cond_tok) in enumerate(streams):
                if si == 0:
                    q, k, v = _inproj_odd(x, mod[l], cond_tok, mix_gain, w_in, None, F32)
                    outputs.setdefault("k", []).append(k.reshape(n_p, s_p, DA_HEADS, HEAD))
                    outputs.setdefault("v", []).append(v.reshape(n_p, s_p, DA_HEADS, HEAD))
                    att = _attn_prompt(q, k, v, lam_p, s_p, lam_init)
                else:
                    q, k, v = _inproj_odd(x, mod[l], cond_tok, mix_gain, w_in, _rope_tables(s_s), BF16)
                    ck = cache_attn_k[:, o].reshape(n_s, past, D)
                    cv = cache_attn_v[:, o].reshape(n_s, past, D)
                    att = _attn_sample(q, k, v, ck, cv, lam_p, n_s, s_s, lam_init)
                mixed.append(([att], None, x))
        final_gain = final_norm_g.astype(F32).reshape(1, D) if l == DEPTH - 1 else None
        streams = [(_mix_ffn(hs, og, x, mod[l], cond_tok, norm_gain, w_out, ffn_gain, l, w1_all, w3_all, w2_all,
                             final_gain), cond_tok)
                   for (hs, og, x), (_, cond_tok) in zip(mixed, streams)]

    y_prompt = streams[0][0].reshape(n_p, s_p, D)
    y_sample = streams[1][0].reshape(n_s, s_s, D)
    stack = lambda name: outputs[name][0][:, None] if len(outputs[name]) == 1 else jnp.stack(outputs[name], axis=1)
    return (y_prompt, y_sample, stack("k"), stack("v"), stack("C"), stack("n"), stack("m"), stack("S"))
```

```python
import functools
import math

import jax
import jax.numpy as jnp
import numpy as np
from jax import lax
from jax.experimental import pallas as pl
from jax.experimental.pallas import tpu as pltpu

F32 = jnp.float32
BF16 = jnp.bfloat16

D = 1024
DEPTH = 2
GRID_W = 64
ML_HEADS = 4
HG_HEADS = 4
HEAD = 128
DA_HEADS = 8
DA_DQK = 64
ROPE_BASE = 10000.0
LOG2E = math.log2(math.e)
EPS = 1e-6
D_FF = ((8 * D // 3 + 255) // 256) * 256
EV_SIZES = (512, 512, 512, 512, 16, 512, 512, 512, 512, 512)
EV_COLS = 9 * 512 + 128

SEG = 256
HALF = SEG // 2
TILE_LEVELS = (1, 2, 4, 8, 16, 32, 64)
TOKEN_TILE = 512
PROJ_TILE = 1024
FFN_CHUNK = 256
KEY_CHUNK = 256
PROMPT_SEQS_PER_STEP = 4
ATTN_LOOKAHEAD = 6
VMEM_LIMIT = 56 * 1024 * 1024


def _cparams(n_axes):
    return pltpu.CompilerParams(dimension_semantics=("arbitrary",) * n_axes, vmem_limit_bytes=VMEM_LIMIT)


def _sigmoid(x):
    return 1.0 / (1.0 + jnp.exp(-x))


def _silu(x):
    return x * _sigmoid(x)


def _log_sigmoid(x):
    return jnp.minimum(x, 0.0) - jnp.log(1.0 + jnp.exp(-jnp.abs(x)))


def _dot(a, b):
    return jnp.dot(a, b, preferred_element_type=F32)


def _dot_nt(a, b):
    return lax.dot_general(a, b, (((1,), (1,)), ((), ())), preferred_element_type=F32)


def _norm_mod(x, gain, mod, k):
    ms = jnp.mean(x * x, axis=-1, keepdims=True)
    return x * lax.rsqrt(ms + EPS) * gain * (1.0 + mod[3 * k + 1:3 * k + 2]) + mod[3 * k:3 * k + 1]


def _mod_row(cond_tokens, tm):
    if cond_tokens is None:
        return lambda i: 0
    return lambda i: 1 + i // (cond_tokens // tm)


def _mod_kernel(c_ref, w_ref, b_ref, o_ref):
    s = _silu(c_ref[...]).astype(BF16)
    o_ref[...] = _dot(s, w_ref[...].astype(BF16)) + b_ref[...]


def _modulation(cond8, ada_w, ada_b):
    n_layers = ada_w.shape[0]
    tn = 1536
    return pl.pallas_call(
        _mod_kernel,
        grid=(n_layers, 6 * D // tn),
        in_specs=[pl.BlockSpec((8, D), lambda l, n: (0, 0)),
                  pl.BlockSpec((None, D, tn), lambda l, n: (l, 0, n)),
                  pl.BlockSpec((None, 1, tn), lambda l, n: (l, 0, n))],
        out_specs=pl.BlockSpec((None, 8, tn), lambda l, n: (l, 0, n)),
        out_shape=jax.ShapeDtypeStruct((n_layers, 8, 6 * D), F32),
        compiler_params=_cparams(2),
        name="ada_modulation",
    )(cond8, ada_w, ada_b.reshape(n_layers, 1, 6 * D))


def _regroup_even_weights(ev_w_in, e):
    g0 = sum(EV_SIZES[:4])
    w_e = ev_w_in[e].astype(BF16)
    return jnp.concatenate([w_e[:, :g0], w_e[:, g0 + 16:], jnp.pad(w_e[:, g0:g0 + 16], ((0, 0), (0, HEAD - 16)))],
                           axis=1)


def _inproj_even_body(x, mod, gain, w_ref, gate_b, lb_logits, e_idx, qkv_ref, gate_ref, hqi_ref, hgf_ref, og_ref):
    h = _norm_mod(x, gain, mod, 0).astype(BF16)

    def proj(c0, n):
        return _dot(h, w_ref[:, c0:c0 + n])

    qkv_ref[:, 0:512] = (proj(0, 512) * (HEAD ** -0.5)).astype(BF16)
    qkv_ref[:, 512:1536] = proj(512, 1024).astype(BF16)
    og_ref[:, 0:512] = proj(1536, 512)
    hqi_ref[:, 0:512] = proj(2048, 512).astype(BF16)
    hqi_ref[:, 512:1024] = proj(3584, 512).astype(BF16)
    mx = jnp.max(lb_logits, axis=0, keepdims=True)
    ex = jnp.exp(lb_logits - mx)
    lb = jnp.sum(ex[0:e_idx + 1], axis=0, keepdims=True) / jnp.sum(ex, axis=0, keepdims=True)
    hgf_ref[:, 0:512] = jnp.log(lb + (1.0 - lb) * _sigmoid(proj(2560, 512)))
    hgf_ref[:, 512:1024] = jnp.log(lb + (1.0 - lb) * _sigmoid(proj(3072, 512)))
    og_ref[:, 512:1024] = proj(4096, 512)
    gt = proj(4608, 128) + gate_b
    lane = lax.broadcasted_iota(jnp.int32, gt.shape, 1)
    gate_ref[...] = jnp.where(lane < 8, gt, jnp.where(lane < 16, _log_sigmoid(gt), 0.0))


def _inproj_even_kernel(x_ref, mod_ref, g_ref, w_ref, gb_ref, lbl_ref, qkv_ref, gate_ref, hqi_ref, hgf_ref,
                        og_ref, *, e_idx):
    _inproj_even_body(x_ref[...], mod_ref[...], g_ref[...], w_ref, gb_ref[...], lbl_ref[...], e_idx,
                      qkv_ref, gate_ref, hqi_ref, hgf_ref, og_ref)


def _inproj_even(x, mod_l, cond_tokens, gain, w, gate_b, lb_logits, e_idx):
    n_tok = x.shape[0]
    tm = PROJ_TILE
    tile_mod_row = _mod_row(cond_tokens, tm)
    tok = lambda n: pl.BlockSpec((tm, n), lambda i: (i, 0))
    full = lambda a: pl.BlockSpec(a.shape, lambda i: (0,) * a.ndim, pipeline_mode=pl.Buffered(1))
    return pl.pallas_call(
        functools.partial(_inproj_even_kernel, e_idx=e_idx),
        grid=(n_tok // tm,),
        in_specs=[tok(D), pl.BlockSpec((None, 6, D), lambda i: (tile_mod_row(i), 0, 0)),
                  full(gain), full(w), full(gate_b), full(lb_logits)],
        out_specs=[tok(1536), tok(128), tok(1024), tok(1024), tok(1024)],
        out_shape=[jax.ShapeDtypeStruct((n_tok, 1536), BF16), jax.ShapeDtypeStruct((n_tok, 128), F32),
                   jax.ShapeDtypeStruct((n_tok, 1024), BF16), jax.ShapeDtypeStruct((n_tok, 1024), F32),
                   jax.ShapeDtypeStruct((n_tok, 1024), F32)],
        compiler_params=_cparams(1),
        name="inproj_even",
    )(x, mod_l, gain, w, gate_b, lb_logits)


def _shift_rows(x, k, fill, up):
    n = x.shape[0]
    if k % 8 == 0:
        pad = jnp.full((k,) + x.shape[1:], 0.0 if fill is None else fill, x.dtype)
        return jnp.concatenate([x[k:], pad], axis=0) if up else jnp.concatenate([pad, x[:n - k]], axis=0)
    y = pltpu.roll(x, (n - k) if up else k, 0)
    if fill is None:
        return y
    row = lax.broadcasted_iota(jnp.int32, x.shape, 0)
    return jnp.where(row >= n - k, fill, y) if up else jnp.where(row < k, fill, y)


def _cummax_rows(x, rev):
    k = 1
    while k < x.shape[0]:
        x = jnp.maximum(x, _shift_rows(x, k, -jnp.inf, up=rev))
        k *= 2
    return x


def _cumsum_rows(tri, x):
    hi = x.astype(BF16)
    r1 = x - hi.astype(F32)
    mid = r1.astype(BF16)
    lo = (r1 - mid.astype(F32)).astype(BF16)
    return _dot(tri, hi) + _dot(tri, mid) + _dot(tri, lo)


def _hgrn_tile(q, kk, f, a, q_b, rev, mask_sc, d, up_small):
    ops = []
    bm = a
    for li, m in enumerate(TILE_LEVELS):
        if m < 8:
            qrole = jnp.logical_not(up_small[li]) if rev else up_small[li]
            if m == 1:
                r = jnp.where(qrole, q * f, kk)
            else:
                x = jnp.where(qrole, _shift_rows(bm, m, None, up=rev), bm)
                r = jnp.where(qrole, q, kk) * jnp.exp(jnp.where(qrole, a - x, x - a))
            if 2 * m < 8:
                bm = jnp.where(qrole, bm, _shift_rows(bm, m, None, up=not rev))
        else:
            pieces = []
            for b0 in range(0, HALF, 2 * m):
                lo, up = slice(b0, b0 + m), slice(b0 + m, b0 + 2 * m)
                if rev:
                    ref = a[b0 + m:b0 + m + 1]
                    pieces += [q[lo] * jnp.exp(a[lo] - ref), kk[up] * jnp.exp(ref - a[up])]
                else:
                    ref = a[b0 + m - 1:b0 + m]
                    pieces += [kk[lo] * jnp.exp(ref - a[lo]), q[up] * jnp.exp(a[up] - ref)]
            r = jnp.concatenate(pieces, axis=0)
        ops.append(r.astype(BF16))
    p = _dot_nt(q_b, kk.astype(BF16)).astype(BF16) * mask_sc[d, len(TILE_LEVELS)]
    for li, rb in enumerate(ops):
        p = p + _dot_nt(rb, rb).astype(BF16) * mask_sc[d, li]
    return p


def _build_scan_constants(tri_sc, mask_sc):
    row = lax.broadcasted_iota(jnp.int32, (SEG, SEG), 0)
    col = lax.broadcasted_iota(jnp.int32, (SEG, SEG), 1)
    tri_sc[0] = (col <= row).astype(BF16)
    tri_sc[1] = (col >= row).astype(BF16)
    trow = lax.broadcasted_iota(jnp.int32, (HALF, HALF), 0)
    tcol = lax.broadcasted_iota(jnp.int32, (HALF, HALF), 1)
    for li, m in enumerate(TILE_LEVELS):
        sh = m.bit_length() - 1
        same = (trow >> (sh + 1)) == (tcol >> (sh + 1))
        t_up = ((trow >> sh) & 1) == 1
        s_up = ((tcol >> sh) & 1) == 1
        mask_sc[0, li] = (same & t_up & jnp.logical_not(s_up)).astype(BF16)
        mask_sc[1, li] = (same & s_up & jnp.logical_not(t_up)).astype(BF16)
    eye = (trow == tcol).astype(BF16)
    mask_sc[0, len(TILE_LEVELS)] = eye
    mask_sc[1, len(TILE_LEVELS)] = eye


def _scan_segment(dir_refs, write_h, tri_sc, mask_sc, m_in, state, emit):
    row = lax.broadcasted_iota(jnp.int32, (SEG, SEG), 0)
    col = lax.broadcasted_iota(jnp.int32, (SEG, SEG), 1)
    lane128 = lax.broadcasted_iota(jnp.int32, (SEG, HEAD), 1)
    e0 = (lane128 == 0).astype(BF16)
    row_half = lax.broadcasted_iota(jnp.int32, (HALF, HEAD), 0)
    up_masks = [((row_half >> (m.bit_length() - 1)) & 1) == 1 for m in TILE_LEVELS if m < 8]
    tmasks = (col <= row, col >= row)
    use_state = state is not None
    want_state = use_state or emit is not None
    m_out_rows = []
    env = {}

    def prep(d):
        _, gate_ref, _, hgf_ref = dir_refs[d]
        rev = d == 1
        last = 0 if rev else SEG - 1
        tri = tri_sc[d]
        slab = gate_ref[...]
        b_al = pltpu.roll(_cumsum_rows(tri, slab), HEAD - 8, 1)
        u = slab - b_al
        mx = jnp.maximum(_cummax_rows(u, rev), m_in)
        mx_last = mx[last:last + 1, :]
        m_out_rows.append(b_al[last:last + 1, :] + mx_last)
        lf_all = hgf_ref[:, 512 * d:512 * (d + 1)]
        f_all = jnp.exp(lf_all)
        env[d] = dict(rev=rev, last=last, mx=mx, w_inter=jnp.exp(m_in - mx), e_den=jnp.exp(-(b_al + mx)),
                      decay=jnp.exp(m_in - mx_last), wg=jnp.exp(u - mx_last), u_t=u.T,
                      a_all=_cumsum_rows(tri, lf_all), f_all=f_all, kk_all=1.0 - f_all)

    def ml_main(d, h):
        e, qkv_ref, c = env[d], dir_refs[d][0], 4 * d + h
        q = qkv_ref[:, h * HEAD:(h + 1) * HEAD]
        k = qkv_ref[:, 512 + h * HEAD:512 + (h + 1) * HEAD]
        v = qkv_ref[:, 1024 + h * HEAD:1024 + (h + 1) * HEAD]
        dm = jnp.where(tmasks[d], jnp.exp(e["u_t"][c:c + 1, :] - e["mx"][:, c:c + 1]), 0.0)
        s = (_dot_nt(q, k) * dm).astype(BF16)
        v_aug = jnp.concatenate([v, e0], axis=1)
        numden = _dot(s, v_aug)
        if use_state:
            numden = numden + e["w_inter"][:, c:c + 1] * _dot(q, state[0][d, h].astype(BF16))
        den = jnp.maximum(jnp.abs(numden[:, HEAD:HEAD + 1]), e["e_den"][:, c:c + 1])
        write_h(d, h * HEAD, numden[:, 0:HEAD] / den)
        e["ml", h] = (k, v_aug)

    def ml_state(d, h):
        e, c = env[d], 4 * d + h
        k, v_aug = e["ml", h]
        kw_t = (k.astype(F32) * e["wg"][:, c:c + 1]).T.astype(BF16)
        upd = _dot(kw_t, v_aug)
        if use_state:
            upd = upd + e["decay"][:, c:c + 1] * state[0][d, h]
            state[0][d, h] = upd
        if emit is not None:
            emit("ml", d, h, upd)

    def hg_main(d, h):
        e, hqi_ref = env[d], dir_refs[d][2]
        rev = e["rev"]
        sl = slice(h * HEAD, (h + 1) * HEAD)
        a, kk, f = e["a_all"][:, sl], e["kk_all"][:, sl], e["f_all"][:, sl]
        q_b = hqi_ref[:, h * HEAD:(h + 1) * HEAD]
        i_b = hqi_ref[:, 512 + h * HEAD:512 + (h + 1) * HEAD]
        q = q_b.astype(F32)
        qh, kh = (0, 1) if rev else (1, 0)
        a_q, a_k = a[qh * HALF:(qh + 1) * HALF], a[kh * HALF:(kh + 1) * HALF]
        a_mid = a[HALF:HALF + 1] if rev else a[HALF - 1:HALF]
        r_q = (q[qh * HALF:(qh + 1) * HALF] * jnp.exp(a_q - a_mid)).astype(BF16)
        r_k = (kk[kh * HALF:(kh + 1) * HALF] * jnp.exp(a_mid - a_k)).astype(BF16)
        p_cross = _dot_nt(r_q, r_k).astype(BF16)
        p_tiles = []
        for t in range(2):
            rows = slice(t * HALF, (t + 1) * HALF)
            p_tiles.append(_hgrn_tile(q[rows], kk[rows], f[rows], a[rows], q_b[rows], rev, mask_sc, d, up_masks))
        i_lo, i_hi = i_b[0:HALF], i_b[HALF:SEG]
        if rev:
            o = jnp.concatenate([_dot(p_tiles[0], i_lo) + _dot(p_cross, i_hi), _dot(p_tiles[1], i_hi)], axis=0)
        else:
            o = jnp.concatenate([_dot(p_tiles[0], i_lo), _dot(p_cross, i_lo) + _dot(p_tiles[1], i_hi)], axis=0)
        st = state[1][d, h] if use_state else None
        if use_state:
            o = o + _dot_nt((q * jnp.exp(a)).astype(BF16), st.astype(BF16))
        write_h(d, 512 + h * HEAD, o)
        e["hg", h] = (a, kk, i_b, st)

    def hg_state(d, h):
        e = env[d]
        a, kk, i_b, st = e["hg", h]
        a_l = a[e["last"]:e["last"] + 1, :]
        kd = (kk * jnp.exp(a_l - a)).astype(BF16)
        st_new = _dot(i_b.astype(F32).T.astype(BF16), kd)
        if use_state:
            st_new = st_new + st * jnp.exp(a_l)
            state[1][d, h] = st_new
        if emit is not None:
            emit("hg", d, h, st_new)

    for d in range(2):
        prep(d)
        for h in range(ML_HEADS):
            ml_main(d, h)
            hg_main(d, h)
            if want_state:
                ml_state(d, h)
                hg_state(d, h)

    lane_row = lax.broadcasted_iota(jnp.int32, (1, HEAD), 1)
    return jnp.where(lane_row < 4, m_out_rows[0], jnp.where(lane_row < 8, m_out_rows[1], 0.0))


def _emit_states(kind, d, h, value, c_ref, n_ref, s_ref):
    if kind == "ml":
        c_ref[d, h] = value[:, 0:HEAD]
        n_ref[d, h:h + 1, :] = value[:, HEAD:2 * HEAD].T[0:1, :]
    else:
        s_ref[d, h] = value.T


def _scan_kernel(*refs, has_init, emit_state, nseg):
    single = nseg == 1
    n_tok_in = 4 if single else 8
    n_h = 1 if single else 2
    n_in = n_tok_in + (4 if has_init else 0)
    n_out = n_h + (4 if emit_state else 0)
    ins, outs, scr = refs[:n_in], refs[n_in:n_in + n_out], refs[n_in + n_out:]
    dir_refs = (ins[0:4], ins[0:4] if single else ins[4:8])
    st_outs = outs[n_h:]
    tri_sc, mask_sc, caug_sc, st_sc, m_sc = scr
    use_state = has_init or nseg > 1
    b_id, j = pl.program_id(0), pl.program_id(1)

    @pl.when(jnp.logical_and(b_id == 0, j == 0))
    def _build_constants():
        _build_scan_constants(tri_sc, mask_sc)

    if use_state:
        @pl.when(j == 0)
        def _init_state():
            if has_init:
                c0_ref, n0_ref, m0_ref, s0_ref = ins[n_tok_in:n_tok_in + 4]
                for d in range(2):
                    for h in range(ML_HEADS):
                        caug_sc[d, h, :, 0:HEAD] = c0_ref[d, h]
                        caug_sc[d, h, :, HEAD:2 * HEAD] = jnp.broadcast_to(n0_ref[d, h:h + 1, :], (HEAD, HEAD)).T
                        st_sc[d, h] = s0_ref[d, h].T
                m_sc[...] = jnp.broadcast_to(m0_ref[...], m_sc.shape)
            else:
                caug_sc[...] = jnp.zeros(caug_sc.shape, F32)
                st_sc[...] = jnp.zeros(st_sc.shape, F32)
                m_sc[...] = jnp.zeros(m_sc.shape, F32)

    def write_h(d, c0, value):
        if single and d == 1:
            outs[0][:, c0:c0 + HEAD] += value
        else:
            outs[0 if single else d][:, c0:c0 + HEAD] = value

    def emit(kind, d, h, value):
        if single:
            _emit_states(kind, d, h, value, st_outs[0], st_outs[1], st_outs[3])
        else:
            pl.when(j == nseg - 1)(lambda: _emit_states(kind, d, h, value, st_outs[0], st_outs[1], st_outs[3]))

    m_in = m_sc[0:1, :] if use_state else jnp.zeros((1, HEAD), F32)
    m_new = _scan_segment(dir_refs, write_h, tri_sc, mask_sc, m_in, (caug_sc, st_sc) if use_state else None,
                          emit if emit_state else None)
    if use_state:
        m_sc[...] = jnp.broadcast_to(m_new, m_sc.shape)
    if emit_state:
        if single:
            st_outs[2][...] = m_new
        else:
            @pl.when(j == nseg - 1)
            def _emit_m():
                st_outs[2][...] = m_new


def _scan(qkv, gate, hqi, hgf, n_seq, nseg, init=None, emit_state=False):
    n_tok = qkv.shape[0]
    fwd = lambda n: pl.BlockSpec((SEG, n), lambda b, j: (b * nseg + j, 0))
    bwd = lambda n: pl.BlockSpec((SEG, n), lambda b, j: (b * nseg + nseg - 1 - j, 0))
    widths = (1536, 128, 1024, 1024)
    single = nseg == 1
    in_specs = [fwd(n) for n in widths] + ([] if single else [bwd(n) for n in widths])
    args = [qkv, gate, hqi, hgf] + ([] if single else [qkv, gate, hqi, hgf])
    mat = pl.BlockSpec((None, 2, 4, HEAD, HEAD), lambda b, j: (b, 0, 0, 0, 0))
    vec = pl.BlockSpec((None, 2, 4, HEAD), lambda b, j: (b, 0, 0, 0))
    sca = pl.BlockSpec((None, 1, HEAD), lambda b, j: (b, 0, 0))
    if init is not None:
        in_specs += [mat, vec, sca, mat]
        args += list(init)
    out_specs = [fwd(1024)] if single else [fwd(1024), bwd(1024)]
    out_shape = [jax.ShapeDtypeStruct((n_tok, 1024), F32)] * len(out_specs)
    if emit_state:
        out_specs += [mat, vec, sca, mat]
        out_shape += [jax.ShapeDtypeStruct((n_seq, 2, 4, HEAD, HEAD), F32),
                      jax.ShapeDtypeStruct((n_seq, 2, 4, HEAD), F32),
                      jax.ShapeDtypeStruct((n_seq, 1, HEAD), F32),
                      jax.ShapeDtypeStruct((n_seq, 2, 4, HEAD, HEAD), F32)]
    return pl.pallas_call(
        functools.partial(_scan_kernel, has_init=init is not None, emit_state=emit_state, nseg=nseg),
        grid=(n_seq, nseg),
        in_specs=in_specs,
        out_specs=out_specs,
        out_shape=out_shape,
        scratch_shapes=[pltpu.VMEM((2, SEG, SEG), BF16),
                        pltpu.VMEM((2, len(TILE_LEVELS) + 1, HALF, HALF), BF16),
                        pltpu.VMEM((2, 4, HEAD, 2 * HEAD), F32),
                        pltpu.VMEM((2, 4, HEAD, HEAD), F32),
                        pltpu.VMEM((8, HEAD), F32)],
        compiler_params=_cparams(2),
        name="bidir_scan",
    )(*args)


def _mix_ffn_tile(h, og, x, mod, norm_gain, wo_ref, ffn_gain, w1_ref, w3_ref, w2_ref, final_gain, o_ref, chunk):
    parts = []
    for g in range(D // HEAD):
        hs = h[:, g * HEAD:(g + 1) * HEAD]
        parts.append(hs * lax.rsqrt(jnp.mean(hs * hs, axis=-1, keepdims=True) + EPS))
    hn = jnp.concatenate(parts, axis=1) * norm_gain
    if og is not None:
        hn = hn * jnp.concatenate([_sigmoid(og[:, 0:512]), _silu(og[:, 512:1024])], axis=1)
    x = x + mod[2:3, :] * _dot(hn.astype(BF16), wo_ref[...])
    hf = _norm_mod(x, ffn_gain, mod, 1).astype(BF16)
    up = lambda c0: (_dot(hf, w1_ref[:, c0:min(c0 + chunk, D_FF)]), _dot(hf, w3_ref[:, c0:min(c0 + chunk, D_FF)]))
    starts = list(range(0, D_FF, chunk))
    acc = jnp.zeros(x.shape, F32)
    nxt = up(starts[0])
    for i, c0 in enumerate(starts):
        a, b = nxt
        if i + 1 < len(starts):
            nxt = up(starts[i + 1])
        acc = acc + _dot((_silu(a) * b).astype(BF16), w2_ref[c0:min(c0 + chunk, D_FF), :])
    y = x + mod[5:6, :] * acc
    if final_gain is not None:
        y = y * lax.rsqrt(jnp.mean(y * y, axis=-1, keepdims=True) + EPS) * final_gain
    o_ref[...] = y


def _mix_ffn_kernel(*refs, n_h, gated, final, chunk):
    h_refs = refs[:n_h]
    pos = n_h
    og_ref = refs[pos] if gated else None
    pos += 1 if gated else 0
    x_ref, mod_ref, ng_ref, wo_ref, g_ref, w1_ref, w3_ref, w2_ref = refs[pos:pos + 8]
    fg_ref = refs[pos + 8] if final else None
    o_ref = refs[-1]
    h = h_refs[0][...]
    for r in h_refs[1:]:
        h = h + r[...]
    _mix_ffn_tile(h, og_ref[...] if gated else None, x_ref[...], mod_ref[...], ng_ref[...], wo_ref, g_ref[...],
                  w1_ref, w3_ref, w2_ref, fg_ref[...] if final else None, o_ref, chunk)


def _mix_ffn(hs, og, x, mod_l, cond_tokens, norm_gain, w_out, ffn_gain, layer, w1, w3, w2, final_gain=None):
    n_tok = x.shape[0]
    tm = TOKEN_TILE
    tile_mod_row = _mod_row(cond_tokens, tm)
    tok = pl.BlockSpec((tm, D), lambda i: (i, 0))
    full = lambda a: pl.BlockSpec(a.shape, lambda i: (0,) * a.ndim, pipeline_mode=pl.Buffered(1))
    of_layer = lambda a: pl.BlockSpec((None,) + a.shape[1:], lambda i: (layer, 0, 0), pipeline_mode=pl.Buffered(1))
    gated = og is not None
    final = final_gain is not None
    consts = [norm_gain, w_out, ffn_gain, w1, w3, w2] + ([final_gain] if final else [])
    args = list(hs) + ([og] if gated else []) + [x, mod_l] + consts
    in_specs = [tok] * (len(hs) + (1 if gated else 0) + 1)
    in_specs += [pl.BlockSpec((None, 6, D), lambda i: (tile_mod_row(i), 0, 0))]
    in_specs += [of_layer(a) if a.ndim == 3 else full(a) for a in consts]
    return pl.pallas_call(
        functools.partial(_mix_ffn_kernel, n_h=len(hs), gated=gated, final=final, chunk=FFN_CHUNK),
        grid=(n_tok // tm,),
        in_specs=in_specs,
        out_specs=tok,
        out_shape=jax.ShapeDtypeStruct((n_tok, D), F32),
        compiler_params=_cparams(1),
        name="mix_ffn",
    )(*args)


def _rope(x, cos, sin_signed):
    lane = lax.broadcasted_iota(jnp.int32, (x.shape[0], HEAD), 1)
    first = (lane & 16) == 0
    parts = []
    for h in range(DA_HEADS):
        xh = x[:, h * HEAD:(h + 1) * HEAD]
        partner = jnp.where(first, pltpu.roll(xh, HEAD - 16, 1), pltpu.roll(xh, 16, 1))
        parts.append(xh * cos + partner * sin_signed)
    return jnp.concatenate(parts, axis=1)


def _inproj_odd_kernel(*refs, rope):
    x_ref, mod_ref, g_ref, w_ref = refs[:4]
    q_ref, k_ref, v_ref = refs[-3:]
    h = _norm_mod(x_ref[...], g_ref[...], mod_ref[...], 0).astype(BF16)
    q = _dot(h, w_ref[:, 0:D])
    k = _dot(h, w_ref[:, D:2 * D])
    v = _dot(h, w_ref[:, 2 * D:3 * D])
    if rope:
        cos, sin_signed = refs[4][...], refs[5][...]
        q = _rope(q, cos, sin_signed)
        k = _rope(k, cos, sin_signed)
    q_ref[...] = (q * (DA_DQK ** -0.5 * LOG2E)).astype(q_ref.dtype)
    k_ref[...] = k.astype(k_ref.dtype)
    v_ref[...] = v.astype(v_ref.dtype)


def _inproj_odd(x, mod_l, cond_tokens, gain, w, rope_tables, kv_dtype):
    n_tok = x.shape[0]
    tm = PROJ_TILE
    tile_mod_row = _mod_row(cond_tokens, tm)
    tok = pl.BlockSpec((tm, D), lambda i: (i, 0))
    full = lambda a: pl.BlockSpec(a.shape, lambda i: (0,) * a.ndim, pipeline_mode=pl.Buffered(1))
    args = [x, mod_l, gain, w]
    in_specs = [tok, pl.BlockSpec((None, 6, D), lambda i: (tile_mod_row(i), 0, 0)), full(gain), full(w)]
    if rope_tables is not None:
        tiles_per_seq = rope_tables[0].shape[0] // tm
        args += list(rope_tables)
        in_specs += [pl.BlockSpec((tm, HEAD), lambda i: (i % tiles_per_seq, 0))] * 2
    return pl.pallas_call(
        functools.partial(_inproj_odd_kernel, rope=rope_tables is not None),
        grid=(n_tok // tm,),
        in_specs=in_specs,
        out_specs=[tok, tok, tok],
        out_shape=[jax.ShapeDtypeStruct((n_tok, D), BF16), jax.ShapeDtypeStruct((n_tok, D), kv_dtype),
                   jax.ShapeDtypeStruct((n_tok, D), kv_dtype)],
        compiler_params=_cparams(1),
        name="inproj_odd",
    )(*args)


def _rope_tables(n_tok):
    quarter = DA_DQK // 4
    tok = np.arange(n_tok)
    pos = np.stack([tok // GRID_W, tok % GRID_W], axis=1).astype(np.float32)
    inv = (np.float32(ROPE_BASE) ** (-np.arange(quarter, dtype=np.float32) / np.float32(quarter))).astype(np.float32)
    lane = np.arange(HEAD)
    ang = (pos[:, (lane // 32) % 2] * inv[lane % quarter][None, :]).astype(np.float32)
    sign = np.where((lane % 32) < quarter, -1.0, 1.0).astype(np.float32)
    return jnp.asarray(np.cos(ang), F32), jnp.asarray(np.sin(ang) * sign[None, :], F32)


def _lambda(lam_ref, lam_init):
    lp = lam_ref[...]
    return (jnp.exp(jnp.sum(lp[0:1] * lp[1:2], axis=-1, keepdims=True))
            - jnp.exp(jnp.sum(lp[2:3] * lp[3:4], axis=-1, keepdims=True)) + lam_init)


VT_ROWS = HEAD + 16


def _qt2(qh):
    qt = qh.astype(F32).T
    row = lax.broadcasted_iota(jnp.int32, qt.shape, 0)
    return jnp.concatenate([jnp.where(row < DA_DQK, qt, 0.0), jnp.where(row >= DA_DQK, qt, 0.0)],
                           axis=1).astype(BF16)


def _vt_aug(vh):
    tk = vh.shape[0]
    row = lax.broadcasted_iota(jnp.int32, (VT_ROWS - HEAD, tk), 0)
    return jnp.concatenate([vh.astype(F32).T, (row == 0).astype(F32)], axis=0).astype(BF16)


def _diff_attn_heads(q_ref, k_fn, vt_fn, n_chunks, lam, o_ref, n_seq=1):
    tq = q_ref.shape[0] // n_seq
    items = [(s, h, c) for s in range(n_seq) for h in range(DA_HEADS) for c in range(n_chunks)]
    qt2, state, pending = {}, {}, []

    def finish(s, h, c, st):
        cm = jnp.max(st, axis=0, keepdims=True)
        if c == 0:
            m_new = cm
            acc = _dot(vt_fn(s, h, c), jnp.exp2(st - m_new).astype(BF16))
        else:
            m, acc = state[s, h]
            m_new = jnp.maximum(m, cm)
            acc = acc * jnp.exp2(m - m_new) + _dot(vt_fn(s, h, c), jnp.exp2(st - m_new).astype(BF16))
        state[s, h] = (m_new, acc)
        if c == n_chunks - 1:
            den = acc[HEAD:HEAD + 1, :]
            o_t = acc[0:HEAD, 0:tq] * (1.0 / den[:, 0:tq]) - acc[0:HEAD, tq:] * (lam / den[:, tq:])
            o_ref[s * tq:(s + 1) * tq, h * HEAD:(h + 1) * HEAD] = o_t.T

    for s, h, c in items:
        if c == 0:
            qt2[s, h] = _qt2(q_ref[s * tq:(s + 1) * tq, h * HEAD:(h + 1) * HEAD])
        pending.append((s, h, c, _dot(k_fn(s, h, c), qt2[s, h])))
        if len(pending) > ATTN_LOOKAHEAD:
            finish(*pending.pop(0))
    while pending:
        finish(*pending.pop(0))


def _attn_prompt_kernel(q_ref, k_ref, v_ref, lam_ref, o_ref, *, lam_init, seq):
    lam = _lambda(lam_ref, lam_init)
    rows = lambda s: slice(s * seq, (s + 1) * seq)
    _diff_attn_heads(q_ref, lambda s, h, c: k_ref[rows(s), h * HEAD:(h + 1) * HEAD].astype(BF16),
                     lambda s, h, c: _vt_aug(v_ref[rows(s), h * HEAD:(h + 1) * HEAD]), 1, lam, o_ref,
                     n_seq=q_ref.shape[0] // seq)


def _attn_prompt(q, k, v, lam_p, seq, lam_init):
    n_tok = q.shape[0]
    tok = pl.BlockSpec((PROMPT_SEQS_PER_STEP * seq, D), lambda b: (b, 0))
    return pl.pallas_call(
        functools.partial(_attn_prompt_kernel, lam_init=lam_init, seq=seq),
        grid=(n_tok // (PROMPT_SEQS_PER_STEP * seq),),
        in_specs=[tok, tok, tok, pl.BlockSpec(lam_p.shape, lambda b: (0, 0))],
        out_specs=tok,
        out_shape=jax.ShapeDtypeStruct((n_tok, D), F32),
        compiler_params=_cparams(1),
        name="diff_attn_prompt",
    )(q, k, v, lam_p)


def _attn_sample_kernel(q_ref, k_ref, v_ref, ck_ref, cv_ref, lam_ref, o_ref, kcat, vtcat, *, lam_init, past):
    n_keys = kcat.shape[0]
    chunks = [(c0, min(KEY_CHUNK, n_keys - c0)) for c0 in range(0, n_keys, KEY_CHUNK)]

    @pl.when(pl.program_id(1) == 0)
    def _gather_keys():
        kcat[0:past, :] = ck_ref[...].astype(BF16)
        kcat[past:, :] = k_ref[...]
        for h in range(DA_HEADS):
            sl = slice(h * HEAD, (h + 1) * HEAD)
            for c0 in range(0, n_keys, past):
                src, r0 = (cv_ref, c0) if c0 < past else (v_ref, c0 - past)
                vtcat[h, :, c0:c0 + past] = _vt_aug(src[r0:r0 + past, sl])

    lam = _lambda(lam_ref, lam_init)
    _diff_attn_heads(q_ref, lambda s, h, c: kcat[chunks[c][0]:chunks[c][0] + chunks[c][1], h * HEAD:(h + 1) * HEAD],
                     lambda s, h, c: vtcat[h, :, chunks[c][0]:chunks[c][0] + chunks[c][1]], len(chunks), lam, o_ref)


def _attn_sample(q, k, v, ck, cv, lam_p, n_seq, seq, lam_init, tq=128):
    past = ck.shape[1]
    qb = pl.BlockSpec((tq, D), lambda b, i: (b * (seq // tq) + i, 0))
    kv = pl.BlockSpec((seq, D), lambda b, i: (b, 0))
    cache = pl.BlockSpec((None, past, D), lambda b, i: (b, 0, 0))
    return pl.pallas_call(
        functools.partial(_attn_sample_kernel, lam_init=lam_init, past=past),
        grid=(n_seq, seq // tq),
        in_specs=[qb, kv, kv, cache, cache, pl.BlockSpec(lam_p.shape, lambda b, i: (0, 0))],
        out_specs=qb,
        out_shape=jax.ShapeDtypeStruct((n_seq * seq, D), F32),
        scratch_shapes=[pltpu.VMEM((past + seq, D), BF16), pltpu.VMEM((DA_HEADS, VT_ROWS, past + seq), BF16)],
        compiler_params=_cparams(2),
        name="diff_attn_sample",
    )(q, k, v, ck, cv, lam_p)


def kernel(x_prompt, x_sample, c, c_ctx, cache_attn_k, cache_attn_v, state_mlstm_C, state_mlstm_n, state_mlstm_m,
           state_hgrn_S, ada_w, ada_b, norm_mix_g, norm_ffn_g, ev_w_in, ev_gate_b, ev_lb_logits, ml_norm_g,
           hg_norm_g, ev_w_out, od_w_in, od_lambda, da_norm_g, od_w_out, ffn_w1, ffn_w3, ffn_w2, final_norm_g):
    assert DEPTH % 2 == 0
    n_p, s_p, _ = x_prompt.shape
    n_s, s_s, _ = x_sample.shape
    past = cache_attn_k.shape[2]
    assert s_p == SEG and s_s % SEG == 0 and s_s % PROJ_TILE == 0 and (n_p * s_p) % PROJ_TILE == 0
    assert PROJ_TILE % TOKEN_TILE == 0
    xp = x_prompt.astype(F32).reshape(n_p * s_p, D)
    xs = x_sample.astype(F32).reshape(n_s * s_s, D)

    cond8 = jnp.zeros((8, D), F32).at[0].set(c_ctx.astype(F32)).at[1:1 + n_s].set(c.astype(F32))
    mod = _modulation(cond8, ada_w.astype(F32), ada_b.astype(F32)).reshape(DEPTH, 8, 6, D)
    streams = [(xp, None), (xs, s_s)]
    outputs = {}
    w1_all, w3_all, w2_all = ffn_w1.astype(BF16), ffn_w3.astype(BF16), ffn_w2.astype(BF16)

    for l in range(DEPTH):
        mix_gain = norm_mix_g[l].astype(F32).reshape(1, D)
        ffn_gain = norm_ffn_g[l].astype(F32).reshape(1, D)
        if l % 2 == 0:
            e = l // 2
            w_in = _regroup_even_weights(ev_w_in, e)
            gate_b = jnp.pad(ev_gate_b[e].astype(F32), (0, HEAD - 16)).reshape(1, HEAD)
            norm_gain = jnp.concatenate([ml_norm_g[e], hg_norm_g[e]]).astype(F32).reshape(1, D)
            w_out = ev_w_out[e].astype(BF16)
            mixed = []
            for si, (x, cond_tok) in enumerate(streams):
                qkv, gate, hqi, hgf, og = _inproj_even(x, mod[l], cond_tok, mix_gain, w_in, gate_b,
                                                       ev_lb_logits.astype(F32), e)
                if si == 0:
                    *hs, c_new, n_new, m_new, s_new = _scan(qkv, gate, hqi, hgf, n_p, s_p // SEG, emit_state=True)
                    outputs.setdefault("C", []).append(c_new)
                    outputs.setdefault("n", []).append(n_new)
                    outputs.setdefault("m", []).append(m_new[:, 0, 0:8].reshape(n_p, 2, 4))
                    outputs.setdefault("S", []).append(s_new)
                else:
                    init = (state_mlstm_C[:, e].astype(F32), state_mlstm_n[:, e].astype(F32),
                            jnp.pad(state_mlstm_m[:, e].astype(F32).reshape(n_s, 1, 8), ((0, 0), (0, 0), (0, HEAD - 8))),
                            state_hgrn_S[:, e].astype(F32))
                    hs = _scan(qkv, gate, hqi, hgf, n_s, s_s // SEG, init=init)
                mixed.append((list(hs), og, x))
        else:
            o = l // 2
            lam_init = 0.8 - 0.6 * math.exp(-0.3 * l)
            w_in = od_w_in[o].astype(BF16)
            norm_gain = (jnp.tile(da_norm_g[o].astype(F32), DA_HEADS) * (1.0 - lam_init)).reshape(1, D)
            w_out = od_w_out[o].astype(BF16)
            lam_p = od_lambda[o].astype(F32)
            mixed = []
            for si, (x, cond_tok) in enumerate(streams):
                if si == 0:
                    q, k, v = _inproj_odd(x, mod[l], cond_tok, mix_gain, w_in, None, F32)
                    outputs.setdefault("k", []).append(k.reshape(n_p, s_p, DA_HEADS, HEAD))
                    outputs.setdefault("v", []).append(v.reshape(n_p, s_p, DA_HEADS, HEAD))
                    att = _attn_prompt(q, k, v, lam_p, s_p, lam_init)
                else:
                    q, k, v = _inproj_odd(x, mod[l], cond_tok, mix_gain, w_in, _rope_tables(s_s), BF16)
                    ck = cache_attn_k[:, o].reshape(n_s, past, D)
                    cv = cache_attn_v[:, o].reshape(n_s, past, D)
                    att = _attn_sample(q, k, v, ck, cv, lam_p, n_s, s_s, lam_init)
                mixed.append(([att], None, x))
        final_gain = final_norm_g.astype(F32).reshape(1, D) if l == DEPTH - 1 else None
        streams = [(_mix_ffn(hs, og, x, mod[l], cond_tok, norm_gain, w_out, ffn_gain, l, w1_all, w3_all, w2_all,
                             final_gain), cond_tok)
                   for (hs, og, x), (_, cond_tok) in zip(mixed, streams)]

    y_prompt = streams[0][0].reshape(n_p, s_p, D)
    y_sample = streams[1][0].reshape(n_s, s_s, D)
    stack = lambda name: outputs[name][0][:, None] if len(outputs[name]) == 1 else jnp.stack(outputs[name], axis=1)
    return (y_prompt, y_sample, stack("k"), stack("v"), stack("C"), stack("n"), stack("m"), stack("S"))
```

```python
import functools
import math

import jax
import jax.numpy as jnp
import numpy as np
from jax import lax
from jax.experimental import pallas as pl
from jax.experimental.pallas import tpu as pltpu

F32 = jnp.float32
BF16 = jnp.bfloat16

D = 1024
DEPTH = 2
GRID_W = 64
ML_HEADS = 4
HG_HEADS = 4
HEAD = 128
DA_HEADS = 8
DA_DQK = 64
ROPE_BASE = 10000.0
LOG2E = math.log2(math.e)
EPS = 1e-6
D_FF = ((8 * D // 3 + 255) // 256) * 256
EV_SIZES = (512, 512, 512, 512, 16, 512, 512, 512, 512, 512)
EV_COLS = 9 * 512 + 128

SEG = 256
HALF = SEG // 2
TILE_LEVELS = (1, 2, 4, 8, 16, 32, 64)
TOKEN_TILE = 512
PROJ_TILE = 1024
FFN_CHUNK = 256
KEY_CHUNK = 256
PROMPT_SEQS_PER_STEP = 4
ATTN_LOOKAHEAD = 6
VMEM_LIMIT = 56 * 1024 * 1024


def _cparams(n_axes):
    return pltpu.CompilerParams(dimension_semantics=("arbitrary",) * n_axes, vmem_limit_bytes=VMEM_LIMIT)


def _sigmoid(x):
    return 1.0 / (1.0 + jnp.exp(-x))


def _silu(x):
    return x * _sigmoid(x)


def _log_sigmoid(x):
    return jnp.minimum(x, 0.0) - jnp.log(1.0 + jnp.exp(-jnp.abs(x)))


def _dot(a, b):
    return jnp.dot(a, b, preferred_element_type=F32)


def _dot_nt(a, b):
    return lax.dot_general(a, b, (((1,), (1,)), ((), ())), preferred_element_type=F32)


def _norm_mod(x, gain, mod, k):
    ms = jnp.mean(x * x, axis=-1, keepdims=True)
    return x * lax.rsqrt(ms + EPS) * gain * (1.0 + mod[3 * k + 1:3 * k + 2]) + mod[3 * k:3 * k + 1]


def _mod_row(cond_tokens, tm):
    if cond_tokens is None:
        return lambda i: 0
    return lambda i: 1 + i // (cond_tokens // tm)


def _mod_kernel(c_ref, w_ref, b_ref, o_ref):
    s = _silu(c_ref[...]).astype(BF16)
    o_ref[...] = _dot(s, w_ref[...].astype(BF16)) + b_ref[...]


def _modulation(cond8, ada_w, ada_b):
    n_layers = ada_w.shape[0]
    tn = 1536
    return pl.pallas_call(
        _mod_kernel,
        grid=(n_layers, 6 * D // tn),
        in_specs=[pl.BlockSpec((8, D), lambda l, n: (0, 0)),
                  pl.BlockSpec((None, D, tn), lambda l, n: (l, 0, n)),
                  pl.BlockSpec((None, 1, tn), lambda l, n: (l, 0, n))],
        out_specs=pl.BlockSpec((None, 8, tn), lambda l, n: (l, 0, n)),
        out_shape=jax.ShapeDtypeStruct((n_layers, 8, 6 * D), F32),
        compiler_params=_cparams(2),
        name="ada_modulation",
    )(cond8, ada_w, ada_b.reshape(n_layers, 1, 6 * D))


def _regroup_even_weights(ev_w_in, e):
    g0 = sum(EV_SIZES[:4])
    w_e = ev_w_in[e].astype(BF16)
    return jnp.concatenate([w_e[:, :g0], w_e[:, g0 + 16:], jnp.pad(w_e[:, g0:g0 + 16], ((0, 0), (0, HEAD - 16)))],
                           axis=1)


def _inproj_even_body(x, mod, gain, w_ref, gate_b, lb_logits, e_idx, qkv_ref, gate_ref, hqi_ref, hgf_ref, og_ref):
    h = _norm_mod(x, gain, mod, 0).astype(BF16)

    def proj(c0, n):
        return _dot(h, w_ref[:, c0:c0 + n])

    qkv_ref[:, 0:512] = (proj(0, 512) * (HEAD ** -0.5)).astype(BF16)
    qkv_ref[:, 512:1536] = proj(512, 1024).astype(BF16)
    og_ref[:, 0:512] = proj(1536, 512)
    hqi_ref[:, 0:512] = proj(2048, 512).astype(BF16)
    hqi_ref[:, 512:1024] = proj(3584, 512).astype(BF16)
    mx = jnp.max(lb_logits, axis=0, keepdims=True)
    ex = jnp.exp(lb_logits - mx)
    lb = jnp.sum(ex[0:e_idx + 1], axis=0, keepdims=True) / jnp.sum(ex, axis=0, keepdims=True)
    hgf_ref[:, 0:512] = jnp.log(lb + (1.0 - lb) * _sigmoid(proj(2560, 512)))
    hgf_ref[:, 512:1024] = jnp.log(lb + (1.0 - lb) * _sigmoid(proj(3072, 512)))
    og_ref[:, 512:1024] = proj(4096, 512)
    gt = proj(4608, 128) + gate_b
    lane = lax.broadcasted_iota(jnp.int32, gt.shape, 1)
    gate_ref[...] = jnp.where(lane < 8, gt, jnp.where(lane < 16, _log_sigmoid(gt), 0.0))


def _inproj_even_kernel(x_ref, mod_ref, g_ref, w_ref, gb_ref, lbl_ref, qkv_ref, gate_ref, hqi_ref, hgf_ref,
                        og_ref, *, e_idx):
    _inproj_even_body(x_ref[...], mod_ref[...], g_ref[...], w_ref, gb_ref[...], lbl_ref[...], e_idx,
                      qkv_ref, gate_ref, hqi_ref, hgf_ref, og_ref)


def _inproj_even(x, mod_l, cond_tokens, gain, w, gate_b, lb_logits, e_idx):
    n_tok = x.shape[0]
    tm = PROJ_TILE
    tile_mod_row = _mod_row(cond_tokens, tm)
    tok = lambda n: pl.BlockSpec((tm, n), lambda i: (i, 0))
    full = lambda a: pl.BlockSpec(a.shape, lambda i: (0,) * a.ndim, pipeline_mode=pl.Buffered(1))
    return pl.pallas_call(
        functools.partial(_inproj_even_kernel, e_idx=e_idx),
        grid=(n_tok // tm,),
        in_specs=[tok(D), pl.BlockSpec((None, 6, D), lambda i: (tile_mod_row(i), 0, 0)),
                  full(gain), full(w), full(gate_b), full(lb_logits)],
        out_specs=[tok(1536), tok(128), tok(1024), tok(1024), tok(1024)],
        out_shape=[jax.ShapeDtypeStruct((n_tok, 1536), BF16), jax.ShapeDtypeStruct((n_tok, 128), F32),
                   jax.ShapeDtypeStruct((n_tok, 1024), BF16), jax.ShapeDtypeStruct((n_tok, 1024), F32),
                   jax.ShapeDtypeStruct((n_tok, 1024), F32)],
        compiler_params=_cparams(1),
        name="inproj_even",
    )(x, mod_l, gain, w, gate_b, lb_logits)


def _shift_rows(x, k, fill, up):
    n = x.shape[0]
    if k % 8 == 0:
        pad = jnp.full((k,) + x.shape[1:], 0.0 if fill is None else fill, x.dtype)
        return jnp.concatenate([x[k:], pad], axis=0) if up else jnp.concatenate([pad, x[:n - k]], axis=0)
    y = pltpu.roll(x, (n - k) if up else k, 0)
    if fill is None:
        return y
    row = lax.broadcasted_iota(jnp.int32, x.shape, 0)
    return jnp.where(row >= n - k, fill, y) if up else jnp.where(row < k, fill, y)


def _cummax_rows(x, rev):
    k = 1
    while k < x.shape[0]:
        x = jnp.maximum(x, _shift_rows(x, k, -jnp.inf, up=rev))
        k *= 2
    return x


def _cumsum_rows(tri, x):
    hi = x.astype(BF16)
    r1 = x - hi.astype(F32)
    mid = r1.astype(BF16)
    lo = (r1 - mid.astype(F32)).astype(BF16)
    return _dot(tri, hi) + _dot(tri, mid) + _dot(tri, lo)


def _hgrn_tile(q, kk, f, a, q_b, rev, mask_sc, d, up_small):
    ops = []
    bm = a
    for li, m in enumerate(TILE_LEVELS):
        if m < 8:
            qrole = jnp.logical_not(up_small[li]) if rev else up_small[li]
            if m == 1:
                r = jnp.where(qrole, q * f, kk)
            else:
                x = jnp.where(qrole, _shift_rows(bm, m, None, up=rev), bm)
                r = jnp.where(qrole, q, kk) * jnp.exp(jnp.where(qrole, a - x, x - a))
            if 2 * m < 8:
                bm = jnp.where(qrole, bm, _shift_rows(bm, m, None, up=not rev))
        else:
            pieces = []
            for b0 in range(0, HALF, 2 * m):
                lo, up = slice(b0, b0 + m), slice(b0 + m, b0 + 2 * m)
                if rev:
                    ref = a[b0 + m:b0 + m + 1]
                    pieces += [q[lo] * jnp.exp(a[lo] - ref), kk[up] * jnp.exp(ref - a[up])]
                else:
                    ref = a[b0 + m - 1:b0 + m]
                    pieces += [kk[lo] * jnp.exp(ref - a[lo]), q[up] * jnp.exp(a[up] - ref)]
            r = jnp.concatenate(pieces, axis=0)
        ops.append(r.astype(BF16))
    p = _dot_nt(q_b, kk.astype(BF16)).astype(BF16) * mask_sc[d, len(TILE_LEVELS)]
    for li, rb in enumerate(ops):
        p = p + _dot_nt(rb, rb).astype(BF16) * mask_sc[d, li]
    return p


def _build_scan_constants(tri_sc, mask_sc):
    row = lax.broadcasted_iota(jnp.int32, (SEG, SEG), 0)
    col = lax.broadcasted_iota(jnp.int32, (SEG, SEG), 1)
    tri_sc[0] = (col <= row).astype(BF16)
    tri_sc[1] = (col >= row).astype(BF16)
    trow = lax.broadcasted_iota(jnp.int32, (HALF, HALF), 0)
    tcol = lax.broadcasted_iota(jnp.int32, (HALF, HALF), 1)
    for li, m in enumerate(TILE_LEVELS):
        sh = m.bit_length() - 1
        same = (trow >> (sh + 1)) == (tcol >> (sh + 1))
        t_up = ((trow >> sh) & 1) == 1
        s_up = ((tcol >> sh) & 1) == 1
        mask_sc[0, li] = (same & t_up & jnp.logical_not(s_up)).astype(BF16)
        mask_sc[1, li] = (same & s_up & jnp.logical_not(t_up)).astype(BF16)
    eye = (trow == tcol).astype(BF16)
    mask_sc[0, len(TILE_LEVELS)] = eye
    mask_sc[1, len(TILE_LEVELS)] = eye


def _scan_segment(dir_refs, write_h, tri_sc, mask_sc, m_in, state, emit):
    row = lax.broadcasted_iota(jnp.int32, (SEG, SEG), 0)
    col = lax.broadcasted_iota(jnp.int32, (SEG, SEG), 1)
    lane128 = lax.broadcasted_iota(jnp.int32, (SEG, HEAD), 1)
    e0 = (lane128 == 0).astype(BF16)
    row_half = lax.broadcasted_iota(jnp.int32, (HALF, HEAD), 0)
    up_masks = [((row_half >> (m.bit_length() - 1)) & 1) == 1 for m in TILE_LEVELS if m < 8]
    tmasks = (col <= row, col >= row)
    use_state = state is not None
    want_state = use_state or emit is not None
    m_out_rows = []
    env = {}

    def prep(d):
        _, gate_ref, _, hgf_ref = dir_refs[d]
        rev = d == 1
        last = 0 if rev else SEG - 1
        tri = tri_sc[d]
        slab = gate_ref[...]
        b_al = pltpu.roll(_cumsum_rows(tri, slab), HEAD - 8, 1)
        u = slab - b_al
        mx = jnp.maximum(_cummax_rows(u, rev), m_in)
        mx_last = mx[last:last + 1, :]
        m_out_rows.append(b_al[last:last + 1, :] + mx_last)
        lf_all = hgf_ref[:, 512 * d:512 * (d + 1)]
        f_all = jnp.exp(lf_all)
        env[d] = dict(rev=rev, last=last, mx=mx, w_inter=jnp.exp(m_in - mx), e_den=jnp.exp(-(b_al + mx)),
                      decay=jnp.exp(m_in - mx_last), wg=jnp.exp(u - mx_last), u_t=u.T,
                      a_all=_cumsum_rows(tri, lf_all), f_all=f_all, kk_all=1.0 - f_all)

    def ml_scores(d, h):
        e, qkv_ref, c = env[d], dir_refs[d][0], 4 * d + h
        q = qkv_ref[:, h * HEAD:(h + 1) * HEAD]
        k = qkv_ref[:, 512 + h * HEAD:512 + (h + 1) * HEAD]
        v = qkv_ref[:, 1024 + h * HEAD:1024 + (h + 1) * HEAD]
        dm = jnp.where(tmasks[d], jnp.exp(e["u_t"][c:c + 1, :] - e["mx"][:, c:c + 1]), 0.0)
        e["ml", h] = (q, k, jnp.concatenate([v, e0], axis=1), (_dot_nt(q, k) * dm).astype(BF16))

    def ml_out(d, h):
        e, c = env[d], 4 * d + h
        q, _, v_aug, s = e["ml", h]
        numden = _dot(s, v_aug)
        if use_state:
            numden = numden + e["w_inter"][:, c:c + 1] * _dot(q, state[0][d, h].astype(BF16))
        den = jnp.maximum(jnp.abs(numden[:, HEAD:HEAD + 1]), e["e_den"][:, c:c + 1])
        write_h(d, h * HEAD, numden[:, 0:HEAD] / den)

    def ml_state(d, h):
        e, c = env[d], 4 * d + h
        _, k, v_aug, _ = e["ml", h]
        kw_t = (k.astype(F32) * e["wg"][:, c:c + 1]).T.astype(BF16)
        upd = _dot(kw_t, v_aug)
        if use_state:
            upd = upd + e["decay"][:, c:c + 1] * state[0][d, h]
            state[0][d, h] = upd
        if emit is not None:
            emit("ml", d, h, upd)

    def hg_cross(d, h):
        e, hqi_ref = env[d], dir_refs[d][2]
        rev = e["rev"]
        sl = slice(h * HEAD, (h + 1) * HEAD)
        a, kk = e["a_all"][:, sl], e["kk_all"][:, sl]
        q_b = hqi_ref[:, h * HEAD:(h + 1) * HEAD]
        q = q_b.astype(F32)
        qh, kh = (0, 1) if rev else (1, 0)
        a_q, a_k = a[qh * HALF:(qh + 1) * HALF], a[kh * HALF:(kh + 1) * HALF]
        a_mid = a[HALF:HALF + 1] if rev else a[HALF - 1:HALF]
        r_q = (q[qh * HALF:(qh + 1) * HALF] * jnp.exp(a_q - a_mid)).astype(BF16)
        r_k = (kk[kh * HALF:(kh + 1) * HALF] * jnp.exp(a_mid - a_k)).astype(BF16)
        e["hg", h] = dict(a=a, kk=kk, f=e["f_all"][:, sl], q_b=q_b, q=q, p_tiles=[],
                          i_b=hqi_ref[:, 512 + h * HEAD:512 + (h + 1) * HEAD], p_cross=_dot_nt(r_q, r_k).astype(BF16))

    def hg_tile(d, h, t):
        g, rows = env[d]["hg", h], slice(t * HALF, (t + 1) * HALF)
        g["p_tiles"].append(_hgrn_tile(g["q"][rows], g["kk"][rows], g["f"][rows], g["a"][rows], g["q_b"][rows],
                                       env[d]["rev"], mask_sc, d, up_masks))

    def hg_out(d, h):
        g = env[d]["hg", h]
        p_tiles, p_cross, i_b = g["p_tiles"], g["p_cross"], g["i_b"]
        i_lo, i_hi = i_b[0:HALF], i_b[HALF:SEG]
        if env[d]["rev"]:
            o = jnp.concatenate([_dot(p_tiles[0], i_lo) + _dot(p_cross, i_hi), _dot(p_tiles[1], i_hi)], axis=0)
        else:
            o = jnp.concatenate([_dot(p_tiles[0], i_lo), _dot(p_cross, i_lo) + _dot(p_tiles[1], i_hi)], axis=0)
        if use_state:
            o = o + _dot_nt((g["q"] * jnp.exp(g["a"])).astype(BF16), state[1][d, h].astype(BF16))
        write_h(d, 512 + h * HEAD, o)

    def hg_state(d, h):
        e, g = env[d], env[d]["hg", h]
        a, kk = g["a"], g["kk"]
        a_l = a[e["last"]:e["last"] + 1, :]
        kd = (kk * jnp.exp(a_l - a)).astype(BF16)
        st_new = _dot(g["i_b"].astype(F32).T.astype(BF16), kd)
        if use_state:
            st_new = st_new + state[1][d, h] * jnp.exp(a_l)
            state[1][d, h] = st_new
        if emit is not None:
            emit("hg", d, h, st_new)

    for d in range(2):
        prep(d)
        for h in range(ML_HEADS):
            ml_scores(d, h)
            hg_cross(d, h)
            hg_tile(d, h, 0)
            ml_out(d, h)
            hg_tile(d, h, 1)
            if want_state:
                ml_state(d, h)
            hg_out(d, h)
            if want_state:
                hg_state(d, h)

    lane_row = lax.broadcasted_iota(jnp.int32, (1, HEAD), 1)
    return jnp.where(lane_row < 4, m_out_rows[0], jnp.where(lane_row < 8, m_out_rows[1], 0.0))


def _emit_states(kind, d, h, value, c_ref, n_ref, s_ref):
    if kind == "ml":
        c_ref[d, h] = value[:, 0:HEAD]
        n_ref[d, h:h + 1, :] = value[:, HEAD:2 * HEAD].T[0:1, :]
    else:
        s_ref[d, h] = value.T


def _scan_kernel(*refs, has_init, emit_state, nseg):
    single = nseg == 1
    n_tok_in = 4 if single else 8
    n_h = 1 if single else 2
    n_in = n_tok_in + (4 if has_init else 0)
    n_out = n_h + (4 if emit_state else 0)
    ins, outs, scr = refs[:n_in], refs[n_in:n_in + n_out], refs[n_in + n_out:]
    dir_refs = (ins[0:4], ins[0:4] if single else ins[4:8])
    st_outs = outs[n_h:]
    tri_sc, mask_sc, caug_sc, st_sc, m_sc = scr
    use_state = has_init or nseg > 1
    b_id, j = pl.program_id(0), pl.program_id(1)

    @pl.when(jnp.logical_and(b_id == 0, j == 0))
    def _build_constants():
        _build_scan_constants(tri_sc, mask_sc)

    if use_state:
        @pl.when(j == 0)
        def _init_state():
            if has_init:
                c0_ref, n0_ref, m0_ref, s0_ref = ins[n_tok_in:n_tok_in + 4]
                for d in range(2):
                    for h in range(ML_HEADS):
                        caug_sc[d, h, :, 0:HEAD] = c0_ref[d, h]
                        caug_sc[d, h, :, HEAD:2 * HEAD] = jnp.broadcast_to(n0_ref[d, h:h + 1, :], (HEAD, HEAD)).T
                        st_sc[d, h] = s0_ref[d, h].T
                m_sc[...] = jnp.broadcast_to(m0_ref[...], m_sc.shape)
            else:
                caug_sc[...] = jnp.zeros(caug_sc.shape, F32)
                st_sc[...] = jnp.zeros(st_sc.shape, F32)
                m_sc[...] = jnp.zeros(m_sc.shape, F32)

    def write_h(d, c0, value):
        if single and d == 1:
            outs[0][:, c0:c0 + HEAD] += value
        else:
            outs[0 if single else d][:, c0:c0 + HEAD] = value

    def emit(kind, d, h, value):
        if single:
            _emit_states(kind, d, h, value, st_outs[0], st_outs[1], st_outs[3])
        else:
            pl.when(j == nseg - 1)(lambda: _emit_states(kind, d, h, value, st_outs[0], st_outs[1], st_outs[3]))

    m_in = m_sc[0:1, :] if use_state else jnp.zeros((1, HEAD), F32)
    m_new = _scan_segment(dir_refs, write_h, tri_sc, mask_sc, m_in, (caug_sc, st_sc) if use_state else None,
                          emit if emit_state else None)
    if use_state:
        m_sc[...] = jnp.broadcast_to(m_new, m_sc.shape)
    if emit_state:
        if single:
            st_outs[2][...] = m_new
        else:
            @pl.when(j == nseg - 1)
            def _emit_m():
                st_outs[2][...] = m_new


def _scan(qkv, gate, hqi, hgf, n_seq, nseg, init=None, emit_state=False):
    n_tok = qkv.shape[0]
    fwd = lambda n: pl.BlockSpec((SEG, n), lambda b, j: (b * nseg + j, 0))
    bwd = lambda n: pl.BlockSpec((SEG, n), lambda b, j: (b * nseg + nseg - 1 - j, 0))
    widths = (1536, 128, 1024, 1024)
    single = nseg == 1
    in_specs = [fwd(n) for n in widths] + ([] if single else [bwd(n) for n in widths])
    args = [qkv, gate, hqi, hgf] + ([] if single else [qkv, gate, hqi, hgf])
    mat = pl.BlockSpec((None, 2, 4, HEAD, HEAD), lambda b, j: (b, 0, 0, 0, 0))
    vec = pl.BlockSpec((None, 2, 4, HEAD), lambda b, j: (b, 0, 0, 0))
    sca = pl.BlockSpec((None, 1, HEAD), lambda b, j: (b, 0, 0))
    if init is not None:
        in_specs += [mat, vec, sca, mat]
        args += list(init)
    out_specs = [fwd(1024)] if single else [fwd(1024), bwd(1024)]
    out_shape = [jax.ShapeDtypeStruct((n_tok, 1024), F32)] * len(out_specs)
    if emit_state:
        out_specs += [mat, vec, sca, mat]
        out_shape += [jax.ShapeDtypeStruct((n_seq, 2, 4, HEAD, HEAD), F32),
                      jax.ShapeDtypeStruct((n_seq, 2, 4, HEAD), F32),
                      jax.ShapeDtypeStruct((n_seq, 1, HEAD), F32),
                      jax.ShapeDtypeStruct((n_seq, 2, 4, HEAD, HEAD), F32)]
    return pl.pallas_call(
        functools.partial(_scan_kernel, has_init=init is not None, emit_state=emit_state, nseg=nseg),
        grid=(n_seq, nseg),
        in_specs=in_specs,
        out_specs=out_specs,
        out_shape=out_shape,
        scratch_shapes=[pltpu.VMEM((2, SEG, SEG), BF16),
                        pltpu.VMEM((2, len(TILE_LEVELS) + 1, HALF, HALF), BF16),
                        pltpu.VMEM((2, 4, HEAD, 2 * HEAD), F32),
                        pltpu.VMEM((2, 4, HEAD, HEAD), F32),
                        pltpu.VMEM((8, HEAD), F32)],
        compiler_params=_cparams(2),
        name="bidir_scan",
    )(*args)


def _mix_ffn_tile(h, og, x, mod, norm_gain, wo_ref, ffn_gain, w1_ref, w3_ref, w2_ref, final_gain, o_ref, chunk):
    parts = []
    for g in range(D // HEAD):
        hs = h[:, g * HEAD:(g + 1) * HEAD]
        parts.append(hs * lax.rsqrt(jnp.mean(hs * hs, axis=-1, keepdims=True) + EPS))
    hn = jnp.concatenate(parts, axis=1) * norm_gain
    if og is not None:
        hn = hn * jnp.concatenate([_sigmoid(og[:, 0:512]), _silu(og[:, 512:1024])], axis=1)
    x = x + mod[2:3, :] * _dot(hn.astype(BF16), wo_ref[...])
    hf = _norm_mod(x, ffn_gain, mod, 1).astype(BF16)
    up = lambda c0: (_dot(hf, w1_ref[:, c0:min(c0 + chunk, D_FF)]), _dot(hf, w3_ref[:, c0:min(c0 + chunk, D_FF)]))
    starts = list(range(0, D_FF, chunk))
    acc = jnp.zeros(x.shape, F32)
    nxt = up(starts[0])
    for i, c0 in enumerate(starts):
        a, b = nxt
        if i + 1 < len(starts):
            nxt = up(starts[i + 1])
        acc = acc + _dot((_silu(a) * b).astype(BF16), w2_ref[c0:min(c0 + chunk, D_FF), :])
    y = x + mod[5:6, :] * acc
    if final_gain is not None:
        y = y * lax.rsqrt(jnp.mean(y * y, axis=-1, keepdims=True) + EPS) * final_gain
    o_ref[...] = y


def _mix_ffn_kernel(*refs, n_h, gated, final, chunk):
    h_refs = refs[:n_h]
    pos = n_h
    og_ref = refs[pos] if gated else None
    pos += 1 if gated else 0
    x_ref, mod_ref, ng_ref, wo_ref, g_ref, w1_ref, w3_ref, w2_ref = refs[pos:pos + 8]
    fg_ref = refs[pos + 8] if final else None
    o_ref = refs[-1]
    h = h_refs[0][...]
    for r in h_refs[1:]:
        h = h + r[...]
    _mix_ffn_tile(h, og_ref[...] if gated else None, x_ref[...], mod_ref[...], ng_ref[...], wo_ref, g_ref[...],
                  w1_ref, w3_ref, w2_ref, fg_ref[...] if final else None, o_ref, chunk)


def _mix_ffn(hs, og, x, mod_l, cond_tokens, norm_gain, w_out, ffn_gain, layer, w1, w3, w2, final_gain=None):
    n_tok = x.shape[0]
    tm = TOKEN_TILE
    tile_mod_row = _mod_row(cond_tokens, tm)
    tok = pl.BlockSpec((tm, D), lambda i: (i, 0))
    full = lambda a: pl.BlockSpec(a.shape, lambda i: (0,) * a.ndim, pipeline_mode=pl.Buffered(1))
    of_layer = lambda a: pl.BlockSpec((None,) + a.shape[1:], lambda i: (layer, 0, 0), pipeline_mode=pl.Buffered(1))
    gated = og is not None
    final = final_gain is not None
    consts = [norm_gain, w_out, ffn_gain, w1, w3, w2] + ([final_gain] if final else [])
    args = list(hs) + ([og] if gated else []) + [x, mod_l] + consts
    in_specs = [tok] * (len(hs) + (1 if gated else 0) + 1)
    in_specs += [pl.BlockSpec((None, 6, D), lambda i: (tile_mod_row(i), 0, 0))]
    in_specs += [of_layer(a) if a.ndim == 3 else full(a) for a in consts]
    return pl.pallas_call(
        functools.partial(_mix_ffn_kernel, n_h=len(hs), gated=gated, final=final, chunk=FFN_CHUNK),
        grid=(n_tok // tm,),
        in_specs=in_specs,
        out_specs=tok,
        out_shape=jax.ShapeDtypeStruct((n_tok, D), F32),
        compiler_params=_cparams(1),
        name="mix_ffn",
    )(*args)


def _rope(x, cos, sin_signed):
    lane = lax.broadcasted_iota(jnp.int32, (x.shape[0], HEAD), 1)
    first = (lane & 16) == 0
    parts = []
    for h in range(DA_HEADS):
        xh = x[:, h * HEAD:(h + 1) * HEAD]
        partner = jnp.where(first, pltpu.roll(xh, HEAD - 16, 1), pltpu.roll(xh, 16, 1))
        parts.append(xh * cos + partner * sin_signed)
    return jnp.concatenate(parts, axis=1)


def _inproj_odd_kernel(*refs, rope):
    x_ref, mod_ref, g_ref, w_ref = refs[:4]
    q_ref, k_ref, v_ref = refs[-3:]
    h = _norm_mod(x_ref[...], g_ref[...], mod_ref[...], 0).astype(BF16)
    q = _dot(h, w_ref[:, 0:D])
    k = _dot(h, w_ref[:, D:2 * D])
    v = _dot(h, w_ref[:, 2 * D:3 * D])
    if rope:
        cos, sin_signed = refs[4][...], refs[5][...]
        q = _rope(q, cos, sin_signed)
        k = _rope(k, cos, sin_signed)
    q_ref[...] = (q * (DA_DQK ** -0.5 * LOG2E)).astype(q_ref.dtype)
    k_ref[...] = k.astype(k_ref.dtype)
    v_ref[...] = v.astype(v_ref.dtype)


def _inproj_odd(x, mod_l, cond_tokens, gain, w, rope_tables, kv_dtype):
    n_tok = x.shape[0]
    tm = PROJ_TILE
    tile_mod_row = _mod_row(cond_tokens, tm)
    tok = pl.BlockSpec((tm, D), lambda i: (i, 0))
    full = lambda a: pl.BlockSpec(a.shape, lambda i: (0,) * a.ndim, pipeline_mode=pl.Buffered(1))
    args = [x, mod_l, gain, w]
    in_specs = [tok, pl.BlockSpec((None, 6, D), lambda i: (tile_mod_row(i), 0, 0)), full(gain), full(w)]
    if rope_tables is not None:
        tiles_per_seq = rope_tables[0].shape[0] // tm
        args += list(rope_tables)
        in_specs += [pl.BlockSpec((tm, HEAD), lambda i: (i % tiles_per_seq, 0))] * 2
    return pl.pallas_call(
        functools.partial(_inproj_odd_kernel, rope=rope_tables is not None),
        grid=(n_tok // tm,),
        in_specs=in_specs,
        out_specs=[tok, tok, tok],
        out_shape=[jax.ShapeDtypeStruct((n_tok, D), BF16), jax.ShapeDtypeStruct((n_tok, D), kv_dtype),
                   jax.ShapeDtypeStruct((n_tok, D), kv_dtype)],
        compiler_params=_cparams(1),
        name="inproj_odd",
    )(*args)


def _rope_tables(n_tok):
    quarter = DA_DQK // 4
    tok = np.arange(n_tok)
    pos = np.stack([tok // GRID_W, tok % GRID_W], axis=1).astype(np.float32)
    inv = (np.float32(ROPE_BASE) ** (-np.arange(quarter, dtype=np.float32) / np.float32(quarter))).astype(np.float32)
    lane = np.arange(HEAD)
    ang = (pos[:, (lane // 32) % 2] * inv[lane % quarter][None, :]).astype(np.float32)
    sign = np.where((lane % 32) < quarter, -1.0, 1.0).astype(np.float32)
    return jnp.asarray(np.cos(ang), F32), jnp.asarray(np.sin(ang) * sign[None, :], F32)


def _lambda(lam_ref, lam_init):
    lp = lam_ref[...]
    return (jnp.exp(jnp.sum(lp[0:1] * lp[1:2], axis=-1, keepdims=True))
            - jnp.exp(jnp.sum(lp[2:3] * lp[3:4], axis=-1, keepdims=True)) + lam_init)


VT_ROWS = HEAD + 16


def _qt2(qh):
    qt = qh.astype(F32).T
    row = lax.broadcasted_iota(jnp.int32, qt.shape, 0)
    return jnp.concatenate([jnp.where(row < DA_DQK, qt, 0.0), jnp.where(row >= DA_DQK, qt, 0.0)],
                           axis=1).astype(BF16)


def _vt_aug(vh):
    tk = vh.shape[0]
    row = lax.broadcasted_iota(jnp.int32, (VT_ROWS - HEAD, tk), 0)
    return jnp.concatenate([vh.astype(F32).T, (row == 0).astype(F32)], axis=0).astype(BF16)


def _diff_attn_heads(q_ref, k_fn, vt_fn, n_chunks, lam, o_ref, n_seq=1):
    tq = q_ref.shape[0] // n_seq
    items = [(s, h, c) for s in range(n_seq) for h in range(DA_HEADS) for c in range(n_chunks)]
    qt2, state, pending = {}, {}, []

    def finish(s, h, c, st):
        cm = jnp.max(st, axis=0, keepdims=True)
        if c == 0:
            m_new = cm
            acc = _dot(vt_fn(s, h, c), jnp.exp2(st - m_new).astype(BF16))
        else:
            m, acc = state[s, h]
            m_new = jnp.maximum(m, cm)
            acc = acc * jnp.exp2(m - m_new) + _dot(vt_fn(s, h, c), jnp.exp2(st - m_new).astype(BF16))
        state[s, h] = (m_new, acc)
        if c == n_chunks - 1:
            den = acc[HEAD:HEAD + 1, :]
            o_t = acc[0:HEAD, 0:tq] * (1.0 / den[:, 0:tq]) - acc[0:HEAD, tq:] * (lam / den[:, tq:])
            o_ref[s * tq:(s + 1) * tq, h * HEAD:(h + 1) * HEAD] = o_t.T

    for s, h, c in items:
        if c == 0:
            qt2[s, h] = _qt2(q_ref[s * tq:(s + 1) * tq, h * HEAD:(h + 1) * HEAD])
        pending.append((s, h, c, _dot(k_fn(s, h, c), qt2[s, h])))
        if len(pending) > ATTN_LOOKAHEAD:
            finish(*pending.pop(0))
    while pending:
        finish(*pending.pop(0))


def _attn_prompt_kernel(q_ref, k_ref, v_ref, lam_ref, o_ref, *, lam_init, seq):
    lam = _lambda(lam_ref, lam_init)
    rows = lambda s: slice(s * seq, (s + 1) * seq)
    _diff_attn_heads(q_ref, lambda s, h, c: k_ref[rows(s), h * HEAD:(h + 1) * HEAD].astype(BF16),
                     lambda s, h, c: _vt_aug(v_ref[rows(s), h * HEAD:(h + 1) * HEAD]), 1, lam, o_ref,
                     n_seq=q_ref.shape[0] // seq)


def _attn_prompt(q, k, v, lam_p, seq, lam_init):
    n_tok = q.shape[0]
    tok = pl.BlockSpec((PROMPT_SEQS_PER_STEP * seq, D), lambda b: (b, 0))
    return pl.pallas_call(
        functools.partial(_attn_prompt_kernel, lam_init=lam_init, seq=seq),
        grid=(n_tok // (PROMPT_SEQS_PER_STEP * seq),),
        in_specs=[tok, tok, tok, pl.BlockSpec(lam_p.shape, lambda b: (0, 0))],
        out_specs=tok,
        out_shape=jax.ShapeDtypeStruct((n_tok, D), F32),
        compiler_params=_cparams(1),
        name="diff_attn_prompt",
    )(q, k, v, lam_p)


def _attn_sample_kernel(q_ref, k_ref, v_ref, ck_ref, cv_ref, lam_ref, o_ref, kcat, vtcat, *, lam_init, past):
    n_keys = kcat.shape[0]
    chunks = [(c0, min(KEY_CHUNK, n_keys - c0)) for c0 in range(0, n_keys, KEY_CHUNK)]

    @pl.when(pl.program_id(1) == 0)
    def _gather_keys():
        kcat[0:past, :] = ck_ref[...].astype(BF16)
        kcat[past:, :] = k_ref[...]
        for h in range(DA_HEADS):
            sl = slice(h * HEAD, (h + 1) * HEAD)
            for c0 in range(0, n_keys, past):
                src, r0 = (cv_ref, c0) if c0 < past else (v_ref, c0 - past)
                vtcat[h, :, c0:c0 + past] = _vt_aug(src[r0:r0 + past, sl])

    lam = _lambda(lam_ref, lam_init)
    _diff_attn_heads(q_ref, lambda s, h, c: kcat[chunks[c][0]:chunks[c][0] + chunks[c][1], h * HEAD:(h + 1) * HEAD],
                     lambda s, h, c: vtcat[h, :, chunks[c][0]:chunks[c][0] + chunks[c][1]], len(chunks), lam, o_ref)


def _attn_sample(q, k, v, ck, cv, lam_p, n_seq, seq, lam_init, tq=128):
    past = ck.shape[1]
    qb = pl.BlockSpec((tq, D), lambda b, i: (b * (seq // tq) + i, 0))
    kv = pl.BlockSpec((seq, D), lambda b, i: (b, 0))
    cache = pl.BlockSpec((None, past, D), lambda b, i: (b, 0, 0))
    return pl.pallas_call(
        functools.partial(_attn_sample_kernel, lam_init=lam_init, past=past),
        grid=(n_seq, seq // tq),
        in_specs=[qb, kv, kv, cache, cache, pl.BlockSpec(lam_p.shape, lambda b, i: (0, 0))],
        out_specs=qb,
        out_shape=jax.ShapeDtypeStruct((n_seq * seq, D), F32),
        scratch_shapes=[pltpu.VMEM((past + seq, D), BF16), pltpu.VMEM((DA_HEADS, VT_ROWS, past + seq), BF16)],
        compiler_params=_cparams(2),
        name="diff_attn_sample",
    )(q, k, v, ck, cv, lam_p)


def kernel(x_prompt, x_sample, c, c_ctx, cache_attn_k, cache_attn_v, state_mlstm_C, state_mlstm_n, state_mlstm_m,
           state_hgrn_S, ada_w, ada_b, norm_mix_g, norm_ffn_g, ev_w_in, ev_gate_b, ev_lb_logits, ml_norm_g,
           hg_norm_g, ev_w_out, od_w_in, od_lambda, da_norm_g, od_w_out, ffn_w1, ffn_w3, ffn_w2, final_norm_g):
    assert DEPTH % 2 == 0
    n_p, s_p, _ = x_prompt.shape
    n_s, s_s, _ = x_sample.shape
    past = cache_attn_k.shape[2]
    assert s_p == SEG and s_s % SEG == 0 and s_s % PROJ_TILE == 0 and (n_p * s_p) % PROJ_TILE == 0
    assert PROJ_TILE % TOKEN_TILE == 0
    xp = x_prompt.astype(F32).reshape(n_p * s_p, D)
    xs = x_sample.astype(F32).reshape(n_s * s_s, D)

    cond8 = jnp.zeros((8, D), F32).at[0].set(c_ctx.astype(F32)).at[1:1 + n_s].set(c.astype(F32))
    mod = _modulation(cond8, ada_w.astype(F32), ada_b.astype(F32)).reshape(DEPTH, 8, 6, D)
    streams = [(xp, None), (xs, s_s)]
    outputs = {}
    w1_all, w3_all, w2_all = ffn_w1.astype(BF16), ffn_w3.astype(BF16), ffn_w2.astype(BF16)

    for l in range(DEPTH):
        mix_gain = norm_mix_g[l].astype(F32).reshape(1, D)
        ffn_gain = norm_ffn_g[l].astype(F32).reshape(1, D)
        if l % 2 == 0:
            e = l // 2
            w_in = _regroup_even_weights(ev_w_in, e)
            gate_b = jnp.pad(ev_gate_b[e].astype(F32), (0, HEAD - 16)).reshape(1, HEAD)
            norm_gain = jnp.concatenate([ml_norm_g[e], hg_norm_g[e]]).astype(F32).reshape(1, D)
            w_out = ev_w_out[e].astype(BF16)
            mixed = []
            for si, (x, cond_tok) in enumerate(streams):
                qkv, gate, hqi, hgf, og = _inproj_even(x, mod[l], cond_tok, mix_gain, w_in, gate_b,
                                                       ev_lb_logits.astype(F32), e)
                if si == 0:
                    *hs, c_new, n_new, m_new, s_new = _scan(qkv, gate, hqi, hgf, n_p, s_p // SEG, emit_state=True)
                    outputs.setdefault("C", []).append(c_new)
                    outputs.setdefault("n", []).append(n_new)
                    outputs.setdefault("m", []).append(m_new[:, 0, 0:8].reshape(n_p, 2, 4))
                    outputs.setdefault("S", []).append(s_new)
                else:
                    init = (state_mlstm_C[:, e].astype(F32), state_mlstm_n[:, e].astype(F32),
                            jnp.pad(state_mlstm_m[:, e].astype(F32).reshape(n_s, 1, 8), ((0, 0), (0, 0), (0, HEAD - 8))),
                            state_hgrn_S[:, e].astype(F32))
                    hs = _scan(qkv, gate, hqi, hgf, n_s, s_s // SEG, init=init)
                mixed.append((list(hs), og, x))
        else:
            o = l // 2
            lam_init = 0.8 - 0.6 * math.exp(-0.3 * l)
            w_in = od_w_in[o].astype(BF16)
            norm_gain = (jnp.tile(da_norm_g[o].astype(F32), DA_HEADS) * (1.0 - lam_init)).reshape(1, D)
            w_out = od_w_out[o].astype(BF16)
            lam_p = od_lambda[o].astype(F32)
            mixed = []
            for si, (x, cond_tok) in enumerate(streams):
                if si == 0:
                    q, k, v = _inproj_odd(x, mod[l], cond_tok, mix_gain, w_in, None, F32)
                    outputs.setdefault("k", []).append(k.reshape(n_p, s_p, DA_HEADS, HEAD))
                    outputs.setdefault("v", []).append(v.reshape(n_p, s_p, DA_HEADS, HEAD))
                    att = _attn_prompt(q, k, v, lam_p, s_p, lam_init)
                else:
                    q, k, v = _inproj_odd(x, mod[l], cond_tok, mix_gain, w_in, _rope_tables(s_s), BF16)
                    ck = cache_attn_k[:, o].reshape(n_s, past, D)
                    cv = cache_attn_v[:, o].reshape(n_s, past, D)
                    att = _attn_sample(q, k, v, ck, cv, lam_p, n_s, s_s, lam_init)
                mixed.append(([att], None, x))
        final_gain = final_norm_g.astype(F32).reshape(1, D) if l == DEPTH - 1 else None
        streams = [(_mix_ffn(hs, og, x, mod[l], cond_tok, norm_gain, w_out, ffn_gain, l, w1_all, w3_all, w2_all,
                             final_gain), cond_tok)
                   for (hs, og, x), (_, cond_tok) in zip(mixed, streams)]

    y_prompt = streams[0][0].reshape(n_p, s_p, D)
    y_sample = streams[1][0].reshape(n_s, s_s, D)
    stack = lambda name: outputs[name][0][:, None] if len(outputs[name]) == 1 else jnp.stack(outputs[name], axis=1)
    return (y_prompt, y_sample, stack("k"), stack("v"), stack("C"), stack("n"), stack("m"), stack("S"))
```

```python
import functools
import math

import jax
import jax.numpy as jnp
import numpy as np
from jax import lax
from jax.experimental import pallas as pl
from jax.experimental.pallas import tpu as pltpu

F32 = jnp.float32
BF16 = jnp.bfloat16

D = 1024
DEPTH = 2
GRID_W = 64
ML_HEADS = 4
HG_HEADS = 4
HEAD = 128
DA_HEADS = 8
DA_DQK = 64
ROPE_BASE = 10000.0
LOG2E = math.log2(math.e)
EPS = 1e-6
D_FF = ((8 * D // 3 + 255) // 256) * 256
EV_SIZES = (512, 512, 512, 512, 16, 512, 512, 512, 512, 512)
EV_COLS = 9 * 512 + 128

SEG = 256
HALF = SEG // 2
TILE_LEVELS = (1, 2, 4, 8, 16, 32, 64)
TOKEN_TILE = 512
PROJ_TILE = 1024
FFN_CHUNK = 256
KEY_CHUNK = 256
PROMPT_SEQS_PER_STEP = 4
ATTN_LOOKAHEAD = 6
VMEM_LIMIT = 56 * 1024 * 1024


def _cparams(n_axes):
    return pltpu.CompilerParams(dimension_semantics=("arbitrary",) * n_axes, vmem_limit_bytes=VMEM_LIMIT)


def _sigmoid(x):
    return 1.0 / (1.0 + jnp.exp(-x))


def _silu(x):
    return x * _sigmoid(x)


def _log_sigmoid(x):
    return jnp.minimum(x, 0.0) - jnp.log(1.0 + jnp.exp(-jnp.abs(x)))


def _dot(a, b):
    return jnp.dot(a, b, preferred_element_type=F32)


def _dot_nt(a, b):
    return lax.dot_general(a, b, (((1,), (1,)), ((), ())), preferred_element_type=F32)


def _norm_mod(x, gain, mod, k):
    ms = jnp.mean(x * x, axis=-1, keepdims=True)
    return x * lax.rsqrt(ms + EPS) * gain * (1.0 + mod[3 * k + 1:3 * k + 2]) + mod[3 * k:3 * k + 1]


def _mod_row(cond_tokens, tm):
    if cond_tokens is None:
        return lambda i: 0
    return lambda i: 1 + i // (cond_tokens // tm)


def _mod_kernel(c_ref, w_ref, b_ref, o_ref):
    s = _silu(c_ref[...]).astype(BF16)
    o_ref[...] = _dot(s, w_ref[...].astype(BF16)) + b_ref[...]


def _modulation(cond8, ada_w, ada_b):
    n_layers = ada_w.shape[0]
    tn = 1536
    return pl.pallas_call(
        _mod_kernel,
        grid=(n_layers, 6 * D // tn),
        in_specs=[pl.BlockSpec((8, D), lambda l, n: (0, 0)),
                  pl.BlockSpec((None, D, tn), lambda l, n: (l, 0, n)),
                  pl.BlockSpec((None, 1, tn), lambda l, n: (l, 0, n))],
        out_specs=pl.BlockSpec((None, 8, tn), lambda l, n: (l, 0, n)),
        out_shape=jax.ShapeDtypeStruct((n_layers, 8, 6 * D), F32),
        compiler_params=_cparams(2),
        name="ada_modulation",
    )(cond8, ada_w, ada_b.reshape(n_layers, 1, 6 * D))


def _regroup_even_weights(ev_w_in, e):
    g0 = sum(EV_SIZES[:4])
    w_e = ev_w_in[e].astype(BF16)
    return jnp.concatenate([w_e[:, :g0], w_e[:, g0 + 16:], jnp.pad(w_e[:, g0:g0 + 16], ((0, 0), (0, HEAD - 16)))],
                           axis=1)


def _inproj_even_body(x, mod, gain, w_ref, gate_b, lb_logits, e_idx, qkv_ref, gate_ref, hqi_ref, hgf_ref, og_ref):
    h = _norm_mod(x, gain, mod, 0).astype(BF16)

    def proj(c0, n):
        return _dot(h, w_ref[:, c0:c0 + n])

    mx = jnp.max(lb_logits, axis=0, keepdims=True)
    ex = jnp.exp(lb_logits - mx)
    lb = jnp.sum(ex[0:e_idx + 1], axis=0, keepdims=True) / jnp.sum(ex, axis=0, keepdims=True)
    hgf_ref[:, 0:512] = jnp.log(lb + (1.0 - lb) * _sigmoid(proj(2560, 512)))
    qkv_ref[:, 0:512] = (proj(0, 512) * (HEAD ** -0.5)).astype(BF16)
    hgf_ref[:, 512:1024] = jnp.log(lb + (1.0 - lb) * _sigmoid(proj(3072, 512)))
    qkv_ref[:, 512:1536] = proj(512, 1024).astype(BF16)
    og_ref[:, 0:512] = proj(1536, 512)
    hqi_ref[:, 0:512] = proj(2048, 512).astype(BF16)
    hqi_ref[:, 512:1024] = proj(3584, 512).astype(BF16)
    og_ref[:, 512:1024] = proj(4096, 512)
    gt = proj(4608, 128) + gate_b
    lane = lax.broadcasted_iota(jnp.int32, gt.shape, 1)
    gate_ref[...] = jnp.where(lane < 8, gt, jnp.where(lane < 16, _log_sigmoid(gt), 0.0))


def _inproj_even_kernel(x_ref, mod_ref, g_ref, w_ref, gb_ref, lbl_ref, qkv_ref, gate_ref, hqi_ref, hgf_ref,
                        og_ref, *, e_idx):
    _inproj_even_body(x_ref[...], mod_ref[...], g_ref[...], w_ref, gb_ref[...], lbl_ref[...], e_idx,
                      qkv_ref, gate_ref, hqi_ref, hgf_ref, og_ref)


def _inproj_even(x, mod_l, cond_tokens, gain, w, gate_b, lb_logits, e_idx):
    n_tok = x.shape[0]
    tm = PROJ_TILE
    tile_mod_row = _mod_row(cond_tokens, tm)
    tok = lambda n: pl.BlockSpec((tm, n), lambda i: (i, 0))
    full = lambda a: pl.BlockSpec(a.shape, lambda i: (0,) * a.ndim, pipeline_mode=pl.Buffered(1))
    return pl.pallas_call(
        functools.partial(_inproj_even_kernel, e_idx=e_idx),
        grid=(n_tok // tm,),
        in_specs=[tok(D), pl.BlockSpec((None, 6, D), lambda i: (tile_mod_row(i), 0, 0)),
                  full(gain), full(w), full(gate_b), full(lb_logits)],
        out_specs=[tok(1536), tok(128), tok(1024), tok(1024), tok(1024)],
        out_shape=[jax.ShapeDtypeStruct((n_tok, 1536), BF16), jax.ShapeDtypeStruct((n_tok, 128), F32),
                   jax.ShapeDtypeStruct((n_tok, 1024), BF16), jax.ShapeDtypeStruct((n_tok, 1024), F32),
                   jax.ShapeDtypeStruct((n_tok, 1024), F32)],
        compiler_params=_cparams(1),
        name="inproj_even",
    )(x, mod_l, gain, w, gate_b, lb_logits)


def _shift_rows(x, k, fill, up):
    n = x.shape[0]
    if k % 8 == 0:
        pad = jnp.full((k,) + x.shape[1:], 0.0 if fill is None else fill, x.dtype)
        return jnp.concatenate([x[k:], pad], axis=0) if up else jnp.concatenate([pad, x[:n - k]], axis=0)
    y = pltpu.roll(x, (n - k) if up else k, 0)
    if fill is None:
        return y
    row = lax.broadcasted_iota(jnp.int32, x.shape, 0)
    return jnp.where(row >= n - k, fill, y) if up else jnp.where(row < k, fill, y)


def _cummax_rows(x, rev):
    k = 1
    while k < x.shape[0]:
        x = jnp.maximum(x, _shift_rows(x, k, -jnp.inf, up=rev))
        k *= 2
    return x


def _cumsum_rows(tri, x):
    hi = x.astype(BF16)
    r1 = x - hi.astype(F32)
    mid = r1.astype(BF16)
    lo = (r1 - mid.astype(F32)).astype(BF16)
    return _dot(tri, hi) + _dot(tri, mid) + _dot(tri, lo)


def _hgrn_tile(q, kk, f, a, q_b, rev, mask_sc, d, up_small):
    ops = []
    bm = a
    for li, m in enumerate(TILE_LEVELS):
        if m < 8:
            qrole = jnp.logical_not(up_small[li]) if rev else up_small[li]
            if m == 1:
                r = jnp.where(qrole, q * f, kk)
            else:
                x = jnp.where(qrole, _shift_rows(bm, m, None, up=rev), bm)
                r = jnp.where(qrole, q, kk) * jnp.exp(jnp.where(qrole, a - x, x - a))
            if 2 * m < 8:
                bm = jnp.where(qrole, bm, _shift_rows(bm, m, None, up=not rev))
        else:
            pieces = []
            for b0 in range(0, HALF, 2 * m):
                lo, up = slice(b0, b0 + m), slice(b0 + m, b0 + 2 * m)
                if rev:
                    ref = a[b0 + m:b0 + m + 1]
                    pieces += [q[lo] * jnp.exp(a[lo] - ref), kk[up] * jnp.exp(ref - a[up])]
                else:
                    ref = a[b0 + m - 1:b0 + m]
                    pieces += [kk[lo] * jnp.exp(ref - a[lo]), q[up] * jnp.exp(a[up] - ref)]
            r = jnp.concatenate(pieces, axis=0)
        ops.append(r.astype(BF16))
    p = _dot_nt(q_b, kk.astype(BF16)).astype(BF16) * mask_sc[d, len(TILE_LEVELS)]
    for li, rb in enumerate(ops):
        p = p + _dot_nt(rb, rb).astype(BF16) * mask_sc[d, li]
    return p


def _build_scan_constants(tri_sc, mask_sc):
    row = lax.broadcasted_iota(jnp.int32, (SEG, SEG), 0)
    col = lax.broadcasted_iota(jnp.int32, (SEG, SEG), 1)
    tri_sc[0] = (col <= row).astype(BF16)
    tri_sc[1] = (col >= row).astype(BF16)
    trow = lax.broadcasted_iota(jnp.int32, (HALF, HALF), 0)
    tcol = lax.broadcasted_iota(jnp.int32, (HALF, HALF), 1)
    for li, m in enumerate(TILE_LEVELS):
        sh = m.bit_length() - 1
        same = (trow >> (sh + 1)) == (tcol >> (sh + 1))
        t_up = ((trow >> sh) & 1) == 1
        s_up = ((tcol >> sh) & 1) == 1
        mask_sc[0, li] = (same & t_up & jnp.logical_not(s_up)).astype(BF16)
        mask_sc[1, li] = (same & s_up & jnp.logical_not(t_up)).astype(BF16)
    eye = (trow == tcol).astype(BF16)
    mask_sc[0, len(TILE_LEVELS)] = eye
    mask_sc[1, len(TILE_LEVELS)] = eye


def _scan_segment(dir_refs, write_h, tri_sc, mask_sc, m_in, state, emit):
    row = lax.broadcasted_iota(jnp.int32, (SEG, SEG), 0)
    col = lax.broadcasted_iota(jnp.int32, (SEG, SEG), 1)
    lane128 = lax.broadcasted_iota(jnp.int32, (SEG, HEAD), 1)
    e0 = (lane128 == 0).astype(BF16)
    row_half = lax.broadcasted_iota(jnp.int32, (HALF, HEAD), 0)
    up_masks = [((row_half >> (m.bit_length() - 1)) & 1) == 1 for m in TILE_LEVELS if m < 8]
    tmasks = (col <= row, col >= row)
    use_state = state is not None
    want_state = use_state or emit is not None
    m_out_rows = []
    env = {}

    def prep(d):
        _, gate_ref, _, hgf_ref = dir_refs[d]
        rev = d == 1
        last = 0 if rev else SEG - 1
        tri = tri_sc[d]
        slab = gate_ref[...]
        b_al = pltpu.roll(_cumsum_rows(tri, slab), HEAD - 8, 1)
        u = slab - b_al
        mx = jnp.maximum(_cummax_rows(u, rev), m_in)
        mx_last = mx[last:last + 1, :]
        m_out_rows.append(b_al[last:last + 1, :] + mx_last)
        lf_all = hgf_ref[:, 512 * d:512 * (d + 1)]
        f_all = jnp.exp(lf_all)
        env[d] = dict(rev=rev, last=last, mx=mx, w_inter=jnp.exp(m_in - mx), e_den=jnp.exp(-(b_al + mx)),
                      decay=jnp.exp(m_in - mx_last), wg=jnp.exp(u - mx_last), u_t=u.T,
                      a_all=_cumsum_rows(tri, lf_all), f_all=f_all, kk_all=1.0 - f_all)

    def ml_scores(d, h):
        e, qkv_ref, c = env[d], dir_refs[d][0], 4 * d + h
        q = qkv_ref[:, h * HEAD:(h + 1) * HEAD]
        k = qkv_ref[:, 512 + h * HEAD:512 + (h + 1) * HEAD]
        v = qkv_ref[:, 1024 + h * HEAD:1024 + (h + 1) * HEAD]
        dm = jnp.where(tmasks[d], jnp.exp(e["u_t"][c:c + 1, :] - e["mx"][:, c:c + 1]), 0.0)
        e["ml", h] = (q, k, jnp.concatenate([v, e0], axis=1), (_dot_nt(q, k) * dm).astype(BF16))

    def ml_out(d, h):
        e, c = env[d], 4 * d + h
        q, _, v_aug, s = e["ml", h]
        numden = _dot(s, v_aug)
        if use_state:
            numden = numden + e["w_inter"][:, c:c + 1] * _dot(q, state[0][d, h].astype(BF16))
        den = jnp.maximum(jnp.abs(numden[:, HEAD:HEAD + 1]), e["e_den"][:, c:c + 1])
        write_h(d, h * HEAD, numden[:, 0:HEAD] / den)

    def ml_state(d, h):
        e, c = env[d], 4 * d + h
        _, k, v_aug, _ = e["ml", h]
        kw_t = (k.astype(F32) * e["wg"][:, c:c + 1]).T.astype(BF16)
        upd = _dot(kw_t, v_aug)
        if use_state:
            upd = upd + e["decay"][:, c:c + 1] * state[0][d, h]
            state[0][d, h] = upd
        if emit is not None:
            emit("ml", d, h, upd)

    def hg_cross(d, h):
        e, hqi_ref = env[d], dir_refs[d][2]
        rev = e["rev"]
        sl = slice(h * HEAD, (h + 1) * HEAD)
        a, kk = e["a_all"][:, sl], e["kk_all"][:, sl]
        q_b = hqi_ref[:, h * HEAD:(h + 1) * HEAD]
        q = q_b.astype(F32)
        qh, kh = (0, 1) if rev else (1, 0)
        a_q, a_k = a[qh * HALF:(qh + 1) * HALF], a[kh * HALF:(kh + 1) * HALF]
        a_mid = a[HALF:HALF + 1] if rev else a[HALF - 1:HALF]
        r_q = (q[qh * HALF:(qh + 1) * HALF] * jnp.exp(a_q - a_mid)).astype(BF16)
        r_k = (kk[kh * HALF:(kh + 1) * HALF] * jnp.exp(a_mid - a_k)).astype(BF16)
        e["hg", h] = dict(a=a, kk=kk, f=e["f_all"][:, sl], q_b=q_b, q=q, p_tiles=[],
                          i_b=hqi_ref[:, 512 + h * HEAD:512 + (h + 1) * HEAD], p_cross=_dot_nt(r_q, r_k).astype(BF16))

    def hg_tile(d, h, t):
        g, rows = env[d]["hg", h], slice(t * HALF, (t + 1) * HALF)
        g["p_tiles"].append(_hgrn_tile(g["q"][rows], g["kk"][rows], g["f"][rows], g["a"][rows], g["q_b"][rows],
                                       env[d]["rev"], mask_sc, d, up_masks))

    def hg_out(d, h):
        g = env[d]["hg", h]
        p_tiles, p_cross, i_b = g["p_tiles"], g["p_cross"], g["i_b"]
        i_lo, i_hi = i_b[0:HALF], i_b[HALF:SEG]
        if env[d]["rev"]:
            o = jnp.concatenate([_dot(p_tiles[0], i_lo) + _dot(p_cross, i_hi), _dot(p_tiles[1], i_hi)], axis=0)
        else:
            o = jnp.concatenate([_dot(p_tiles[0], i_lo), _dot(p_cross, i_lo) + _dot(p_tiles[1], i_hi)], axis=0)
        if use_state:
            o = o + _dot_nt((g["q"] * jnp.exp(g["a"])).astype(BF16), state[1][d, h].astype(BF16))
        write_h(d, 512 + h * HEAD, o)

    def hg_state(d, h):
        e, g = env[d], env[d]["hg", h]
        a, kk = g["a"], g["kk"]
        a_l = a[e["last"]:e["last"] + 1, :]
        kd = (kk * jnp.exp(a_l - a)).astype(BF16)
        st_new = _dot(g["i_b"].astype(F32).T.astype(BF16), kd)
        if use_state:
            st_new = st_new + state[1][d, h] * jnp.exp(a_l)
            state[1][d, h] = st_new
        if emit is not None:
            emit("hg", d, h, st_new)

    prep(0)
    prep(1)
    for h in range(ML_HEADS):
        for d in range(2):
            ml_scores(d, h)
            hg_cross(d, h)
            hg_tile(d, h, 0)
            ml_out(d, h)
            hg_tile(d, h, 1)
            if want_state:
                ml_state(d, h)
            hg_out(d, h)
            if want_state:
                hg_state(d, h)

    lane_row = lax.broadcasted_iota(jnp.int32, (1, HEAD), 1)
    return jnp.where(lane_row < 4, m_out_rows[0], jnp.where(lane_row < 8, m_out_rows[1], 0.0))


def _emit_states(kind, d, h, value, c_ref, n_ref, s_ref):
    if kind == "ml":
        c_ref[d, h] = value[:, 0:HEAD]
        n_ref[d, h:h + 1, :] = value[:, HEAD:2 * HEAD].T[0:1, :]
    else:
        s_ref[d, h] = value.T


def _scan_kernel(*refs, has_init, emit_state, nseg):
    single = nseg == 1
    n_tok_in = 4 if single else 8
    n_h = 1 if single else 2
    n_in = n_tok_in + (4 if has_init else 0)
    n_out = n_h + (4 if emit_state else 0)
    ins, outs, scr = refs[:n_in], refs[n_in:n_in + n_out], refs[n_in + n_out:]
    dir_refs = (ins[0:4], ins[0:4] if single else ins[4:8])
    st_outs = outs[n_h:]
    tri_sc, mask_sc, caug_sc, st_sc, m_sc = scr
    use_state = has_init or nseg > 1
    b_id, j = pl.program_id(0), pl.program_id(1)

    @pl.when(jnp.logical_and(b_id == 0, j == 0))
    def _build_constants():
        _build_scan_constants(tri_sc, mask_sc)

    if use_state:
        @pl.when(j == 0)
        def _init_state():
            if has_init:
                c0_ref, n0_ref, m0_ref, s0_ref = ins[n_tok_in:n_tok_in + 4]
                for d in range(2):
                    for h in range(ML_HEADS):
                        caug_sc[d, h, :, 0:HEAD] = c0_ref[d, h]
                        caug_sc[d, h, :, HEAD:2 * HEAD] = jnp.broadcast_to(n0_ref[d, h:h + 1, :], (HEAD, HEAD)).T
                        st_sc[d, h] = s0_ref[d, h].T
                m_sc[...] = jnp.broadcast_to(m0_ref[...], m_sc.shape)
            else:
                caug_sc[...] = jnp.zeros(caug_sc.shape, F32)
                st_sc[...] = jnp.zeros(st_sc.shape, F32)
                m_sc[...] = jnp.zeros(m_sc.shape, F32)

    def write_h(d, c0, value):
        if single and d == 1:
            outs[0][:, c0:c0 + HEAD] += value
        else:
            outs[0 if single else d][:, c0:c0 + HEAD] = value

    def emit(kind, d, h, value):
        if single:
            _emit_states(kind, d, h, value, st_outs[0], st_outs[1], st_outs[3])
        else:
            pl.when(j == nseg - 1)(lambda: _emit_states(kind, d, h, value, st_outs[0], st_outs[1], st_outs[3]))

    m_in = m_sc[0:1, :] if use_state else jnp.zeros((1, HEAD), F32)
    m_new = _scan_segment(dir_refs, write_h, tri_sc, mask_sc, m_in, (caug_sc, st_sc) if use_state else None,
                          emit if emit_state else None)
    if use_state:
        m_sc[...] = jnp.broadcast_to(m_new, m_sc.shape)
    if emit_state:
        if single:
            st_outs[2][...] = m_new
        else:
            @pl.when(j == nseg - 1)
            def _emit_m():
                st_outs[2][...] = m_new


def _scan(qkv, gate, hqi, hgf, n_seq, nseg, init=None, emit_state=False):
    n_tok = qkv.shape[0]
    fwd = lambda n: pl.BlockSpec((SEG, n), lambda b, j: (b * nseg + j, 0))
    bwd = lambda n: pl.BlockSpec((SEG, n), lambda b, j: (b * nseg + nseg - 1 - j, 0))
    widths = (1536, 128, 1024, 1024)
    single = nseg == 1
    in_specs = [fwd(n) for n in widths] + ([] if single else [bwd(n) for n in widths])
    args = [qkv, gate, hqi, hgf] + ([] if single else [qkv, gate, hqi, hgf])
    mat = pl.BlockSpec((None, 2, 4, HEAD, HEAD), lambda b, j: (b, 0, 0, 0, 0))
    vec = pl.BlockSpec((None, 2, 4, HEAD), lambda b, j: (b, 0, 0, 0))
    sca = pl.BlockSpec((None, 1, HEAD), lambda b, j: (b, 0, 0))
    if init is not None:
        in_specs += [mat, vec, sca, mat]
        args += list(init)
    out_specs = [fwd(1024)] if single else [fwd(1024), bwd(1024)]
    out_shape = [jax.ShapeDtypeStruct((n_tok, 1024), F32)] * len(out_specs)
    if emit_state:
        out_specs += [mat, vec, sca, mat]
        out_shape += [jax.ShapeDtypeStruct((n_seq, 2, 4, HEAD, HEAD), F32),
                      jax.ShapeDtypeStruct((n_seq, 2, 4, HEAD), F32),
                      jax.ShapeDtypeStruct((n_seq, 1, HEAD), F32),
                      jax.ShapeDtypeStruct((n_seq, 2, 4, HEAD, HEAD), F32)]
    return pl.pallas_call(
        functools.partial(_scan_kernel, has_init=init is not None, emit_state=emit_state, nseg=nseg),
        grid=(n_seq, nseg),
        in_specs=in_specs,
        out_specs=out_specs,
        out_shape=out_shape,
        scratch_shapes=[pltpu.VMEM((2, SEG, SEG), BF16),
                        pltpu.VMEM((2, len(TILE_LEVELS) + 1, HALF, HALF), BF16),
                        pltpu.VMEM((2, 4, HEAD, 2 * HEAD), F32),
                        pltpu.VMEM((2, 4, HEAD, HEAD), F32),
                        pltpu.VMEM((8, HEAD), F32)],
        compiler_params=_cparams(2),
        name="bidir_scan",
    )(*args)


def _mix_ffn_tile(h, og, x, mod, norm_gain, wo_ref, ffn_gain, w1_ref, w3_ref, w2_ref, final_gain, o_ref, chunk):
    parts = []
    for g in range(D // HEAD):
        hs = h[:, g * HEAD:(g + 1) * HEAD]
        parts.append(hs * lax.rsqrt(jnp.mean(hs * hs, axis=-1, keepdims=True) + EPS))
    hn = jnp.concatenate(parts, axis=1) * norm_gain
    if og is not None:
        hn = hn * jnp.concatenate([_sigmoid(og[:, 0:512]), _silu(og[:, 512:1024])], axis=1)
    x = x + mod[2:3, :] * _dot(hn.astype(BF16), wo_ref[...])
    hf = _norm_mod(x, ffn_gain, mod, 1).astype(BF16)
    up = lambda c0: (_dot(hf, w1_ref[:, c0:min(c0 + chunk, D_FF)]), _dot(hf, w3_ref[:, c0:min(c0 + chunk, D_FF)]))
    starts = list(range(0, D_FF, chunk))
    acc = jnp.zeros(x.shape, F32)
    nxt = up(starts[0])
    for i, c0 in enumerate(starts):
        a, b = nxt
        if i + 1 < len(starts):
            nxt = up(starts[i + 1])
        acc = acc + _dot((_silu(a) * b).astype(BF16), w2_ref[c0:min(c0 + chunk, D_FF), :])
    y = x + mod[5:6, :] * acc
    if final_gain is not None:
        y = y * lax.rsqrt(jnp.mean(y * y, axis=-1, keepdims=True) + EPS) * final_gain
    o_ref[...] = y


def _mix_ffn_kernel(*refs, n_h, gated, final, chunk):
    h_refs = refs[:n_h]
    pos = n_h
    og_ref = refs[pos] if gated else None
    pos += 1 if gated else 0
    x_ref, mod_ref, ng_ref, wo_ref, g_ref, w1_ref, w3_ref, w2_ref = refs[pos:pos + 8]
    fg_ref = refs[pos + 8] if final else None
    o_ref = refs[-1]
    h = h_refs[0][...]
    for r in h_refs[1:]:
        h = h + r[...]
    _mix_ffn_tile(h, og_ref[...] if gated else None, x_ref[...], mod_ref[...], ng_ref[...], wo_ref, g_ref[...],
                  w1_ref, w3_ref, w2_ref, fg_ref[...] if final else None, o_ref, chunk)


def _mix_ffn(hs, og, x, mod_l, cond_tokens, norm_gain, w_out, ffn_gain, layer, w1, w3, w2, final_gain=None):
    n_tok = x.shape[0]
    tm = TOKEN_TILE
    tile_mod_row = _mod_row(cond_tokens, tm)
    tok = pl.BlockSpec((tm, D), lambda i: (i, 0))
    full = lambda a: pl.BlockSpec(a.shape, lambda i: (0,) * a.ndim, pipeline_mode=pl.Buffered(1))
    of_layer = lambda a: pl.BlockSpec((None,) + a.shape[1:], lambda i: (layer, 0, 0), pipeline_mode=pl.Buffered(1))
    gated = og is not None
    final = final_gain is not None
    consts = [norm_gain, w_out, ffn_gain, w1, w3, w2] + ([final_gain] if final else [])
    args = list(hs) + ([og] if gated else []) + [x, mod_l] + consts
    in_specs = [tok] * (len(hs) + (1 if gated else 0) + 1)
    in_specs += [pl.BlockSpec((None, 6, D), lambda i: (tile_mod_row(i), 0, 0))]
    in_specs += [of_layer(a) if a.ndim == 3 else full(a) for a in consts]
    return pl.pallas_call(
        functools.partial(_mix_ffn_kernel, n_h=len(hs), gated=gated, final=final, chunk=FFN_CHUNK),
        grid=(n_tok // tm,),
        in_specs=in_specs,
        out_specs=tok,
        out_shape=jax.ShapeDtypeStruct((n_tok, D), F32),
        compiler_params=_cparams(1),
        name="mix_ffn",
    )(*args)


def _rope(x, cos, sin_signed):
    lane = lax.broadcasted_iota(jnp.int32, (x.shape[0], HEAD), 1)
    first = (lane & 16) == 0
    parts = []
    for h in range(DA_HEADS):
        xh = x[:, h * HEAD:(h + 1) * HEAD]
        partner = jnp.where(first, pltpu.roll(xh, HEAD - 16, 1), pltpu.roll(xh, 16, 1))
        parts.append(xh * cos + partner * sin_signed)
    return jnp.concatenate(parts, axis=1)


def _inproj_odd_kernel(*refs, rope):
    x_ref, mod_ref, g_ref, w_ref = refs[:4]
    q_ref, k_ref, v_ref = refs[-3:]
    h = _norm_mod(x_ref[...], g_ref[...], mod_ref[...], 0).astype(BF16)
    q = _dot(h, w_ref[:, 0:D])
    k = _dot(h, w_ref[:, D:2 * D])
    v = _dot(h, w_ref[:, 2 * D:3 * D])
    if rope:
        cos, sin_signed = refs[4][...], refs[5][...]
        q = _rope(q, cos, sin_signed)
        k = _rope(k, cos, sin_signed)
    q_ref[...] = (q * (DA_DQK ** -0.5 * LOG2E)).astype(q_ref.dtype)
    k_ref[...] = k.astype(k_ref.dtype)
    v_ref[...] = v.astype(v_ref.dtype)


def _inproj_odd(x, mod_l, cond_tokens, gain, w, rope_tables, kv_dtype):
    n_tok = x.shape[0]
    tm = PROJ_TILE
    tile_mod_row = _mod_row(cond_tokens, tm)
    tok = pl.BlockSpec((tm, D), lambda i: (i, 0))
    full = lambda a: pl.BlockSpec(a.shape, lambda i: (0,) * a.ndim, pipeline_mode=pl.Buffered(1))
    args = [x, mod_l, gain, w]
    in_specs = [tok, pl.BlockSpec((None, 6, D), lambda i: (tile_mod_row(i), 0, 0)), full(gain), full(w)]
    if rope_tables is not None:
        tiles_per_seq = rope_tables[0].shape[0] // tm
        args += list(rope_tables)
        in_specs += [pl.BlockSpec((tm, HEAD), lambda i: (i % tiles_per_seq, 0))] * 2
    return pl.pallas_call(
        functools.partial(_inproj_odd_kernel, rope=rope_tables is not None),
        grid=(n_tok // tm,),
        in_specs=in_specs,
        out_specs=[tok, tok, tok],
        out_shape=[jax.ShapeDtypeStruct((n_tok, D), BF16), jax.ShapeDtypeStruct((n_tok, D), kv_dtype),
                   jax.ShapeDtypeStruct((n_tok, D), kv_dtype)],
        compiler_params=_cparams(1),
        name="inproj_odd",
    )(*args)


def _rope_tables(n_tok):
    quarter = DA_DQK // 4
    tok = np.arange(n_tok)
    pos = np.stack([tok // GRID_W, tok % GRID_W], axis=1).astype(np.float32)
    inv = (np.float32(ROPE_BASE) ** (-np.arange(quarter, dtype=np.float32) / np.float32(quarter))).astype(np.float32)
    lane = np.arange(HEAD)
    ang = (pos[:, (lane // 32) % 2] * inv[lane % quarter][None, :]).astype(np.float32)
    sign = np.where((lane % 32) < quarter, -1.0, 1.0).astype(np.float32)
    return jnp.asarray(np.cos(ang), F32), jnp.asarray(np.sin(ang) * sign[None, :], F32)


def _lambda(lam_ref, lam_init):
    lp = lam_ref[...]
    return (jnp.exp(jnp.sum(lp[0:1] * lp[1:2], axis=-1, keepdims=True))
            - jnp.exp(jnp.sum(lp[2:3] * lp[3:4], axis=-1, keepdims=True)) + lam_init)


VT_ROWS = HEAD + 16


def _qt2(qh):
    qt = qh.astype(F32).T
    row = lax.broadcasted_iota(jnp.int32, qt.shape, 0)
    return jnp.concatenate([jnp.where(row < DA_DQK, qt, 0.0), jnp.where(row >= DA_DQK, qt, 0.0)],
                           axis=1).astype(BF16)


def _vt_aug(vh):
    tk = vh.shape[0]
    row = lax.broadcasted_iota(jnp.int32, (VT_ROWS - HEAD, tk), 0)
    return jnp.concatenate([vh.astype(F32).T, (row == 0).astype(F32)], axis=0).astype(BF16)


def _diff_attn_heads(q_ref, k_fn, vt_fn, n_chunks, lam, o_ref, n_seq=1):
    tq = q_ref.shape[0] // n_seq
    items = [(s, h, c) for s in range(n_seq) for h in range(DA_HEADS) for c in range(n_chunks)]
    qt2, state, pending = {}, {}, []

    def finish(s, h, c, st):
        cm = jnp.max(st, axis=0, keepdims=True)
        if c == 0:
            m_new = cm
            acc = _dot(vt_fn(s, h, c), jnp.exp2(st - m_new).astype(BF16))
        else:
            m, acc = state[s, h]
            m_new = jnp.maximum(m, cm)
            acc = acc * jnp.exp2(m - m_new) + _dot(vt_fn(s, h, c), jnp.exp2(st - m_new).astype(BF16))
        state[s, h] = (m_new, acc)
        if c == n_chunks - 1:
            den = acc[HEAD:HEAD + 1, :]
            o_t = acc[0:HEAD, 0:tq] * (1.0 / den[:, 0:tq]) - acc[0:HEAD, tq:] * (lam / den[:, tq:])
            o_ref[s * tq:(s + 1) * tq, h * HEAD:(h + 1) * HEAD] = o_t.T

    for s, h, c in items:
        if c == 0:
            qt2[s, h] = _qt2(q_ref[s * tq:(s + 1) * tq, h * HEAD:(h + 1) * HEAD])
        pending.append((s, h, c, _dot(k_fn(s, h, c), qt2[s, h])))
        if len(pending) > ATTN_LOOKAHEAD:
            finish(*pending.pop(0))
    while pending:
        finish(*pending.pop(0))


def _attn_prompt_kernel(q_ref, k_ref, v_ref, lam_ref, o_ref, *, lam_init, seq):
    lam = _lambda(lam_ref, lam_init)
    rows = lambda s: slice(s * seq, (s + 1) * seq)
    _diff_attn_heads(q_ref, lambda s, h, c: k_ref[rows(s), h * HEAD:(h + 1) * HEAD].astype(BF16),
                     lambda s, h, c: _vt_aug(v_ref[rows(s), h * HEAD:(h + 1) * HEAD]), 1, lam, o_ref,
                     n_seq=q_ref.shape[0] // seq)


def _attn_prompt(q, k, v, lam_p, seq, lam_init):
    n_tok = q.shape[0]
    tok = pl.BlockSpec((PROMPT_SEQS_PER_STEP * seq, D), lambda b: (b, 0))
    return pl.pallas_call(
        functools.partial(_attn_prompt_kernel, lam_init=lam_init, seq=seq),
        grid=(n_tok // (PROMPT_SEQS_PER_STEP * seq),),
        in_specs=[tok, tok, tok, pl.BlockSpec(lam_p.shape, lambda b: (0, 0))],
        out_specs=tok,
        out_shape=jax.ShapeDtypeStruct((n_tok, D), F32),
        compiler_params=_cparams(1),
        name="diff_attn_prompt",
    )(q, k, v, lam_p)


def _attn_sample_kernel(q_ref, k_ref, v_ref, ck_ref, cv_ref, lam_ref, o_ref, kcat, vtcat, *, lam_init, past):
    n_keys = kcat.shape[0]
    chunks = [(c0, min(KEY_CHUNK, n_keys - c0)) for c0 in range(0, n_keys, KEY_CHUNK)]

    @pl.when(pl.program_id(1) == 0)
    def _gather_keys():
        kcat[0:past, :] = ck_ref[...].astype(BF16)
        kcat[past:, :] = k_ref[...]
        for h in range(DA_HEADS):
            sl = slice(h * HEAD, (h + 1) * HEAD)
            for c0 in range(0, n_keys, past):
                src, r0 = (cv_ref, c0) if c0 < past else (v_ref, c0 - past)
                vtcat[h, :, c0:c0 + past] = _vt_aug(src[r0:r0 + past, sl])

    lam = _lambda(lam_ref, lam_init)
    _diff_attn_heads(q_ref, lambda s, h, c: kcat[chunks[c][0]:chunks[c][0] + chunks[c][1], h * HEAD:(h + 1) * HEAD],
                     lambda s, h, c: vtcat[h, :, chunks[c][0]:chunks[c][0] + chunks[c][1]], len(chunks), lam, o_ref)


def _attn_sample(q, k, v, ck, cv, lam_p, n_seq, seq, lam_init, tq=128):
    past = ck.shape[1]
    qb = pl.BlockSpec((tq, D), lambda b, i: (b * (seq // tq) + i, 0))
    kv = pl.BlockSpec((seq, D), lambda b, i: (b, 0))
    cache = pl.BlockSpec((None, past, D), lambda b, i: (b, 0, 0))
    return pl.pallas_call(
        functools.partial(_attn_sample_kernel, lam_init=lam_init, past=past),
        grid=(n_seq, seq // tq),
        in_specs=[qb, kv, kv, cache, cache, pl.BlockSpec(lam_p.shape, lambda b, i: (0, 0))],
        out_specs=qb,
        out_shape=jax.ShapeDtypeStruct((n_seq * seq, D), F32),
        scratch_shapes=[pltpu.VMEM((past + seq, D), BF16), pltpu.VMEM((DA_HEADS, VT_ROWS, past + seq), BF16)],
        compiler_params=_cparams(2),
        name="diff_attn_sample",
    )(q, k, v, ck, cv, lam_p)


def kernel(x_prompt, x_sample, c, c_ctx, cache_attn_k, cache_attn_v, state_mlstm_C, state_mlstm_n, state_mlstm_m,
           state_hgrn_S, ada_w, ada_b, norm_mix_g, norm_ffn_g, ev_w_in, ev_gate_b, ev_lb_logits, ml_norm_g,
           hg_norm_g, ev_w_out, od_w_in, od_lambda, da_norm_g, od_w_out, ffn_w1, ffn_w3, ffn_w2, final_norm_g):
    assert DEPTH % 2 == 0
    n_p, s_p, _ = x_prompt.shape
    n_s, s_s, _ = x_sample.shape
    past = cache_attn_k.shape[2]
    assert s_p == SEG and s_s % SEG == 0 and s_s % PROJ_TILE == 0 and (n_p * s_p) % PROJ_TILE == 0
    assert PROJ_TILE % TOKEN_TILE == 0
    xp = x_prompt.astype(F32).reshape(n_p * s_p, D)
    xs = x_sample.astype(F32).reshape(n_s * s_s, D)

    cond8 = jnp.zeros((8, D), F32).at[0].set(c_ctx.astype(F32)).at[1:1 + n_s].set(c.astype(F32))
    mod = _modulation(cond8, ada_w.astype(F32), ada_b.astype(F32)).reshape(DEPTH, 8, 6, D)
    streams = [(xp, None), (xs, s_s)]
    outputs = {}
    w1_all, w3_all, w2_all = ffn_w1.astype(BF16), ffn_w3.astype(BF16), ffn_w2.astype(BF16)

    for l in range(DEPTH):
        mix_gain = norm_mix_g[l].astype(F32).reshape(1, D)
        ffn_gain = norm_ffn_g[l].astype(F32).reshape(1, D)
        if l % 2 == 0:
            e = l // 2
            w_in = _regroup_even_weights(ev_w_in, e)
            gate_b = jnp.pad(ev_gate_b[e].astype(F32), (0, HEAD - 16)).reshape(1, HEAD)
            norm_gain = jnp.concatenate([ml_norm_g[e], hg_norm_g[e]]).astype(F32).reshape(1, D)
            w_out = ev_w_out[e].astype(BF16)
            mixed = []
            for si, (x, cond_tok) in enumerate(streams):
                qkv, gate, hqi, hgf, og = _inproj_even(x, mod[l], cond_tok, mix_gain, w_in, gate_b,
                                                       ev_lb_logits.astype(F32), e)
                if si == 0:
                    *hs, c_new, n_new, m_new, s_new = _scan(qkv, gate, hqi, hgf, n_p, s_p // SEG, emit_state=True)
                    outputs.setdefault("C", []).append(c_new)
                    outputs.setdefault("n", []).append(n_new)
                    outputs.setdefault("m", []).append(m_new[:, 0, 0:8].reshape(n_p, 2, 4))
                    outputs.setdefault("S", []).append(s_new)
                else:
                    init = (state_mlstm_C[:, e].astype(F32), state_mlstm_n[:, e].astype(F32),
                            jnp.pad(state_mlstm_m[:, e].astype(F32).reshape(n_s, 1, 8), ((0, 0), (0, 0), (0, HEAD - 8))),
                            state_hgrn_S[:, e].astype(F32))
                    hs = _scan(qkv, gate, hqi, hgf, n_s, s_s // SEG, init=init)
                mixed.append((list(hs), og, x))
        else:
            o = l // 2
            lam_init = 0.8 - 0.6 * math.exp(-0.3 * l)
            w_in = od_w_in[o].astype(BF16)
            norm_gain = (jnp.tile(da_norm_g[o].astype(F32), DA_HEADS) * (1.0 - lam_init)).reshape(1, D)
            w_out = od_w_out[o].astype(BF16)
            lam_p = od_lambda[o].astype(F32)
            mixed = []
            for si, (x, cond_tok) in enumerate(streams):
                if si == 0:
                    q, k, v = _inproj_odd(x, mod[l], cond_tok, mix_gain, w_in, None, F32)
                    outputs.setdefault("k", []).append(k.reshape(n_p, s_p, DA_HEADS, HEAD))
                    outputs.setdefault("v", []).append(v.reshape(n_p, s_p, DA_HEADS, HEAD))
                    att = _attn_prompt(q, k, v, lam_p, s_p, lam_init)
                else:
                    q, k, v = _inproj_odd(x, mod[l], cond_tok, mix_gain, w_in, _rope_tables(s_s), BF16)
                    ck = cache_attn_k[:, o].reshape(n_s, past, D)
                    cv = cache_attn_v[:, o].reshape(n_s, past, D)
                    att = _attn_sample(q, k, v, ck, cv, lam_p, n_s, s_s, lam_init)
                mixed.append(([att], None, x))
        final_gain = final_norm_g.astype(F32).reshape(1, D) if l == DEPTH - 1 else None
        streams = [(_mix_ffn(hs, og, x, mod[l], cond_tok, norm_gain, w_out, ffn_gain, l, w1_all, w3_all, w2_all,
                             final_gain), cond_tok)
                   for (hs, og, x), (_, cond_tok) in zip(mixed, streams)]

    y_prompt = streams[0][0].reshape(n_p, s_p, D)
    y_sample = streams[1][0].reshape(n_s, s_s, D)
    stack = lambda name: outputs[name][0][:, None] if len(outputs[name]) == 1 else jnp.stack(outputs[name], axis=1)
    return (y_prompt, y_sample, stack("k"), stack("v"), stack("C"), stack("n"), stack("m"), stack("S"))
```

```python
import functools
import math

import jax
import jax.numpy as jnp
import numpy as np
from jax import lax
from jax.experimental import pallas as pl
from jax.experimental.pallas import tpu as pltpu

F32 = jnp.float32
BF16 = jnp.bfloat16

D = 1024
DEPTH = 2
GRID_W = 64
ML_HEADS = 4
HG_HEADS = 4
HEAD = 128
DA_HEADS = 8
DA_DQK = 64
ROPE_BASE = 10000.0
LOG2E = math.log2(math.e)
EPS = 1e-6
D_FF = ((8 * D // 3 + 255) // 256) * 256
EV_SIZES = (512, 512, 512, 512, 16, 512, 512, 512, 512, 512)
EV_COLS = 9 * 512 + 128

SEG = 256
HALF = SEG // 2
TILE_LEVELS = (1, 2, 4, 8, 16, 32, 64)
TOKEN_TILE = 512
PROJ_TILE = 1024
FFN_CHUNK = 256
KEY_CHUNK = 256
PROMPT_SEQS_PER_STEP = 4
ATTN_LOOKAHEAD = 6
VMEM_LIMIT = 56 * 1024 * 1024


def _cparams(n_axes):
    return pltpu.CompilerParams(dimension_semantics=("arbitrary",) * n_axes, vmem_limit_bytes=VMEM_LIMIT)


def _sigmoid(x):
    return 1.0 / (1.0 + jnp.exp(-x))


def _silu(x):
    return x * _sigmoid(x)


def _log_sigmoid(x):
    return jnp.minimum(x, 0.0) - jnp.log(1.0 + jnp.exp(-jnp.abs(x)))


def _dot(a, b):
    return jnp.dot(a, b, preferred_element_type=F32)


def _dot_nt(a, b):
    return lax.dot_general(a, b, (((1,), (1,)), ((), ())), preferred_element_type=F32)


def _norm_mod(x, gain, mod, k):
    ms = jnp.mean(x * x, axis=-1, keepdims=True)
    return x * lax.rsqrt(ms + EPS) * gain * (1.0 + mod[3 * k + 1:3 * k + 2]) + mod[3 * k:3 * k + 1]


def _mod_row(cond_tokens, tm):
    if cond_tokens is None:
        return lambda i: 0
    return lambda i: 1 + i // (cond_tokens // tm)


def _mod_kernel(c_ref, w_ref, b_ref, o_ref):
    s = _silu(c_ref[...]).astype(BF16)
    o_ref[...] = _dot(s, w_ref[...].astype(BF16)) + b_ref[...]


def _modulation(cond8, ada_w, ada_b):
    n_layers = ada_w.shape[0]
    tn = 1536
    return pl.pallas_call(
        _mod_kernel,
        grid=(n_layers, 6 * D // tn),
        in_specs=[pl.BlockSpec((8, D), lambda l, n: (0, 0)),
                  pl.BlockSpec((None, D, tn), lambda l, n: (l, 0, n)),
                  pl.BlockSpec((None, 1, tn), lambda l, n: (l, 0, n))],
        out_specs=pl.BlockSpec((None, 8, tn), lambda l, n: (l, 0, n)),
        out_shape=jax.ShapeDtypeStruct((n_layers, 8, 6 * D), F32),
        compiler_params=_cparams(2),
        name="ada_modulation",
    )(cond8, ada_w, ada_b.reshape(n_layers, 1, 6 * D))


def _regroup_even_weights(ev_w_in, e):
    g0 = sum(EV_SIZES[:4])
    w_e = ev_w_in[e].astype(BF16)
    return jnp.concatenate([w_e[:, :g0], w_e[:, g0 + 16:], jnp.pad(w_e[:, g0:g0 + 16], ((0, 0), (0, HEAD - 16)))],
                           axis=1)


def _inproj_even_body(x, mod, gain, w_ref, gate_b, lb_logits, e_idx, qkv_ref, gate_ref, hqi_ref, hgf_ref, og_ref):
    h = _norm_mod(x, gain, mod, 0).astype(BF16)

    def proj(c0, n):
        return _dot(h, w_ref[:, c0:c0 + n])

    mx = jnp.max(lb_logits, axis=0, keepdims=True)
    ex = jnp.exp(lb_logits - mx)
    lb = jnp.sum(ex[0:e_idx + 1], axis=0, keepdims=True) / jnp.sum(ex, axis=0, keepdims=True)
    hgf_ref[:, 0:512] = jnp.log(lb + (1.0 - lb) * _sigmoid(proj(2560, 512)))
    qkv_ref[:, 0:512] = (proj(0, 512) * (HEAD ** -0.5)).astype(BF16)
    hgf_ref[:, 512:1024] = jnp.log(lb + (1.0 - lb) * _sigmoid(proj(3072, 512)))
    qkv_ref[:, 512:1536] = proj(512, 1024).astype(BF16)
    og_ref[:, 0:512] = proj(1536, 512)
    hqi_ref[:, 0:512] = proj(2048, 512).astype(BF16)
    hqi_ref[:, 512:1024] = proj(3584, 512).astype(BF16)
    og_ref[:, 512:1024] = proj(4096, 512)
    gt = proj(4608, 128) + gate_b
    lane = lax.broadcasted_iota(jnp.int32, gt.shape, 1)
    gate_ref[...] = jnp.where(lane < 8, gt, jnp.where(lane < 16, _log_sigmoid(gt), 0.0))


def _inproj_even_kernel(x_ref, mod_ref, g_ref, w_ref, gb_ref, lbl_ref, qkv_ref, gate_ref, hqi_ref, hgf_ref,
                        og_ref, *, e_idx):
    _inproj_even_body(x_ref[...], mod_ref[...], g_ref[...], w_ref, gb_ref[...], lbl_ref[...], e_idx,
                      qkv_ref, gate_ref, hqi_ref, hgf_ref, og_ref)


def _inproj_even(x, mod_l, cond_tokens, gain, w, gate_b, lb_logits, e_idx):
    n_tok = x.shape[0]
    tm = PROJ_TILE
    tile_mod_row = _mod_row(cond_tokens, tm)
    tok = lambda n: pl.BlockSpec((tm, n), lambda i: (i, 0))
    full = lambda a: pl.BlockSpec(a.shape, lambda i: (0,) * a.ndim, pipeline_mode=pl.Buffered(1))
    return pl.pallas_call(
        functools.partial(_inproj_even_kernel, e_idx=e_idx),
        grid=(n_tok // tm,),
        in_specs=[tok(D), pl.BlockSpec((None, 6, D), lambda i: (tile_mod_row(i), 0, 0)),
                  full(gain), full(w), full(gate_b), full(lb_logits)],
        out_specs=[tok(1536), tok(128), tok(1024), tok(1024), tok(1024)],
        out_shape=[jax.ShapeDtypeStruct((n_tok, 1536), BF16), jax.ShapeDtypeStruct((n_tok, 128), F32),
                   jax.ShapeDtypeStruct((n_tok, 1024), BF16), jax.ShapeDtypeStruct((n_tok, 1024), F32),
                   jax.ShapeDtypeStruct((n_tok, 1024), F32)],
        compiler_params=_cparams(1),
        name="inproj_even",
    )(x, mod_l, gain, w, gate_b, lb_logits)


def _shift_rows(x, k, fill, up):
    n = x.shape[0]
    if k % 8 == 0:
        pad = jnp.full((k,) + x.shape[1:], 0.0 if fill is None else fill, x.dtype)
        return jnp.concatenate([x[k:], pad], axis=0) if up else jnp.concatenate([pad, x[:n - k]], axis=0)
    y = pltpu.roll(x, (n - k) if up else k, 0)
    if fill is None:
        return y
    row = lax.broadcasted_iota(jnp.int32, x.shape, 0)
    return jnp.where(row >= n - k, fill, y) if up else jnp.where(row < k, fill, y)


def _cummax_rows(x, rev):
    k = 1
    while k < x.shape[0]:
        x = jnp.maximum(x, _shift_rows(x, k, -jnp.inf, up=rev))
        k *= 2
    return x


def _cumsum_rows(tri, x):
    hi = x.astype(BF16)
    r1 = x - hi.astype(F32)
    mid = r1.astype(BF16)
    lo = (r1 - mid.astype(F32)).astype(BF16)
    return _dot(tri, hi) + _dot(tri, mid) + _dot(tri, lo)


def _hgrn_tile(q, kk, f, a, q_b, rev, mask_sc, d, up_small):
    ops = []
    bm = a
    for li, m in enumerate(TILE_LEVELS):
        if m < 8:
            qrole = jnp.logical_not(up_small[li]) if rev else up_small[li]
            if m == 1:
                r = jnp.where(qrole, q * f, kk)
            else:
                x = jnp.where(qrole, _shift_rows(bm, m, None, up=rev), bm)
                r = jnp.where(qrole, q, kk) * jnp.exp(jnp.where(qrole, a - x, x - a))
            if 2 * m < 8:
                bm = jnp.where(qrole, bm, _shift_rows(bm, m, None, up=not rev))
        else:
            pieces = []
            for b0 in range(0, HALF, 2 * m):
                lo, up = slice(b0, b0 + m), slice(b0 + m, b0 + 2 * m)
                if rev:
                    ref = a[b0 + m:b0 + m + 1]
                    pieces += [q[lo] * jnp.exp(a[lo] - ref), kk[up] * jnp.exp(ref - a[up])]
                else:
                    ref = a[b0 + m - 1:b0 + m]
                    pieces += [kk[lo] * jnp.exp(ref - a[lo]), q[up] * jnp.exp(a[up] - ref)]
            r = jnp.concatenate(pieces, axis=0)
        ops.append(r.astype(BF16))
    p = _dot_nt(q_b, kk.astype(BF16)).astype(BF16) * mask_sc[d, len(TILE_LEVELS)]
    for li, rb in enumerate(ops):
        p = p + _dot_nt(rb, rb).astype(BF16) * mask_sc[d, li]
    return p


def _build_scan_constants(tri_sc, mask_sc):
    row = lax.broadcasted_iota(jnp.int32, (SEG, SEG), 0)
    col = lax.broadcasted_iota(jnp.int32, (SEG, SEG), 1)
    tri_sc[0] = (col <= row).astype(BF16)
    tri_sc[1] = (col >= row).astype(BF16)
    trow = lax.broadcasted_iota(jnp.int32, (HALF, HALF), 0)
    tcol = lax.broadcasted_iota(jnp.int32, (HALF, HALF), 1)
    for li, m in enumerate(TILE_LEVELS):
        sh = m.bit_length() - 1
        same = (trow >> (sh + 1)) == (tcol >> (sh + 1))
        t_up = ((trow >> sh) & 1) == 1
        s_up = ((tcol >> sh) & 1) == 1
        mask_sc[0, li] = (same & t_up & jnp.logical_not(s_up)).astype(BF16)
        mask_sc[1, li] = (same & s_up & jnp.logical_not(t_up)).astype(BF16)
    eye = (trow == tcol).astype(BF16)
    mask_sc[0, len(TILE_LEVELS)] = eye
    mask_sc[1, len(TILE_LEVELS)] = eye


def _scan_segment(dir_refs, write_h, tri_sc, mask_sc, m_in, state, emit):
    row = lax.broadcasted_iota(jnp.int32, (SEG, SEG), 0)
    col = lax.broadcasted_iota(jnp.int32, (SEG, SEG), 1)
    lane128 = lax.broadcasted_iota(jnp.int32, (SEG, HEAD), 1)
    e0 = (lane128 == 0).astype(BF16)
    row_half = lax.broadcasted_iota(jnp.int32, (HALF, HEAD), 0)
    up_masks = [((row_half >> (m.bit_length() - 1)) & 1) == 1 for m in TILE_LEVELS if m < 8]
    tmasks = (col <= row, col >= row)
    use_state = state is not None
    want_state = use_state or emit is not None
    m_out_rows = []
    env = {}

    def prep(d):
        _, gate_ref, _, hgf_ref = dir_refs[d]
        rev = d == 1
        last = 0 if rev else SEG - 1
        tri = tri_sc[d]
        slab = gate_ref[...]
        b_al = pltpu.roll(_cumsum_rows(tri, slab), HEAD - 8, 1)
        u = slab - b_al
        mx = jnp.maximum(_cummax_rows(u, rev), m_in)
        mx_last = mx[last:last + 1, :]
        m_out_rows.append(b_al[last:last + 1, :] + mx_last)
        lf_all = hgf_ref[:, 512 * d:512 * (d + 1)]
        f_all = jnp.exp(lf_all)
        env[d] = dict(rev=rev, last=last, mx=mx, w_inter=jnp.exp(m_in - mx), e_den=jnp.exp(-(b_al + mx)),
                      decay=jnp.exp(m_in - mx_last), wg=jnp.exp(u - mx_last), u_t=u.T,
                      a_all=_cumsum_rows(tri, lf_all), f_all=f_all, kk_all=1.0 - f_all)

    def ml_scores(d, h):
        e, qkv_ref, c = env[d], dir_refs[d][0], 4 * d + h
        q = qkv_ref[:, h * HEAD:(h + 1) * HEAD]
        k = qkv_ref[:, 512 + h * HEAD:512 + (h + 1) * HEAD]
        v = qkv_ref[:, 1024 + h * HEAD:1024 + (h + 1) * HEAD]
        dm = jnp.where(tmasks[d], jnp.exp(e["u_t"][c:c + 1, :] - e["mx"][:, c:c + 1]), 0.0)
        e["ml", h] = (q, k, jnp.concatenate([v, e0], axis=1), (_dot_nt(q, k) * dm).astype(BF16))

    def ml_out(d, h):
        e, c = env[d], 4 * d + h
        q, _, v_aug, s = e["ml", h]
        numden = _dot(s, v_aug)
        if use_state:
            numden = numden + e["w_inter"][:, c:c + 1] * _dot(q, state[0][d, h].astype(BF16))
        den = jnp.maximum(jnp.abs(numden[:, HEAD:HEAD + 1]), e["e_den"][:, c:c + 1])
        write_h(d, h * HEAD, numden[:, 0:HEAD] / den)

    def ml_state(d, h):
        e, c = env[d], 4 * d + h
        _, k, v_aug, _ = e["ml", h]
        kw_t = (k.astype(F32) * e["wg"][:, c:c + 1]).T.astype(BF16)
        upd = _dot(kw_t, v_aug)
        if use_state:
            upd = upd + e["decay"][:, c:c + 1] * state[0][d, h]
            state[0][d, h] = upd
        if emit is not None:
            emit("ml", d, h, upd)

    def hg_cross(d, h):
        e, hqi_ref = env[d], dir_refs[d][2]
        rev = e["rev"]
        sl = slice(h * HEAD, (h + 1) * HEAD)
        a, kk = e["a_all"][:, sl], e["kk_all"][:, sl]
        q_b = hqi_ref[:, h * HEAD:(h + 1) * HEAD]
        q = q_b.astype(F32)
        qh, kh = (0, 1) if rev else (1, 0)
        a_q, a_k = a[qh * HALF:(qh + 1) * HALF], a[kh * HALF:(kh + 1) * HALF]
        a_mid = a[HALF:HALF + 1] if rev else a[HALF - 1:HALF]
        r_q = (q[qh * HALF:(qh + 1) * HALF] * jnp.exp(a_q - a_mid)).astype(BF16)
        r_k = (kk[kh * HALF:(kh + 1) * HALF] * jnp.exp(a_mid - a_k)).astype(BF16)
        e["hg", h] = dict(a=a, kk=kk, f=e["f_all"][:, sl], q_b=q_b, q=q, p_tiles=[],
                          i_b=hqi_ref[:, 512 + h * HEAD:512 + (h + 1) * HEAD], p_cross=_dot_nt(r_q, r_k).astype(BF16))

    def hg_tile(d, h, t):
        g, rows = env[d]["hg", h], slice(t * HALF, (t + 1) * HALF)
        g["p_tiles"].append(_hgrn_tile(g["q"][rows], g["kk"][rows], g["f"][rows], g["a"][rows], g["q_b"][rows],
                                       env[d]["rev"], mask_sc, d, up_masks))

    def hg_out(d, h):
        g = env[d]["hg", h]
        p_tiles, p_cross, i_b = g["p_tiles"], g["p_cross"], g["i_b"]
        i_lo, i_hi = i_b[0:HALF], i_b[HALF:SEG]
        if env[d]["rev"]:
            o = jnp.concatenate([_dot(p_tiles[0], i_lo) + _dot(p_cross, i_hi), _dot(p_tiles[1], i_hi)], axis=0)
        else:
            o = jnp.concatenate([_dot(p_tiles[0], i_lo), _dot(p_cross, i_lo) + _dot(p_tiles[1], i_hi)], axis=0)
        if use_state:
            o = o + _dot_nt((g["q"] * jnp.exp(g["a"])).astype(BF16), state[1][d, h].astype(BF16))
        write_h(d, 512 + h * HEAD, o)

    def hg_state(d, h):
        e, g = env[d], env[d]["hg", h]
        a, kk = g["a"], g["kk"]
        a_l = a[e["last"]:e["last"] + 1, :]
        kd = (kk * jnp.exp(a_l - a)).astype(BF16)
        st_new = _dot(g["i_b"].astype(F32).T.astype(BF16), kd)
        if use_state:
            st_new = st_new + state[1][d, h] * jnp.exp(a_l)
            state[1][d, h] = st_new
        if emit is not None:
            emit("hg", d, h, st_new)

    prep(0)
    prep(1)
    pieces = [ml_scores, hg_cross, functools.partial(hg_tile, t=0), ml_out, functools.partial(hg_tile, t=1)]
    pieces += [ml_state, hg_out, hg_state] if want_state else [hg_out]
    for piece in pieces:
        for h in range(ML_HEADS):
            for d in range(2):
                piece(d, h)

    lane_row = lax.broadcasted_iota(jnp.int32, (1, HEAD), 1)
    return jnp.where(lane_row < 4, m_out_rows[0], jnp.where(lane_row < 8, m_out_rows[1], 0.0))


def _emit_states(kind, d, h, value, c_ref, n_ref, s_ref):
    if kind == "ml":
        c_ref[d, h] = value[:, 0:HEAD]
        n_ref[d, h:h + 1, :] = value[:, HEAD:2 * HEAD].T[0:1, :]
    else:
        s_ref[d, h] = value.T


def _scan_kernel(*refs, has_init, emit_state, nseg):
    single = nseg == 1
    n_tok_in = 4 if single else 8
    n_h = 1 if single else 2
    n_in = n_tok_in + (4 if has_init else 0)
    n_out = n_h + (4 if emit_state else 0)
    ins, outs, scr = refs[:n_in], refs[n_in:n_in + n_out], refs[n_in + n_out:]
    dir_refs = (ins[0:4], ins[0:4] if single else ins[4:8])
    st_outs = outs[n_h:]
    tri_sc, mask_sc, caug_sc, st_sc, m_sc = scr
    use_state = has_init or nseg > 1
    b_id, j = pl.program_id(0), pl.program_id(1)

    @pl.when(jnp.logical_and(b_id == 0, j == 0))
    def _build_constants():
        _build_scan_constants(tri_sc, mask_sc)

    if use_state:
        @pl.when(j == 0)
        def _init_state():
            if has_init:
                c0_ref, n0_ref, m0_ref, s0_ref = ins[n_tok_in:n_tok_in + 4]
                for d in range(2):
                    for h in range(ML_HEADS):
                        caug_sc[d, h, :, 0:HEAD] = c0_ref[d, h]
                        caug_sc[d, h, :, HEAD:2 * HEAD] = jnp.broadcast_to(n0_ref[d, h:h + 1, :], (HEAD, HEAD)).T
                        st_sc[d, h] = s0_ref[d, h].T
                m_sc[...] = jnp.broadcast_to(m0_ref[...], m_sc.shape)
            else:
                caug_sc[...] = jnp.zeros(caug_sc.shape, F32)
                st_sc[...] = jnp.zeros(st_sc.shape, F32)
                m_sc[...] = jnp.zeros(m_sc.shape, F32)

    def write_h(d, c0, value):
        if single and d == 1:
            outs[0][:, c0:c0 + HEAD] += value
        else:
            outs[0 if single else d][:, c0:c0 + HEAD] = value

    def emit(kind, d, h, value):
        if single:
            _emit_states(kind, d, h, value, st_outs[0], st_outs[1], st_outs[3])
        else:
            pl.when(j == nseg - 1)(lambda: _emit_states(kind, d, h, value, st_outs[0], st_outs[1], st_outs[3]))

    m_in = m_sc[0:1, :] if use_state else jnp.zeros((1, HEAD), F32)
    m_new = _scan_segment(dir_refs, write_h, tri_sc, mask_sc, m_in, (caug_sc, st_sc) if use_state else None,
                          emit if emit_state else None)
    if use_state:
        m_sc[...] = jnp.broadcast_to(m_new, m_sc.shape)
    if emit_state:
        if single:
            st_outs[2][...] = m_new
        else:
            @pl.when(j == nseg - 1)
            def _emit_m():
                st_outs[2][...] = m_new


def _scan(qkv, gate, hqi, hgf, n_seq, nseg, init=None, emit_state=False):
    n_tok = qkv.shape[0]
    fwd = lambda n: pl.BlockSpec((SEG, n), lambda b, j: (b * nseg + j, 0))
    bwd = lambda n: pl.BlockSpec((SEG, n), lambda b, j: (b * nseg + nseg - 1 - j, 0))
    widths = (1536, 128, 1024, 1024)
    single = nseg == 1
    in_specs = [fwd(n) for n in widths] + ([] if single else [bwd(n) for n in widths])
    args = [qkv, gate, hqi, hgf] + ([] if single else [qkv, gate, hqi, hgf])
    mat = pl.BlockSpec((None, 2, 4, HEAD, HEAD), lambda b, j: (b, 0, 0, 0, 0))
    vec = pl.BlockSpec((None, 2, 4, HEAD), lambda b, j: (b, 0, 0, 0))
    sca = pl.BlockSpec((None, 1, HEAD), lambda b, j: (b, 0, 0))
    if init is not None:
        in_specs += [mat, vec, sca, mat]
        args += list(init)
    out_specs = [fwd(1024)] if single else [fwd(1024), bwd(1024)]
    out_shape = [jax.ShapeDtypeStruct((n_tok, 1024), F32)] * len(out_specs)
    if emit_state:
        out_specs += [mat, vec, sca, mat]
        out_shape += [jax.ShapeDtypeStruct((n_seq, 2, 4, HEAD, HEAD), F32),
                      jax.ShapeDtypeStruct((n_seq, 2, 4, HEAD), F32),
                      jax.ShapeDtypeStruct((n_seq, 1, HEAD), F32),
                      jax.ShapeDtypeStruct((n_seq, 2, 4, HEAD, HEAD), F32)]
    return pl.pallas_call(
        functools.partial(_scan_kernel, has_init=init is not None, emit_state=emit_state, nseg=nseg),
        grid=(n_seq, nseg),
        in_specs=in_specs,
        out_specs=out_specs,
        out_shape=out_shape,
        scratch_shapes=[pltpu.VMEM((2, SEG, SEG), BF16),
                        pltpu.VMEM((2, len(TILE_LEVELS) + 1, HALF, HALF), BF16),
                        pltpu.VMEM((2, 4, HEAD, 2 * HEAD), F32),
                        pltpu.VMEM((2, 4, HEAD, HEAD), F32),
                        pltpu.VMEM((8, HEAD), F32)],
        compiler_params=_cparams(2),
        name="bidir_scan",
    )(*args)


def _mix_ffn_tile(h, og, x, mod, norm_gain, wo_ref, ffn_gain, w1_ref, w3_ref, w2_ref, final_gain, o_ref, chunk):
    parts = []
    for g in range(D // HEAD):
        hs = h[:, g * HEAD:(g + 1) * HEAD]
        parts.append(hs * lax.rsqrt(jnp.mean(hs * hs, axis=-1, keepdims=True) + EPS))
    hn = jnp.concatenate(parts, axis=1) * norm_gain
    if og is not None:
        hn = hn * jnp.concatenate([_sigmoid(og[:, 0:512]), _silu(og[:, 512:1024])], axis=1)
    x = x + mod[2:3, :] * _dot(hn.astype(BF16), wo_ref[...])
    hf = _norm_mod(x, ffn_gain, mod, 1).astype(BF16)
    up = lambda c0: (_dot(hf, w1_ref[:, c0:min(c0 + chunk, D_FF)]), _dot(hf, w3_ref[:, c0:min(c0 + chunk, D_FF)]))
    starts = list(range(0, D_FF, chunk))
    acc = jnp.zeros(x.shape, F32)
    nxt = up(starts[0])
    for i, c0 in enumerate(starts):
        a, b = nxt
        if i + 1 < len(starts):
            nxt = up(starts[i + 1])
        acc = acc + _dot((_silu(a) * b).astype(BF16), w2_ref[c0:min(c0 + chunk, D_FF), :])
    y = x + mod[5:6, :] * acc
    if final_gain is not None:
        y = y * lax.rsqrt(jnp.mean(y * y, axis=-1, keepdims=True) + EPS) * final_gain
    o_ref[...] = y


def _mix_ffn_kernel(*refs, n_h, gated, final, chunk):
    h_refs = refs[:n_h]
    pos = n_h
    og_ref = refs[pos] if gated else None
    pos += 1 if gated else 0
    x_ref, mod_ref, ng_ref, wo_ref, g_ref, w1_ref, w3_ref, w2_ref = refs[pos:pos + 8]
    fg_ref = refs[pos + 8] if final else None
    o_ref = refs[-1]
    h = h_refs[0][...]
    for r in h_refs[1:]:
        h = h + r[...]
    _mix_ffn_tile(h, og_ref[...] if gated else None, x_ref[...], mod_ref[...], ng_ref[...], wo_ref, g_ref[...],
                  w1_ref, w3_ref, w2_ref, fg_ref[...] if final else None, o_ref, chunk)


def _mix_ffn(hs, og, x, mod_l, cond_tokens, norm_gain, w_out, ffn_gain, layer, w1, w3, w2, final_gain=None):
    n_tok = x.shape[0]
    tm = TOKEN_TILE
    tile_mod_row = _mod_row(cond_tokens, tm)
    tok = pl.BlockSpec((tm, D), lambda i: (i, 0))
    full = lambda a: pl.BlockSpec(a.shape, lambda i: (0,) * a.ndim, pipeline_mode=pl.Buffered(1))
    of_layer = lambda a: pl.BlockSpec((None,) + a.shape[1:], lambda i: (layer, 0, 0), pipeline_mode=pl.Buffered(1))
    gated = og is not None
    final = final_gain is not None
    consts = [norm_gain, w_out, ffn_gain, w1, w3, w2] + ([final_gain] if final else [])
    args = list(hs) + ([og] if gated else []) + [x, mod_l] + consts
    in_specs = [tok] * (len(hs) + (1 if gated else 0) + 1)
    in_specs += [pl.BlockSpec((None, 6, D), lambda i: (tile_mod_row(i), 0, 0))]
    in_specs += [of_layer(a) if a.ndim == 3 else full(a) for a in consts]
    return pl.pallas_call(
        functools.partial(_mix_ffn_kernel, n_h=len(hs), gated=gated, final=final, chunk=FFN_CHUNK),
        grid=(n_tok // tm,),
        in_specs=in_specs,
        out_specs=tok,
        out_shape=jax.ShapeDtypeStruct((n_tok, D), F32),
        compiler_params=_cparams(1),
        name="mix_ffn",
    )(*args)


def _rope(x, cos, sin_signed):
    lane = lax.broadcasted_iota(jnp.int32, (x.shape[0], HEAD), 1)
    first = (lane & 16) == 0
    parts = []
    for h in range(DA_HEADS):
        xh = x[:, h * HEAD:(h + 1) * HEAD]
        partner = jnp.where(first, pltpu.roll(xh, HEAD - 16, 1), pltpu.roll(xh, 16, 1))
        parts.append(xh * cos + partner * sin_signed)
    return jnp.concatenate(parts, axis=1)


def _inproj_odd_kernel(*refs, rope):
    x_ref, mod_ref, g_ref, w_ref = refs[:4]
    q_ref, k_ref, v_ref = refs[-3:]
    h = _norm_mod(x_ref[...], g_ref[...], mod_ref[...], 0).astype(BF16)
    q = _dot(h, w_ref[:, 0:D])
    k = _dot(h, w_ref[:, D:2 * D])
    v = _dot(h, w_ref[:, 2 * D:3 * D])
    if rope:
        cos, sin_signed = refs[4][...], refs[5][...]
        q = _rope(q, cos, sin_signed)
        k = _rope(k, cos, sin_signed)
    q_ref[...] = (q * (DA_DQK ** -0.5 * LOG2E)).astype(q_ref.dtype)
    k_ref[...] = k.astype(k_ref.dtype)
    v_ref[...] = v.astype(v_ref.dtype)


def _inproj_odd(x, mod_l, cond_tokens, gain, w, rope_tables, kv_dtype):
    n_tok = x.shape[0]
    tm = PROJ_TILE
    tile_mod_row = _mod_row(cond_tokens, tm)
    tok = pl.BlockSpec((tm, D), lambda i: (i, 0))
    full = lambda a: pl.BlockSpec(a.shape, lambda i: (0,) * a.ndim, pipeline_mode=pl.Buffered(1))
    args = [x, mod_l, gain, w]
    in_specs = [tok, pl.BlockSpec((None, 6, D), lambda i: (tile_mod_row(i), 0, 0)), full(gain), full(w)]
    if rope_tables is not None:
        tiles_per_seq = rope_tables[0].shape[0] // tm
        args += list(rope_tables)
        in_specs += [pl.BlockSpec((tm, HEAD), lambda i: (i % tiles_per_seq, 0))] * 2
    return pl.pallas_call(
        functools.partial(_inproj_odd_kernel, rope=rope_tables is not None),
        grid=(n_tok // tm,),
        in_specs=in_specs,
        out_specs=[tok, tok, tok],
        out_shape=[jax.ShapeDtypeStruct((n_tok, D), BF16), jax.ShapeDtypeStruct((n_tok, D), kv_dtype),
                   jax.ShapeDtypeStruct((n_tok, D), kv_dtype)],
        compiler_params=_cparams(1),
        name="inproj_odd",
    )(*args)


def _rope_tables(n_tok):
    quarter = DA_DQK // 4
    tok = np.arange(n_tok)
    pos = np.stack([tok // GRID_W, tok % GRID_W], axis=1).astype(np.float32)
    inv = (np.float32(ROPE_BASE) ** (-np.arange(quarter, dtype=np.float32) / np.float32(quarter))).astype(np.float32)
    lane = np.arange(HEAD)
    ang = (pos[:, (lane // 32) % 2] * inv[lane % quarter][None, :]).astype(np.float32)
    sign = np.where((lane % 32) < quarter, -1.0, 1.0).astype(np.float32)
    return jnp.asarray(np.cos(ang), F32), jnp.asarray(np.sin(ang) * sign[None, :], F32)


def _lambda(lam_ref, lam_init):
    lp = lam_ref[...]
    return (jnp.exp(jnp.sum(lp[0:1] * lp[1:2], axis=-1, keepdims=True))
            - jnp.exp(jnp.sum(lp[2:3] * lp[3:4], axis=-1, keepdims=True)) + lam_init)


VT_ROWS = HEAD + 16


def _qt2(qh):
    qt = qh.astype(F32).T
    row = lax.broadcasted_iota(jnp.int32, qt.shape, 0)
    return jnp.concatenate([jnp.where(row < DA_DQK, qt, 0.0), jnp.where(row >= DA_DQK, qt, 0.0)],
                           axis=1).astype(BF16)


def _vt_aug(vh):
    tk = vh.shape[0]
    row = lax.broadcasted_iota(jnp.int32, (VT_ROWS - HEAD, tk), 0)
    return jnp.concatenate([vh.astype(F32).T, (row == 0).astype(F32)], axis=0).astype(BF16)


def _diff_attn_heads(q_ref, k_fn, vt_fn, n_chunks, lam, o_ref, n_seq=1):
    tq = q_ref.shape[0] // n_seq
    items = [(s, h, c) for s in range(n_seq) for h in range(DA_HEADS) for c in range(n_chunks)]
    qt2, state, pending = {}, {}, []

    def finish(s, h, c, st):
        cm = jnp.max(st, axis=0, keepdims=True)
        if c == 0:
            m_new = cm
            acc = _dot(vt_fn(s, h, c), jnp.exp2(st - m_new).astype(BF16))
        else:
            m, acc = state[s, h]
            m_new = jnp.maximum(m, cm)
            acc = acc * jnp.exp2(m - m_new) + _dot(vt_fn(s, h, c), jnp.exp2(st - m_new).astype(BF16))
        state[s, h] = (m_new, acc)
        if c == n_chunks - 1:
            den = acc[HEAD:HEAD + 1, :]
            o_t = acc[0:HEAD, 0:tq] * (1.0 / den[:, 0:tq]) - acc[0:HEAD, tq:] * (lam / den[:, tq:])
            o_ref[s * tq:(s + 1) * tq, h * HEAD:(h + 1) * HEAD] = o_t.T

    for s, h, c in items:
        if c == 0:
            qt2[s, h] = _qt2(q_ref[s * tq:(s + 1) * tq, h * HEAD:(h + 1) * HEAD])
        pending.append((s, h, c, _dot(k_fn(s, h, c), qt2[s, h])))
        if len(pending) > ATTN_LOOKAHEAD:
            finish(*pending.pop(0))
    while pending:
        finish(*pending.pop(0))


def _attn_prompt_kernel(q_ref, k_ref, v_ref, lam_ref, o_ref, *, lam_init, seq):
    lam = _lambda(lam_ref, lam_init)
    rows = lambda s: slice(s * seq, (s + 1) * seq)
    _diff_attn_heads(q_ref, lambda s, h, c: k_ref[rows(s), h * HEAD:(h + 1) * HEAD].astype(BF16),
                     lambda s, h, c: _vt_aug(v_ref[rows(s), h * HEAD:(h + 1) * HEAD]), 1, lam, o_ref,
                     n_seq=q_ref.shape[0] // seq)


def _attn_prompt(q, k, v, lam_p, seq, lam_init):
    n_tok = q.shape[0]
    tok = pl.BlockSpec((PROMPT_SEQS_PER_STEP * seq, D), lambda b: (b, 0))
    return pl.pallas_call(
        functools.partial(_attn_prompt_kernel, lam_init=lam_init, seq=seq),
        grid=(n_tok // (PROMPT_SEQS_PER_STEP * seq),),
        in_specs=[tok, tok, tok, pl.BlockSpec(lam_p.shape, lambda b: (0, 0))],
        out_specs=tok,
        out_shape=jax.ShapeDtypeStruct((n_tok, D), F32),
        compiler_params=_cparams(1),
        name="diff_attn_prompt",
    )(q, k, v, lam_p)


def _attn_sample_kernel(q_ref, k_ref, v_ref, ck_ref, cv_ref, lam_ref, o_ref, kcat, vtcat, *, lam_init, past):
    n_keys = kcat.shape[0]
    chunks = [(c0, min(KEY_CHUNK, n_keys - c0)) for c0 in range(0, n_keys, KEY_CHUNK)]

    @pl.when(pl.program_id(1) == 0)
    def _gather_keys():
        kcat[0:past, :] = ck_ref[...].astype(BF16)
        kcat[past:, :] = k_ref[...]
        for h in range(DA_HEADS):
            sl = slice(h * HEAD, (h + 1) * HEAD)
            for c0 in range(0, n_keys, past):
                src, r0 = (cv_ref, c0) if c0 < past else (v_ref, c0 - past)
                vtcat[h, :, c0:c0 + past] = _vt_aug(src[r0:r0 + past, sl])

    lam = _lambda(lam_ref, lam_init)
    _diff_attn_heads(q_ref, lambda s, h, c: kcat[chunks[c][0]:chunks[c][0] + chunks[c][1], h * HEAD:(h + 1) * HEAD],
                     lambda s, h, c: vtcat[h, :, chunks[c][0]:chunks[c][0] + chunks[c][1]], len(chunks), lam, o_ref)


def _attn_sample(q, k, v, ck, cv, lam_p, n_seq, seq, lam_init, tq=128):
    past = ck.shape[1]
    qb = pl.BlockSpec((tq, D), lambda b, i: (b * (seq // tq) + i, 0))
    kv = pl.BlockSpec((seq, D), lambda b, i: (b, 0))
    cache = pl.BlockSpec((None, past, D), lambda b, i: (b, 0, 0))
    return pl.pallas_call(
        functools.partial(_attn_sample_kernel, lam_init=lam_init, past=past),
        grid=(n_seq, seq // tq),
        in_specs=[qb, kv, kv, cache, cache, pl.BlockSpec(lam_p.shape, lambda b, i: (0, 0))],
        out_specs=qb,
        out_shape=jax.ShapeDtypeStruct((n_seq * seq, D), F32),
        scratch_shapes=[pltpu.VMEM((past + seq, D), BF16), pltpu.VMEM((DA_HEADS, VT_ROWS, past + seq), BF16)],
        compiler_params=_cparams(2),
        name="diff_attn_sample",
    )(q, k, v, ck, cv, lam_p)


def kernel(x_prompt, x_sample, c, c_ctx, cache_attn_k, cache_attn_v, state_mlstm_C, state_mlstm_n, state_mlstm_m,
           state_hgrn_S, ada_w, ada_b, norm_mix_g, norm_ffn_g, ev_w_in, ev_gate_b, ev_lb_logits, ml_norm_g,
           hg_norm_g, ev_w_out, od_w_in, od_lambda, da_norm_g, od_w_out, ffn_w1, ffn_w3, ffn_w2, final_norm_g):
    assert DEPTH % 2 == 0
    n_p, s_p, _ = x_prompt.shape
    n_s, s_s, _ = x_sample.shape
    past = cache_attn_k.shape[2]
    assert s_p == SEG and s_s % SEG == 0 and s_s % PROJ_TILE == 0 and (n_p * s_p) % PROJ_TILE == 0
    assert PROJ_TILE % TOKEN_TILE == 0
    xp = x_prompt.astype(F32).reshape(n_p * s_p, D)
    xs = x_sample.astype(F32).reshape(n_s * s_s, D)

    cond8 = jnp.zeros((8, D), F32).at[0].set(c_ctx.astype(F32)).at[1:1 + n_s].set(c.astype(F32))
    mod = _modulation(cond8, ada_w.astype(F32), ada_b.astype(F32)).reshape(DEPTH, 8, 6, D)
    streams = [(xp, None), (xs, s_s)]
    outputs = {}
    w1_all, w3_all, w2_all = ffn_w1.astype(BF16), ffn_w3.astype(BF16), ffn_w2.astype(BF16)

    for l in range(DEPTH):
        mix_gain = norm_mix_g[l].astype(F32).reshape(1, D)
        ffn_gain = norm_ffn_g[l].astype(F32).reshape(1, D)
        if l % 2 == 0:
            e = l // 2
            w_in = _regroup_even_weights(ev_w_in, e)
            gate_b = jnp.pad(ev_gate_b[e].astype(F32), (0, HEAD - 16)).reshape(1, HEAD)
            norm_gain = jnp.concatenate([ml_norm_g[e], hg_norm_g[e]]).astype(F32).reshape(1, D)
            w_out = ev_w_out[e].astype(BF16)
            mixed = []
            for si, (x, cond_tok) in enumerate(streams):
                qkv, gate, hqi, hgf, og = _inproj_even(x, mod[l], cond_tok, mix_gain, w_in, gate_b,
                                                       ev_lb_logits.astype(F32), e)
                if si == 0:
                    *hs, c_new, n_new, m_new, s_new = _scan(qkv, gate, hqi, hgf, n_p, s_p // SEG, emit_state=True)
                    outputs.setdefault("C", []).append(c_new)
                    outputs.setdefault("n", []).append(n_new)
                    outputs.setdefault("m", []).append(m_new[:, 0, 0:8].reshape(n_p, 2, 4))
                    outputs.setdefault("S", []).append(s_new)
                else:
                    init = (state_mlstm_C[:, e].astype(F32), state_mlstm_n[:, e].astype(F32),
                            jnp.pad(state_mlstm_m[:, e].astype(F32).reshape(n_s, 1, 8), ((0, 0), (0, 0), (0, HEAD - 8))),
                            state_hgrn_S[:, e].astype(F32))
                    hs = _scan(qkv, gate, hqi, hgf, n_s, s_s // SEG, init=init)
                mixed.append((list(hs), og, x))
        else:
            o = l // 2
            lam_init = 0.8 - 0.6 * math.exp(-0.3 * l)
            w_in = od_w_in[o].astype(BF16)
            norm_gain = (jnp.tile(da_norm_g[o].astype(F32), DA_HEADS) * (1.0 - lam_init)).reshape(1, D)
            w_out = od_w_out[o].astype(BF16)
            lam_p = od_lambda[o].astype(F32)
            mixed = []
            for si, (x, cond_tok) in enumerate(streams):
                if si == 0:
                    q, k, v = _inproj_odd(x, mod[l], cond_tok, mix_gain, w_in, None, F32)
                    outputs.setdefault("k", []).append(k.reshape(n_p, s_p, DA_HEADS, HEAD))
                    outputs.setdefault("v", []).append(v.reshape(n_p, s_p, DA_HEADS, HEAD))
                    att = _attn_prompt(q, k, v, lam_p, s_p, lam_init)
                else:
                    q, k, v = _inproj_odd(x, mod[l], cond_tok, mix_gain, w_in, _rope_tables(s_s), BF16)
                    ck = cache_attn_k[:, o].reshape(n_s, past, D)
                    cv = cache_attn_v[:, o].reshape(n_s, past, D)
                    att = _attn_sample(q, k, v, ck, cv, lam_p, n_s, s_s, lam_init)
                mixed.append(([att], None, x))
        final_gain = final_norm_g.astype(F32).reshape(1, D) if l == DEPTH - 1 else None
        streams = [(_mix_ffn(hs, og, x, mod[l], cond_tok, norm_gain, w_out, ffn_gain, l, w1_all, w3_all, w2_all,
                             final_gain), cond_tok)
                   for (hs, og, x), (_, cond_tok) in zip(mixed, streams)]

    y_prompt = streams[0][0].reshape(n_p, s_p, D)
    y_sample = streams[1][0].reshape(n_s, s_s, D)
    stack = lambda name: outputs[name][0][:, None] if len(outputs[name]) == 1 else jnp.stack(outputs[name], axis=1)
    return (y_prompt, y_sample, stack("k"), stack("v"), stack("C"), stack("n"), stack("m"), stack("S"))
```

```python
import functools
import math

import jax
import jax.numpy as jnp
import numpy as np
from jax import lax
from jax.experimental import pallas as pl
from jax.experimental.pallas import tpu as pltpu

F32 = jnp.float32
BF16 = jnp.bfloat16

D = 1024
DEPTH = 2
GRID_W = 64
ML_HEADS = 4
HG_HEADS = 4
HEAD = 128
DA_HEADS = 8
DA_DQK = 64
ROPE_BASE = 10000.0
LOG2E = math.log2(math.e)
EPS = 1e-6
D_FF = ((8 * D // 3 + 255) // 256) * 256
EV_SIZES = (512, 512, 512, 512, 16, 512, 512, 512, 512, 512)
EV_COLS = 9 * 512 + 128

SEG = 256
HALF = SEG // 2
TILE_LEVELS = (1, 2, 4, 8, 16, 32, 64)
TOKEN_TILE = 512
PROJ_TILE = 1024
FFN_CHUNK = 256
KEY_CHUNK = 256
PROMPT_SEQS_PER_STEP = 4
ATTN_LOOKAHEAD = 6
VMEM_LIMIT = 56 * 1024 * 1024


def _cparams(n_axes):
    return pltpu.CompilerParams(dimension_semantics=("arbitrary",) * n_axes, vmem_limit_bytes=VMEM_LIMIT)


def _sigmoid(x):
    return 1.0 / (1.0 + jnp.exp(-x))


def _silu(x):
    return x * _sigmoid(x)


def _log_sigmoid(x):
    return jnp.minimum(x, 0.0) - jnp.log(1.0 + jnp.exp(-jnp.abs(x)))


def _dot(a, b):
    return jnp.dot(a, b, preferred_element_type=F32)


def _dot_nt(a, b):
    return lax.dot_general(a, b, (((1,), (1,)), ((), ())), preferred_element_type=F32)


def _norm_mod(x, gain, mod, k):
    ms = jnp.mean(x * x, axis=-1, keepdims=True)
    return x * lax.rsqrt(ms + EPS) * gain * (1.0 + mod[3 * k + 1:3 * k + 2]) + mod[3 * k:3 * k + 1]


def _mod_row(cond_tokens, tm):
    if cond_tokens is None:
        return lambda i: 0
    return lambda i: 1 + i // (cond_tokens // tm)


def _mod_kernel(c_ref, w_ref, b_ref, o_ref):
    s = _silu(c_ref[...]).astype(BF16)
    o_ref[...] = _dot(s, w_ref[...].astype(BF16)) + b_ref[...]


def _modulation(cond8, ada_w, ada_b):
    n_layers = ada_w.shape[0]
    tn = 1536
    return pl.pallas_call(
        _mod_kernel,
        grid=(n_layers, 6 * D // tn),
        in_specs=[pl.BlockSpec((8, D), lambda l, n: (0, 0)),
                  pl.BlockSpec((None, D, tn), lambda l, n: (l, 0, n)),
                  pl.BlockSpec((None, 1, tn), lambda l, n: (l, 0, n))],
        out_specs=pl.BlockSpec((None, 8, tn), lambda l, n: (l, 0, n)),
        out_shape=jax.ShapeDtypeStruct((n_layers, 8, 6 * D), F32),
        compiler_params=_cparams(2),
        name="ada_modulation",
    )(cond8, ada_w, ada_b.reshape(n_layers, 1, 6 * D))


def _regroup_even_weights(ev_w_in, e):
    g0 = sum(EV_SIZES[:4])
    w_e = ev_w_in[e].astype(BF16)
    return jnp.concatenate([w_e[:, :g0], w_e[:, g0 + 16:], jnp.pad(w_e[:, g0:g0 + 16], ((0, 0), (0, HEAD - 16)))],
                           axis=1)


def _inproj_even_body(x, mod, gain, w_ref, gate_b, lb_logits, e_idx, qkv_ref, gate_ref, hqi_ref, hgf_ref, og_ref):
    h = _norm_mod(x, gain, mod, 0).astype(BF16)

    def proj(c0, n):
        return _dot(h, w_ref[:, c0:c0 + n])

    mx = jnp.max(lb_logits, axis=0, keepdims=True)
    ex = jnp.exp(lb_logits - mx)
    lb = jnp.sum(ex[0:e_idx + 1], axis=0, keepdims=True) / jnp.sum(ex, axis=0, keepdims=True)
    hgf_ref[:, 0:512] = jnp.log(lb + (1.0 - lb) * _sigmoid(proj(2560, 512)))
    qkv_ref[:, 0:512] = (proj(0, 512) * (HEAD ** -0.5)).astype(BF16)
    hgf_ref[:, 512:1024] = jnp.log(lb + (1.0 - lb) * _sigmoid(proj(3072, 512)))
    qkv_ref[:, 512:1536] = proj(512, 1024).astype(BF16)
    og_ref[:, 0:512] = proj(1536, 512)
    hqi_ref[:, 0:512] = proj(2048, 512).astype(BF16)
    hqi_ref[:, 512:1024] = proj(3584, 512).astype(BF16)
    og_ref[:, 512:1024] = proj(4096, 512)
    gt = proj(4608, 128) + gate_b
    lane = lax.broadcasted_iota(jnp.int32, gt.shape, 1)
    gate_ref[...] = jnp.where(lane < 8, gt, jnp.where(lane < 16, _log_sigmoid(gt), 0.0))


def _inproj_even_kernel(x_ref, mod_ref, g_ref, w_ref, gb_ref, lbl_ref, qkv_ref, gate_ref, hqi_ref, hgf_ref,
                        og_ref, *, e_idx):
    _inproj_even_body(x_ref[...], mod_ref[...], g_ref[...], w_ref, gb_ref[...], lbl_ref[...], e_idx,
                      qkv_ref, gate_ref, hqi_ref, hgf_ref, og_ref)


def _inproj_even(x, mod_l, cond_tokens, gain, w, gate_b, lb_logits, e_idx):
    n_tok = x.shape[0]
    tm = PROJ_TILE
    tile_mod_row = _mod_row(cond_tokens, tm)
    tok = lambda n: pl.BlockSpec((tm, n), lambda i: (i, 0))
    full = lambda a: pl.BlockSpec(a.shape, lambda i: (0,) * a.ndim, pipeline_mode=pl.Buffered(1))
    return pl.pallas_call(
        functools.partial(_inproj_even_kernel, e_idx=e_idx),
        grid=(n_tok // tm,),
        in_specs=[tok(D), pl.BlockSpec((None, 6, D), lambda i: (tile_mod_row(i), 0, 0)),
                  full(gain), full(w), full(gate_b), full(lb_logits)],
        out_specs=[tok(1536), tok(128), tok(1024), tok(1024), tok(1024)],
        out_shape=[jax.ShapeDtypeStruct((n_tok, 1536), BF16), jax.ShapeDtypeStruct((n_tok, 128), F32),
                   jax.ShapeDtypeStruct((n_tok, 1024), BF16), jax.ShapeDtypeStruct((n_tok, 1024), F32),
                   jax.ShapeDtypeStruct((n_tok, 1024), F32)],
        compiler_params=_cparams(1),
        name="inproj_even",
    )(x, mod_l, gain, w, gate_b, lb_logits)


def _shift_rows(x, k, fill, up):
    n = x.shape[0]
    if k % 8 == 0:
        pad = jnp.full((k,) + x.shape[1:], 0.0 if fill is None else fill, x.dtype)
        return jnp.concatenate([x[k:], pad], axis=0) if up else jnp.concatenate([pad, x[:n - k]], axis=0)
    y = pltpu.roll(x, (n - k) if up else k, 0)
    if fill is None:
        return y
    row = lax.broadcasted_iota(jnp.int32, x.shape, 0)
    return jnp.where(row >= n - k, fill, y) if up else jnp.where(row < k, fill, y)


def _cummax_rows(x, rev):
    k = 1
    while k < x.shape[0]:
        x = jnp.maximum(x, _shift_rows(x, k, -jnp.inf, up=rev))
        k *= 2
    return x


def _cumsum_rows(tri, x):
    hi = x.astype(BF16)
    r1 = x - hi.astype(F32)
    mid = r1.astype(BF16)
    lo = (r1 - mid.astype(F32)).astype(BF16)
    return _dot(tri, hi) + _dot(tri, mid) + _dot(tri, lo)


def _hgrn_tile(q, kk, f, a, q_b, rev, mask_sc, d, up_small):
    ops = []
    bm = a
    for li, m in enumerate(TILE_LEVELS):
        if m < 8:
            qrole = jnp.logical_not(up_small[li]) if rev else up_small[li]
            if m == 1:
                r = jnp.where(qrole, q * f, kk)
            else:
                x = jnp.where(qrole, _shift_rows(bm, m, None, up=rev), bm)
                r = jnp.where(qrole, q, kk) * jnp.exp(jnp.where(qrole, a - x, x - a))
            if 2 * m < 8:
                bm = jnp.where(qrole, bm, _shift_rows(bm, m, None, up=not rev))
        else:
            pieces = []
            for b0 in range(0, HALF, 2 * m):
                lo, up = slice(b0, b0 + m), slice(b0 + m, b0 + 2 * m)
                if rev:
                    ref = a[b0 + m:b0 + m + 1]
                    pieces += [q[lo] * jnp.exp(a[lo] - ref), kk[up] * jnp.exp(ref - a[up])]
                else:
                    ref = a[b0 + m - 1:b0 + m]
                    pieces += [kk[lo] * jnp.exp(ref - a[lo]), q[up] * jnp.exp(a[up] - ref)]
            r = jnp.concatenate(pieces, axis=0)
        ops.append(r.astype(BF16))
    p = _dot_nt(q_b, kk.astype(BF16)).astype(BF16) * mask_sc[d, len(TILE_LEVELS)]
    for li, rb in enumerate(ops):
        p = p + _dot_nt(rb, rb).astype(BF16) * mask_sc[d, li]
    return p


def _build_scan_constants(tri_sc, mask_sc):
    row = lax.broadcasted_iota(jnp.int32, (SEG, SEG), 0)
    col = lax.broadcasted_iota(jnp.int32, (SEG, SEG), 1)
    tri_sc[0] = (col <= row).astype(BF16)
    tri_sc[1] = (col >= row).astype(BF16)
    trow = lax.broadcasted_iota(jnp.int32, (HALF, HALF), 0)
    tcol = lax.broadcasted_iota(jnp.int32, (HALF, HALF), 1)
    for li, m in enumerate(TILE_LEVELS):
        sh = m.bit_length() - 1
        same = (trow >> (sh + 1)) == (tcol >> (sh + 1))
        t_up = ((trow >> sh) & 1) == 1
        s_up = ((tcol >> sh) & 1) == 1
        mask_sc[0, li] = (same & t_up & jnp.logical_not(s_up)).astype(BF16)
        mask_sc[1, li] = (same & s_up & jnp.logical_not(t_up)).astype(BF16)
    eye = (trow == tcol).astype(BF16)
    mask_sc[0, len(TILE_LEVELS)] = eye
    mask_sc[1, len(TILE_LEVELS)] = eye


def _scan_segment(dir_refs, write_h, tri_sc, mask_sc, m_in, state, emit):
    row = lax.broadcasted_iota(jnp.int32, (SEG, SEG), 0)
    col = lax.broadcasted_iota(jnp.int32, (SEG, SEG), 1)
    lane128 = lax.broadcasted_iota(jnp.int32, (SEG, HEAD), 1)
    e0 = (lane128 == 0).astype(BF16)
    row_half = lax.broadcasted_iota(jnp.int32, (HALF, HEAD), 0)
    up_masks = [((row_half >> (m.bit_length() - 1)) & 1) == 1 for m in TILE_LEVELS if m < 8]
    tmasks = (col <= row, col >= row)
    use_state = state is not None
    want_state = use_state or emit is not None
    m_out_rows = []
    env = {}

    def prep(d):
        _, gate_ref, _, hgf_ref = dir_refs[d]
        rev = d == 1
        last = 0 if rev else SEG - 1
        tri = tri_sc[d]
        slab = gate_ref[...]
        b_al = pltpu.roll(_cumsum_rows(tri, slab), HEAD - 8, 1)
        u = slab - b_al
        mx = jnp.maximum(_cummax_rows(u, rev), m_in)
        mx_last = mx[last:last + 1, :]
        m_out_rows.append(b_al[last:last + 1, :] + mx_last)
        lf_all = hgf_ref[:, 512 * d:512 * (d + 1)]
        f_all = jnp.exp(lf_all)
        env[d] = dict(rev=rev, last=last, mx=mx, w_inter=jnp.exp(m_in - mx), e_den=jnp.exp(-(b_al + mx)),
                      decay=jnp.exp(m_in - mx_last), wg=jnp.exp(u - mx_last), u_t=u.T,
                      a_all=_cumsum_rows(tri, lf_all), f_all=f_all, kk_all=1.0 - f_all)

    def ml_scores(d, h):
        e, qkv_ref, c = env[d], dir_refs[d][0], 4 * d + h
        q = qkv_ref[:, h * HEAD:(h + 1) * HEAD]
        k = qkv_ref[:, 512 + h * HEAD:512 + (h + 1) * HEAD]
        v = qkv_ref[:, 1024 + h * HEAD:1024 + (h + 1) * HEAD]
        dm = jnp.where(tmasks[d], jnp.exp(e["u_t"][c:c + 1, :] - e["mx"][:, c:c + 1]), 0.0)
        e["ml", h] = (q, k, jnp.concatenate([v, e0], axis=1), (_dot_nt(q, k) * dm).astype(BF16))

    def ml_out(d, h):
        e, c = env[d], 4 * d + h
        q, _, v_aug, s = e["ml", h]
        numden = _dot(s, v_aug)
        if use_state:
            numden = numden + e["w_inter"][:, c:c + 1] * _dot(q, state[0][d, h].astype(BF16))
        den = jnp.maximum(jnp.abs(numden[:, HEAD:HEAD + 1]), e["e_den"][:, c:c + 1])
        write_h(d, h * HEAD, numden[:, 0:HEAD] / den)

    def ml_state(d, h):
        e, c = env[d], 4 * d + h
        _, k, v_aug, _ = e["ml", h]
        kw_t = (k.astype(F32) * e["wg"][:, c:c + 1]).T.astype(BF16)
        upd = _dot(kw_t, v_aug)
        if use_state:
            upd = upd + e["decay"][:, c:c + 1] * state[0][d, h]
            state[0][d, h] = upd
        if emit is not None:
            emit("ml", d, h, upd)

    def hg_cross(d, h):
        e, hqi_ref = env[d], dir_refs[d][2]
        rev = e["rev"]
        sl = slice(h * HEAD, (h + 1) * HEAD)
        a, kk = e["a_all"][:, sl], e["kk_all"][:, sl]
        q_b = hqi_ref[:, h * HEAD:(h + 1) * HEAD]
        q = q_b.astype(F32)
        qh, kh = (0, 1) if rev else (1, 0)
        a_q, a_k = a[qh * HALF:(qh + 1) * HALF], a[kh * HALF:(kh + 1) * HALF]
        a_mid = a[HALF:HALF + 1] if rev else a[HALF - 1:HALF]
        r_q = (q[qh * HALF:(qh + 1) * HALF] * jnp.exp(a_q - a_mid)).astype(BF16)
        r_k = (kk[kh * HALF:(kh + 1) * HALF] * jnp.exp(a_mid - a_k)).astype(BF16)
        e["hg", h] = dict(a=a, kk=kk, f=e["f_all"][:, sl], q_b=q_b, q=q, p_tiles=[],
                          i_b=hqi_ref[:, 512 + h * HEAD:512 + (h + 1) * HEAD], p_cross=_dot_nt(r_q, r_k).astype(BF16))

    def hg_tile(d, h, t):
        g, rows = env[d]["hg", h], slice(t * HALF, (t + 1) * HALF)
        g["p_tiles"].append(_hgrn_tile(g["q"][rows], g["kk"][rows], g["f"][rows], g["a"][rows], g["q_b"][rows],
                                       env[d]["rev"], mask_sc, d, up_masks))

    def hg_out(d, h):
        g = env[d]["hg", h]
        p_tiles, p_cross, i_b = g["p_tiles"], g["p_cross"], g["i_b"]
        i_lo, i_hi = i_b[0:HALF], i_b[HALF:SEG]
        if env[d]["rev"]:
            o = jnp.concatenate([_dot(p_tiles[0], i_lo) + _dot(p_cross, i_hi), _dot(p_tiles[1], i_hi)], axis=0)
        else:
            o = jnp.concatenate([_dot(p_tiles[0], i_lo), _dot(p_cross, i_lo) + _dot(p_tiles[1], i_hi)], axis=0)
        if use_state:
            o = o + _dot_nt((g["q"] * jnp.exp(g["a"])).astype(BF16), state[1][d, h].astype(BF16))
        write_h(d, 512 + h * HEAD, o)

    def hg_state(d, h):
        e, g = env[d], env[d]["hg", h]
        a, kk = g["a"], g["kk"]
        a_l = a[e["last"]:e["last"] + 1, :]
        kd = (kk * jnp.exp(a_l - a)).astype(BF16)
        st_new = _dot(g["i_b"].astype(F32).T.astype(BF16), kd)
        if use_state:
            st_new = st_new + state[1][d, h] * jnp.exp(a_l)
            state[1][d, h] = st_new
        if emit is not None:
            emit("hg", d, h, st_new)

    prep(0)
    prep(1)
    pieces = [ml_scores, hg_cross, functools.partial(hg_tile, t=0), ml_out, functools.partial(hg_tile, t=1)]
    pieces += [ml_state, hg_out, hg_state] if want_state else [hg_out]
    for piece in pieces:
        for h in range(ML_HEADS):
            for d in range(2):
                piece(d, h)

    lane_row = lax.broadcasted_iota(jnp.int32, (1, HEAD), 1)
    return jnp.where(lane_row < 4, m_out_rows[0], jnp.where(lane_row < 8, m_out_rows[1], 0.0))


def _emit_states(kind, d, h, value, c_ref, n_ref, s_ref):
    if kind == "ml":
        c_ref[d, h] = value[:, 0:HEAD]
        n_ref[d, h:h + 1, :] = value[:, HEAD:2 * HEAD].T[0:1, :]
    else:
        s_ref[d, h] = value.T


def _scan_kernel(*refs, has_init, emit_state, nseg):
    single = nseg == 1
    n_tok_in = 4 if single else 8
    n_h = 1 if single else 2
    n_in = n_tok_in + (4 if has_init else 0)
    n_out = n_h + (4 if emit_state else 0)
    ins, outs, scr = refs[:n_in], refs[n_in:n_in + n_out], refs[n_in + n_out:]
    dir_refs = (ins[0:4], ins[0:4] if single else ins[4:8])
    st_outs = outs[n_h:]
    tri_sc, mask_sc, caug_sc, st_sc, m_sc = scr
    use_state = has_init or nseg > 1
    b_id, j = pl.program_id(0), pl.program_id(1)

    @pl.when(jnp.logical_and(b_id == 0, j == 0))
    def _build_constants():
        _build_scan_constants(tri_sc, mask_sc)

    if use_state:
        @pl.when(j == 0)
        def _init_state():
            if has_init:
                c0_ref, n0_ref, m0_ref, s0_ref = ins[n_tok_in:n_tok_in + 4]
                for d in range(2):
                    for h in range(ML_HEADS):
                        caug_sc[d, h, :, 0:HEAD] = c0_ref[d, h]
                        caug_sc[d, h, :, HEAD:2 * HEAD] = jnp.broadcast_to(n0_ref[d, h:h + 1, :], (HEAD, HEAD)).T
                        st_sc[d, h] = s0_ref[d, h].T
                m_sc[...] = jnp.broadcast_to(m0_ref[...], m_sc.shape)
            else:
                caug_sc[...] = jnp.zeros(caug_sc.shape, F32)
                st_sc[...] = jnp.zeros(st_sc.shape, F32)
                m_sc[...] = jnp.zeros(m_sc.shape, F32)

    def write_h(d, c0, value):
        if single and d == 1:
            outs[0][:, c0:c0 + HEAD] += value
        else:
            outs[0 if single else d][:, c0:c0 + HEAD] = value

    def emit(kind, d, h, value):
        if single:
            _emit_states(kind, d, h, value, st_outs[0], st_outs[1], st_outs[3])
        else:
            pl.when(j == nseg - 1)(lambda: _emit_states(kind, d, h, value, st_outs[0], st_outs[1], st_outs[3]))

    m_in = m_sc[0:1, :] if use_state else jnp.zeros((1, HEAD), F32)
    m_new = _scan_segment(dir_refs, write_h, tri_sc, mask_sc, m_in, (caug_sc, st_sc) if use_state else None,
                          emit if emit_state else None)
    if use_state:
        m_sc[...] = jnp.broadcast_to(m_new, m_sc.shape)
    if emit_state:
        if single:
            st_outs[2][...] = m_new
        else:
            @pl.when(j == nseg - 1)
            def _emit_m():
                st_outs[2][...] = m_new


def _scan(qkv, gate, hqi, hgf, n_seq, nseg, init=None, emit_state=False):
    n_tok = qkv.shape[0]
    fwd = lambda n: pl.BlockSpec((SEG, n), lambda b, j: (b * nseg + j, 0))
    bwd = lambda n: pl.BlockSpec((SEG, n), lambda b, j: (b * nseg + nseg - 1 - j, 0))
    widths = (1536, 128, 1024, 1024)
    single = nseg == 1
    in_specs = [fwd(n) for n in widths] + ([] if single else [bwd(n) for n in widths])
    args = [qkv, gate, hqi, hgf] + ([] if single else [qkv, gate, hqi, hgf])
    mat = pl.BlockSpec((None, 2, 4, HEAD, HEAD), lambda b, j: (b, 0, 0, 0, 0))
    vec = pl.BlockSpec((None, 2, 4, HEAD), lambda b, j: (b, 0, 0, 0))
    sca = pl.BlockSpec((None, 1, HEAD), lambda b, j: (b, 0, 0))
    if init is not None:
        in_specs += [mat, vec, sca, mat]
        args += list(init)
    out_specs = [fwd(1024)] if single else [fwd(1024), bwd(1024)]
    out_shape = [jax.ShapeDtypeStruct((n_tok, 1024), F32)] * len(out_specs)
    if emit_state:
        out_specs += [mat, vec, sca, mat]
        out_shape += [jax.ShapeDtypeStruct((n_seq, 2, 4, HEAD, HEAD), F32),
                      jax.ShapeDtypeStruct((n_seq, 2, 4, HEAD), F32),
                      jax.ShapeDtypeStruct((n_seq, 1, HEAD), F32),
                      jax.ShapeDtypeStruct((n_seq, 2, 4, HEAD, HEAD), F32)]
    return pl.pallas_call(
        functools.partial(_scan_kernel, has_init=init is not None, emit_state=emit_state, nseg=nseg),
        grid=(n_seq, nseg),
        in_specs=in_specs,
        out_specs=out_specs,
        out_shape=out_shape,
        scratch_shapes=[pltpu.VMEM((2, SEG, SEG), BF16),
                        pltpu.VMEM((2, len(TILE_LEVELS) + 1, HALF, HALF), BF16),
                        pltpu.VMEM((2, 4, HEAD, 2 * HEAD), F32),
                        pltpu.VMEM((2, 4, HEAD, HEAD), F32),
                        pltpu.VMEM((8, HEAD), F32)],
        compiler_params=_cparams(2),
        name="bidir_scan",
    )(*args)


def _mix_ffn_tile(h, og, x, mod, norm_gain, wo_ref, ffn_gain, w1_ref, w3_ref, w2_ref, final_gain, o_ref, chunk):
    parts = []
    for g in range(D // HEAD):
        hs = h[:, g * HEAD:(g + 1) * HEAD]
        parts.append(hs * lax.rsqrt(jnp.mean(hs * hs, axis=-1, keepdims=True) + EPS))
    hn = jnp.concatenate(parts, axis=1) * norm_gain
    if og is not None:
        hn = hn * jnp.concatenate([_sigmoid(og[:, 0:512]), _silu(og[:, 512:1024])], axis=1)
    x = x + mod[2:3, :] * _dot(hn.astype(BF16), wo_ref[...])
    hf = _norm_mod(x, ffn_gain, mod, 1).astype(BF16)
    up = lambda c0: (_dot(hf, w1_ref[:, c0:min(c0 + chunk, D_FF)]), _dot(hf, w3_ref[:, c0:min(c0 + chunk, D_FF)]))
    starts = list(range(0, D_FF, chunk))
    acc = jnp.zeros(x.shape, F32)
    nxt = up(starts[0])
    for i, c0 in enumerate(starts):
        a, b = nxt
        if i + 1 < len(starts):
            nxt = up(starts[i + 1])
        acc = acc + _dot((_silu(a) * b).astype(BF16), w2_ref[c0:min(c0 + chunk, D_FF), :])
    y = x + mod[5:6, :] * acc
    if final_gain is not None:
        y = y * lax.rsqrt(jnp.mean(y * y, axis=-1, keepdims=True) + EPS) * final_gain
    o_ref[...] = y


def _mix_ffn_kernel(*refs, n_h, gated, final, chunk):
    h_refs = refs[:n_h]
    pos = n_h
    og_ref = refs[pos] if gated else None
    pos += 1 if gated else 0
    x_ref, mod_ref, ng_ref, wo_ref, g_ref, w1_ref, w3_ref, w2_ref = refs[pos:pos + 8]
    fg_ref = refs[pos + 8] if final else None
    o_ref = refs[-1]
    h = h_refs[0][...]
    for r in h_refs[1:]:
        h = h + r[...]
    _mix_ffn_tile(h, og_ref[...] if gated else None, x_ref[...], mod_ref[...], ng_ref[...], wo_ref, g_ref[...],
                  w1_ref, w3_ref, w2_ref, fg_ref[...] if final else None, o_ref, chunk)


def _mix_ffn(hs, og, x, mod_l, cond_tokens, norm_gain, w_out, ffn_gain, layer, w1, w3, w2, final_gain=None):
    n_tok = x.shape[0]
    tm = TOKEN_TILE
    tile_mod_row = _mod_row(cond_tokens, tm)
    tok = pl.BlockSpec((tm, D), lambda i: (i, 0))
    full = lambda a: pl.BlockSpec(a.shape, lambda i: (0,) * a.ndim, pipeline_mode=pl.Buffered(1))
    of_layer = lambda a: pl.BlockSpec((None,) + a.shape[1:], lambda i: (layer, 0, 0), pipeline_mode=pl.Buffered(1))
    gated = og is not None
    final = final_gain is not None
    consts = [norm_gain, w_out, ffn_gain, w1, w3, w2] + ([final_gain] if final else [])
    args = list(hs) + ([og] if gated else []) + [x, mod_l] + consts
    in_specs = [tok] * (len(hs) + (1 if gated else 0) + 1)
    in_specs += [pl.BlockSpec((None, 6, D), lambda i: (tile_mod_row(i), 0, 0))]
    in_specs += [of_layer(a) if a.ndim == 3 else full(a) for a in consts]
    return pl.pallas_call(
        functools.partial(_mix_ffn_kernel, n_h=len(hs), gated=gated, final=final, chunk=FFN_CHUNK),
        grid=(n_tok // tm,),
        in_specs=in_specs,
        out_specs=tok,
        out_shape=jax.ShapeDtypeStruct((n_tok, D), F32),
        compiler_params=_cparams(1),
        name="mix_ffn",
    )(*args)


def _rope(x, cos, sin_signed):
    lane = lax.broadcasted_iota(jnp.int32, (x.shape[0], HEAD), 1)
    first = (lane & 16) == 0
    parts = []
    for h in range(DA_HEADS):
        xh = x[:, h * HEAD:(h + 1) * HEAD]
        partner = jnp.where(first, pltpu.roll(xh, HEAD - 16, 1), pltpu.roll(xh, 16, 1))
        parts.append(xh * cos + partner * sin_signed)
    return jnp.concatenate(parts, axis=1)


def _inproj_odd_kernel(*refs, rope):
    x_ref, mod_ref, g_ref, w_ref = refs[:4]
    q_ref, k_ref, v_ref = refs[-3:]
    h = _norm_mod(x_ref[...], g_ref[...], mod_ref[...], 0).astype(BF16)
    q = _dot(h, w_ref[:, 0:D])
    k = _dot(h, w_ref[:, D:2 * D])
    v = _dot(h, w_ref[:, 2 * D:3 * D])
    if rope:
        cos, sin_signed = refs[4][...], refs[5][...]
        q = _rope(q, cos, sin_signed)
        k = _rope(k, cos, sin_signed)
    q_ref[...] = (q * (DA_DQK ** -0.5 * LOG2E)).astype(q_ref.dtype)
    k_ref[...] = k.astype(k_ref.dtype)
    v_ref[...] = v.astype(v_ref.dtype)


def _inproj_odd(x, mod_l, cond_tokens, gain, w, rope_tables, kv_dtype):
    n_tok = x.shape[0]
    tm = PROJ_TILE
    tile_mod_row = _mod_row(cond_tokens, tm)
    tok = pl.BlockSpec((tm, D), lambda i: (i, 0))
    full = lambda a: pl.BlockSpec(a.shape, lambda i: (0,) * a.ndim, pipeline_mode=pl.Buffered(1))
    args = [x, mod_l, gain, w]
    in_specs = [tok, pl.BlockSpec((None, 6, D), lambda i: (tile_mod_row(i), 0, 0)), full(gain), full(w)]
    if rope_tables is not None:
        tiles_per_seq = rope_tables[0].shape[0] // tm
        args += list(rope_tables)
        in_specs += [pl.BlockSpec((tm, HEAD), lambda i: (i % tiles_per_seq, 0))] * 2
    return pl.pallas_call(
        functools.partial(_inproj_odd_kernel, rope=rope_tables is not None),
        grid=(n_tok // tm,),
        in_specs=in_specs,
        out_specs=[tok, tok, tok],
        out_shape=[jax.ShapeDtypeStruct((n_tok, D), BF16), jax.ShapeDtypeStruct((n_tok, D), kv_dtype),
                   jax.ShapeDtypeStruct((n_tok, D), kv_dtype)],
        compiler_params=_cparams(1),
        name="inproj_odd",
    )(*args)


def _rope_tables(n_tok):
    quarter = DA_DQK // 4
    tok = np.arange(n_tok)
    pos = np.stack([tok // GRID_W, tok % GRID_W], axis=1).astype(np.float32)
    inv = (np.float32(ROPE_BASE) ** (-np.arange(quarter, dtype=np.float32) / np.float32(quarter))).astype(np.float32)
    lane = np.arange(HEAD)
    ang = (pos[:, (lane // 32) % 2] * inv[lane % quarter][None, :]).astype(np.float32)
    sign = np.where((lane % 32) < quarter, -1.0, 1.0).astype(np.float32)
    return jnp.asarray(np.cos(ang), F32), jnp.asarray(np.sin(ang) * sign[None, :], F32)


def _lambda(lam_ref, lam_init):
    lp = lam_ref[...]
    return (jnp.exp(jnp.sum(lp[0:1] * lp[1:2], axis=-1, keepdims=True))
            - jnp.exp(jnp.sum(lp[2:3] * lp[3:4], axis=-1, keepdims=True)) + lam_init)


VT_ROWS = HEAD + 16


def _qt2(qh):
    qt = qh.astype(F32).T
    row = lax.broadcasted_iota(jnp.int32, qt.shape, 0)
    return jnp.concatenate([jnp.where(row < DA_DQK, qt, 0.0), jnp.where(row >= DA_DQK, qt, 0.0)],
                           axis=1).astype(BF16)


def _vt_aug(vh):
    tk = vh.shape[0]
    row = lax.broadcasted_iota(jnp.int32, (VT_ROWS - HEAD, tk), 0)
    return jnp.concatenate([vh.astype(F32).T, (row == 0).astype(F32)], axis=0).astype(BF16)


def _diff_attn_heads(q_ref, k_fn, vt_fn, n_chunks, lam, o_ref, n_seq=1):
    tq = q_ref.shape[0] // n_seq
    items = [(s, h, c) for s in range(n_seq) for h in range(DA_HEADS) for c in range(n_chunks)]
    qt2, state, pending = {}, {}, []

    def finish(s, h, c, st):
        cm = jnp.max(st, axis=0, keepdims=True)
        if c == 0:
            m_new = cm
            acc = _dot(vt_fn(s, h, c), jnp.exp2(st - m_new).astype(BF16))
        else:
            m, acc = state[s, h]
            m_new = jnp.maximum(m, cm)
            acc = acc * jnp.exp2(m - m_new) + _dot(vt_fn(s, h, c), jnp.exp2(st - m_new).astype(BF16))
        state[s, h] = (m_new, acc)
        if c == n_chunks - 1:
            den = acc[HEAD:HEAD + 1, :]
            o_t = acc[0:HEAD, 0:tq] * (1.0 / den[:, 0:tq]) - acc[0:HEAD, tq:] * (lam / den[:, tq:])
            o_ref[s * tq:(s + 1) * tq, h * HEAD:(h + 1) * HEAD] = o_t.T

    for s, h, c in items:
        if c == 0:
            qt2[s, h] = _qt2(q_ref[s * tq:(s + 1) * tq, h * HEAD:(h + 1) * HEAD])
        pending.append((s, h, c, _dot(k_fn(s, h, c), qt2[s, h])))
        if len(pending) > ATTN_LOOKAHEAD:
            finish(*pending.pop(0))
    while pending:
        finish(*pending.pop(0))


def _attn_prompt_kernel(q_ref, k_ref, v_ref, lam_ref, o_ref, *, lam_init, seq):
    lam = _lambda(lam_ref, lam_init)
    rows = lambda s: slice(s * seq, (s + 1) * seq)
    _diff_attn_heads(q_ref, lambda s, h, c: k_ref[rows(s), h * HEAD:(h + 1) * HEAD].astype(BF16),
                     lambda s, h, c: _vt_aug(v_ref[rows(s), h * HEAD:(h + 1) * HEAD]), 1, lam, o_ref,
                     n_seq=q_ref.shape[0] // seq)


def _attn_prompt(q, k, v, lam_p, seq, lam_init):
    n_tok = q.shape[0]
    tok = pl.BlockSpec((PROMPT_SEQS_PER_STEP * seq, D), lambda b: (b, 0))
    return pl.pallas_call(
        functools.partial(_attn_prompt_kernel, lam_init=lam_init, seq=seq),
        grid=(n_tok // (PROMPT_SEQS_PER_STEP * seq),),
        in_specs=[tok, tok, tok, pl.BlockSpec(lam_p.shape, lambda b: (0, 0))],
        out_specs=tok,
        out_shape=jax.ShapeDtypeStruct((n_tok, D), F32),
        compiler_params=_cparams(1),
        name="diff_attn_prompt",
    )(q, k, v, lam_p)


def _attn_sample_kernel(q_ref, k_ref, v_ref, ck_ref, cv_ref, lam_ref, o_ref, kcat, vtcat, *, lam_init, past):
    n_keys = kcat.shape[0]
    chunks = [(c0, min(KEY_CHUNK, n_keys - c0)) for c0 in range(0, n_keys, KEY_CHUNK)]

    @pl.when(pl.program_id(1) == 0)
    def _gather_keys():
        kcat[0:past, :] = ck_ref[...].astype(BF16)
        kcat[past:, :] = k_ref[...]
        for h in range(DA_HEADS):
            sl = slice(h * HEAD, (h + 1) * HEAD)
            for c0 in range(0, n_keys, past):
                src, r0 = (cv_ref, c0) if c0 < past else (v_ref, c0 - past)
                vtcat[h, :, c0:c0 + past] = _vt_aug(src[r0:r0 + past, sl])

    lam = _lambda(lam_ref, lam_init)
    _diff_attn_heads(q_ref, lambda s, h, c: kcat[chunks[c][0]:chunks[c][0] + chunks[c][1], h * HEAD:(h + 1) * HEAD],
                     lambda s, h, c: vtcat[h, :, chunks[c][0]:chunks[c][0] + chunks[c][1]], len(chunks), lam, o_ref)


def _attn_sample(q, k, v, ck, cv, lam_p, n_seq, seq, lam_init, tq=128):
    past = ck.shape[1]
    assert seq % past == 0 and past % KEY_CHUNK == 0 and seq % tq == 0
    qb = pl.BlockSpec((tq, D), lambda b, i: (b * (seq // tq) + i, 0))
    kv = pl.BlockSpec((seq, D), lambda b, i: (b, 0))
    cache = pl.BlockSpec((None, past, D), lambda b, i: (b, 0, 0))
    return pl.pallas_call(
        functools.partial(_attn_sample_kernel, lam_init=lam_init, past=past),
        grid=(n_seq, seq // tq),
        in_specs=[qb, kv, kv, cache, cache, pl.BlockSpec(lam_p.shape, lambda b, i: (0, 0))],
        out_specs=qb,
        out_shape=jax.ShapeDtypeStruct((n_seq * seq, D), F32),
        scratch_shapes=[pltpu.VMEM((past + seq, D), BF16), pltpu.VMEM((DA_HEADS, VT_ROWS, past + seq), BF16)],
        compiler_params=_cparams(2),
        name="diff_attn_sample",
    )(q, k, v, ck, cv, lam_p)


def kernel(x_prompt, x_sample, c, c_ctx, cache_attn_k, cache_attn_v, state_mlstm_C, state_mlstm_n, state_mlstm_m,
           state_hgrn_S, ada_w, ada_b, norm_mix_g, norm_ffn_g, ev_w_in, ev_gate_b, ev_lb_logits, ml_norm_g,
           hg_norm_g, ev_w_out, od_w_in, od_lambda, da_norm_g, od_w_out, ffn_w1, ffn_w3, ffn_w2, final_norm_g):
    assert DEPTH % 2 == 0
    n_p, s_p, _ = x_prompt.shape
    n_s, s_s, _ = x_sample.shape
    past = cache_attn_k.shape[2]
    assert s_p == SEG and s_s % SEG == 0 and s_s % PROJ_TILE == 0 and (n_p * s_p) % PROJ_TILE == 0
    assert PROJ_TILE % TOKEN_TILE == 0
    xp = x_prompt.astype(F32).reshape(n_p * s_p, D)
    xs = x_sample.astype(F32).reshape(n_s * s_s, D)

    cond8 = jnp.zeros((8, D), F32).at[0].set(c_ctx.astype(F32)).at[1:1 + n_s].set(c.astype(F32))
    mod = _modulation(cond8, ada_w.astype(F32), ada_b.astype(F32)).reshape(DEPTH, 8, 6, D)
    streams = [(xp, None), (xs, s_s)]
    outputs = {}
    w1_all, w3_all, w2_all = ffn_w1.astype(BF16), ffn_w3.astype(BF16), ffn_w2.astype(BF16)

    for l in range(DEPTH):
        mix_gain = norm_mix_g[l].astype(F32).reshape(1, D)
        ffn_gain = norm_ffn_g[l].astype(F32).reshape(1, D)
        if l % 2 == 0:
            e = l // 2
            w_in = _regroup_even_weights(ev_w_in, e)
            gate_b = jnp.pad(ev_gate_b[e].astype(F32), (0, HEAD - 16)).reshape(1, HEAD)
            norm_gain = jnp.concatenate([ml_norm_g[e], hg_norm_g[e]]).astype(F32).reshape(1, D)
            w_out = ev_w_out[e].astype(BF16)
            mixed = []
            for si, (x, cond_tok) in enumerate(streams):
                qkv, gate, hqi, hgf, og = _inproj_even(x, mod[l], cond_tok, mix_gain, w_in, gate_b,
                                                       ev_lb_logits.astype(F32), e)
                if si == 0:
                    *hs, c_new, n_new, m_new, s_new = _scan(qkv, gate, hqi, hgf, n_p, s_p // SEG, emit_state=True)
                    outputs.setdefault("C", []).append(c_new)
                    outputs.setdefault("n", []).append(n_new)
                    outputs.setdefault("m", []).append(m_new[:, 0, 0:8].reshape(n_p, 2, 4))
                    outputs.setdefault("S", []).append(s_new)
                else:
                    init = (state_mlstm_C[:, e].astype(F32), state_mlstm_n[:, e].astype(F32),
                            jnp.pad(state_mlstm_m[:, e].astype(F32).reshape(n_s, 1, 8), ((0, 0), (0, 0), (0, HEAD - 8))),
                            state_hgrn_S[:, e].astype(F32))
                    hs = _scan(qkv, gate, hqi, hgf, n_s, s_s // SEG, init=init)
                mixed.append((list(hs), og, x))
        else:
            o = l // 2
            lam_init = 0.8 - 0.6 * math.exp(-0.3 * l)
            w_in = od_w_in[o].astype(BF16)
            norm_gain = (jnp.tile(da_norm_g[o].astype(F32), DA_HEADS) * (1.0 - lam_init)).reshape(1, D)
            w_out = od_w_out[o].astype(BF16)
            lam_p = od_lambda[o].astype(F32)
            mixed = []
            for si, (x, cond_tok) in enumerate(streams):
                if si == 0:
                    q, k, v = _inproj_odd(x, mod[l], cond_tok, mix_gain, w_in, None, F32)
                    outputs.setdefault("k", []).append(k.reshape(n_p, s_p, DA_HEADS, HEAD))
                    outputs.setdefault("v", []).append(v.reshape(n_p, s_p, DA_HEADS, HEAD))
                    att = _attn_prompt(q, k, v, lam_p, s_p, lam_init)
                else:
                    q, k, v = _inproj_odd(x, mod[l], cond_tok, mix_gain, w_in, _rope_tables(s_s), BF16)
                    ck = cache_attn_k[:, o].reshape(n_s, past, D)
                    cv = cache_attn_v[:, o].reshape(n_s, past, D)
                    att = _attn_sample(q, k, v, ck, cv, lam_p, n_s, s_s, lam_init)
                mixed.append(([att], None, x))
        final_gain = final_norm_g.astype(F32).reshape(1, D) if l == DEPTH - 1 else None
        streams = [(_mix_ffn(hs, og, x, mod[l], cond_tok, norm_gain, w_out, ffn_gain, l, w1_all, w3_all, w2_all,
                             final_gain), cond_tok)
                   for (hs, og, x), (_, cond_tok) in zip(mixed, streams)]

    y_prompt = streams[0][0].reshape(n_p, s_p, D)
    y_sample = streams[1][0].reshape(n_s, s_s, D)
    stack = lambda name: outputs[name][0][:, None] if len(outputs[name]) == 1 else jnp.stack(outputs[name], axis=1)
    return (y_prompt, y_sample, stack("k"), stack("v"), stack("C"), stack("n"), stack("m"), stack("S"))
```
